```python
import jax
import jax.numpy as jnp
from jax import lax
import numpy as np


D_MODEL = 1024
BATCH = 8
SEQ = 8192
DEPTH = 1

GRID_W = 64
CTX_LEN = 256
RET_HEADS = 4
RET_W = D_MODEL // 2
RET_HEAD_DIM = RET_W // RET_HEADS
MLSTM_HEADS = 4
MLSTM_W = D_MODEL // 2
MLSTM_HEAD_DIM = MLSTM_W // MLSTM_HEADS
MIX_W = RET_W + MLSTM_W
GATE_COLS = 4 * MLSTM_HEADS
IN_COLS = 4 * RET_W + 4 * MLSTM_W + GATE_COLS
CONV_W = 3
CHUNK = 128
ROPE_BASE = 10000.0
N_EXPERTS = 64
TOP_K = 8
N_GROUPS = 8
TOPK_GROUPS = 4
EXPERT_FF = 256
SHARED_FF = 256
ROUTED_SCALE = 2.5
EXPERT_BLOCK = 128
N_MOD = 6
EPS = 1e-6

kernel_name = 'hybrid_retention_mlstm_moe_dit_block'


def rmsnorm(x, g):
    x32 = x.astype(jnp.float32)
    y = x32 * lax.rsqrt(jnp.mean(x32 * x32, axis=-1, keepdims=True) + EPS)
    return (y * g.astype(jnp.float32)).astype(x.dtype)


def modulate(h, shift, scale):
    return h * (1.0 + scale) + shift


def head_norm(t, gain):
    t32 = t.astype(jnp.float32)
    mu = jnp.mean(t32, axis=-1, keepdims=True)
    var = jnp.mean(jnp.square(t32 - mu), axis=-1, keepdims=True)
    y = (t32 - mu) * lax.rsqrt(var + EPS)
    B, H, N, Dh = t.shape
    return y.transpose(0, 2, 1, 3).reshape(B, N, H * Dh) * gain.astype(jnp.float32)


def split_heads(t, n_heads):
    B, N, W = t.shape
    return t.reshape(B, N, n_heads, W // n_heads).transpose(0, 2, 1, 3)


def flip_seq(t):
    return None if t is None else jnp.flip(t, axis=2)


def to_chunks(t):
    B, H, N = t.shape[:3]
    return jnp.moveaxis(t.reshape(B, H, N // CHUNK, CHUNK, *t.shape[3:]), 2, 0)


def from_chunks(t):
    t = jnp.moveaxis(t, 0, 2)
    B, H, nc, C = t.shape[:4]
    return t.reshape(B, H, nc * C, *t.shape[4:])


def rope_tables(rows, cols):
    quarter = RET_HEAD_DIM // 4
    freqs = ROPE_BASE ** (-jnp.arange(quarter, dtype=jnp.float32) / quarter)
    ang = jnp.concatenate([rows[:, None] * freqs, cols[:, None] * freqs], axis=-1)
    return jnp.cos(ang), jnp.sin(ang)


def rope_2d(t, cos, sin):
    half = t.shape[-1] // 2
    t1, t2 = t[..., :half], t[..., half:]
    cos = cos.astype(t.dtype)[None, None]
    sin = sin.astype(t.dtype)[None, None]
    return jnp.concatenate([t1 * cos - t2 * sin, t2 * cos + t1 * sin], axis=-1)


def conv3(t, w, b):
    tp = jnp.pad(t, ((0, 0), (1, 1), (0, 0)))
    return tp[:, :-2] * w[0] + tp[:, 1:-1] * w[1] + tp[:, 2:] * w[2] + b


def retention_scan(q, k, v, log_gamma, s0):
    idx = jnp.arange(CHUNK, dtype=jnp.float32)
    lg = log_gamma[:, None]
    k_decay = jnp.exp((CHUNK - 1 - idx) * lg)[None, :, :, None]
    q_decay = jnp.exp((idx + 1) * lg)[None, :, :, None]
    chunk_decay = jnp.exp(CHUNK * log_gamma)[None, :, None, None]
    rel = idx[:, None] - idx[None, :]
    intra = jnp.where(rel >= 0, jnp.exp(jnp.maximum(rel, 0.0) * lg[:, :, None]), 0.0)[None]
    xs = (to_chunks(k), to_chunks(v)) + (() if q is None else (to_chunks(q),))

    def step(S, blk):
        kb, vb = blk[0], blk[1]
        S_next = S * chunk_decay + jnp.einsum('bhcd,bhce->bhde', kb * k_decay, vb)
        if q is None:
            return S_next, None
        qb = blk[2]
        scores = jnp.einsum('bhid,bhjd->bhij', qb, kb) * intra
        out = (jnp.einsum('bhij,bhje->bhie', scores, vb)
               + jnp.einsum('bhid,bhde->bhie', qb * q_decay, S))
        return S_next, out

    s_fin, outs = lax.scan(step, s0, xs)
    return (None if q is None else from_chunks(outs)), s_fin


def mlstm_scan(q, k, v, i_pre, f_pre, state0):
    idx = jnp.arange(CHUNK)
    causal = (idx[:, None] >= idx[None, :])
    log_f = jax.nn.log_sigmoid(f_pre)
    xs = (to_chunks(k), to_chunks(v), to_chunks(i_pre), to_chunks(log_f)) + (
        () if q is None else (to_chunks(q),))

    def step(carry, blk):
        Cm, n, m = carry
        kb, vb, ib, lfb = blk[0], blk[1], blk[2], blk[3]
        b = jnp.cumsum(lfb, axis=-1)
        b_tot = b[..., -1]
        log_ws = b_tot[..., None] - b + ib
        m_next = jnp.maximum(b_tot + m, jnp.max(log_ws, axis=-1))
        decay_prev = jnp.exp(b_tot + m - m_next)
        ws = jnp.exp(log_ws - m_next[..., None])
        C_next = decay_prev[..., None, None] * Cm + jnp.einsum('bhcd,bhce->bhde', kb * ws[..., None], vb)
        n_next = decay_prev[..., None] * n + jnp.einsum('bhc,bhcd->bhd', ws, kb)
        if q is None:
            return (C_next, n_next, m_next), None
        qb = blk[4]
        log_w = jnp.where(causal, b[..., :, None] - b[..., None, :] + ib[..., None, :], -jnp.inf)
        log_inter = b + m[..., None]
        m_i = jnp.maximum(log_inter, jnp.max(log_w, axis=-1))
        w = jnp.exp(log_w - m_i[..., None])
        inter = jnp.exp(log_inter - m_i)
        scores = jnp.einsum('bhid,bhjd->bhij', qb, kb) * w
        num = (jnp.einsum('bhij,bhje->bhie', scores, vb)
               + inter[..., None] * jnp.einsum('bhid,bhde->bhie', qb, Cm))
        den = jnp.sum(scores, axis=-1) + inter * jnp.einsum('bhid,bhd->bhi', qb, n)
        h = num / jnp.maximum(jnp.abs(den), jnp.exp(-m_i))[..., None]
        return (C_next, n_next, m_next), h

    s_fin, outs = lax.scan(step, state0, xs)
    return (None if q is None else from_chunks(outs)), s_fin


def zero_states(B):
    r = jnp.zeros((B, RET_HEADS, RET_HEAD_DIM, RET_HEAD_DIM), jnp.float32)
    m = (jnp.zeros((B, MLSTM_HEADS, MLSTM_HEAD_DIM, MLSTM_HEAD_DIM), jnp.float32),
         jnp.zeros((B, MLSTM_HEADS, MLSTM_HEAD_DIM), jnp.float32),
         jnp.zeros((B, MLSTM_HEADS), jnp.float32))
    return (r, r, m, m)


def mixer_sublayer(h, rope, init, w_in, log_decay, ret_norm_g, conv_w, conv_b,
                   gate_b, mlstm_norm_g, w_out, with_output):
    B, N, _ = h.shape
    R, M = RET_W, MLSTM_W
    p = h @ w_in
    rq, rk, rv, rg = p[..., :R], p[..., R:2 * R], p[..., 2 * R:3 * R], p[..., 3 * R:4 * R]
    o0 = 4 * R
    mqk = jax.nn.silu(conv3(p[..., o0:o0 + 2 * M], conv_w, conv_b))
    mv = p[..., o0 + 2 * M:o0 + 3 * M]
    mo = p[..., o0 + 3 * M:o0 + 4 * M]
    gates = (p[..., o0 + 4 * M:].astype(jnp.float32) + gate_b.astype(jnp.float32).reshape(-1))
    gates = gates.reshape(B, N, 4, MLSTM_HEADS).transpose(2, 0, 3, 1)
    i_f, f_f, i_b, f_b = gates[0], gates[1], gates[2], gates[3]

    rq = split_heads(rq, RET_HEADS) * (RET_HEAD_DIM ** -0.5)
    rk = split_heads(rk, RET_HEADS)
    if rope is not None:
        rq = rope_2d(rq, *rope)
        rk = rope_2d(rk, *rope)
    rv = split_heads(rv, RET_HEADS)
    mq = split_heads(mqk[..., :M], MLSTM_HEADS)
    mk = split_heads(mqk[..., M:], MLSTM_HEADS) * (MLSTM_HEAD_DIM ** -0.5)
    mv = split_heads(mv, MLSTM_HEADS)
    log_gamma = -jnp.exp(log_decay.astype(jnp.float32))

    qr = rq if with_output else None
    qm = mq if with_output else None
    rs_f, rs_b, ms_f, ms_b = init
    ret_f, rs_f = retention_scan(qr, rk, rv, log_gamma[0], rs_f)
    ret_b, rs_b = retention_scan(flip_seq(qr), flip_seq(rk), flip_seq(rv), log_gamma[1], rs_b)
    ml_f, ms_f = mlstm_scan(qm, mk, mv, i_f, f_f, ms_f)
    ml_b, ms_b = mlstm_scan(flip_seq(qm), flip_seq(mk), flip_seq(mv), flip_seq(i_b), flip_seq(f_b), ms_b)
    states = (rs_f, rs_b, ms_f, ms_b)
    if not with_output:
        return None, states
    ret = head_norm(ret_f + flip_seq(ret_b), ret_norm_g) * jax.nn.silu(rg.astype(jnp.float32))
    mls = head_norm(ml_f + flip_seq(ml_b), mlstm_norm_g) * jax.nn.sigmoid(mo.astype(jnp.float32))
    y = jnp.concatenate([ret, mls], axis=-1).astype(h.dtype) @ w_out
    return y, states


def swiglu(t, wg, wu, wd):
    return (jax.nn.silu(t @ wg) * (t @ wu)) @ wd


def routed_experts(t, eidx, wts, w_gate, w_up, w_down):
    T, D = t.shape
    A = T * TOP_K
    e_flat = eidx.reshape(-1)
    tok_flat = jnp.repeat(jnp.arange(T, dtype=jnp.int32), TOP_K)
    w_flat = wts.reshape(-1)
    order = jnp.argsort(e_flat)
    e_s, tok_s, w_s = e_flat[order], tok_flat[order], w_flat[order]
    sizes = jnp.zeros((N_EXPERTS,), jnp.int32).at[e_flat].add(1)
    starts = jnp.cumsum(sizes) - sizes
    padded = (sizes + EXPERT_BLOCK - 1) // EXPERT_BLOCK * EXPERT_BLOCK
    pends = jnp.cumsum(padded)
    pstarts = pends - padded
    dest = pstarts[e_s] + (jnp.arange(A, dtype=jnp.int32) - starts[e_s])
    P = -(-(A + N_EXPERTS * (EXPERT_BLOCK - 1)) // EXPERT_BLOCK) * EXPERT_BLOCK
    NB = P // EXPERT_BLOCK
    buf_tok = jnp.zeros((P,), jnp.int32).at[dest].set(tok_s)
    buf_w = jnp.zeros((P,), t.dtype).at[dest].set(w_s)
    block_e = jnp.clip(jnp.searchsorted(pends, jnp.arange(NB, dtype=jnp.int32) * EXPERT_BLOCK,
                                        side='right'), 0, N_EXPERTS - 1)

    def body(y, blk):
        tok, wb, e = blk
        xb = t[tok]
        hb = jax.nn.silu(xb @ w_gate[e]) * (xb @ w_up[e])
        return y.at[tok].add((hb @ w_down[e]) * wb[:, None]), None

    y, _ = lax.scan(body, jnp.zeros_like(t),
                    (buf_tok.reshape(NB, EXPERT_BLOCK), buf_w.reshape(NB, EXPERT_BLOCK), block_e))
    return y


def moe_ffn(h, w_router, router_bias, w_gate, w_up, w_down, ws_gate, ws_up, ws_down):
    B, N, D = h.shape
    T = B * N
    t = h.reshape(T, D)
    scores = jax.nn.sigmoid((t @ w_router).astype(jnp.float32))
    sel = scores + router_bias.astype(jnp.float32)
    grp = lax.top_k(sel.reshape(T, N_GROUPS, N_EXPERTS // N_GROUPS), 2)[0].sum(-1)
    _, gidx = lax.top_k(grp, TOPK_GROUPS)
    gmask = jax.nn.one_hot(gidx, N_GROUPS, dtype=jnp.float32).sum(1) > 0
    emask = jnp.repeat(gmask, N_EXPERTS // N_GROUPS, axis=1)
    _, eidx = lax.top_k(jnp.where(emask, sel, -jnp.inf), TOP_K)
    wts = jnp.take_along_axis(scores, eidx, axis=1)
    wts = wts / jnp.sum(wts, axis=-1, keepdims=True) * ROUTED_SCALE
    routed = routed_experts(t, eidx, wts.astype(t.dtype), w_gate, w_up, w_down)
    shared = swiglu(t, ws_gate, ws_up, ws_down)
    return (routed + shared).reshape(B, N, D)


def setup_inputs(seed: int = 0) -> dict:
    key = jax.random.key(seed)
    ks = jax.random.split(key, 24)
    f32 = jnp.float32

    def nrm(k, shape, scale):
        return jax.random.normal(k, shape, f32) * scale

    D = D_MODEL
    decay_base = jnp.asarray(np.log(-np.log(1.0 - 2.0 ** (-5.0 - np.arange(RET_HEADS)))), dtype=f32)
    fb = jnp.linspace(3.0, 6.0, MLSTM_HEADS, dtype=f32)
    zb = jnp.zeros((MLSTM_HEADS,), f32)
    gate_base = jnp.stack([zb, fb, zb, fb])
    return {
        'x': nrm(ks[0], (BATCH, SEQ, D), 1.0),
        'c': nrm(ks[1], (BATCH, D), 1.0),
        'ctx': nrm(ks[2], (BATCH, CTX_LEN, D), 1.0),
        'c_ctx': nrm(ks[3], (D,), 1.0),
        'w_mod': nrm(ks[4], (DEPTH, D, N_MOD * D), D ** -0.5),
        'b_mod': nrm(ks[5], (DEPTH, N_MOD * D), 0.02),
        'norm_g': 1.0 + nrm(ks[6], (DEPTH, 4, D), 0.05),
        'w_in': nrm(ks[7], (DEPTH, D, IN_COLS), D ** -0.5),
        'ret_log_decay': decay_base[None, None, :] + nrm(ks[8], (DEPTH, 2, RET_HEADS), 0.1),
        'ret_norm_g': 1.0 + nrm(ks[9], (DEPTH, RET_W), 0.05),
        'mlstm_conv_w': nrm(ks[10], (DEPTH, CONV_W, 2 * MLSTM_W), CONV_W ** -0.5),
        'mlstm_conv_b': nrm(ks[11], (DEPTH, 2 * MLSTM_W), 0.02),
        'mlstm_gate_b': gate_base[None] + nrm(ks[12], (DEPTH, 4, MLSTM_HEADS), 0.1),
        'mlstm_norm_g': 1.0 + nrm(ks[13], (DEPTH, MLSTM_W), 0.05),
        'w_out': nrm(ks[14], (DEPTH, MIX_W, D), MIX_W ** -0.5),
        'w_router': nrm(ks[15], (DEPTH, D, N_EXPERTS), D ** -0.5),
        'router_bias': nrm(ks[16], (DEPTH, N_EXPERTS), 0.01),
        'w_gate': nrm(ks[17], (DEPTH, N_EXPERTS, D, EXPERT_FF), D ** -0.5),
        'w_up': nrm(ks[18], (DEPTH, N_EXPERTS, D, EXPERT_FF), D ** -0.5),
        'w_down': nrm(ks[19], (DEPTH, N_EXPERTS, EXPERT_FF, D), EXPERT_FF ** -0.5),
        'ws_gate': nrm(ks[20], (DEPTH, D, SHARED_FF), D ** -0.5),
        'ws_up': nrm(ks[21], (DEPTH, D, SHARED_FF), D ** -0.5),
        'ws_down': nrm(ks[22], (DEPTH, SHARED_FF, D), SHARED_FF ** -0.5),
    }


def reference(x, c, ctx, c_ctx, w_mod, b_mod, norm_g, w_in, ret_log_decay, ret_norm_g,
              mlstm_conv_w, mlstm_conv_b, mlstm_gate_b, mlstm_norm_g, w_out, w_router,
              router_bias, w_gate, w_up, w_down, ws_gate, ws_up, ws_down):
    B, N, D = x.shape
    ROWS = N // GRID_W
    rows = jnp.repeat(jnp.arange(ROWS, dtype=jnp.float32), GRID_W)
    cols = jnp.tile(jnp.arange(GRID_W, dtype=jnp.float32), ROWS)
    rope = rope_tables(rows, cols)
    for l in range(DEPTH):
        last = l == DEPTH - 1
        mod = (jax.nn.silu(c) @ w_mod[l] + b_mod[l]).reshape(B, N_MOD, D)
        sh1, sc1, g1, sh2, sc2, g2 = [mod[:, i][:, None, :] for i in range(N_MOD)]
        mod_c = (jax.nn.silu(c_ctx) @ w_mod[l] + b_mod[l]).reshape(N_MOD, D)
        csh1, csc1, cg1, csh2, csc2, cg2 = [mod_c[i] for i in range(N_MOD)]
        mix_params = (w_in[l], ret_log_decay[l], ret_norm_g[l], mlstm_conv_w[l], mlstm_conv_b[l],
                      mlstm_gate_b[l], mlstm_norm_g[l], w_out[l])
        h_ctx = modulate(rmsnorm(ctx, norm_g[l, 0]), csh1, csc1)
        y_ctx, ctx_states = mixer_sublayer(h_ctx, None, zero_states(ctx.shape[0]), *mix_params,
                                           with_output=not last)
        h_lat = modulate(rmsnorm(x, norm_g[l, 0]), sh1, sc1)
        y_lat, _ = mixer_sublayer(h_lat, rope, ctx_states, *mix_params, with_output=True)
        x = x + g1 * rmsnorm(y_lat, norm_g[l, 1])
        moe_params = (w_router[l], router_bias[l], w_gate[l], w_up[l], w_down[l],
                      ws_gate[l], ws_up[l], ws_down[l])
        if not last:
            ctx = ctx + cg1 * rmsnorm(y_ctx, norm_g[l, 1])
            hc = modulate(rmsnorm(ctx, norm_g[l, 2]), csh2, csc2)
            ctx = ctx + cg2 * rmsnorm(moe_ffn(hc, *moe_params), norm_g[l, 3])
        h = modulate(rmsnorm(x, norm_g[l, 2]), sh2, sc2)
        x = x + g2 * rmsnorm(moe_ffn(h, *moe_params), norm_g[l, 3])
    return x
```

```python
import functools

import jax
import jax.numpy as jnp
from jax import lax
from jax.experimental import pallas as pl
from jax.experimental.pallas import tpu as pltpu

F32 = jnp.float32
BF16 = jnp.bfloat16
I32 = jnp.int32

EPS = 1e-6
LANES = 128
CHUNK = 128
HEADS = 4
GRID_W = 64
ROPE_BASE = 10000.0
N_GROUPS = 8
TOPK_GROUPS = 4
TOP_K = 8
ROUTED_SCALE = 2.5
N_MOD = 6
ROW_TILE = 128
NEG_INF = float("-inf")


def _sigmoid(v):
    return 1.0 / (1.0 + jnp.exp(-v))


def _silu(v):
    return v * _sigmoid(v)


def _log_sigmoid(v):
    return jnp.minimum(v, 0.0) - jnp.log(1.0 + jnp.exp(-jnp.abs(v)))


def _dot(a, b):
    return jnp.dot(a, b, preferred_element_type=F32)


def _dot_nt(a, b):
    return lax.dot_general(a, b, (((1,), (1,)), ((), ())), preferred_element_type=F32)


def _dot_tn(a, b):
    return lax.dot_general(a, b, (((0,), (0,)), ((), ())), preferred_element_type=F32)


def _split3(a):
    hi = a.astype(BF16)
    r = a - hi.astype(F32)
    mid = r.astype(BF16)
    lo = (r - mid.astype(F32)).astype(BF16)
    return hi, mid, lo


def _rms(v, g):
    ms = jnp.mean(v * v, axis=-1, keepdims=True)
    return v * lax.rsqrt(ms + EPS) * g


def _pack_pair(lo, hi):
    lo_b = pltpu.bitcast(lo.astype(BF16).astype(F32), jnp.uint32) >> 16
    hi_b = pltpu.bitcast(hi.astype(BF16).astype(F32), jnp.uint32) & jnp.uint32(0xFFFF0000)
    return pltpu.bitcast(lo_b | hi_b, I32)


def _unpack_pair(w):
    u = pltpu.bitcast(w, jnp.uint32)
    lo = pltpu.bitcast(u << 16, F32)
    hi = pltpu.bitcast(u & jnp.uint32(0xFFFF0000), F32)
    return lo, hi


def _pack_rows(v, ref):
    for s in range(v.shape[1] // 256):
        ref[:, s, :] = _pack_pair(v[:, s * 256:s * 256 + 128], v[:, s * 256 + 128:(s + 1) * 256])


def _unpack_rows(ref):
    parts = []
    for s in range(ref.shape[1]):
        lo, hi = _unpack_pair(ref[:, s, :])
        parts += [lo, hi]
    return jnp.concatenate(parts, axis=1)


def _mod_kernel(c_ref, w_ref, b_ref, o_ref):
    a = _silu(c_ref[...])
    o_ref[...] = jnp.dot(a, w_ref[...], preferred_element_type=F32,
                         precision=lax.Precision.HIGHEST) + b_ref[...]


def _modulation(cc, w_mod, b_mod):
    rows, d = cc.shape
    cols = w_mod.shape[1]
    tn = d
    return pl.pallas_call(
        _mod_kernel,
        grid=(cols // tn,),
        in_specs=[pl.BlockSpec((rows, d), lambda j: (0, 0)),
                  pl.BlockSpec((d, tn), lambda j: (0, j)),
                  pl.BlockSpec((1, tn), lambda j: (0, j))],
        out_specs=pl.BlockSpec((rows, tn), lambda j: (0, j)),
        out_shape=jax.ShapeDtypeStruct((rows, cols), F32),
        name="mod",
    )(cc, w_mod, b_mod.reshape(1, cols))


def _inproj_kernel(x_ref, xp_ref, xn_ref, mod_ref, g_ref, w_ref, wgt_ref, wg_ref, cw_ref, cb_ref,
                   gbr_ref, gbc_ref, cos_ref, sin_ref, p_ref, gr_ref, gc_ref, *, ts, d):
    i = pl.program_id(1)
    last = pl.num_programs(1) - 1
    r_w = d // 2
    shift = mod_ref[0, 0:1, :]
    scale = mod_ref[0, 1:2, :]
    g = g_ref[...]

    def normmod(v):
        return _rms(v, g) * (1.0 + scale) + shift

    hb = normmod(x_ref[0]).astype(BF16)
    halo = jnp.concatenate([xp_ref[0], xn_ref[0]], axis=0)
    ph = _dot(normmod(halo).astype(BF16), w_ref[:, 4 * r_w:4 * r_w + 2 * r_w])
    prev_row = jnp.where(i == 0, 0.0, ph[7:8, :])
    next_row = jnp.where(i == last, 0.0, ph[8:9, :])

    cos2 = cos_ref[...]
    sin2 = sin_ref[...]
    rows = lax.broadcasted_iota(I32, (ts, r_w), 0)
    qscale = LANES ** -0.5

    for j in range(8):
        acc = _dot(hb, w_ref[:, j * r_w:(j + 1) * r_w])
        if j in (0, 1):
            if j == 0:
                acc = acc * qscale
            parts = []
            for h in range(HEADS):
                t = acc[:, h * LANES:(h + 1) * LANES]
                parts.append(t * cos2 + pltpu.roll(t, LANES // 2, axis=1) * sin2)
            acc = jnp.concatenate(parts, axis=1)
        elif j in (4, 5):
            c0 = (j - 4) * r_w
            pr = prev_row[:, c0:c0 + r_w]
            nx = next_row[:, c0:c0 + r_w]
            down = jnp.where(rows == 0, pr, pltpu.roll(acc, 1, axis=0))
            up = jnp.where(rows == ts - 1, nx, pltpu.roll(acc, ts - 1, axis=0))
            cw = cw_ref[:, c0:c0 + r_w]
            acc = down * cw[0:1, :] + acc * cw[1:2, :] + up * cw[2:3, :] + cb_ref[:, c0:c0 + r_w]
            acc = _silu(acc)
            if j == 5:
                acc = acc * qscale
        p_ref[0, :, j * r_w:(j + 1) * r_w] = acc.astype(BF16)

    gr = _dot_nt(wgt_ref[...], hb) + gbr_ref[...]
    ch_r = lax.broadcasted_iota(I32, gr.shape, 0)
    gr_ref[0] = jnp.where((ch_r // HEADS) % 2 == 1, _log_sigmoid(gr), gr)
    gc = _dot(hb, wg_ref[...]) + gbc_ref[...]
    ch_c = lax.broadcasted_iota(I32, gc.shape, 1)
    gc_ref[0] = jnp.where((ch_c // HEADS) % 2 == 1, _log_sigmoid(gc), gc)


def _inproj(x, mod3, mod_row, g, w_main, wgt, wg, conv_w, conv_b, gb_col, gb_row, cos2, sin2, ts):
    b, n, d = x.shape
    nt = n // ts
    nb8 = n // 8
    hb = ts // 8
    cols = w_main.shape[1]
    if mod_row is None:
        mod_map = lambda bi, i: (bi, 0, 0)
    else:
        mod_map = lambda bi, i: (mod_row, 0, 0)
    const2 = lambda bi, i: (0, 0)
    kern = functools.partial(_inproj_kernel, ts=ts, d=d)
    return pl.pallas_call(
        kern,
        grid=(b, nt),
        in_specs=[
            pl.BlockSpec((1, ts, d), lambda bi, i: (bi, i, 0)),
            pl.BlockSpec((1, 8, d), lambda bi, i: (bi, jnp.maximum(i * hb - 1, 0), 0)),
            pl.BlockSpec((1, 8, d), lambda bi, i: (bi, jnp.minimum((i + 1) * hb, nb8 - 1), 0)),
            pl.BlockSpec((1, N_MOD, d), mod_map),
            pl.BlockSpec((1, d), const2),
            pl.BlockSpec((d, cols), const2),
            pl.BlockSpec((16, d), const2),
            pl.BlockSpec((d, 16), const2),
            pl.BlockSpec((3, d), const2),
            pl.BlockSpec((1, d), const2),
            pl.BlockSpec((16, 1), const2),
            pl.BlockSpec((1, 16), const2),
            pl.BlockSpec((ts, LANES), lambda bi, i: (i, 0)),
            pl.BlockSpec((ts, LANES), lambda bi, i: (i, 0)),
        ],
        out_specs=[
            pl.BlockSpec((1, ts, cols), lambda bi, i: (bi, i, 0)),
            pl.BlockSpec((1, 16, ts), lambda bi, i: (bi, 0, i)),
            pl.BlockSpec((1, ts, 16), lambda bi, i: (bi, i, 0)),
        ],
        out_shape=[
            jax.ShapeDtypeStruct((b, n, cols), BF16),
            jax.ShapeDtypeStruct((b, 16, n), F32),
            jax.ShapeDtypeStruct((b, n, 16), F32),
        ],
        compiler_params=pltpu.CompilerParams(dimension_semantics=("parallel", "parallel")),
        name="inproj",
    )(x, x, x, mod3, g, w_main, wgt, wg, conv_w, conv_b, gb_col, gb_row, cos2, sin2)


def _scan_kernel(pf_ref, pb_ref, grf_ref, grb_ref, gcf_ref, gcb_ref, intra_ref, kd_ref, qd_ref, cd_ref,
                 rs0_ref, mc0_ref, mm0_ref, *out_refs, with_output, r_w):
    if with_output:
        of_ref, ob_ref, rs_ref, mc_ref, mm_ref = out_refs
    else:
        rs_ref, mc_ref, mm_ref = out_refs
    j = pl.program_id(1)

    @pl.when(j == 0)
    def _():
        rs_ref[...] = rs0_ref[...]
        mc_ref[...] = mc0_ref[...]
        mm_ref[...] = mm0_ref[...]

    c = CHUNK
    row = lax.broadcasted_iota(I32, (c, c), 0)
    col = lax.broadcasted_iota(I32, (c, c), 1)
    tri_le = (row <= col)
    tri_ge = (row >= col)
    ones_ext = jnp.where(lax.broadcasted_iota(I32, (c, LANES), 1) == 0, 1.0, 0.0).astype(BF16)

    def cumsums(gr, gc, fwd):
        m_row = jnp.where(tri_le if fwd else tri_ge, 1.0, 0.0).astype(BF16)
        m_col = jnp.where(tri_ge if fwd else tri_le, 1.0, 0.0).astype(BF16)
        b_row = sum(_dot(piece, m_row) for piece in _split3(gr))
        b_col = sum(_dot(m_col, piece) for piece in _split3(gc))
        return b_row, b_col

    for dr in range(2):
        fwd = dr == 0
        p_ref = pf_ref if fwd else pb_ref
        o_ref = None
        if with_output:
            o_ref = of_ref if fwd else ob_ref
        gr = (grf_ref if fwd else grb_ref)[0]
        gc = (gcf_ref if fwd else gcb_ref)[0]
        cs_row, cs_col = cumsums(gr, gc, fwd)
        causal = tri_ge if fwd else tri_le

        for h in range(HEADS):
            hs = slice(h * LANES, (h + 1) * LANES)
            st = dr * HEADS + h
            k = p_ref[0, :, r_w + h * LANES:r_w + (h + 1) * LANES]
            v = p_ref[0, :, 2 * r_w + h * LANES:2 * r_w + (h + 1) * LANES]
            s_prev = rs_ref[0, st]
            ks = (k.astype(F32) * kd_ref[st]).astype(BF16)
            upd = _dot_tn(ks, v)
            if with_output:
                q = p_ref[0, :, hs]
                sc = _dot_nt(q, k) * intra_ref[st]
                out = _dot(sc.astype(BF16), v) + qd_ref[st] * _dot(q, s_prev.astype(BF16))
                o_ref[0, :, hs] = out.astype(BF16)
            rs_ref[0, st] = s_prev * cd_ref[st:st + 1, :] + upd

            o0 = 4 * r_w
            mk = p_ref[0, :, o0 + r_w + h * LANES:o0 + r_w + (h + 1) * LANES]
            mv = p_ref[0, :, o0 + 2 * r_w + h * LANES:o0 + 2 * r_w + (h + 1) * LANES]
            ci = dr * 2 * HEADS + h
            cf = ci + HEADS
            ib_row = gr[ci:ci + 1, :]
            b_row = cs_row[cf:cf + 1, :]
            ib_col = gc[:, ci:ci + 1]
            b_col = cs_col[:, cf:cf + 1]
            b_tot = b_row[:, c - 1:c] if fwd else b_row[:, 0:1]
            m_prev = mm_ref[0, st:st + 1, 0:1]
            log_ws_row = b_tot - b_row + ib_row
            m_next = jnp.maximum(b_tot + m_prev, jnp.max(log_ws_row, axis=1, keepdims=True))
            decay_prev = jnp.exp(b_tot + m_prev - m_next)
            ws_col = jnp.exp(b_tot - b_col + ib_col - m_next)
            kw = (mk.astype(F32) * ws_col).astype(BF16)
            v_ext = jnp.concatenate([mv, ones_ext], axis=1)
            upd_c = _dot_tn(kw, v_ext)
            c_prev = mc_ref[0, st]
            if with_output:
                mq = p_ref[0, :, o0 + h * LANES:o0 + (h + 1) * LANES]
                dm = jnp.where(causal, b_col - b_row + ib_row, NEG_INF)
                log_inter = b_col + m_prev
                m_i = jnp.maximum(log_inter, jnp.max(dm, axis=1, keepdims=True))
                w = jnp.exp(dm - m_i)
                inter = jnp.exp(log_inter - m_i)
                sc = _dot_nt(mq, mk) * w
                hx = _dot(sc.astype(BF16), v_ext) + inter * _dot(mq, c_prev.astype(BF16))
                den = hx[:, LANES:LANES + 1]
                hout = hx[:, :LANES] / jnp.maximum(jnp.abs(den), jnp.exp(-m_i))
                o_ref[0, :, r_w + h * LANES:r_w + (h + 1) * LANES] = hout.astype(BF16)
            mc_ref[0, st] = decay_prev * c_prev + upd_c
            mm_ref[0, st:st + 1, :] = jnp.broadcast_to(m_next, (1, LANES))


def _scan(p, g_row, g_col, tabs, states, with_output):
    b, n, cols = p.shape
    nch = n // CHUNK
    r_w = cols // 8
    intra, kd, qd, cd = tabs
    rs0, mc0, mm0 = states
    nst = 2 * HEADS
    fwd3 = lambda bi, j: (bi, j, 0)
    bwd3 = lambda bi, j: (bi, nch - 1 - j, 0)
    c3 = lambda bi, j: (0, 0, 0)
    st4 = lambda bi, j: (bi, 0, 0, 0)
    in_specs = [
        pl.BlockSpec((1, CHUNK, cols), fwd3),
        pl.BlockSpec((1, CHUNK, cols), bwd3),
        pl.BlockSpec((1, 16, CHUNK), lambda bi, j: (bi, 0, j)),
        pl.BlockSpec((1, 16, CHUNK), lambda bi, j: (bi, 0, nch - 1 - j)),
        pl.BlockSpec((1, CHUNK, 16), fwd3),
        pl.BlockSpec((1, CHUNK, 16), bwd3),
        pl.BlockSpec((nst, CHUNK, LANES), c3),
        pl.BlockSpec((nst, CHUNK, LANES), c3),
        pl.BlockSpec((nst, CHUNK, LANES), c3),
        pl.BlockSpec((nst, LANES), lambda bi, j: (0, 0)),
        pl.BlockSpec((1, nst, LANES, LANES), st4),
        pl.BlockSpec((1, nst, LANES, 2 * LANES), st4),
        pl.BlockSpec((1, nst, LANES), lambda bi, j: (bi, 0, 0)),
    ]
    st_specs = [
        pl.BlockSpec((1, nst, LANES, LANES), st4),
        pl.BlockSpec((1, nst, LANES, 2 * LANES), st4),
        pl.BlockSpec((1, nst, LANES), lambda bi, j: (bi, 0, 0)),
    ]
    st_shapes = [
        jax.ShapeDtypeStruct((b, nst, LANES, LANES), F32),
        jax.ShapeDtypeStruct((b, nst, LANES, 2 * LANES), F32),
        jax.ShapeDtypeStruct((b, nst, LANES), F32),
    ]
    if with_output:
        out_specs = [pl.BlockSpec((1, CHUNK, 2 * r_w), fwd3), pl.BlockSpec((1, CHUNK, 2 * r_w), bwd3)] + st_specs
        out_shape = [jax.ShapeDtypeStruct((b, n, 2 * r_w), BF16)] * 2 + st_shapes
    else:
        out_specs, out_shape = st_specs, st_shapes
    kern = functools.partial(_scan_kernel, with_output=with_output, r_w=r_w)
    return pl.pallas_call(
        kern,
        grid=(b, nch),
        in_specs=in_specs,
        out_specs=out_specs,
        out_shape=out_shape,
        compiler_params=pltpu.CompilerParams(dimension_semantics=("parallel", "arbitrary")),
        name="scan_out" if with_output else "scan_state",
    )(p, p, g_row, g_row, g_col, g_col, intra, kd, qd, cd, rs0, mc0, mm0)


def _post_kernel(x_ref, of_ref, ob_ref, rg_ref, mo_ref, mod_ref, ng_ref, hg_ref, wo_ref, wrh_ref, wrl_ref,
                 rb_ref, su_ref, x1_ref, hp_ref, ei_ref, pos_ref, wt_ref, cnt_ref, carry_ref, *, ts, d, n_exp):
    step = pl.program_id(0) * pl.num_programs(1) + pl.program_id(1)

    @pl.when(step == 0)
    def _():
        carry_ref[...] = jnp.zeros_like(carry_ref)

    r_w = d // 2
    s = of_ref[0].astype(F32) + ob_ref[0].astype(F32)
    parts = []
    for gi in range(2 * HEADS):
        sl = s[:, gi * LANES:(gi + 1) * LANES]
        mu = jnp.mean(sl, axis=-1, keepdims=True)
        dv = sl - mu
        var = jnp.mean(dv * dv, axis=-1, keepdims=True)
        y = dv * lax.rsqrt(var + EPS) * hg_ref[:, gi * LANES:(gi + 1) * LANES]
        if gi < HEADS:
            gate = _silu(rg_ref[0, :, gi * LANES:(gi + 1) * LANES].astype(F32))
        else:
            gate = _sigmoid(mo_ref[0, :, (gi - HEADS) * LANES:(gi - HEADS + 1) * LANES].astype(F32))
        parts.append((y * gate).astype(BF16))
    mixed = jnp.concatenate(parts, axis=1)
    y = _dot(mixed, wo_ref[...])
    g1 = mod_ref[0, 2:3, :]
    sh2 = mod_ref[0, 3:4, :]
    sc2 = mod_ref[0, 4:5, :]
    x1 = x_ref[0] + g1 * _rms(y, ng_ref[1:2, :])
    x1_ref[0] = x1
    h2 = _rms(x1, ng_ref[2:3, :]) * (1.0 + sc2) + sh2
    _pack_rows(h2, hp_ref)

    h_hi = h2.astype(BF16)
    h_lo = (h2 - h_hi.astype(F32)).astype(BF16)
    logits = _dot_nt(wrh_ref[...], h_hi) + _dot_nt(wrh_ref[...], h_lo) + _dot_nt(wrl_ref[...], h_hi)
    scores = _sigmoid(logits)
    sel = scores + rb_ref[...]
    gsz = n_exp // N_GROUPS
    iota_g = lax.broadcasted_iota(I32, (gsz, ts), 0).astype(F32)
    grp = []
    for gi in range(N_GROUPS):
        blk = sel[gi * gsz:(gi + 1) * gsz, :]
        m1 = jnp.max(blk, axis=0, keepdims=True)
        i1 = jnp.min(jnp.where(blk == m1, iota_g, float(gsz)), axis=0, keepdims=True)
        m2 = jnp.max(jnp.where(iota_g == i1, NEG_INF, blk), axis=0, keepdims=True)
        grp.append(m1 + m2)
    masked_parts = []
    for gi in range(N_GROUPS):
        rank = jnp.zeros((1, ts), F32)
        for gj in range(N_GROUPS):
            if gj == gi:
                continue
            beats = (grp[gj] >= grp[gi]) if gj < gi else (grp[gj] > grp[gi])
            rank = rank + jnp.where(beats, 1.0, 0.0)
        keep = rank < float(TOPK_GROUPS)
        masked_parts.append(jnp.where(keep, sel[gi * gsz:(gi + 1) * gsz, :], NEG_INF))
    masked = jnp.concatenate(masked_parts, axis=0)

    iota_e = lax.broadcasted_iota(I32, (n_exp, ts), 0).astype(F32)
    picks, wsel = [], []
    selmask = jnp.zeros((n_exp, ts), F32)
    for _ in range(TOP_K):
        mx = jnp.max(masked, axis=0, keepdims=True)
        ei = jnp.min(jnp.where(masked == mx, iota_e, float(n_exp)), axis=0, keepdims=True)
        hit = iota_e == ei
        wsel.append(jnp.sum(jnp.where(hit, scores, 0.0), axis=0, keepdims=True))
        selmask = jnp.where(hit, 1.0, selmask)
        masked = jnp.where(hit, NEG_INF, masked)
        picks.append(ei)
    wsum = wsel[0]
    for kk in range(1, TOP_K):
        wsum = wsum + wsel[kk]
    before = carry_ref[:, 0:1] + _dot(selmask.astype(BF16), su_ref[...])
    for kk in range(TOP_K):
        hit = iota_e == picks[kk]
        ei_ref[kk:kk + 1, :] = picks[kk].astype(I32)
        pos_ref[kk:kk + 1, :] = jnp.sum(jnp.where(hit, before, 0.0), axis=0, keepdims=True).astype(I32)
        wt_ref[kk:kk + 1, :] = wsel[kk] / wsum * ROUTED_SCALE
    total = carry_ref[...] + jnp.sum(selmask, axis=1, keepdims=True)
    carry_ref[...] = total
    cnt_ref[...] = total


def _post(x, o_f, o_b, p, mod3, norm_g, head_g, w_out, wr_hi, wr_lo, rbias, ts):
    b, n, d = x.shape
    nt = n // ts
    t_all = b * n
    n_exp = wr_hi.shape[0]
    r_w = d // 2
    su = jnp.where(lax.broadcasted_iota(I32, (ts, ts), 0) < lax.broadcasted_iota(I32, (ts, ts), 1),
                   1.0, 0.0).astype(BF16)
    tok3 = lambda bi, i: (bi, i, 0)
    c2 = lambda bi, i: (0, 0)
    flat = lambda bi, i: (0, bi * nt + i)
    kern = functools.partial(_post_kernel, ts=ts, d=d, n_exp=n_exp)
    return pl.pallas_call(
        kern,
        grid=(b, nt),
        in_specs=[
            pl.BlockSpec((1, ts, d), tok3),
            pl.BlockSpec((1, ts, d), tok3),
            pl.BlockSpec((1, ts, d), tok3),
            pl.BlockSpec((1, ts, r_w), lambda bi, i: (bi, i, 3)),
            pl.BlockSpec((1, ts, r_w), lambda bi, i: (bi, i, 7)),
            pl.BlockSpec((1, N_MOD, d), lambda bi, i: (bi, 0, 0)),
            pl.BlockSpec((4, d), c2),
            pl.BlockSpec((1, d), c2),
            pl.BlockSpec((d, d), c2),
            pl.BlockSpec((n_exp, d), c2),
            pl.BlockSpec((n_exp, d), c2),
            pl.BlockSpec((n_exp, 1), c2),
            pl.BlockSpec((ts, ts), c2),
        ],
        out_specs=[
            pl.BlockSpec((1, ts, d), tok3),
            pl.BlockSpec((ts, d // 256, LANES), lambda bi, i: (bi * nt + i, 0, 0)),
            pl.BlockSpec((TOP_K, ts), flat),
            pl.BlockSpec((TOP_K, ts), flat),
            pl.BlockSpec((TOP_K, ts), flat),
            pl.BlockSpec((n_exp, LANES), c2),
        ],
        out_shape=[
            jax.ShapeDtypeStruct((b, n, d), F32),
            jax.ShapeDtypeStruct((t_all, d // 256, LANES), I32),
            jax.ShapeDtypeStruct((TOP_K, t_all), I32),
            jax.ShapeDtypeStruct((TOP_K, t_all), I32),
            jax.ShapeDtypeStruct((TOP_K, t_all), F32),
            jax.ShapeDtypeStruct((n_exp, LANES), F32),
        ],
        scratch_shapes=[pltpu.VMEM((n_exp, LANES), F32)],
        compiler_params=pltpu.CompilerParams(dimension_semantics=("arbitrary", "arbitrary")),
        name="post",
    )(x, o_f, o_b, p, p, mod3, norm_g, head_g, w_out, wr_hi, wr_lo, rbias, su)


def _row_copy_out(x_ref, xs_hbm, sem, r, dst):
    return pltpu.make_async_copy(x_ref.at[r], xs_hbm.at[dst], sem)


def _dispatch_kernel(dest_ref, x_ref, xs_hbm, sem):
    def body(r, carry):
        for kk in range(TOP_K):
            _row_copy_out(x_ref, xs_hbm, sem, r, dest_ref[0, 0, kk * ROW_TILE + r]).start()
        return carry

    lax.fori_loop(0, ROW_TILE, body, 0)

    def drain(r, carry):
        for kk in range(TOP_K):
            _row_copy_out(x_ref, xs_hbm, sem, 0, 0).wait()
        return carry

    lax.fori_loop(0, ROW_TILE, drain, 0)


def _dispatch(hp, dest_tiles, p_rows):
    t_all, s, _ = hp.shape
    nt = t_all // ROW_TILE
    return pl.pallas_call(
        _dispatch_kernel,
        grid=(nt,),
        in_specs=[
            pl.BlockSpec((1, 1, TOP_K * ROW_TILE), lambda i: (i, 0, 0), memory_space=pltpu.SMEM),
            pl.BlockSpec((ROW_TILE, s, LANES), lambda i: (i, 0, 0)),
        ],
        out_specs=pl.BlockSpec(memory_space=pl.ANY),
        out_shape=jax.ShapeDtypeStruct((p_rows, s, LANES), I32),
        scratch_shapes=[pltpu.SemaphoreType.DMA(())],
        compiler_params=pltpu.CompilerParams(dimension_semantics=("arbitrary",), has_side_effects=True),
        name="dispatch",
    )(dest_tiles, hp)


def _expert_kernel(be_ref, xs_ref, wg_ref, wu_ref, wd_ref, ys_ref):
    del be_ref
    xb = _unpack_rows(xs_ref).astype(BF16)
    a = _silu(_dot(xb, wg_ref[0])) * _dot(xb, wu_ref[0])
    _pack_rows(_dot(a.astype(BF16), wd_ref[0]), ys_ref)


def _experts(xs, block_e, w_gate, w_up, w_down, blk):
    p_rows, s, _ = xs.shape
    n_exp, d, ff = w_gate.shape
    nb = p_rows // blk
    grid_spec = pltpu.PrefetchScalarGridSpec(
        num_scalar_prefetch=1,
        grid=(nb,),
        in_specs=[
            pl.BlockSpec((blk, s, LANES), lambda i, be: (i, 0, 0)),
            pl.BlockSpec((1, d, ff), lambda i, be: (be[i], 0, 0)),
            pl.BlockSpec((1, d, ff), lambda i, be: (be[i], 0, 0)),
            pl.BlockSpec((1, ff, d), lambda i, be: (be[i], 0, 0)),
        ],
        out_specs=pl.BlockSpec((blk, s, LANES), lambda i, be: (i, 0, 0)),
    )
    return pl.pallas_call(
        _expert_kernel,
        grid_spec=grid_spec,
        out_shape=jax.ShapeDtypeStruct((p_rows, s, LANES), I32),
        compiler_params=pltpu.CompilerParams(dimension_semantics=("arbitrary",)),
        name="experts",
    )(block_e, xs, w_gate, w_up, w_down)


def _row_copy_in(ys_hbm, gbuf, sem, src, kk, r):
    return pltpu.make_async_copy(ys_hbm.at[src], gbuf.at[kk, r], sem)


def _combine_kernel(dest_ref, ys_hbm, wt_ref, x1_ref, hp_ref, mod_ref, ng_ref, sg_ref, su_ref, sd_ref,
                    o_ref, gbuf, sem, *, d):
    def issue(r, carry):
        for kk in range(TOP_K):
            _row_copy_in(ys_hbm, gbuf, sem, dest_ref[0, 0, kk * ROW_TILE + r], kk, r).start()
        return carry

    lax.fori_loop(0, ROW_TILE, issue, 0)

    xb = _unpack_rows(hp_ref).astype(BF16)
    a = _silu(_dot(xb, sg_ref[...])) * _dot(xb, su_ref[...])
    tot = _dot(a.astype(BF16), sd_ref[...])

    def drain(r, carry):
        for kk in range(TOP_K):
            _row_copy_in(ys_hbm, gbuf, sem, 0, 0, 0).wait()
        return carry

    lax.fori_loop(0, ROW_TILE, drain, 0)

    for kk in range(TOP_K):
        tot = tot + wt_ref[:, kk:kk + 1] * _unpack_rows(gbuf.at[kk])
    g2 = mod_ref[0, 5:6, :]
    o_ref[...] = x1_ref[...] + g2 * _rms(tot, ng_ref[3:4, :])


def _combine(ys, dest_tiles, wts_tm, x1_flat, hp, mod3, norm_g, ws_gate, ws_up, ws_down, n_seq):
    t_all, d = x1_flat.shape
    s = hp.shape[1]
    nt = t_all // ROW_TILE
    per_b = n_seq // ROW_TILE
    ff = ws_gate.shape[1]
    c2 = lambda i: (0, 0)
    kern = functools.partial(_combine_kernel, d=d)
    return pl.pallas_call(
        kern,
        grid=(nt,),
        in_specs=[
            pl.BlockSpec((1, 1, TOP_K * ROW_TILE), lambda i: (i, 0, 0), memory_space=pltpu.SMEM),
            pl.BlockSpec(memory_space=pl.ANY),
            pl.BlockSpec((ROW_TILE, TOP_K), lambda i: (i, 0)),
            pl.BlockSpec((ROW_TILE, d), lambda i: (i, 0)),
            pl.BlockSpec((ROW_TILE, s, LANES), lambda i: (i, 0, 0)),
            pl.BlockSpec((1, N_MOD, d), lambda i: (i // per_b, 0, 0)),
            pl.BlockSpec((4, d), c2),
            pl.BlockSpec((d, ff), c2),
            pl.BlockSpec((d, ff), c2),
            pl.BlockSpec((ff, d), c2),
        ],
        out_specs=pl.BlockSpec((ROW_TILE, d), lambda i: (i, 0)),
        out_shape=jax.ShapeDtypeStruct((t_all, d), F32),
        scratch_shapes=[pltpu.VMEM((TOP_K, ROW_TILE, s, LANES), I32), pltpu.SemaphoreType.DMA(())],
        compiler_params=pltpu.CompilerParams(dimension_semantics=("arbitrary",)),
        name="combine",
    )(dest_tiles, ys, wts_tm, x1_flat, hp, mod3, norm_g, ws_gate, ws_up, ws_down)


def _rope_tables(n):
    rows = jnp.repeat(jnp.arange(n // GRID_W, dtype=F32), GRID_W)
    cols = jnp.tile(jnp.arange(GRID_W, dtype=F32), n // GRID_W)
    quarter = LANES // 4
    freqs = ROPE_BASE ** (-jnp.arange(quarter, dtype=F32) / quarter)
    ang = jnp.concatenate([rows[:, None] * freqs, cols[:, None] * freqs], axis=-1)
    cos, sin = jnp.cos(ang), jnp.sin(ang)
    return jnp.concatenate([cos, cos], axis=-1), jnp.concatenate([-sin, sin], axis=-1)


def _retention_tables(log_decay):
    lg = -jnp.exp(log_decay.astype(F32))
    idx = jnp.arange(CHUNK, dtype=F32)
    rel = idx[:, None] - idx[None, :]
    lg3 = lg[:, :, None, None]
    intra_f = jnp.where(rel >= 0, jnp.exp(jnp.maximum(rel, 0.0) * lg3[0]), 0.0)
    intra_b = jnp.where(rel <= 0, jnp.exp(jnp.maximum(-rel, 0.0) * lg3[1]), 0.0)
    kd_f = jnp.exp((CHUNK - 1 - idx)[None, :] * lg[0][:, None])
    kd_b = jnp.exp(idx[None, :] * lg[1][:, None])
    qd_f = jnp.exp((idx + 1)[None, :] * lg[0][:, None])
    qd_b = jnp.exp((CHUNK - idx)[None, :] * lg[1][:, None])
    bc = lambda t: jnp.broadcast_to(t[:, :, None], (HEADS, CHUNK, LANES))
    intra = jnp.concatenate([intra_f, intra_b], axis=0)
    kd = jnp.concatenate([bc(kd_f), bc(kd_b)], axis=0)
    qd = jnp.concatenate([bc(qd_f), bc(qd_b)], axis=0)
    cd = jnp.broadcast_to(jnp.exp(CHUNK * lg).reshape(2 * HEADS, 1), (2 * HEADS, LANES))
    return intra, kd, qd, cd


def kernel(x, c, ctx, c_ctx, w_mod, b_mod, norm_g, w_in, ret_log_decay, ret_norm_g, mlstm_conv_w,
           mlstm_conv_b, mlstm_gate_b, mlstm_norm_g, w_out, w_router, router_bias, w_gate, w_up, w_down,
           ws_gate, ws_up, ws_down):
    b, n, d = x.shape
    n_ctx = ctx.shape[1]
    depth = w_mod.shape[0]
    assert depth == 1, "only the single-layer configuration is implemented"
    assert d // 2 // HEADS == LANES
    n_exp = w_router.shape[2]
    t_all = b * n
    r_w = d // 2
    main_cols = 8 * r_w
    l = 0

    pad = (-(b + 1)) % 8
    cc = jnp.concatenate([c, c_ctx[None, :], jnp.zeros((pad, d), F32)], axis=0)
    mod3 = _modulation(cc, w_mod[l], b_mod[l]).reshape(b + 1 + pad, N_MOD, d)

    w_main = w_in[l, :, :main_cols].astype(BF16)
    wg = w_in[l, :, main_cols:].astype(BF16)
    wgt = wg.T
    gb = mlstm_gate_b[l].reshape(-1).astype(F32)
    tabs = _retention_tables(ret_log_decay[l])
    head_g = jnp.concatenate([ret_norm_g[l], mlstm_norm_g[l]]).reshape(1, d).astype(F32)
    wr = w_router[l].T.astype(F32)
    wr_hi = wr.astype(BF16)
    wr_lo = (wr - wr_hi.astype(F32)).astype(BF16)

    def inproj(seq, mod_row, ts):
        cos2, sin2 = _rope_tables(n) if mod_row is None else (
            jnp.ones((seq.shape[1], LANES), F32), jnp.zeros((seq.shape[1], LANES), F32))
        return _inproj(seq, mod3, mod_row, norm_g[l, 0:1], w_main, wgt, wg, mlstm_conv_w[l],
                       mlstm_conv_b[l].reshape(1, -1), gb.reshape(16, 1), gb.reshape(1, 16), cos2, sin2, ts)

    nst = 2 * HEADS
    zero_states = (jnp.zeros((b, nst, LANES, LANES), F32), jnp.zeros((b, nst, LANES, 2 * LANES), F32),
                   jnp.zeros((b, nst, LANES), F32))
    p_c, gr_c, gc_c = inproj(ctx, b, min(n_ctx, 512))
    ctx_states = _scan(p_c, gr_c, gc_c, tabs, zero_states, with_output=False)

    ts = min(n, 512)
    p_l, gr_l, gc_l = inproj(x, None, ts)
    o_f, o_b, _, _, _ = _scan(p_l, gr_l, gc_l, tabs, tuple(ctx_states), with_output=True)
    ts_post = min(n, 256)
    x1, hp, eidx_t, pos_t, wts_t, counts = _post(
        x, o_f, o_b, p_l, mod3, norm_g[l], head_g, w_out[l].astype(BF16), wr_hi, wr_lo,
        router_bias[l].reshape(n_exp, 1).astype(F32), ts_post)

    blk = 512 if t_all * TOP_K >= 64 * 512 else 128
    sizes = counts[:, 0].astype(I32)
    padded = (sizes + blk - 1) // blk * blk
    pends = jnp.cumsum(padded)
    pstarts = pends - padded
    p_rows = -(-(t_all * TOP_K + n_exp * (blk - 1)) // blk) * blk
    nb = p_rows // blk
    dest_t = pstarts[eidx_t] + pos_t
    nt = t_all // ROW_TILE
    dest_tiles = dest_t.reshape(TOP_K, nt, ROW_TILE).transpose(1, 0, 2).reshape(nt, 1, TOP_K * ROW_TILE)
    block_e = jnp.clip(jnp.searchsorted(pends, jnp.arange(nb, dtype=I32) * blk, side='right'),
                       0, n_exp - 1).astype(I32)

    xs = _dispatch(hp, dest_tiles, p_rows)
    ys = _experts(xs, block_e, w_gate[l].astype(BF16), w_up[l].astype(BF16), w_down[l].astype(BF16), blk)
    out = _combine(ys, dest_tiles, wts_t.T, x1.reshape(t_all, d), hp, mod3, norm_g[l],
                   ws_gate[l].astype(BF16), ws_up[l].astype(BF16), ws_down[l].astype(BF16), n)
    return out.reshape(b, n, d)
```

```python
import functools

import jax
import jax.numpy as jnp
from jax import lax
from jax.experimental import pallas as pl
from jax.experimental.pallas import tpu as pltpu

F32 = jnp.float32
BF16 = jnp.bfloat16
I32 = jnp.int32

EPS = 1e-6
LANES = 128
CHUNK = 128
HEADS = 4
GRID_W = 64
ROPE_BASE = 10000.0
N_GROUPS = 8
TOPK_GROUPS = 4
TOP_K = 8
ROUTED_SCALE = 2.5
N_MOD = 6
MOE_TILE = 256
MOE_CHUNK = 256
SLOT_ROWS = 48
EXPERT_BLOCK = 512
NEG_INF = float("-inf")


def _sigmoid(v):
    return 1.0 / (1.0 + jnp.exp(-v))


def _silu(v):
    return v * _sigmoid(v)


def _log_sigmoid(v):
    return jnp.minimum(v, 0.0) - jnp.log(1.0 + jnp.exp(-jnp.abs(v)))


def _dot(a, b):
    return jnp.dot(a, b, preferred_element_type=F32)


def _dot_nt(a, b):
    return lax.dot_general(a, b, (((1,), (1,)), ((), ())), preferred_element_type=F32)


def _dot_tn(a, b):
    return lax.dot_general(a, b, (((0,), (0,)), ((), ())), preferred_element_type=F32)


def _split3(a):
    hi = a.astype(BF16)
    r = a - hi.astype(F32)
    mid = r.astype(BF16)
    lo = (r - mid.astype(F32)).astype(BF16)
    return hi, mid, lo


def _rms(v, g):
    ms = jnp.mean(v * v, axis=-1, keepdims=True)
    return v * lax.rsqrt(ms + EPS) * g


def _pack_pair(lo, hi):
    lo_b = pltpu.bitcast(lo.astype(BF16).astype(F32), jnp.uint32) >> 16
    hi_b = pltpu.bitcast(hi.astype(BF16).astype(F32), jnp.uint32) & jnp.uint32(0xFFFF0000)
    return pltpu.bitcast(lo_b | hi_b, I32)


def _unpack_pair(w):
    u = pltpu.bitcast(w, jnp.uint32)
    lo = pltpu.bitcast(u << 16, F32)
    hi = pltpu.bitcast(u & jnp.uint32(0xFFFF0000), F32)
    return lo, hi


def _pack_words(v):
    return jnp.concatenate(
        [_pack_pair(v[:, s * 256:s * 256 + 128], v[:, s * 256 + 128:(s + 1) * 256])
         for s in range(v.shape[1] // 256)], axis=1)


def _unpack_words(w):
    parts = []
    for s in range(w.shape[1] // LANES):
        lo, hi = _unpack_pair(w[:, s * LANES:(s + 1) * LANES])
        parts += [lo, hi]
    return jnp.concatenate(parts, axis=1)


def _mod_kernel(c_ref, w_ref, b_ref, o_ref):
    a = _silu(c_ref[...])
    o_ref[...] = jnp.dot(a, w_ref[...], preferred_element_type=F32,
                         precision=lax.Precision.HIGHEST) + b_ref[...]


def _modulation(cc, w_mod, b_mod):
    rows, d = cc.shape
    cols = w_mod.shape[1]
    tn = d
    return pl.pallas_call(
        _mod_kernel,
        grid=(cols // tn,),
        in_specs=[pl.BlockSpec((rows, d), lambda j: (0, 0)),
                  pl.BlockSpec((d, tn), lambda j: (0, j)),
                  pl.BlockSpec((1, tn), lambda j: (0, j))],
        out_specs=pl.BlockSpec((rows, tn), lambda j: (0, j)),
        out_shape=jax.ShapeDtypeStruct((rows, cols), F32),
        name="mod",
    )(cc, w_mod, b_mod.reshape(1, cols))


def _inproj_kernel(x_ref, xp_ref, xn_ref, mod_ref, g_ref, w_ref, wgt_ref, wg_ref, cw_ref, cb_ref,
                   gbr_ref, gbc_ref, cos_ref, sin_ref, p_ref, gr_ref, gc_ref, *, ts, d):
    i = pl.program_id(1)
    last = pl.num_programs(1) - 1
    r_w = d // 2
    shift = mod_ref[0, 0:1, :]
    scale = mod_ref[0, 1:2, :]
    g = g_ref[...]

    def normmod(v):
        return _rms(v, g) * (1.0 + scale) + shift

    hb = normmod(x_ref[0]).astype(BF16)
    halo = jnp.concatenate([xp_ref[0], xn_ref[0]], axis=0)
    ph = _dot(normmod(halo).astype(BF16), w_ref[:, 4 * r_w:4 * r_w + 2 * r_w])
    prev_row = jnp.where(i == 0, 0.0, ph[7:8, :])
    next_row = jnp.where(i == last, 0.0, ph[8:9, :])

    cos2 = cos_ref[...]
    sin2 = sin_ref[...]
    rows = lax.broadcasted_iota(I32, (ts, r_w), 0)
    qscale = LANES ** -0.5

    for j in range(8):
        acc = _dot(hb, w_ref[:, j * r_w:(j + 1) * r_w])
        if j in (0, 1):
            if j == 0:
                acc = acc * qscale
            parts = []
            for h in range(HEADS):
                t = acc[:, h * LANES:(h + 1) * LANES]
                parts.append(t * cos2 + pltpu.roll(t, LANES // 2, axis=1) * sin2)
            acc = jnp.concatenate(parts, axis=1)
        elif j in (4, 5):
            c0 = (j - 4) * r_w
            pr = prev_row[:, c0:c0 + r_w]
            nx = next_row[:, c0:c0 + r_w]
            down = jnp.where(rows == 0, pr, pltpu.roll(acc, 1, axis=0))
            up = jnp.where(rows == ts - 1, nx, pltpu.roll(acc, ts - 1, axis=0))
            cw = cw_ref[:, c0:c0 + r_w]
            acc = down * cw[0:1, :] + acc * cw[1:2, :] + up * cw[2:3, :] + cb_ref[:, c0:c0 + r_w]
            acc = _silu(acc)
            if j == 5:
                acc = acc * qscale
        p_ref[0, :, j * r_w:(j + 1) * r_w] = acc.astype(BF16)

    gr = _dot_nt(wgt_ref[...], hb) + gbr_ref[...]
    ch_r = lax.broadcasted_iota(I32, gr.shape, 0)
    gr_ref[0] = jnp.where((ch_r // HEADS) % 2 == 1, _log_sigmoid(gr), gr)
    gc = _dot(hb, wg_ref[...]) + gbc_ref[...]
    ch_c = lax.broadcasted_iota(I32, gc.shape, 1)
    gc_ref[0] = jnp.where((ch_c // HEADS) % 2 == 1, _log_sigmoid(gc), gc)


def _inproj(x, mod3, mod_row, g, w_main, wgt, wg, conv_w, conv_b, gb_col, gb_row, cos2, sin2, ts):
    b, n, d = x.shape
    nt = n // ts
    nb8 = n // 8
    hb = ts // 8
    cols = w_main.shape[1]
    if mod_row is None:
        mod_map = lambda bi, i: (bi, 0, 0)
    else:
        mod_map = lambda bi, i: (mod_row, 0, 0)
    const2 = lambda bi, i: (0, 0)
    kern = functools.partial(_inproj_kernel, ts=ts, d=d)
    return pl.pallas_call(
        kern,
        grid=(b, nt),
        in_specs=[
            pl.BlockSpec((1, ts, d), lambda bi, i: (bi, i, 0)),
            pl.BlockSpec((1, 8, d), lambda bi, i: (bi, jnp.maximum(i * hb - 1, 0), 0)),
            pl.BlockSpec((1, 8, d), lambda bi, i: (bi, jnp.minimum((i + 1) * hb, nb8 - 1), 0)),
            pl.BlockSpec((1, N_MOD, d), mod_map),
            pl.BlockSpec((1, d), const2),
            pl.BlockSpec((d, cols), const2),
            pl.BlockSpec((16, d), const2),
            pl.BlockSpec((d, 16), const2),
            pl.BlockSpec((3, d), const2),
            pl.BlockSpec((1, d), const2),
            pl.BlockSpec((16, 1), const2),
            pl.BlockSpec((1, 16), const2),
            pl.BlockSpec((ts, LANES), lambda bi, i: (i, 0)),
            pl.BlockSpec((ts, LANES), lambda bi, i: (i, 0)),
        ],
        out_specs=[
            pl.BlockSpec((1, ts, cols), lambda bi, i: (bi, i, 0)),
            pl.BlockSpec((1, 16, ts), lambda bi, i: (bi, 0, i)),
            pl.BlockSpec((1, ts, 16), lambda bi, i: (bi, i, 0)),
        ],
        out_shape=[
            jax.ShapeDtypeStruct((b, n, cols), BF16),
            jax.ShapeDtypeStruct((b, 16, n), F32),
            jax.ShapeDtypeStruct((b, n, 16), F32),
        ],
        compiler_params=pltpu.CompilerParams(dimension_semantics=("parallel", "parallel")),
        name="inproj",
    )(x, x, x, mod3, g, w_main, wgt, wg, conv_w, conv_b, gb_col, gb_row, cos2, sin2)


def _scan_kernel(pf_ref, pb_ref, grf_ref, grb_ref, gcf_ref, gcb_ref, intra_ref, kd_ref, qd_ref, cd_ref,
                 rs0_ref, mc0_ref, mm0_ref, *out_refs, with_output, r_w):
    if with_output:
        of_ref, ob_ref, rs_ref, mc_ref, mm_ref = out_refs
    else:
        rs_ref, mc_ref, mm_ref = out_refs
    j = pl.program_id(1)

    @pl.when(j == 0)
    def _():
        rs_ref[...] = rs0_ref[...]
        mc_ref[...] = mc0_ref[...]
        mm_ref[...] = mm0_ref[...]

    c = CHUNK
    row = lax.broadcasted_iota(I32, (c, c), 0)
    col = lax.broadcasted_iota(I32, (c, c), 1)
    tri_le = (row <= col)
    tri_ge = (row >= col)
    ones_ext = jnp.where(lax.broadcasted_iota(I32, (c, LANES), 1) == 0, 1.0, 0.0).astype(BF16)

    def cumsums(gr, gc, fwd):
        m_row = jnp.where(tri_le if fwd else tri_ge, 1.0, 0.0).astype(BF16)
        m_col = jnp.where(tri_ge if fwd else tri_le, 1.0, 0.0).astype(BF16)
        b_row = sum(_dot(piece, m_row) for piece in _split3(gr))
        b_col = sum(_dot(m_col, piece) for piece in _split3(gc))
        return b_row, b_col

    for dr in range(2):
        fwd = dr == 0
        p_ref = pf_ref if fwd else pb_ref
        o_ref = None
        if with_output:
            o_ref = of_ref if fwd else ob_ref
        gr = (grf_ref if fwd else grb_ref)[0]
        gc = (gcf_ref if fwd else gcb_ref)[0]
        cs_row, cs_col = cumsums(gr, gc, fwd)
        causal = tri_ge if fwd else tri_le

        for h in range(HEADS):
            hs = slice(h * LANES, (h + 1) * LANES)
            st = dr * HEADS + h
            k = p_ref[0, :, r_w + h * LANES:r_w + (h + 1) * LANES]
            v = p_ref[0, :, 2 * r_w + h * LANES:2 * r_w + (h + 1) * LANES]
            s_prev = rs_ref[0, st]
            ks = (k.astype(F32) * kd_ref[st]).astype(BF16)
            upd = _dot_tn(ks, v)
            if with_output:
                q = p_ref[0, :, hs]
                sc = _dot_nt(q, k) * intra_ref[st]
                out = _dot(sc.astype(BF16), v) + qd_ref[st] * _dot(q, s_prev.astype(BF16))
                o_ref[0, :, hs] = out.astype(BF16)
            rs_ref[0, st] = s_prev * cd_ref[st:st + 1, :] + upd

            o0 = 4 * r_w
            mk = p_ref[0, :, o0 + r_w + h * LANES:o0 + r_w + (h + 1) * LANES]
            mv = p_ref[0, :, o0 + 2 * r_w + h * LANES:o0 + 2 * r_w + (h + 1) * LANES]
            ci = dr * 2 * HEADS + h
            cf = ci + HEADS
            ib_row = gr[ci:ci + 1, :]
            b_row = cs_row[cf:cf + 1, :]
            ib_col = gc[:, ci:ci + 1]
            b_col = cs_col[:, cf:cf + 1]
            b_tot = b_row[:, c - 1:c] if fwd else b_row[:, 0:1]
            m_prev = mm_ref[0, st:st + 1, 0:1]
            log_ws_row = b_tot - b_row + ib_row
            m_next = jnp.maximum(b_tot + m_prev, jnp.max(log_ws_row, axis=1, keepdims=True))
            decay_prev = jnp.exp(b_tot + m_prev - m_next)
            ws_col = jnp.exp(b_tot - b_col + ib_col - m_next)
            kw = (mk.astype(F32) * ws_col).astype(BF16)
            v_ext = jnp.concatenate([mv, ones_ext], axis=1)
            upd_c = _dot_tn(kw, v_ext)
            c_prev = mc_ref[0, st]
            if with_output:
                mq = p_ref[0, :, o0 + h * LANES:o0 + (h + 1) * LANES]
                dm = jnp.where(causal, b_col - b_row + ib_row, NEG_INF)
                log_inter = b_col + m_prev
                m_i = jnp.maximum(log_inter, jnp.max(dm, axis=1, keepdims=True))
                w = jnp.exp(dm - m_i)
                inter = jnp.exp(log_inter - m_i)
                sc = _dot_nt(mq, mk) * w
                hx = _dot(sc.astype(BF16), v_ext) + inter * _dot(mq, c_prev.astype(BF16))
                den = hx[:, LANES:LANES + 1]
                hout = hx[:, :LANES] / jnp.maximum(jnp.abs(den), jnp.exp(-m_i))
                o_ref[0, :, r_w + h * LANES:r_w + (h + 1) * LANES] = hout.astype(BF16)
            mc_ref[0, st] = decay_prev * c_prev + upd_c
            mm_ref[0, st:st + 1, :] = jnp.broadcast_to(m_next, (1, LANES))


def _scan(p, g_row, g_col, tabs, states, with_output):
    b, n, cols = p.shape
    nch = n // CHUNK
    r_w = cols // 8
    intra, kd, qd, cd = tabs
    rs0, mc0, mm0 = states
    nst = 2 * HEADS
    fwd3 = lambda bi, j: (bi, j, 0)
    bwd3 = lambda bi, j: (bi, nch - 1 - j, 0)
    c3 = lambda bi, j: (0, 0, 0)
    st4 = lambda bi, j: (bi, 0, 0, 0)
    in_specs = [
        pl.BlockSpec((1, CHUNK, cols), fwd3),
        pl.BlockSpec((1, CHUNK, cols), bwd3),
        pl.BlockSpec((1, 16, CHUNK), lambda bi, j: (bi, 0, j)),
        pl.BlockSpec((1, 16, CHUNK), lambda bi, j: (bi, 0, nch - 1 - j)),
        pl.BlockSpec((1, CHUNK, 16), fwd3),
        pl.BlockSpec((1, CHUNK, 16), bwd3),
        pl.BlockSpec((nst, CHUNK, LANES), c3),
        pl.BlockSpec((nst, CHUNK, LANES), c3),
        pl.BlockSpec((nst, CHUNK, LANES), c3),
        pl.BlockSpec((nst, LANES), lambda bi, j: (0, 0)),
        pl.BlockSpec((1, nst, LANES, LANES), st4),
        pl.BlockSpec((1, nst, LANES, 2 * LANES), st4),
        pl.BlockSpec((1, nst, LANES), lambda bi, j: (bi, 0, 0)),
    ]
    st_specs = [
        pl.BlockSpec((1, nst, LANES, LANES), st4),
        pl.BlockSpec((1, nst, LANES, 2 * LANES), st4),
        pl.BlockSpec((1, nst, LANES), lambda bi, j: (bi, 0, 0)),
    ]
    st_shapes = [
        jax.ShapeDtypeStruct((b, nst, LANES, LANES), F32),
        jax.ShapeDtypeStruct((b, nst, LANES, 2 * LANES), F32),
        jax.ShapeDtypeStruct((b, nst, LANES), F32),
    ]
    if with_output:
        out_specs = [pl.BlockSpec((1, CHUNK, 2 * r_w), fwd3), pl.BlockSpec((1, CHUNK, 2 * r_w), bwd3)] + st_specs
        out_shape = [jax.ShapeDtypeStruct((b, n, 2 * r_w), BF16)] * 2 + st_shapes
    else:
        out_specs, out_shape = st_specs, st_shapes
    kern = functools.partial(_scan_kernel, with_output=with_output, r_w=r_w)
    return pl.pallas_call(
        kern,
        grid=(b, nch),
        in_specs=in_specs,
        out_specs=out_specs,
        out_shape=out_shape,
        compiler_params=pltpu.CompilerParams(dimension_semantics=("parallel", "arbitrary")),
        name="scan_out" if with_output else "scan_state",
    )(p, p, g_row, g_row, g_col, g_col, intra, kd, qd, cd, rs0, mc0, mm0)


def _post_kernel(x_ref, of_ref, ob_ref, rg_ref, mo_ref, mod_ref, ng_ref, hg_ref, wo_ref, wrh_ref, wrl_ref,
                 rb_ref, su_ref, lt_ref, x1_ref, h2_ref, pos_ref, wt_ref, cnt_ref, *, ts, d, n_exp):
    s = of_ref[0].astype(F32) + ob_ref[0].astype(F32)
    parts = []
    for gi in range(2 * HEADS):
        sl = s[:, gi * LANES:(gi + 1) * LANES]
        mu = jnp.mean(sl, axis=-1, keepdims=True)
        dv = sl - mu
        var = jnp.mean(dv * dv, axis=-1, keepdims=True)
        y = dv * lax.rsqrt(var + EPS) * hg_ref[:, gi * LANES:(gi + 1) * LANES]
        if gi < HEADS:
            gate = _silu(rg_ref[0, :, gi * LANES:(gi + 1) * LANES].astype(F32))
        else:
            gate = _sigmoid(mo_ref[0, :, (gi - HEADS) * LANES:(gi - HEADS + 1) * LANES].astype(F32))
        parts.append((y * gate).astype(BF16))
    mixed = jnp.concatenate(parts, axis=1)
    y = _dot(mixed, wo_ref[...])
    g1 = mod_ref[0, 2:3, :]
    sh2 = mod_ref[0, 3:4, :]
    sc2 = mod_ref[0, 4:5, :]
    x1 = x_ref[0] + g1 * _rms(y, ng_ref[1:2, :])
    x1_ref[0] = x1
    h2 = _rms(x1, ng_ref[2:3, :]) * (1.0 + sc2) + sh2
    h_hi = h2.astype(BF16)
    h2_ref[...] = h_hi

    h_lo = (h2 - h_hi.astype(F32)).astype(BF16)
    logits = _dot_nt(wrh_ref[...], h_hi) + _dot_nt(wrh_ref[...], h_lo) + _dot_nt(wrl_ref[...], h_hi)
    scores = _sigmoid(logits)
    sel = scores + rb_ref[...]
    gsz = n_exp // N_GROUPS
    iota_g = lax.broadcasted_iota(I32, (gsz, ts), 0).astype(F32)
    grp = []
    for gi in range(N_GROUPS):
        blk = sel[gi * gsz:(gi + 1) * gsz, :]
        m1 = jnp.max(blk, axis=0, keepdims=True)
        i1 = jnp.min(jnp.where(blk == m1, iota_g, float(gsz)), axis=0, keepdims=True)
        m2 = jnp.max(jnp.where(iota_g == i1, NEG_INF, blk), axis=0, keepdims=True)
        grp.append(m1 + m2)
    masked_parts = []
    for gi in range(N_GROUPS):
        rank = jnp.zeros((1, ts), F32)
        for gj in range(N_GROUPS):
            if gj == gi:
                continue
            beats = (grp[gj] >= grp[gi]) if gj < gi else (grp[gj] > grp[gi])
            rank = rank + jnp.where(beats, 1.0, 0.0)
        keep = rank < float(TOPK_GROUPS)
        masked_parts.append(jnp.where(keep, sel[gi * gsz:(gi + 1) * gsz, :], NEG_INF))
    masked = jnp.concatenate(masked_parts, axis=0)

    iota_e = lax.broadcasted_iota(I32, (n_exp, ts), 0).astype(F32)
    picks, wsel = [], []
    selmask = jnp.zeros((n_exp, ts), F32)
    for _ in range(TOP_K):
        mx = jnp.max(masked, axis=0, keepdims=True)
        ei = jnp.min(jnp.where(masked == mx, iota_e, float(n_exp)), axis=0, keepdims=True)
        hit = iota_e == ei
        wsel.append(jnp.sum(jnp.where(hit, scores, 0.0), axis=0, keepdims=True))
        selmask = jnp.where(hit, 1.0, selmask)
        masked = jnp.where(hit, NEG_INF, masked)
        picks.append(ei)
    wsum = wsel[0]
    for kk in range(1, TOP_K):
        wsum = wsum + wsel[kk]
    rank = _dot(selmask.astype(BF16), su_ref[...])
    cnt = jnp.sum(selmask, axis=1, keepdims=True)
    ov = jnp.floor((jnp.maximum(cnt - SLOT_ROWS, 0.0) + 7.0) * 0.125) * 8.0
    ov_start = _dot(lt_ref[...], jnp.broadcast_to(ov, (n_exp, LANES)).astype(BF16))[:, 0:1]
    slot = jnp.where(rank < SLOT_ROWS, iota_e * SLOT_ROWS + rank,
                     float(n_exp * SLOT_ROWS) + ov_start + rank - SLOT_ROWS)
    for kk in range(TOP_K):
        hit = iota_e == picks[kk]
        pos_ref[kk:kk + 1, :] = jnp.sum(jnp.where(hit, slot, 0.0), axis=0, keepdims=True).astype(I32)
        wt_ref[kk:kk + 1, :] = wsel[kk] / wsum * ROUTED_SCALE
    cnt_ref[0] = _dot_nt(jnp.ones((8, ts), BF16), selmask.astype(BF16))


def _post(x, o_f, o_b, p, mod3, norm_g, head_g, w_out, wr_hi, wr_lo, rbias, ts):
    b, n, d = x.shape
    nt = n // ts
    t_all = b * n
    n_exp = wr_hi.shape[0]
    r_w = d // 2
    su = jnp.where(lax.broadcasted_iota(I32, (ts, ts), 0) < lax.broadcasted_iota(I32, (ts, ts), 1),
                   1.0, 0.0).astype(BF16)
    lt = jnp.where(lax.broadcasted_iota(I32, (n_exp, n_exp), 0) > lax.broadcasted_iota(I32, (n_exp, n_exp), 1),
                   1.0, 0.0).astype(BF16)
    tok3 = lambda bi, i: (bi, i, 0)
    c2 = lambda bi, i: (0, 0)
    flat = lambda bi, i: (0, bi * nt + i)
    kern = functools.partial(_post_kernel, ts=ts, d=d, n_exp=n_exp)
    return pl.pallas_call(
        kern,
        grid=(b, nt),
        in_specs=[
            pl.BlockSpec((1, ts, d), tok3),
            pl.BlockSpec((1, ts, d), tok3),
            pl.BlockSpec((1, ts, d), tok3),
            pl.BlockSpec((1, ts, r_w), lambda bi, i: (bi, i, 3)),
            pl.BlockSpec((1, ts, r_w), lambda bi, i: (bi, i, 7)),
            pl.BlockSpec((1, N_MOD, d), lambda bi, i: (bi, 0, 0)),
            pl.BlockSpec((4, d), c2),
            pl.BlockSpec((1, d), c2),
            pl.BlockSpec((d, d), c2),
            pl.BlockSpec((n_exp, d), c2),
            pl.BlockSpec((n_exp, d), c2),
            pl.BlockSpec((n_exp, 1), c2),
            pl.BlockSpec((ts, ts), c2),
            pl.BlockSpec((n_exp, n_exp), c2),
        ],
        out_specs=[
            pl.BlockSpec((1, ts, d), tok3),
            pl.BlockSpec((ts, d), lambda bi, i: (bi * nt + i, 0)),
            pl.BlockSpec((TOP_K, ts), flat),
            pl.BlockSpec((TOP_K, ts), flat),
            pl.BlockSpec((1, 8, n_exp), lambda bi, i: (bi * nt + i, 0, 0)),
        ],
        out_shape=[
            jax.ShapeDtypeStruct((b, n, d), F32),
            jax.ShapeDtypeStruct((t_all, d), BF16),
            jax.ShapeDtypeStruct((TOP_K, t_all), I32),
            jax.ShapeDtypeStruct((TOP_K, t_all), F32),
            jax.ShapeDtypeStruct((b * nt, 8, n_exp), F32),
        ],
        compiler_params=pltpu.CompilerParams(dimension_semantics=("parallel", "parallel")),
        name="post",
    )(x, o_f, o_b, p, p, mod3, norm_g, head_g, w_out, wr_hi, wr_lo, rbias, su, lt)


def _slot_copies(meta_ref, stage, hbm, sem, n_exp, to_hbm):
    copies = []
    for e in range(n_exp):
        rows = hbm.at[pl.ds(pl.multiple_of(meta_ref[0, 0, e], 8), SLOT_ROWS)]
        slot = stage.at[pl.ds(e * SLOT_ROWS, SLOT_ROWS)]
        copies.append(pltpu.make_async_copy(slot, rows, sem) if to_hbm
                      else pltpu.make_async_copy(rows, slot, sem))
    return copies


def _overflow_copy(meta_ref, stage, hbm, sem, n_exp, e, i, to_hbm):
    src = n_exp * SLOT_ROWS + meta_ref[0, 0, 2 * n_exp + e] + 8 * i
    dst = meta_ref[0, 0, e] + SLOT_ROWS + 8 * i
    spill = stage.at[pl.ds(pl.multiple_of(src, 8), 8)]
    rows = hbm.at[pl.ds(pl.multiple_of(dst, 8), 8)]
    return pltpu.make_async_copy(spill, rows, sem) if to_hbm else pltpu.make_async_copy(rows, spill, sem)


def _for_overflow_pieces(meta_ref, n_exp, fn):
    def per_expert(e, carry):
        def per_piece(i, c2):
            fn(e, i)
            return c2
        return lax.fori_loop(0, meta_ref[0, 0, n_exp + e], per_piece, carry)
    lax.fori_loop(0, n_exp, per_expert, 0)


def _pick_matrix_rows(pos_ref, base, ts):
    rows = lax.broadcasted_iota(I32, (MOE_CHUNK, ts), 0) + base
    hit = jnp.zeros((MOE_CHUNK, ts), F32)
    for kk in range(TOP_K):
        hit = jnp.where(pos_ref[kk:kk + 1, :] == rows, 1.0, hit)
    return hit.astype(BF16)


def _zero_fill(seg_ref, xs_hbm, stage, sem, n_exp, n_blocks):
    tail = SLOT_ROWS + EXPERT_BLOCK
    stage[0:tail, :] = jnp.zeros((tail, stage.shape[1]), I32)
    tails = []
    for e in range(n_exp):
        start = jnp.maximum(seg_ref[0, 0, e] - tail, 0)
        tails.append(pltpu.make_async_copy(stage.at[pl.ds(0, tail)],
                                           xs_hbm.at[pl.ds(pl.multiple_of(start, 8), tail)], sem))
    for cp in tails:
        cp.start()
    n_used = seg_ref[0, 0, n_exp]

    def block_copy(i):
        row = pl.multiple_of(i * EXPERT_BLOCK, EXPERT_BLOCK)
        return pltpu.make_async_copy(stage.at[pl.ds(0, EXPERT_BLOCK)], xs_hbm.at[pl.ds(row, EXPERT_BLOCK)], sem)

    def start_block(i, carry):
        block_copy(i).start()
        return carry

    def wait_block(i, carry):
        block_copy(i).wait()
        return carry

    lax.fori_loop(n_used, n_blocks, start_block, 0)
    for cp in tails:
        cp.wait()
    lax.fori_loop(n_used, n_blocks, wait_block, 0)


def _dispatch_kernel(meta_ref, seg_ref, x_ref, pos_ref, xs_hbm, stage, sem, *, ts, n_exp, n_blocks):
    @pl.when(pl.program_id(0) == 0)
    def _():
        _zero_fill(seg_ref, xs_hbm, stage, sem, n_exp, n_blocks)

    x = x_ref[...]
    n_main = n_exp * SLOT_ROWS // MOE_CHUNK
    for ci in range(n_main):
        stage[ci * MOE_CHUNK:(ci + 1) * MOE_CHUNK, :] = _pack_words(
            _dot(_pick_matrix_rows(pos_ref, ci * MOE_CHUNK, ts), x))
    n_spill = meta_ref[0, 0, 3 * n_exp]

    def spill_chunk(ci, carry):
        base = pl.multiple_of(n_exp * SLOT_ROWS + ci * MOE_CHUNK, MOE_CHUNK)
        stage[pl.ds(base, MOE_CHUNK), :] = _pack_words(_dot(_pick_matrix_rows(pos_ref, base, ts), x))
        return carry

    lax.fori_loop(0, n_spill, spill_chunk, 0)

    copies = _slot_copies(meta_ref, stage, xs_hbm, sem, n_exp, True)
    for cp in copies:
        cp.start()
    _for_overflow_pieces(meta_ref, n_exp,
                         lambda e, i: _overflow_copy(meta_ref, stage, xs_hbm, sem, n_exp, e, i, True).start())
    for cp in copies:
        cp.wait()
    _for_overflow_pieces(meta_ref, n_exp,
                         lambda e, i: _overflow_copy(meta_ref, stage, xs_hbm, sem, n_exp, e, i, True).wait())


def _dispatch(h2, pos_t, meta, seg, p_rows, ts, n_exp):
    t_all, d = h2.shape
    nt = t_all // ts
    stage_rows = n_exp * SLOT_ROWS + ts * TOP_K
    kern = functools.partial(_dispatch_kernel, ts=ts, n_exp=n_exp, n_blocks=p_rows // EXPERT_BLOCK)
    return pl.pallas_call(
        kern,
        grid=(nt,),
        in_specs=[
            pl.BlockSpec((1, 1, meta.shape[2]), lambda i: (i, 0, 0), memory_space=pltpu.SMEM),
            pl.BlockSpec((1, 1, seg.shape[2]), lambda i: (0, 0, 0), memory_space=pltpu.SMEM),
            pl.BlockSpec((ts, d), lambda i: (i, 0)),
            pl.BlockSpec((TOP_K, ts), lambda i: (0, i)),
        ],
        out_specs=pl.BlockSpec(memory_space=pl.ANY),
        out_shape=jax.ShapeDtypeStruct((p_rows, d // 2), I32),
        scratch_shapes=[pltpu.VMEM((stage_rows, d // 2), I32), pltpu.SemaphoreType.DMA(())],
        compiler_params=pltpu.CompilerParams(dimension_semantics=("arbitrary",), has_side_effects=True),
        name="dispatch",
    )(meta, seg, h2, pos_t)


def _expert_kernel(be_ref, nu_ref, xs_ref, wg_ref, wu_ref, wd_ref, ys_ref):
    del be_ref
    i = pl.program_id(0)

    @pl.when(i < nu_ref[0])
    def _():
        xb = _unpack_words(xs_ref[...]).astype(BF16)
        a = _silu(_dot(xb, wg_ref[0])) * _dot(xb, wu_ref[0])
        ys_ref[...] = _pack_words(_dot(a.astype(BF16), wd_ref[0]))

    @pl.when(i >= nu_ref[0])
    def _():
        ys_ref[...] = jnp.zeros_like(ys_ref)


def _experts(xs, block_e, n_used, w_gate, w_up, w_down, blk):
    p_rows, dw = xs.shape
    n_exp, d, ff = w_gate.shape
    nb = p_rows // blk
    used = lambda i, nu: jnp.minimum(i, nu[0] - 1)
    grid_spec = pltpu.PrefetchScalarGridSpec(
        num_scalar_prefetch=2,
        grid=(nb,),
        in_specs=[
            pl.BlockSpec((blk, dw), lambda i, be, nu: (used(i, nu), 0)),
            pl.BlockSpec((1, d, ff), lambda i, be, nu: (be[used(i, nu)], 0, 0)),
            pl.BlockSpec((1, d, ff), lambda i, be, nu: (be[used(i, nu)], 0, 0)),
            pl.BlockSpec((1, ff, d), lambda i, be, nu: (be[used(i, nu)], 0, 0)),
        ],
        out_specs=pl.BlockSpec((blk, dw), lambda i, be, nu: (i, 0)),
    )
    return pl.pallas_call(
        _expert_kernel,
        grid_spec=grid_spec,
        out_shape=jax.ShapeDtypeStruct((p_rows, dw), I32),
        compiler_params=pltpu.CompilerParams(dimension_semantics=("arbitrary",)),
        name="experts",
    )(block_e, n_used, xs, w_gate, w_up, w_down)


def _weight_matrix_cols(pos_ref, wt_ref, base, ts):
    cols = lax.broadcasted_iota(I32, (ts, MOE_CHUNK), 1) + base
    w = jnp.zeros((ts, MOE_CHUNK), F32)
    for kk in range(TOP_K):
        w = jnp.where(pos_ref[:, kk:kk + 1] == cols, wt_ref[:, kk:kk + 1], w)
    return w.astype(BF16)


def _combine_kernel(meta_ref, ys_hbm, pos_ref, wt_ref, x1_ref, h2_ref, mod_ref, ng_ref, sg_ref, su_ref, sd_ref,
                    o_ref, stage, acc, sem, *, ts, n_exp):
    n_spill = meta_ref[0, 0, 3 * n_exp]
    spill0 = n_exp * SLOT_ROWS

    def clear_chunk(ci, carry):
        base = pl.multiple_of(spill0 + ci * MOE_CHUNK, MOE_CHUNK)
        stage[pl.ds(base, MOE_CHUNK), :] = jnp.zeros((MOE_CHUNK, stage.shape[1]), I32)
        return carry

    lax.fori_loop(0, n_spill, clear_chunk, 0)

    copies = _slot_copies(meta_ref, stage, ys_hbm, sem, n_exp, False)
    for cp in copies:
        cp.start()
    _for_overflow_pieces(meta_ref, n_exp,
                         lambda e, i: _overflow_copy(meta_ref, stage, ys_hbm, sem, n_exp, e, i, False).start())

    xb = h2_ref[...]
    a = _silu(_dot(xb, sg_ref[...])) * _dot(xb, su_ref[...])
    acc[...] = _dot(a.astype(BF16), sd_ref[...])

    for cp in copies:
        cp.wait()
    _for_overflow_pieces(meta_ref, n_exp,
                         lambda e, i: _overflow_copy(meta_ref, stage, ys_hbm, sem, n_exp, e, i, False).wait())

    tot = acc[...]
    for ci in range(spill0 // MOE_CHUNK):
        rows = _unpack_words(stage[ci * MOE_CHUNK:(ci + 1) * MOE_CHUNK, :]).astype(BF16)
        tot = tot + _dot(_weight_matrix_cols(pos_ref, wt_ref, ci * MOE_CHUNK, ts), rows)
    acc[...] = tot

    def spill_chunk(ci, carry):
        base = pl.multiple_of(spill0 + ci * MOE_CHUNK, MOE_CHUNK)
        rows = _unpack_words(stage[pl.ds(base, MOE_CHUNK), :]).astype(BF16)
        acc[...] += _dot(_weight_matrix_cols(pos_ref, wt_ref, base, ts), rows)
        return carry

    lax.fori_loop(0, n_spill, spill_chunk, 0)
    g2 = mod_ref[0, 5:6, :]
    o_ref[...] = x1_ref[...] + g2 * _rms(acc[...], ng_ref[3:4, :])


def _combine(ys, meta, pos_tm, wts_tm, x1_flat, h2, mod3, norm_g, ws_gate, ws_up, ws_down, n_seq, ts, n_exp):
    t_all, d = x1_flat.shape
    nt = t_all // ts
    per_b = n_seq // ts
    ff = ws_gate.shape[1]
    stage_rows = n_exp * SLOT_ROWS + ts * TOP_K
    c2 = lambda i: (0, 0)
    kern = functools.partial(_combine_kernel, ts=ts, n_exp=n_exp)
    return pl.pallas_call(
        kern,
        grid=(nt,),
        in_specs=[
            pl.BlockSpec((1, 1, meta.shape[2]), lambda i: (i, 0, 0), memory_space=pltpu.SMEM),
            pl.BlockSpec(memory_space=pl.ANY),
            pl.BlockSpec((ts, TOP_K), lambda i: (i, 0)),
            pl.BlockSpec((ts, TOP_K), lambda i: (i, 0)),
            pl.BlockSpec((ts, d), lambda i: (i, 0)),
            pl.BlockSpec((ts, d), lambda i: (i, 0)),
            pl.BlockSpec((1, N_MOD, d), lambda i: (i // per_b, 0, 0)),
            pl.BlockSpec((4, d), c2),
            pl.BlockSpec((d, ff), c2),
            pl.BlockSpec((d, ff), c2),
            pl.BlockSpec((ff, d), c2),
        ],
        out_specs=pl.BlockSpec((ts, d), lambda i: (i, 0)),
        out_shape=jax.ShapeDtypeStruct((t_all, d), F32),
        scratch_shapes=[pltpu.VMEM((stage_rows, d // 2), I32), pltpu.VMEM((ts, d), F32),
                        pltpu.SemaphoreType.DMA(())],
        compiler_params=pltpu.CompilerParams(dimension_semantics=("arbitrary",)),
        name="combine",
    )(meta, ys, pos_tm, wts_tm, x1_flat, h2, mod3, norm_g, ws_gate, ws_up, ws_down)


def _rope_tables(n):
    rows = jnp.repeat(jnp.arange(n // GRID_W, dtype=F32), GRID_W)
    cols = jnp.tile(jnp.arange(GRID_W, dtype=F32), n // GRID_W)
    quarter = LANES // 4
    freqs = ROPE_BASE ** (-jnp.arange(quarter, dtype=F32) / quarter)
    ang = jnp.concatenate([rows[:, None] * freqs, cols[:, None] * freqs], axis=-1)
    cos, sin = jnp.cos(ang), jnp.sin(ang)
    return jnp.concatenate([cos, cos], axis=-1), jnp.concatenate([-sin, sin], axis=-1)


def _retention_tables(log_decay):
    lg = -jnp.exp(log_decay.astype(F32))
    idx = jnp.arange(CHUNK, dtype=F32)
    rel = idx[:, None] - idx[None, :]
    lg3 = lg[:, :, None, None]
    intra_f = jnp.where(rel >= 0, jnp.exp(jnp.maximum(rel, 0.0) * lg3[0]), 0.0)
    intra_b = jnp.where(rel <= 0, jnp.exp(jnp.maximum(-rel, 0.0) * lg3[1]), 0.0)
    kd_f = jnp.exp((CHUNK - 1 - idx)[None, :] * lg[0][:, None])
    kd_b = jnp.exp(idx[None, :] * lg[1][:, None])
    qd_f = jnp.exp((idx + 1)[None, :] * lg[0][:, None])
    qd_b = jnp.exp((CHUNK - idx)[None, :] * lg[1][:, None])
    bc = lambda t: jnp.broadcast_to(t[:, :, None], (HEADS, CHUNK, LANES))
    intra = jnp.concatenate([intra_f, intra_b], axis=0)
    kd = jnp.concatenate([bc(kd_f), bc(kd_b)], axis=0)
    qd = jnp.concatenate([bc(qd_f), bc(qd_b)], axis=0)
    cd = jnp.broadcast_to(jnp.exp(CHUNK * lg).reshape(2 * HEADS, 1), (2 * HEADS, LANES))
    return intra, kd, qd, cd


def kernel(x, c, ctx, c_ctx, w_mod, b_mod, norm_g, w_in, ret_log_decay, ret_norm_g, mlstm_conv_w,
           mlstm_conv_b, mlstm_gate_b, mlstm_norm_g, w_out, w_router, router_bias, w_gate, w_up, w_down,
           ws_gate, ws_up, ws_down):
    b, n, d = x.shape
    n_ctx = ctx.shape[1]
    depth = w_mod.shape[0]
    assert depth == 1, "only the single-layer configuration is implemented"
    assert d // 2 // HEADS == LANES
    n_exp = w_router.shape[2]
    t_all = b * n
    r_w = d // 2
    main_cols = 8 * r_w
    l = 0

    pad = (-(b + 1)) % 8
    cc = jnp.concatenate([c, c_ctx[None, :], jnp.zeros((pad, d), F32)], axis=0)
    mod3 = _modulation(cc, w_mod[l], b_mod[l]).reshape(b + 1 + pad, N_MOD, d)

    w_main = w_in[l, :, :main_cols].astype(BF16)
    wg = w_in[l, :, main_cols:].astype(BF16)
    wgt = wg.T
    gb = mlstm_gate_b[l].reshape(-1).astype(F32)
    tabs = _retention_tables(ret_log_decay[l])
    head_g = jnp.concatenate([ret_norm_g[l], mlstm_norm_g[l]]).reshape(1, d).astype(F32)
    wr = w_router[l].T.astype(F32)
    wr_hi = wr.astype(BF16)
    wr_lo = (wr - wr_hi.astype(F32)).astype(BF16)

    def inproj(seq, mod_row, ts):
        cos2, sin2 = _rope_tables(n) if mod_row is None else (
            jnp.ones((seq.shape[1], LANES), F32), jnp.zeros((seq.shape[1], LANES), F32))
        return _inproj(seq, mod3, mod_row, norm_g[l, 0:1], w_main, wgt, wg, mlstm_conv_w[l],
                       mlstm_conv_b[l].reshape(1, -1), gb.reshape(16, 1), gb.reshape(1, 16), cos2, sin2, ts)

    nst = 2 * HEADS
    zero_states = (jnp.zeros((b, nst, LANES, LANES), F32), jnp.zeros((b, nst, LANES, 2 * LANES), F32),
                   jnp.zeros((b, nst, LANES), F32))
    p_c, gr_c, gc_c = inproj(ctx, b, min(n_ctx, 512))
    ctx_states = _scan(p_c, gr_c, gc_c, tabs, zero_states, with_output=False)

    ts = min(n, 512)
    p_l, gr_l, gc_l = inproj(x, None, ts)
    o_f, o_b, _, _, _ = _scan(p_l, gr_l, gc_l, tabs, tuple(ctx_states), with_output=True)
    ts_moe = MOE_TILE
    x1, h2, pos_t, wts_t, tile_cnt = _post(
        x, o_f, o_b, p_l, mod3, norm_g[l], head_g, w_out[l].astype(BF16), wr_hi, wr_lo,
        router_bias[l].reshape(n_exp, 1).astype(F32), ts_moe)

    nt = t_all // ts_moe
    cnt = tile_cnt[:, 0, :].astype(I32)
    cnt8 = (cnt + 7) // 8 * 8
    seg_cap = (jnp.sum(cnt8, axis=0) + SLOT_ROWS + EXPERT_BLOCK - 1) // EXPERT_BLOCK * EXPERT_BLOCK
    seg_end = jnp.cumsum(seg_cap)
    run_start = (seg_end - seg_cap)[None, :] + jnp.cumsum(cnt8, axis=0) - cnt8
    ov_rows = jnp.maximum(cnt8 - SLOT_ROWS, 0)
    ov_off = jnp.cumsum(ov_rows, axis=1) - ov_rows
    n_spill = (jnp.sum(ov_rows, axis=1, keepdims=True) + MOE_CHUNK - 1) // MOE_CHUNK
    meta = jnp.concatenate([run_start, ov_rows // 8, ov_off, n_spill], axis=1).astype(I32)
    meta = jnp.pad(meta, ((0, 0), (0, (-meta.shape[1]) % LANES))).reshape(nt, 1, -1)
    p_rows = -(-(t_all * TOP_K + nt * n_exp * 7 + n_exp * (SLOT_ROWS + EXPERT_BLOCK - 1))
               // EXPERT_BLOCK) * EXPERT_BLOCK
    nb = p_rows // EXPERT_BLOCK
    blk_first = jnp.arange(nb, dtype=I32) * EXPERT_BLOCK
    block_e = jnp.minimum(jnp.sum((seg_end[None, :] <= blk_first[:, None]).astype(I32), axis=1), n_exp - 1)
    n_used = (seg_end[-1:] // EXPERT_BLOCK).astype(I32)
    seg = jnp.concatenate([seg_end.astype(I32), n_used])
    seg = jnp.pad(seg, (0, (-seg.shape[0]) % LANES)).reshape(1, 1, -1)

    xs = _dispatch(h2, pos_t, meta, seg, p_rows, ts_moe, n_exp)
    ys = _experts(xs, block_e, n_used, w_gate[l].astype(BF16), w_up[l].astype(BF16), w_down[l].astype(BF16),
                  EXPERT_BLOCK)
    out = _combine(ys, meta, pos_t.T, wts_t.T, x1.reshape(t_all, d), h2, mod3, norm_g[l],
                   ws_gate[l].astype(BF16), ws_up[l].astype(BF16), ws_down[l].astype(BF16), n, ts_moe, n_exp)
    return out.reshape(b, n, d)
```

```python
import functools

import jax
import jax.numpy as jnp
from jax import lax
from jax.experimental import pallas as pl
from jax.experimental.pallas import tpu as pltpu

F32 = jnp.float32
BF16 = jnp.bfloat16
I32 = jnp.int32

EPS = 1e-6
LANES = 128
CHUNK = 128
HEADS = 4
GRID_W = 64
ROPE_BASE = 10000.0
N_GROUPS = 8
TOPK_GROUPS = 4
TOP_K = 8
ROUTED_SCALE = 2.5
N_MOD = 6
MOE_TILE = 256
MOE_CHUNK = 256
ROW_ALIGN = 16
SLOT_ROWS = 48
SLOT_GROUP = 8
EXPERT_BLOCK = 512
NEG_INF = float("-inf")


def _sigmoid(v):
    return 1.0 / (1.0 + jnp.exp(-v))


def _silu(v):
    return v * _sigmoid(v)


def _log_sigmoid(v):
    return jnp.minimum(v, 0.0) - jnp.log(1.0 + jnp.exp(-jnp.abs(v)))


def _dot(a, b):
    return jnp.dot(a, b, preferred_element_type=F32)


def _dot_nt(a, b):
    return lax.dot_general(a, b, (((1,), (1,)), ((), ())), preferred_element_type=F32)


def _dot_tn(a, b):
    return lax.dot_general(a, b, (((0,), (0,)), ((), ())), preferred_element_type=F32)


def _split3(a):
    hi = a.astype(BF16)
    r = a - hi.astype(F32)
    mid = r.astype(BF16)
    lo = (r - mid.astype(F32)).astype(BF16)
    return hi, mid, lo


def _rms(v, g):
    ms = jnp.mean(v * v, axis=-1, keepdims=True)
    return v * lax.rsqrt(ms + EPS) * g


def _mod_kernel(c_ref, w_ref, b_ref, o_ref):
    a = _silu(c_ref[...])
    o_ref[...] = jnp.dot(a, w_ref[...], preferred_element_type=F32,
                         precision=lax.Precision.HIGHEST) + b_ref[...]


def _modulation(cc, w_mod, b_mod):
    rows, d = cc.shape
    cols = w_mod.shape[1]
    tn = d
    return pl.pallas_call(
        _mod_kernel,
        grid=(cols // tn,),
        in_specs=[pl.BlockSpec((rows, d), lambda j: (0, 0)),
                  pl.BlockSpec((d, tn), lambda j: (0, j)),
                  pl.BlockSpec((1, tn), lambda j: (0, j))],
        out_specs=pl.BlockSpec((rows, tn), lambda j: (0, j)),
        out_shape=jax.ShapeDtypeStruct((rows, cols), F32),
        name="mod",
    )(cc, w_mod, b_mod.reshape(1, cols))


def _inproj_kernel(x_ref, xp_ref, xn_ref, mod_ref, g_ref, w_ref, wgt_ref, wg_ref, cw_ref, cb_ref,
                   gbr_ref, gbc_ref, cos_ref, sin_ref, p_ref, gr_ref, gc_ref, *, ts, d):
    i = pl.program_id(1)
    last = pl.num_programs(1) - 1
    r_w = d // 2
    shift = mod_ref[0, 0:1, :]
    scale = mod_ref[0, 1:2, :]
    g = g_ref[...]

    def normmod(v):
        return _rms(v, g) * (1.0 + scale) + shift

    hb = normmod(x_ref[0]).astype(BF16)
    halo = jnp.concatenate([xp_ref[0], xn_ref[0]], axis=0)
    ph = _dot(normmod(halo).astype(BF16), w_ref[:, 4 * r_w:4 * r_w + 2 * r_w])
    prev_row = jnp.where(i == 0, 0.0, ph[7:8, :])
    next_row = jnp.where(i == last, 0.0, ph[8:9, :])

    cos2 = cos_ref[...]
    sin2 = sin_ref[...]
    rows = lax.broadcasted_iota(I32, (ts, r_w), 0)
    qscale = LANES ** -0.5

    for j in range(8):
        acc = _dot(hb, w_ref[:, j * r_w:(j + 1) * r_w])
        if j in (0, 1):
            if j == 0:
                acc = acc * qscale
            parts = []
            for h in range(HEADS):
                t = acc[:, h * LANES:(h + 1) * LANES]
                parts.append(t * cos2 + pltpu.roll(t, LANES // 2, axis=1) * sin2)
            acc = jnp.concatenate(parts, axis=1)
        elif j in (4, 5):
            c0 = (j - 4) * r_w
            pr = prev_row[:, c0:c0 + r_w]
            nx = next_row[:, c0:c0 + r_w]
            down = jnp.where(rows == 0, pr, pltpu.roll(acc, 1, axis=0))
            up = jnp.where(rows == ts - 1, nx, pltpu.roll(acc, ts - 1, axis=0))
            cw = cw_ref[:, c0:c0 + r_w]
            acc = down * cw[0:1, :] + acc * cw[1:2, :] + up * cw[2:3, :] + cb_ref[:, c0:c0 + r_w]
            acc = _silu(acc)
            if j == 5:
                acc = acc * qscale
        p_ref[0, :, j * r_w:(j + 1) * r_w] = acc.astype(BF16)

    gr = _dot_nt(wgt_ref[...], hb) + gbr_ref[...]
    ch_r = lax.broadcasted_iota(I32, gr.shape, 0)
    gr_ref[0] = jnp.where((ch_r // HEADS) % 2 == 1, _log_sigmoid(gr), gr)
    gc = _dot(hb, wg_ref[...]) + gbc_ref[...]
    ch_c = lax.broadcasted_iota(I32, gc.shape, 1)
    gc_ref[0] = jnp.where((ch_c // HEADS) % 2 == 1, _log_sigmoid(gc), gc)


def _inproj(x, mod3, mod_row, g, w_main, wgt, wg, conv_w, conv_b, gb_col, gb_row, cos2, sin2, ts):
    b, n, d = x.shape
    nt = n // ts
    nb8 = n // 8
    hb = ts // 8
    cols = w_main.shape[1]
    if mod_row is None:
        mod_map = lambda bi, i: (bi, 0, 0)
    else:
        mod_map = lambda bi, i: (mod_row, 0, 0)
    const2 = lambda bi, i: (0, 0)
    kern = functools.partial(_inproj_kernel, ts=ts, d=d)
    return pl.pallas_call(
        kern,
        grid=(b, nt),
        in_specs=[
            pl.BlockSpec((1, ts, d), lambda bi, i: (bi, i, 0)),
            pl.BlockSpec((1, 8, d), lambda bi, i: (bi, jnp.maximum(i * hb - 1, 0), 0)),
            pl.BlockSpec((1, 8, d), lambda bi, i: (bi, jnp.minimum((i + 1) * hb, nb8 - 1), 0)),
            pl.BlockSpec((1, N_MOD, d), mod_map),
            pl.BlockSpec((1, d), const2),
            pl.BlockSpec((d, cols), const2),
            pl.BlockSpec((16, d), const2),
            pl.BlockSpec((d, 16), const2),
            pl.BlockSpec((3, d), const2),
            pl.BlockSpec((1, d), const2),
            pl.BlockSpec((16, 1), const2),
            pl.BlockSpec((1, 16), const2),
            pl.BlockSpec((ts, LANES), lambda bi, i: (i, 0)),
            pl.BlockSpec((ts, LANES), lambda bi, i: (i, 0)),
        ],
        out_specs=[
            pl.BlockSpec((1, ts, cols), lambda bi, i: (bi, i, 0)),
            pl.BlockSpec((1, 16, ts), lambda bi, i: (bi, 0, i)),
            pl.BlockSpec((1, ts, 16), lambda bi, i: (bi, i, 0)),
        ],
        out_shape=[
            jax.ShapeDtypeStruct((b, n, cols), BF16),
            jax.ShapeDtypeStruct((b, 16, n), F32),
            jax.ShapeDtypeStruct((b, n, 16), F32),
        ],
        compiler_params=pltpu.CompilerParams(dimension_semantics=("parallel", "parallel")),
        name="inproj",
    )(x, x, x, mod3, g, w_main, wgt, wg, conv_w, conv_b, gb_col, gb_row, cos2, sin2)


def _scan_kernel(pf_ref, pb_ref, grf_ref, grb_ref, gcf_ref, gcb_ref, intra_ref, kd_ref, qd_ref, cd_ref,
                 rs0_ref, mc0_ref, mm0_ref, *out_refs, with_output, r_w):
    if with_output:
        of_ref, ob_ref, rs_ref, mc_ref, mm_ref = out_refs
    else:
        rs_ref, mc_ref, mm_ref = out_refs
    j = pl.program_id(1)

    @pl.when(j == 0)
    def _():
        rs_ref[...] = rs0_ref[...]
        mc_ref[...] = mc0_ref[...]
        mm_ref[...] = mm0_ref[...]

    c = CHUNK
    row = lax.broadcasted_iota(I32, (c, c), 0)
    col = lax.broadcasted_iota(I32, (c, c), 1)
    tri_le = (row <= col)
    tri_ge = (row >= col)
    ones_ext = jnp.where(lax.broadcasted_iota(I32, (c, LANES), 1) == 0, 1.0, 0.0).astype(BF16)

    def cumsums(gr, gc, fwd):
        m_row = jnp.where(tri_le if fwd else tri_ge, 1.0, 0.0).astype(BF16)
        m_col = jnp.where(tri_ge if fwd else tri_le, 1.0, 0.0).astype(BF16)
        b_row = sum(_dot(piece, m_row) for piece in _split3(gr))
        b_col = sum(_dot(m_col, piece) for piece in _split3(gc))
        return b_row, b_col

    for dr in range(2):
        fwd = dr == 0
        p_ref = pf_ref if fwd else pb_ref
        o_ref = None
        if with_output:
            o_ref = of_ref if fwd else ob_ref
        gr = (grf_ref if fwd else grb_ref)[0]
        gc = (gcf_ref if fwd else gcb_ref)[0]
        cs_row, cs_col = cumsums(gr, gc, fwd)
        causal = tri_ge if fwd else tri_le

        for h in range(HEADS):
            hs = slice(h * LANES, (h + 1) * LANES)
            st = dr * HEADS + h
            k = p_ref[0, :, r_w + h * LANES:r_w + (h + 1) * LANES]
            v = p_ref[0, :, 2 * r_w + h * LANES:2 * r_w + (h + 1) * LANES]
            s_prev = rs_ref[0, st]
            ks = (k.astype(F32) * kd_ref[st]).astype(BF16)
            upd = _dot_tn(ks, v)
            if with_output:
                q = p_ref[0, :, hs]
                sc = _dot_nt(q, k) * intra_ref[st]
                out = _dot(sc.astype(BF16), v) + qd_ref[st] * _dot(q, s_prev.astype(BF16))
                o_ref[0, :, hs] = out.astype(BF16)
            rs_ref[0, st] = s_prev * cd_ref[st:st + 1, :] + upd

            o0 = 4 * r_w
            mk = p_ref[0, :, o0 + r_w + h * LANES:o0 + r_w + (h + 1) * LANES]
            mv = p_ref[0, :, o0 + 2 * r_w + h * LANES:o0 + 2 * r_w + (h + 1) * LANES]
            ci = dr * 2 * HEADS + h
            cf = ci + HEADS
            ib_row = gr[ci:ci + 1, :]
            b_row = cs_row[cf:cf + 1, :]
            ib_col = gc[:, ci:ci + 1]
            b_col = cs_col[:, cf:cf + 1]
            b_tot = b_row[:, c - 1:c] if fwd else b_row[:, 0:1]
            m_prev = mm_ref[0, st:st + 1, 0:1]
            log_ws_row = b_tot - b_row + ib_row
            m_next = jnp.maximum(b_tot + m_prev, jnp.max(log_ws_row, axis=1, keepdims=True))
            decay_prev = jnp.exp(b_tot + m_prev - m_next)
            ws_col = jnp.exp(b_tot - b_col + ib_col - m_next)
            kw = (mk.astype(F32) * ws_col).astype(BF16)
            v_ext = jnp.concatenate([mv, ones_ext], axis=1)
            upd_c = _dot_tn(kw, v_ext)
            c_prev = mc_ref[0, st]
            if with_output:
                mq = p_ref[0, :, o0 + h * LANES:o0 + (h + 1) * LANES]
                dm = jnp.where(causal, b_col - b_row + ib_row, NEG_INF)
                log_inter = b_col + m_prev
                m_i = jnp.maximum(log_inter, jnp.max(dm, axis=1, keepdims=True))
                w = jnp.exp(dm - m_i)
                inter = jnp.exp(log_inter - m_i)
                sc = _dot_nt(mq, mk) * w
                hx = _dot(sc.astype(BF16), v_ext) + inter * _dot(mq, c_prev.astype(BF16))
                den = hx[:, LANES:LANES + 1]
                hout = hx[:, :LANES] / jnp.maximum(jnp.abs(den), jnp.exp(-m_i))
                o_ref[0, :, r_w + h * LANES:r_w + (h + 1) * LANES] = hout.astype(BF16)
            mc_ref[0, st] = decay_prev * c_prev + upd_c
            mm_ref[0, st:st + 1, :] = jnp.broadcast_to(m_next, (1, LANES))


def _scan(p, g_row, g_col, tabs, states, with_output):
    b, n, cols = p.shape
    nch = n // CHUNK
    r_w = cols // 8
    intra, kd, qd, cd = tabs
    rs0, mc0, mm0 = states
    nst = 2 * HEADS
    fwd3 = lambda bi, j: (bi, j, 0)
    bwd3 = lambda bi, j: (bi, nch - 1 - j, 0)
    c3 = lambda bi, j: (0, 0, 0)
    st4 = lambda bi, j: (bi, 0, 0, 0)
    in_specs = [
        pl.BlockSpec((1, CHUNK, cols), fwd3),
        pl.BlockSpec((1, CHUNK, cols), bwd3),
        pl.BlockSpec((1, 16, CHUNK), lambda bi, j: (bi, 0, j)),
        pl.BlockSpec((1, 16, CHUNK), lambda bi, j: (bi, 0, nch - 1 - j)),
        pl.BlockSpec((1, CHUNK, 16), fwd3),
        pl.BlockSpec((1, CHUNK, 16), bwd3),
        pl.BlockSpec((nst, CHUNK, LANES), c3),
        pl.BlockSpec((nst, CHUNK, LANES), c3),
        pl.BlockSpec((nst, CHUNK, LANES), c3),
        pl.BlockSpec((nst, LANES), lambda bi, j: (0, 0)),
        pl.BlockSpec((1, nst, LANES, LANES), st4),
        pl.BlockSpec((1, nst, LANES, 2 * LANES), st4),
        pl.BlockSpec((1, nst, LANES), lambda bi, j: (bi, 0, 0)),
    ]
    st_specs = [
        pl.BlockSpec((1, nst, LANES, LANES), st4),
        pl.BlockSpec((1, nst, LANES, 2 * LANES), st4),
        pl.BlockSpec((1, nst, LANES), lambda bi, j: (bi, 0, 0)),
    ]
    st_shapes = [
        jax.ShapeDtypeStruct((b, nst, LANES, LANES), F32),
        jax.ShapeDtypeStruct((b, nst, LANES, 2 * LANES), F32),
        jax.ShapeDtypeStruct((b, nst, LANES), F32),
    ]
    if with_output:
        out_specs = [pl.BlockSpec((1, CHUNK, 2 * r_w), fwd3), pl.BlockSpec((1, CHUNK, 2 * r_w), bwd3)] + st_specs
        out_shape = [jax.ShapeDtypeStruct((b, n, 2 * r_w), BF16)] * 2 + st_shapes
    else:
        out_specs, out_shape = st_specs, st_shapes
    kern = functools.partial(_scan_kernel, with_output=with_output, r_w=r_w)
    return pl.pallas_call(
        kern,
        grid=(b, nch),
        in_specs=in_specs,
        out_specs=out_specs,
        out_shape=out_shape,
        compiler_params=pltpu.CompilerParams(dimension_semantics=("parallel", "arbitrary")),
        name="scan_out" if with_output else "scan_state",
    )(p, p, g_row, g_row, g_col, g_col, intra, kd, qd, cd, rs0, mc0, mm0)


def _post_kernel(x_ref, of_ref, ob_ref, rg_ref, mo_ref, mod_ref, ng_ref, hg_ref, wo_ref, wrh_ref, wrl_ref,
                 rb_ref, su_ref, x1_ref, h2_ref, rk_ref, wd_ref, cnt_ref, *, ts, d, n_exp):
    s = of_ref[0].astype(F32) + ob_ref[0].astype(F32)
    parts = []
    for gi in range(2 * HEADS):
        sl = s[:, gi * LANES:(gi + 1) * LANES]
        mu = jnp.mean(sl, axis=-1, keepdims=True)
        dv = sl - mu
        var = jnp.mean(dv * dv, axis=-1, keepdims=True)
        y = dv * lax.rsqrt(var + EPS) * hg_ref[:, gi * LANES:(gi + 1) * LANES]
        if gi < HEADS:
            gate = _silu(rg_ref[0, :, gi * LANES:(gi + 1) * LANES].astype(F32))
        else:
            gate = _sigmoid(mo_ref[0, :, (gi - HEADS) * LANES:(gi - HEADS + 1) * LANES].astype(F32))
        parts.append((y * gate).astype(BF16))
    mixed = jnp.concatenate(parts, axis=1)
    y = _dot(mixed, wo_ref[...])
    g1 = mod_ref[0, 2:3, :]
    sh2 = mod_ref[0, 3:4, :]
    sc2 = mod_ref[0, 4:5, :]
    x1 = x_ref[0] + g1 * _rms(y, ng_ref[1:2, :])
    x1_ref[0] = x1
    h2 = _rms(x1, ng_ref[2:3, :]) * (1.0 + sc2) + sh2
    h_hi = h2.astype(BF16)
    h2_ref[...] = h_hi

    h_lo = (h2 - h_hi.astype(F32)).astype(BF16)
    logits = _dot_nt(wrh_ref[...], h_hi) + _dot_nt(wrh_ref[...], h_lo) + _dot_nt(wrl_ref[...], h_hi)
    scores = _sigmoid(logits)
    sel = scores + rb_ref[...]
    gsz = n_exp // N_GROUPS
    iota_g = lax.broadcasted_iota(I32, (gsz, ts), 0).astype(F32)
    grp = []
    for gi in range(N_GROUPS):
        blk = sel[gi * gsz:(gi + 1) * gsz, :]
        m1 = jnp.max(blk, axis=0, keepdims=True)
        i1 = jnp.min(jnp.where(blk == m1, iota_g, float(gsz)), axis=0, keepdims=True)
        m2 = jnp.max(jnp.where(iota_g == i1, NEG_INF, blk), axis=0, keepdims=True)
        grp.append(m1 + m2)
    masked_parts = []
    for gi in range(N_GROUPS):
        rank = jnp.zeros((1, ts), F32)
        for gj in range(N_GROUPS):
            if gj == gi:
                continue
            beats = (grp[gj] >= grp[gi]) if gj < gi else (grp[gj] > grp[gi])
            rank = rank + jnp.where(beats, 1.0, 0.0)
        keep = rank < float(TOPK_GROUPS)
        masked_parts.append(jnp.where(keep, sel[gi * gsz:(gi + 1) * gsz, :], NEG_INF))
    masked = jnp.concatenate(masked_parts, axis=0)

    iota_e = lax.broadcasted_iota(I32, (n_exp, ts), 0).astype(F32)
    selmask = jnp.zeros((n_exp, ts), F32)
    for _ in range(TOP_K):
        mx = jnp.max(masked, axis=0, keepdims=True)
        ei = jnp.min(jnp.where(masked == mx, iota_e, float(n_exp)), axis=0, keepdims=True)
        hit = iota_e == ei
        selmask = jnp.where(hit, 1.0, selmask)
        masked = jnp.where(hit, NEG_INF, masked)
    picked = selmask > 0.0
    wsel = jnp.where(picked, scores, 0.0)
    wd_ref[...] = wsel / jnp.sum(wsel, axis=0, keepdims=True) * ROUTED_SCALE
    rank = _dot(selmask.astype(BF16), su_ref[...])
    rk_ref[...] = jnp.where(picked, rank, -1.0)
    cnt_ref[0] = _dot_nt(jnp.ones((8, ts), BF16), selmask.astype(BF16))


def _post(x, o_f, o_b, p, mod3, norm_g, head_g, w_out, wr_hi, wr_lo, rbias, ts):
    b, n, d = x.shape
    nt = n // ts
    t_all = b * n
    n_exp = wr_hi.shape[0]
    r_w = d // 2
    su = jnp.where(lax.broadcasted_iota(I32, (ts, ts), 0) < lax.broadcasted_iota(I32, (ts, ts), 1),
                   1.0, 0.0).astype(BF16)
    tok3 = lambda bi, i: (bi, i, 0)
    c2 = lambda bi, i: (0, 0)
    flat = lambda bi, i: (0, bi * nt + i)
    kern = functools.partial(_post_kernel, ts=ts, d=d, n_exp=n_exp)
    return pl.pallas_call(
        kern,
        grid=(b, nt),
        in_specs=[
            pl.BlockSpec((1, ts, d), tok3),
            pl.BlockSpec((1, ts, d), tok3),
            pl.BlockSpec((1, ts, d), tok3),
            pl.BlockSpec((1, ts, r_w), lambda bi, i: (bi, i, 3)),
            pl.BlockSpec((1, ts, r_w), lambda bi, i: (bi, i, 7)),
            pl.BlockSpec((1, N_MOD, d), lambda bi, i: (bi, 0, 0)),
            pl.BlockSpec((4, d), c2),
            pl.BlockSpec((1, d), c2),
            pl.BlockSpec((d, d), c2),
            pl.BlockSpec((n_exp, d), c2),
            pl.BlockSpec((n_exp, d), c2),
            pl.BlockSpec((n_exp, 1), c2),
            pl.BlockSpec((ts, ts), c2),
        ],
        out_specs=[
            pl.BlockSpec((1, ts, d), tok3),
            pl.BlockSpec((ts, d), lambda bi, i: (bi * nt + i, 0)),
            pl.BlockSpec((n_exp, ts), flat),
            pl.BlockSpec((n_exp, ts), flat),
            pl.BlockSpec((1, 8, n_exp), lambda bi, i: (bi * nt + i, 0, 0)),
        ],
        out_shape=[
            jax.ShapeDtypeStruct((b, n, d), F32),
            jax.ShapeDtypeStruct((t_all, d), BF16),
            jax.ShapeDtypeStruct((n_exp, t_all), F32),
            jax.ShapeDtypeStruct((n_exp, t_all), F32),
            jax.ShapeDtypeStruct((b * nt, 8, n_exp), F32),
        ],
        compiler_params=pltpu.CompilerParams(dimension_semantics=("parallel", "parallel")),
        name="post",
    )(x, o_f, o_b, p, p, mod3, norm_g, head_g, w_out, wr_hi, wr_lo, rbias, su)


def _slot_copies(meta_ref, stage, hbm, sem, experts, to_hbm):
    copies = []
    for e in experts:
        rows = hbm.at[pl.ds(pl.multiple_of(meta_ref[0, 0, e], ROW_ALIGN), SLOT_ROWS)]
        slot = stage.at[pl.ds(e * SLOT_ROWS, SLOT_ROWS)]
        copies.append(pltpu.make_async_copy(slot, rows, sem) if to_hbm
                      else pltpu.make_async_copy(rows, slot, sem))
    return copies


def _overflow_copy(meta_ref, spill, hbm, sem, n_exp, e, i, to_hbm):
    src = meta_ref[0, 0, 2 * n_exp + e] + ROW_ALIGN * i
    dst = meta_ref[0, 0, e] + SLOT_ROWS + ROW_ALIGN * i
    piece = spill.at[pl.ds(pl.multiple_of(src, ROW_ALIGN), ROW_ALIGN)]
    rows = hbm.at[pl.ds(pl.multiple_of(dst, ROW_ALIGN), ROW_ALIGN)]
    return pltpu.make_async_copy(piece, rows, sem) if to_hbm else pltpu.make_async_copy(rows, piece, sem)


def _for_overflow_pieces(meta_ref, n_exp, fn):
    def per_expert(e, carry):
        def per_piece(i, c2):
            fn(e, i)
            return c2
        return lax.fori_loop(0, meta_ref[0, 0, n_exp + e], per_piece, carry)
    lax.fori_loop(0, n_exp, per_expert, 0)


def _spill_matrix_rows(meta_ref, rk_ref, wd_ref, base, ts, n_exp):
    rows = (lax.broadcasted_iota(I32, (MOE_CHUNK, ts), 0) + base).astype(F32)

    def per_expert(e, hit):
        rk = rk_ref[pl.ds(e, 1), :]
        val = 1.0 if wd_ref is None else wd_ref[pl.ds(e, 1), :]
        target = jnp.where(rk >= SLOT_ROWS, rk - SLOT_ROWS + meta_ref[0, 0, 2 * n_exp + e].astype(F32), -1.0)
        return jnp.where(target == rows, val, hit)

    return lax.fori_loop(0, n_exp, per_expert, jnp.zeros((MOE_CHUNK, ts), F32)).astype(BF16)


def _zero_fill(seg_ref, xs_hbm, stage, sem, n_exp, n_blocks):
    tail = SLOT_ROWS + EXPERT_BLOCK
    stage[0:tail, :] = jnp.zeros((tail, stage.shape[1]), stage.dtype)
    tails = []
    for e in range(n_exp):
        start = jnp.maximum(seg_ref[0, 0, e] - tail, 0)
        tails.append(pltpu.make_async_copy(stage.at[pl.ds(0, tail)],
                                           xs_hbm.at[pl.ds(pl.multiple_of(start, ROW_ALIGN), tail)], sem))
    for cp in tails:
        cp.start()
    n_used = seg_ref[0, 0, n_exp]

    def block_copy(i):
        row = pl.multiple_of(i * EXPERT_BLOCK, EXPERT_BLOCK)
        return pltpu.make_async_copy(stage.at[pl.ds(0, EXPERT_BLOCK)], xs_hbm.at[pl.ds(row, EXPERT_BLOCK)], sem)

    def start_block(i, carry):
        block_copy(i).start()
        return carry

    def wait_block(i, carry):
        block_copy(i).wait()
        return carry

    lax.fori_loop(n_used, n_blocks, start_block, 0)
    for cp in tails:
        cp.wait()
    lax.fori_loop(n_used, n_blocks, wait_block, 0)


def _dispatch_kernel(meta_ref, seg_ref, x_ref, rk_ref, xs_hbm, stage, spill, sem, sem_ov, *,
                     ts, n_exp, n_blocks):
    @pl.when(pl.program_id(0) == 0)
    def _():
        _zero_fill(seg_ref, xs_hbm, stage, sem_ov, n_exp, n_blocks)

    x = x_ref[...]
    slot_row = lax.broadcasted_iota(I32, (SLOT_ROWS, ts), 0).astype(F32)
    group_rows = SLOT_GROUP * SLOT_ROWS
    copies = []
    for g in range(n_exp // SLOT_GROUP):
        experts = range(g * SLOT_GROUP, (g + 1) * SLOT_GROUP)
        pick = jnp.concatenate(
            [jnp.where(rk_ref[e:e + 1, :] == slot_row, 1.0, 0.0) for e in experts], axis=0).astype(BF16)
        stage[g * group_rows:(g + 1) * group_rows, :] = _dot(pick, x).astype(BF16)
        started = _slot_copies(meta_ref, stage, xs_hbm, sem, experts, True)
        for cp in started:
            cp.start()
        copies += started
    n_spill = meta_ref[0, 0, 3 * n_exp]

    def spill_chunk(ci, carry):
        base = pl.multiple_of(ci * MOE_CHUNK, MOE_CHUNK)
        spill[pl.ds(base, MOE_CHUNK), :] = _dot(
            _spill_matrix_rows(meta_ref, rk_ref, None, base, ts, n_exp), x).astype(BF16)
        return carry

    lax.fori_loop(0, n_spill, spill_chunk, 0)
    _for_overflow_pieces(meta_ref, n_exp,
                         lambda e, k: _overflow_copy(meta_ref, spill, xs_hbm, sem_ov, n_exp, e, k, True).start())
    _for_overflow_pieces(meta_ref, n_exp,
                         lambda e, k: _overflow_copy(meta_ref, spill, xs_hbm, sem_ov, n_exp, e, k, True).wait())
    for cp in copies:
        cp.wait()


def _dispatch(h2, rank, meta, seg, p_rows, ts, n_exp):
    t_all, d = h2.shape
    nt = t_all // ts
    kern = functools.partial(_dispatch_kernel, ts=ts, n_exp=n_exp, n_blocks=p_rows // EXPERT_BLOCK)
    return pl.pallas_call(
        kern,
        grid=(nt,),
        in_specs=[
            pl.BlockSpec((1, 1, meta.shape[2]), lambda i: (i, 0, 0), memory_space=pltpu.SMEM),
            pl.BlockSpec((1, 1, seg.shape[2]), lambda i: (0, 0, 0), memory_space=pltpu.SMEM),
            pl.BlockSpec((ts, d), lambda i: (i, 0)),
            pl.BlockSpec((n_exp, ts), lambda i: (0, i)),
        ],
        out_specs=pl.BlockSpec(memory_space=pl.ANY),
        out_shape=jax.ShapeDtypeStruct((p_rows, d), BF16),
        scratch_shapes=[pltpu.VMEM((n_exp * SLOT_ROWS, d), BF16),
                        pltpu.VMEM((ts * TOP_K, d), BF16),
                        pltpu.SemaphoreType.DMA(()), pltpu.SemaphoreType.DMA(())],
        compiler_params=pltpu.CompilerParams(dimension_semantics=("arbitrary",), has_side_effects=True),
        name="dispatch",
    )(meta, seg, h2, rank)


def _expert_kernel(be_ref, nu_ref, xs_ref, wg_ref, wu_ref, wd_ref, ys_ref):
    del be_ref
    i = pl.program_id(0)

    @pl.when(i < nu_ref[0])
    def _():
        xb = xs_ref[...]
        a = _silu(_dot(xb, wg_ref[0])) * _dot(xb, wu_ref[0])
        ys_ref[...] = _dot(a.astype(BF16), wd_ref[0]).astype(BF16)

    @pl.when(i >= nu_ref[0])
    def _():
        ys_ref[...] = jnp.zeros_like(ys_ref)


def _experts(xs, block_e, n_used, w_gate, w_up, w_down, blk):
    p_rows, dw = xs.shape
    n_exp, d, ff = w_gate.shape
    nb = p_rows // blk
    used = lambda i, nu: jnp.minimum(i, nu[0] - 1)
    grid_spec = pltpu.PrefetchScalarGridSpec(
        num_scalar_prefetch=2,
        grid=(nb,),
        in_specs=[
            pl.BlockSpec((blk, dw), lambda i, be, nu: (used(i, nu), 0)),
            pl.BlockSpec((1, d, ff), lambda i, be, nu: (be[used(i, nu)], 0, 0)),
            pl.BlockSpec((1, d, ff), lambda i, be, nu: (be[used(i, nu)], 0, 0)),
            pl.BlockSpec((1, ff, d), lambda i, be, nu: (be[used(i, nu)], 0, 0)),
        ],
        out_specs=pl.BlockSpec((blk, dw), lambda i, be, nu: (i, 0)),
    )
    return pl.pallas_call(
        _expert_kernel,
        grid_spec=grid_spec,
        out_shape=jax.ShapeDtypeStruct((p_rows, dw), BF16),
        compiler_params=pltpu.CompilerParams(dimension_semantics=("arbitrary",)),
        name="experts",
    )(block_e, n_used, xs, w_gate, w_up, w_down)


def _combine_kernel(meta_ref, ys_hbm, rkt_ref, wdt_ref, rk_ref, wd_ref, ex_ref, rp_ref, x1_ref, h2_ref,
                    mod_ref, ng_ref, sg_ref, su_ref, sd_ref, o_ref, stage, spill, acc, sems, sem_ov, *,
                    ts, n_exp):
    n_spill = meta_ref[0, 0, 3 * n_exp]
    n_groups = n_exp // SLOT_GROUP
    groups = [_slot_copies(meta_ref, stage, ys_hbm, sems.at[g],
                           range(g * SLOT_GROUP, (g + 1) * SLOT_GROUP), False) for g in range(n_groups)]
    for copies in groups:
        for cp in copies:
            cp.start()

    def clear_chunk(ci, carry):
        base = pl.multiple_of(ci * MOE_CHUNK, MOE_CHUNK)
        spill[pl.ds(base, MOE_CHUNK), :] = jnp.zeros((MOE_CHUNK, spill.shape[1]), spill.dtype)
        return carry

    lax.fori_loop(0, n_spill, clear_chunk, 0)
    _for_overflow_pieces(meta_ref, n_exp,
                         lambda e, k: _overflow_copy(meta_ref, spill, ys_hbm, sem_ov, n_exp, e, k, False).start())

    xb = h2_ref[...]
    a = _silu(_dot(xb, sg_ref[...])) * _dot(xb, su_ref[...])
    tot = _dot(a.astype(BF16), sd_ref[...])

    rank_lanes = _dot(rkt_ref[...].astype(BF16), ex_ref[...])
    weight_lanes = _dot(wdt_ref[...].astype(BF16), ex_ref[...])

    group_rows = SLOT_GROUP * SLOT_ROWS
    for g in range(n_groups):
        cols = slice(g * group_rows, (g + 1) * group_rows)
        unmix = jnp.where(rank_lanes[:, cols] == rp_ref[:, cols], weight_lanes[:, cols], 0.0).astype(BF16)
        for cp in groups[g]:
            cp.wait()
        tot = tot + _dot(unmix, stage[g * group_rows:(g + 1) * group_rows, :])
    acc[...] = tot

    _for_overflow_pieces(meta_ref, n_exp,
                         lambda e, k: _overflow_copy(meta_ref, spill, ys_hbm, sem_ov, n_exp, e, k, False).wait())

    def spill_chunk(ci, carry):
        base = pl.multiple_of(ci * MOE_CHUNK, MOE_CHUNK)
        acc[...] += _dot_tn(_spill_matrix_rows(meta_ref, rk_ref, wd_ref, base, ts, n_exp),
                            spill[pl.ds(base, MOE_CHUNK), :])
        return carry

    lax.fori_loop(0, n_spill, spill_chunk, 0)
    g2 = mod_ref[0, 5:6, :]
    o_ref[...] = x1_ref[...] + g2 * _rms(acc[...], ng_ref[3:4, :])


def _combine(ys, meta, rank_tm, wd_tm, rank, wd, x1_flat, h2, mod3, norm_g, ws_gate, ws_up, ws_down,
             n_seq, ts, n_exp):
    t_all, d = x1_flat.shape
    nt = t_all // ts
    per_b = n_seq // ts
    ff = ws_gate.shape[1]
    lanes = n_exp * SLOT_ROWS
    lane = lax.broadcasted_iota(I32, (n_exp, lanes), 1)
    expand = jnp.where(lane // SLOT_ROWS == lax.broadcasted_iota(I32, (n_exp, lanes), 0), 1.0, 0.0).astype(BF16)
    slot_rank = (jnp.arange(lanes, dtype=I32) % SLOT_ROWS).astype(F32).reshape(1, lanes)
    c2 = lambda i: (0, 0)
    kern = functools.partial(_combine_kernel, ts=ts, n_exp=n_exp)
    return pl.pallas_call(
        kern,
        grid=(nt,),
        in_specs=[
            pl.BlockSpec((1, 1, meta.shape[2]), lambda i: (i, 0, 0), memory_space=pltpu.SMEM),
            pl.BlockSpec(memory_space=pl.ANY),
            pl.BlockSpec((ts, n_exp), lambda i: (i, 0)),
            pl.BlockSpec((ts, n_exp), lambda i: (i, 0)),
            pl.BlockSpec((n_exp, ts), lambda i: (0, i)),
            pl.BlockSpec((n_exp, ts), lambda i: (0, i)),
            pl.BlockSpec((n_exp, lanes), c2),
            pl.BlockSpec((1, lanes), c2),
            pl.BlockSpec((ts, d), lambda i: (i, 0)),
            pl.BlockSpec((ts, d), lambda i: (i, 0)),
            pl.BlockSpec((1, N_MOD, d), lambda i: (i // per_b, 0, 0)),
            pl.BlockSpec((4, d), c2),
            pl.BlockSpec((d, ff), c2),
            pl.BlockSpec((d, ff), c2),
            pl.BlockSpec((ff, d), c2),
        ],
        out_specs=pl.BlockSpec((ts, d), lambda i: (i, 0)),
        out_shape=jax.ShapeDtypeStruct((t_all, d), F32),
        scratch_shapes=[pltpu.VMEM((lanes, d), BF16), pltpu.VMEM((ts * TOP_K, d), BF16),
                        pltpu.VMEM((ts, d), F32),
                        pltpu.SemaphoreType.DMA((n_exp // SLOT_GROUP,)), pltpu.SemaphoreType.DMA(())],
        compiler_params=pltpu.CompilerParams(dimension_semantics=("arbitrary",)),
        name="combine",
    )(meta, ys, rank_tm, wd_tm, rank, wd, expand, slot_rank, x1_flat, h2, mod3, norm_g,
      ws_gate, ws_up, ws_down)


def _rope_tables(n):
    rows = jnp.repeat(jnp.arange(n // GRID_W, dtype=F32), GRID_W)
    cols = jnp.tile(jnp.arange(GRID_W, dtype=F32), n // GRID_W)
    quarter = LANES // 4
    freqs = ROPE_BASE ** (-jnp.arange(quarter, dtype=F32) / quarter)
    ang = jnp.concatenate([rows[:, None] * freqs, cols[:, None] * freqs], axis=-1)
    cos, sin = jnp.cos(ang), jnp.sin(ang)
    return jnp.concatenate([cos, cos], axis=-1), jnp.concatenate([-sin, sin], axis=-1)


def _retention_tables(log_decay):
    lg = -jnp.exp(log_decay.astype(F32))
    idx = jnp.arange(CHUNK, dtype=F32)
    rel = idx[:, None] - idx[None, :]
    lg3 = lg[:, :, None, None]
    intra_f = jnp.where(rel >= 0, jnp.exp(jnp.maximum(rel, 0.0) * lg3[0]), 0.0)
    intra_b = jnp.where(rel <= 0, jnp.exp(jnp.maximum(-rel, 0.0) * lg3[1]), 0.0)
    kd_f = jnp.exp((CHUNK - 1 - idx)[None, :] * lg[0][:, None])
    kd_b = jnp.exp(idx[None, :] * lg[1][:, None])
    qd_f = jnp.exp((idx + 1)[None, :] * lg[0][:, None])
    qd_b = jnp.exp((CHUNK - idx)[None, :] * lg[1][:, None])
    bc = lambda t: jnp.broadcast_to(t[:, :, None], (HEADS, CHUNK, LANES))
    intra = jnp.concatenate([intra_f, intra_b], axis=0)
    kd = jnp.concatenate([bc(kd_f), bc(kd_b)], axis=0)
    qd = jnp.concatenate([bc(qd_f), bc(qd_b)], axis=0)
    cd = jnp.broadcast_to(jnp.exp(CHUNK * lg).reshape(2 * HEADS, 1), (2 * HEADS, LANES))
    return intra, kd, qd, cd


def kernel(x, c, ctx, c_ctx, w_mod, b_mod, norm_g, w_in, ret_log_decay, ret_norm_g, mlstm_conv_w,
           mlstm_conv_b, mlstm_gate_b, mlstm_norm_g, w_out, w_router, router_bias, w_gate, w_up, w_down,
           ws_gate, ws_up, ws_down):
    b, n, d = x.shape
    n_ctx = ctx.shape[1]
    depth = w_mod.shape[0]
    assert depth == 1, "only the single-layer configuration is implemented"
    assert d // 2 // HEADS == LANES
    n_exp = w_router.shape[2]
    t_all = b * n
    r_w = d // 2
    main_cols = 8 * r_w
    l = 0

    pad = (-(b + 1)) % 8
    cc = jnp.concatenate([c, c_ctx[None, :], jnp.zeros((pad, d), F32)], axis=0)
    mod3 = _modulation(cc, w_mod[l], b_mod[l]).reshape(b + 1 + pad, N_MOD, d)

    w_main = w_in[l, :, :main_cols].astype(BF16)
    wg = w_in[l, :, main_cols:].astype(BF16)
    wgt = wg.T
    gb = mlstm_gate_b[l].reshape(-1).astype(F32)
    tabs = _retention_tables(ret_log_decay[l])
    head_g = jnp.concatenate([ret_norm_g[l], mlstm_norm_g[l]]).reshape(1, d).astype(F32)
    wr = w_router[l].T.astype(F32)
    wr_hi = wr.astype(BF16)
    wr_lo = (wr - wr_hi.astype(F32)).astype(BF16)

    def inproj(seq, mod_row, ts):
        cos2, sin2 = _rope_tables(n) if mod_row is None else (
            jnp.ones((seq.shape[1], LANES), F32), jnp.zeros((seq.shape[1], LANES), F32))
        return _inproj(seq, mod3, mod_row, norm_g[l, 0:1], w_main, wgt, wg, mlstm_conv_w[l],
                       mlstm_conv_b[l].reshape(1, -1), gb.reshape(16, 1), gb.reshape(1, 16), cos2, sin2, ts)

    nst = 2 * HEADS
    zero_states = (jnp.zeros((b, nst, LANES, LANES), F32), jnp.zeros((b, nst, LANES, 2 * LANES), F32),
                   jnp.zeros((b, nst, LANES), F32))
    p_c, gr_c, gc_c = inproj(ctx, b, min(n_ctx, 512))
    ctx_states = _scan(p_c, gr_c, gc_c, tabs, zero_states, with_output=False)

    ts = min(n, 512)
    p_l, gr_l, gc_l = inproj(x, None, ts)
    o_f, o_b, _, _, _ = _scan(p_l, gr_l, gc_l, tabs, tuple(ctx_states), with_output=True)
    ts_moe = MOE_TILE
    x1, h2, rank, wdense, tile_cnt = _post(
        x, o_f, o_b, p_l, mod3, norm_g[l], head_g, w_out[l].astype(BF16), wr_hi, wr_lo,
        router_bias[l].reshape(n_exp, 1).astype(F32), ts_moe)

    nt = t_all // ts_moe
    cnt = tile_cnt[:, 0, :].astype(I32)
    run_rows = (cnt + ROW_ALIGN - 1) // ROW_ALIGN * ROW_ALIGN
    seg_cap = (jnp.sum(run_rows, axis=0) + SLOT_ROWS + EXPERT_BLOCK - 1) // EXPERT_BLOCK * EXPERT_BLOCK
    seg_end = jnp.cumsum(seg_cap)
    run_start = (seg_end - seg_cap)[None, :] + jnp.cumsum(run_rows, axis=0) - run_rows
    ov_rows = jnp.maximum(run_rows - SLOT_ROWS, 0)
    ov_off = jnp.cumsum(ov_rows, axis=1) - ov_rows
    n_spill = (jnp.sum(ov_rows, axis=1, keepdims=True) + MOE_CHUNK - 1) // MOE_CHUNK
    meta = jnp.concatenate([run_start, ov_rows // ROW_ALIGN, ov_off, n_spill], axis=1).astype(I32)
    meta = jnp.pad(meta, ((0, 0), (0, (-meta.shape[1]) % LANES))).reshape(nt, 1, -1)
    p_rows = -(-(t_all * TOP_K + nt * n_exp * (ROW_ALIGN - 1) + n_exp * (SLOT_ROWS + EXPERT_BLOCK - 1))
               // EXPERT_BLOCK) * EXPERT_BLOCK
    nb = p_rows // EXPERT_BLOCK
    blk_first = jnp.arange(nb, dtype=I32) * EXPERT_BLOCK
    block_e = jnp.minimum(jnp.sum((seg_end[None, :] <= blk_first[:, None]).astype(I32), axis=1), n_exp - 1)
    n_used = (seg_end[-1:] // EXPERT_BLOCK).astype(I32)
    seg = jnp.concatenate([seg_end.astype(I32), n_used])
    seg = jnp.pad(seg, (0, (-seg.shape[0]) % LANES)).reshape(1, 1, -1)

    xs = _dispatch(h2, rank, meta, seg, p_rows, ts_moe, n_exp)
    ys = _experts(xs, block_e, n_used, w_gate[l].astype(BF16), w_up[l].astype(BF16), w_down[l].astype(BF16),
                  EXPERT_BLOCK)
    out = _combine(ys, meta, rank.T, wdense.T, rank, wdense, x1.reshape(t_all, d), h2, mod3, norm_g[l],
                   ws_gate[l].astype(BF16), ws_up[l].astype(BF16), ws_down[l].astype(BF16), n, ts_moe, n_exp)
    return out.reshape(b, n, d)
```

```python
import functools

import jax
import jax.numpy as jnp
from jax import lax
from jax.experimental import pallas as pl
from jax.experimental.pallas import tpu as pltpu

F32 = jnp.float32
BF16 = jnp.bfloat16
I32 = jnp.int32

EPS = 1e-6
LANES = 128
CHUNK = 128
HEADS = 4
GRID_W = 64
ROPE_BASE = 10000.0
N_GROUPS = 8
TOPK_GROUPS = 4
TOP_K = 8
ROUTED_SCALE = 2.5
N_MOD = 6
MOE_TILE = 256
SPILL_CHUNK = 64
ROW_ALIGN = 16
SLOT_ROWS = 48
SLOT_GROUP = 8
EXPERT_BLOCK = 512
NEG_INF = float("-inf")


def _sigmoid(v):
    return 1.0 / (1.0 + jnp.exp(-v))


def _silu(v):
    return v * _sigmoid(v)


def _log_sigmoid(v):
    return jnp.minimum(v, 0.0) - jnp.log(1.0 + jnp.exp(-jnp.abs(v)))


def _dot(a, b):
    return jnp.dot(a, b, preferred_element_type=F32)


def _dot_nt(a, b):
    return lax.dot_general(a, b, (((1,), (1,)), ((), ())), preferred_element_type=F32)


def _dot_tn(a, b):
    return lax.dot_general(a, b, (((0,), (0,)), ((), ())), preferred_element_type=F32)


def _split3(a):
    hi = a.astype(BF16)
    r = a - hi.astype(F32)
    mid = r.astype(BF16)
    lo = (r - mid.astype(F32)).astype(BF16)
    return hi, mid, lo


def _rms(v, g):
    ms = jnp.mean(v * v, axis=-1, keepdims=True)
    return v * lax.rsqrt(ms + EPS) * g


def _mod_kernel(c_ref, w_ref, b_ref, o_ref):
    a = _silu(c_ref[...])
    o_ref[...] = jnp.dot(a, w_ref[...], preferred_element_type=F32,
                         precision=lax.Precision.HIGHEST) + b_ref[...]


def _modulation(cc, w_mod, b_mod):
    rows, d = cc.shape
    cols = w_mod.shape[1]
    tn = d
    return pl.pallas_call(
        _mod_kernel,
        grid=(cols // tn,),
        in_specs=[pl.BlockSpec((rows, d), lambda j: (0, 0)),
                  pl.BlockSpec((d, tn), lambda j: (0, j)),
                  pl.BlockSpec((1, tn), lambda j: (0, j))],
        out_specs=pl.BlockSpec((rows, tn), lambda j: (0, j)),
        out_shape=jax.ShapeDtypeStruct((rows, cols), F32),
        name="mod",
    )(cc, w_mod, b_mod.reshape(1, cols))


def _inproj_kernel(x_ref, xp_ref, xn_ref, mod_ref, g_ref, w_ref, wgt_ref, wg_ref, cw_ref, cb_ref,
                   gbr_ref, gbc_ref, cos_ref, sin_ref, p_ref, gr_ref, gc_ref, *, ts, d):
    i = pl.program_id(1)
    last = pl.num_programs(1) - 1
    r_w = d // 2
    shift = mod_ref[0, 0:1, :]
    scale = mod_ref[0, 1:2, :]
    g = g_ref[...]

    def normmod(v):
        return _rms(v, g) * (1.0 + scale) + shift

    hb = normmod(x_ref[0]).astype(BF16)
    halo = jnp.concatenate([xp_ref[0], xn_ref[0]], axis=0)
    ph = _dot(normmod(halo).astype(BF16), w_ref[:, 4 * r_w:4 * r_w + 2 * r_w])
    prev_row = jnp.where(i == 0, 0.0, ph[7:8, :])
    next_row = jnp.where(i == last, 0.0, ph[8:9, :])

    cos2 = cos_ref[...]
    sin2 = sin_ref[...]
    rows = lax.broadcasted_iota(I32, (ts, r_w), 0)
    qscale = LANES ** -0.5

    for j in range(8):
        acc = _dot(hb, w_ref[:, j * r_w:(j + 1) * r_w])
        if j in (0, 1):
            if j == 0:
                acc = acc * qscale
            parts = []
            for h in range(HEADS):
                t = acc[:, h * LANES:(h + 1) * LANES]
                parts.append(t * cos2 + pltpu.roll(t, LANES // 2, axis=1) * sin2)
            acc = jnp.concatenate(parts, axis=1)
        elif j in (4, 5):
            c0 = (j - 4) * r_w
            pr = prev_row[:, c0:c0 + r_w]
            nx = next_row[:, c0:c0 + r_w]
            down = jnp.where(rows == 0, pr, pltpu.roll(acc, 1, axis=0))
            up = jnp.where(rows == ts - 1, nx, pltpu.roll(acc, ts - 1, axis=0))
            cw = cw_ref[:, c0:c0 + r_w]
            acc = down * cw[0:1, :] + acc * cw[1:2, :] + up * cw[2:3, :] + cb_ref[:, c0:c0 + r_w]
            acc = _silu(acc)
            if j == 5:
                acc = acc * qscale
        p_ref[0, :, j * r_w:(j + 1) * r_w] = acc.astype(BF16)

    gr = _dot_nt(wgt_ref[...], hb) + gbr_ref[...]
    ch_r = lax.broadcasted_iota(I32, gr.shape, 0)
    gr_ref[0] = jnp.where((ch_r // HEADS) % 2 == 1, _log_sigmoid(gr), gr)
    gc = _dot(hb, wg_ref[...]) + gbc_ref[...]
    ch_c = lax.broadcasted_iota(I32, gc.shape, 1)
    gc_ref[0] = jnp.where((ch_c // HEADS) % 2 == 1, _log_sigmoid(gc), gc)


def _inproj(x, mod3, mod_row, g, w_main, wgt, wg, conv_w, conv_b, gb_col, gb_row, cos2, sin2, ts):
    b, n, d = x.shape
    nt = n // ts
    nb8 = n // 8
    hb = ts // 8
    cols = w_main.shape[1]
    if mod_row is None:
        mod_map = lambda bi, i: (bi, 0, 0)
    else:
        mod_map = lambda bi, i: (mod_row, 0, 0)
    const2 = lambda bi, i: (0, 0)
    kern = functools.partial(_inproj_kernel, ts=ts, d=d)
    return pl.pallas_call(
        kern,
        grid=(b, nt),
        in_specs=[
            pl.BlockSpec((1, ts, d), lambda bi, i: (bi, i, 0)),
            pl.BlockSpec((1, 8, d), lambda bi, i: (bi, jnp.maximum(i * hb - 1, 0), 0)),
            pl.BlockSpec((1, 8, d), lambda bi, i: (bi, jnp.minimum((i + 1) * hb, nb8 - 1), 0)),
            pl.BlockSpec((1, N_MOD, d), mod_map),
            pl.BlockSpec((1, d), const2),
            pl.BlockSpec((d, cols), const2),
            pl.BlockSpec((16, d), const2),
            pl.BlockSpec((d, 16), const2),
            pl.BlockSpec((3, d), const2),
            pl.BlockSpec((1, d), const2),
            pl.BlockSpec((16, 1), const2),
            pl.BlockSpec((1, 16), const2),
            pl.BlockSpec((ts, LANES), lambda bi, i: (i, 0)),
            pl.BlockSpec((ts, LANES), lambda bi, i: (i, 0)),
        ],
        out_specs=[
            pl.BlockSpec((1, ts, cols), lambda bi, i: (bi, i, 0)),
            pl.BlockSpec((1, 16, ts), lambda bi, i: (bi, 0, i)),
            pl.BlockSpec((1, ts, 16), lambda bi, i: (bi, i, 0)),
        ],
        out_shape=[
            jax.ShapeDtypeStruct((b, n, cols), BF16),
            jax.ShapeDtypeStruct((b, 16, n), F32),
            jax.ShapeDtypeStruct((b, n, 16), F32),
        ],
        compiler_params=pltpu.CompilerParams(dimension_semantics=("parallel", "parallel")),
        name="inproj",
    )(x, x, x, mod3, g, w_main, wgt, wg, conv_w, conv_b, gb_col, gb_row, cos2, sin2)


def _scan_kernel(pf_ref, pb_ref, grf_ref, grb_ref, gcf_ref, gcb_ref, intra_ref, kd_ref, qd_ref, cd_ref,
                 rs0_ref, mc0_ref, mm0_ref, *out_refs, with_output, r_w):
    if with_output:
        of_ref, ob_ref, rs_ref, mc_ref, mm_ref = out_refs
    else:
        rs_ref, mc_ref, mm_ref = out_refs
    j = pl.program_id(1)

    @pl.when(j == 0)
    def _():
        rs_ref[...] = rs0_ref[...]
        mc_ref[...] = mc0_ref[...]
        mm_ref[...] = mm0_ref[...]

    c = CHUNK
    row = lax.broadcasted_iota(I32, (c, c), 0)
    col = lax.broadcasted_iota(I32, (c, c), 1)
    tri_le = (row <= col)
    tri_ge = (row >= col)
    ones_ext = jnp.where(lax.broadcasted_iota(I32, (c, LANES), 1) == 0, 1.0, 0.0).astype(BF16)

    def cumsums(gr, gc, fwd):
        m_row = jnp.where(tri_le if fwd else tri_ge, 1.0, 0.0).astype(BF16)
        m_col = jnp.where(tri_ge if fwd else tri_le, 1.0, 0.0).astype(BF16)
        b_row = sum(_dot(piece, m_row) for piece in _split3(gr))
        b_col = sum(_dot(m_col, piece) for piece in _split3(gc))
        return b_row, b_col

    for dr in range(2):
        fwd = dr == 0
        p_ref = pf_ref if fwd else pb_ref
        o_ref = None
        if with_output:
            o_ref = of_ref if fwd else ob_ref
        gr = (grf_ref if fwd else grb_ref)[0]
        gc = (gcf_ref if fwd else gcb_ref)[0]
        cs_row, cs_col = cumsums(gr, gc, fwd)
        causal = tri_ge if fwd else tri_le

        for h in range(HEADS):
            hs = slice(h * LANES, (h + 1) * LANES)
            st = dr * HEADS + h
            k = p_ref[0, :, r_w + h * LANES:r_w + (h + 1) * LANES]
            v = p_ref[0, :, 2 * r_w + h * LANES:2 * r_w + (h + 1) * LANES]
            s_prev = rs_ref[0, st]
            ks = (k.astype(F32) * kd_ref[st]).astype(BF16)
            upd = _dot_tn(ks, v)
            if with_output:
                q = p_ref[0, :, hs]
                sc = _dot_nt(q, k) * intra_ref[st]
                out = _dot(sc.astype(BF16), v) + qd_ref[st] * _dot(q, s_prev.astype(BF16))
                o_ref[0, :, hs] = out.astype(BF16)
            rs_ref[0, st] = s_prev * cd_ref[st:st + 1, :] + upd

            o0 = 4 * r_w
            mk = p_ref[0, :, o0 + r_w + h * LANES:o0 + r_w + (h + 1) * LANES]
            mv = p_ref[0, :, o0 + 2 * r_w + h * LANES:o0 + 2 * r_w + (h + 1) * LANES]
            ci = dr * 2 * HEADS + h
            cf = ci + HEADS
            ib_row = gr[ci:ci + 1, :]
            b_row = cs_row[cf:cf + 1, :]
            ib_col = gc[:, ci:ci + 1]
            b_col = cs_col[:, cf:cf + 1]
            b_tot = b_row[:, c - 1:c] if fwd else b_row[:, 0:1]
            m_prev = mm_ref[0, st:st + 1, 0:1]
            log_ws_row = b_tot - b_row + ib_row
            m_next = jnp.maximum(b_tot + m_prev, jnp.max(log_ws_row, axis=1, keepdims=True))
            decay_prev = jnp.exp(b_tot + m_prev - m_next)
            ws_col = jnp.exp(b_tot - b_col + ib_col - m_next)
            kw = (mk.astype(F32) * ws_col).astype(BF16)
            v_ext = jnp.concatenate([mv, ones_ext], axis=1)
            upd_c = _dot_tn(kw, v_ext)
            c_prev = mc_ref[0, st]
            if with_output:
                mq = p_ref[0, :, o0 + h * LANES:o0 + (h + 1) * LANES]
                dm = jnp.where(causal, b_col - b_row + ib_row, NEG_INF)
                log_inter = b_col + m_prev
                m_i = jnp.maximum(log_inter, jnp.max(dm, axis=1, keepdims=True))
                w = jnp.exp(dm - m_i)
                inter = jnp.exp(log_inter - m_i)
                sc = _dot_nt(mq, mk) * w
                hx = _dot(sc.astype(BF16), v_ext) + inter * _dot(mq, c_prev.astype(BF16))
                den = hx[:, LANES:LANES + 1]
                hout = hx[:, :LANES] / jnp.maximum(jnp.abs(den), jnp.exp(-m_i))
                o_ref[0, :, r_w + h * LANES:r_w + (h + 1) * LANES] = hout.astype(BF16)
            mc_ref[0, st] = decay_prev * c_prev + upd_c
            mm_ref[0, st:st + 1, :] = jnp.broadcast_to(m_next, (1, LANES))


def _scan(p, g_row, g_col, tabs, states, with_output):
    b, n, cols = p.shape
    nch = n // CHUNK
    r_w = cols // 8
    intra, kd, qd, cd = tabs
    rs0, mc0, mm0 = states
    nst = 2 * HEADS
    fwd3 = lambda bi, j: (bi, j, 0)
    bwd3 = lambda bi, j: (bi, nch - 1 - j, 0)
    c3 = lambda bi, j: (0, 0, 0)
    st4 = lambda bi, j: (bi, 0, 0, 0)
    in_specs = [
        pl.BlockSpec((1, CHUNK, cols), fwd3),
        pl.BlockSpec((1, CHUNK, cols), bwd3),
        pl.BlockSpec((1, 16, CHUNK), lambda bi, j: (bi, 0, j)),
        pl.BlockSpec((1, 16, CHUNK), lambda bi, j: (bi, 0, nch - 1 - j)),
        pl.BlockSpec((1, CHUNK, 16), fwd3),
        pl.BlockSpec((1, CHUNK, 16), bwd3),
        pl.BlockSpec((nst, CHUNK, LANES), c3),
        pl.BlockSpec((nst, CHUNK, LANES), c3),
        pl.BlockSpec((nst, CHUNK, LANES), c3),
        pl.BlockSpec((nst, LANES), lambda bi, j: (0, 0)),
        pl.BlockSpec((1, nst, LANES, LANES), st4),
        pl.BlockSpec((1, nst, LANES, 2 * LANES), st4),
        pl.BlockSpec((1, nst, LANES), lambda bi, j: (bi, 0, 0)),
    ]
    st_specs = [
        pl.BlockSpec((1, nst, LANES, LANES), st4),
        pl.BlockSpec((1, nst, LANES, 2 * LANES), st4),
        pl.BlockSpec((1, nst, LANES), lambda bi, j: (bi, 0, 0)),
    ]
    st_shapes = [
        jax.ShapeDtypeStruct((b, nst, LANES, LANES), F32),
        jax.ShapeDtypeStruct((b, nst, LANES, 2 * LANES), F32),
        jax.ShapeDtypeStruct((b, nst, LANES), F32),
    ]
    if with_output:
        out_specs = [pl.BlockSpec((1, CHUNK, 2 * r_w), fwd3), pl.BlockSpec((1, CHUNK, 2 * r_w), bwd3)] + st_specs
        out_shape = [jax.ShapeDtypeStruct((b, n, 2 * r_w), BF16)] * 2 + st_shapes
    else:
        out_specs, out_shape = st_specs, st_shapes
    kern = functools.partial(_scan_kernel, with_output=with_output, r_w=r_w)
    return pl.pallas_call(
        kern,
        grid=(b, nch),
        in_specs=in_specs,
        out_specs=out_specs,
        out_shape=out_shape,
        compiler_params=pltpu.CompilerParams(dimension_semantics=("parallel", "arbitrary")),
        name="scan_out" if with_output else "scan_state",
    )(p, p, g_row, g_row, g_col, g_col, intra, kd, qd, cd, rs0, mc0, mm0)


def _post_kernel(x_ref, of_ref, ob_ref, rg_ref, mo_ref, mod_ref, ng_ref, hg_ref, wo_ref, wrh_ref, wrl_ref,
                 rb_ref, su_ref, x1_ref, h2_ref, rk_ref, wd_ref, cnt_ref, *, ts, d, n_exp):
    s = of_ref[0].astype(F32) + ob_ref[0].astype(F32)
    parts = []
    for gi in range(2 * HEADS):
        sl = s[:, gi * LANES:(gi + 1) * LANES]
        mu = jnp.mean(sl, axis=-1, keepdims=True)
        dv = sl - mu
        var = jnp.mean(dv * dv, axis=-1, keepdims=True)
        y = dv * lax.rsqrt(var + EPS) * hg_ref[:, gi * LANES:(gi + 1) * LANES]
        if gi < HEADS:
            gate = _silu(rg_ref[0, :, gi * LANES:(gi + 1) * LANES].astype(F32))
        else:
            gate = _sigmoid(mo_ref[0, :, (gi - HEADS) * LANES:(gi - HEADS + 1) * LANES].astype(F32))
        parts.append((y * gate).astype(BF16))
    mixed = jnp.concatenate(parts, axis=1)
    y = _dot(mixed, wo_ref[...])
    g1 = mod_ref[0, 2:3, :]
    sh2 = mod_ref[0, 3:4, :]
    sc2 = mod_ref[0, 4:5, :]
    x1 = x_ref[0] + g1 * _rms(y, ng_ref[1:2, :])
    x1_ref[0] = x1
    h2 = _rms(x1, ng_ref[2:3, :]) * (1.0 + sc2) + sh2
    h_hi = h2.astype(BF16)
    h2_ref[...] = h_hi

    h_lo = (h2 - h_hi.astype(F32)).astype(BF16)
    logits = _dot_nt(wrh_ref[...], h_hi) + _dot_nt(wrh_ref[...], h_lo) + _dot_nt(wrl_ref[...], h_hi)
    scores = _sigmoid(logits)
    sel = scores + rb_ref[...]
    gsz = n_exp // N_GROUPS
    iota_g = lax.broadcasted_iota(I32, (gsz, ts), 0).astype(F32)
    grp = []
    for gi in range(N_GROUPS):
        blk = sel[gi * gsz:(gi + 1) * gsz, :]
        m1 = jnp.max(blk, axis=0, keepdims=True)
        i1 = jnp.min(jnp.where(blk == m1, iota_g, float(gsz)), axis=0, keepdims=True)
        m2 = jnp.max(jnp.where(iota_g == i1, NEG_INF, blk), axis=0, keepdims=True)
        grp.append(m1 + m2)
    masked_parts = []
    for gi in range(N_GROUPS):
        rank = jnp.zeros((1, ts), F32)
        for gj in range(N_GROUPS):
            if gj == gi:
                continue
            beats = (grp[gj] >= grp[gi]) if gj < gi else (grp[gj] > grp[gi])
            rank = rank + jnp.where(beats, 1.0, 0.0)
        keep = rank < float(TOPK_GROUPS)
        masked_parts.append(jnp.where(keep, sel[gi * gsz:(gi + 1) * gsz, :], NEG_INF))
    masked = jnp.concatenate(masked_parts, axis=0)

    iota_e = lax.broadcasted_iota(I32, (n_exp, ts), 0).astype(F32)
    selmask = jnp.zeros((n_exp, ts), F32)
    for _ in range(TOP_K):
        mx = jnp.max(masked, axis=0, keepdims=True)
        ei = jnp.min(jnp.where(masked == mx, iota_e, float(n_exp)), axis=0, keepdims=True)
        hit = iota_e == ei
        selmask = jnp.where(hit, 1.0, selmask)
        masked = jnp.where(hit, NEG_INF, masked)
    picked = selmask > 0.0
    wsel = jnp.where(picked, scores, 0.0)
    wd_ref[...] = wsel / jnp.sum(wsel, axis=0, keepdims=True) * ROUTED_SCALE
    rank = _dot(selmask.astype(BF16), su_ref[...])
    rk_ref[...] = jnp.where(picked, rank, -1.0)
    cnt_ref[0] = _dot_nt(jnp.ones((8, ts), BF16), selmask.astype(BF16))


def _post(x, o_f, o_b, p, mod3, norm_g, head_g, w_out, wr_hi, wr_lo, rbias, ts):
    b, n, d = x.shape
    nt = n // ts
    t_all = b * n
    n_exp = wr_hi.shape[0]
    r_w = d // 2
    su = jnp.where(lax.broadcasted_iota(I32, (ts, ts), 0) < lax.broadcasted_iota(I32, (ts, ts), 1),
                   1.0, 0.0).astype(BF16)
    tok3 = lambda bi, i: (bi, i, 0)
    c2 = lambda bi, i: (0, 0)
    flat = lambda bi, i: (0, bi * nt + i)
    kern = functools.partial(_post_kernel, ts=ts, d=d, n_exp=n_exp)
    return pl.pallas_call(
        kern,
        grid=(b, nt),
        in_specs=[
            pl.BlockSpec((1, ts, d), tok3),
            pl.BlockSpec((1, ts, d), tok3),
            pl.BlockSpec((1, ts, d), tok3),
            pl.BlockSpec((1, ts, r_w), lambda bi, i: (bi, i, 3)),
            pl.BlockSpec((1, ts, r_w), lambda bi, i: (bi, i, 7)),
            pl.BlockSpec((1, N_MOD, d), lambda bi, i: (bi, 0, 0)),
            pl.BlockSpec((4, d), c2),
            pl.BlockSpec((1, d), c2),
            pl.BlockSpec((d, d), c2),
            pl.BlockSpec((n_exp, d), c2),
            pl.BlockSpec((n_exp, d), c2),
            pl.BlockSpec((n_exp, 1), c2),
            pl.BlockSpec((ts, ts), c2),
        ],
        out_specs=[
            pl.BlockSpec((1, ts, d), tok3),
            pl.BlockSpec((ts, d), lambda bi, i: (bi * nt + i, 0)),
            pl.BlockSpec((n_exp, ts), flat),
            pl.BlockSpec((n_exp, ts), flat),
            pl.BlockSpec((1, 8, n_exp), lambda bi, i: (bi * nt + i, 0, 0)),
        ],
        out_shape=[
            jax.ShapeDtypeStruct((b, n, d), F32),
            jax.ShapeDtypeStruct((t_all, d), BF16),
            jax.ShapeDtypeStruct((n_exp, t_all), F32),
            jax.ShapeDtypeStruct((n_exp, t_all), F32),
            jax.ShapeDtypeStruct((b * nt, 8, n_exp), F32),
        ],
        compiler_params=pltpu.CompilerParams(dimension_semantics=("parallel", "parallel")),
        name="post",
    )(x, o_f, o_b, p, p, mod3, norm_g, head_g, w_out, wr_hi, wr_lo, rbias, su)


def _slot_copies(meta_ref, stage, hbm, sem, experts, to_hbm):
    copies = []
    for e in experts:
        rows = hbm.at[pl.ds(pl.multiple_of(meta_ref[0, 0, e], ROW_ALIGN), SLOT_ROWS)]
        slot = stage.at[pl.ds(e * SLOT_ROWS, SLOT_ROWS)]
        copies.append(pltpu.make_async_copy(slot, rows, sem) if to_hbm
                      else pltpu.make_async_copy(rows, slot, sem))
    return copies


def _overflow_copy(meta_ref, spill, hbm, sem, n_exp, j, i, to_hbm):
    e = meta_ref[0, 0, n_exp + j]
    src = meta_ref[0, 0, 3 * n_exp + j] + ROW_ALIGN * i
    dst = meta_ref[0, 0, e] + SLOT_ROWS + ROW_ALIGN * i
    piece = spill.at[pl.ds(pl.multiple_of(src, ROW_ALIGN), ROW_ALIGN)]
    rows = hbm.at[pl.ds(pl.multiple_of(dst, ROW_ALIGN), ROW_ALIGN)]
    return pltpu.make_async_copy(piece, rows, sem) if to_hbm else pltpu.make_async_copy(rows, piece, sem)


def _for_overflow_pieces(meta_ref, n_exp, fn):
    def per_expert(j, carry):
        def per_piece(i, c2):
            fn(j, i)
            return c2
        return lax.fori_loop(0, meta_ref[0, 0, 2 * n_exp + j], per_piece, carry)
    lax.fori_loop(0, meta_ref[0, 0, 4 * n_exp], per_expert, 0)


def _spill_matrix_rows(meta_ref, rk_ref, wd_ref, base, ts, n_exp):
    rows = (lax.broadcasted_iota(I32, (SPILL_CHUNK, ts), 0) + base).astype(F32)

    def per_expert(j, hit):
        e = meta_ref[0, 0, n_exp + j]
        rk = rk_ref[pl.ds(e, 1), :]
        val = 1.0 if wd_ref is None else wd_ref[pl.ds(e, 1), :]
        target = jnp.where(rk >= SLOT_ROWS, rk - SLOT_ROWS + meta_ref[0, 0, 3 * n_exp + j].astype(F32), -1.0)
        return jnp.where(target == rows, val, hit)

    return lax.fori_loop(0, meta_ref[0, 0, 4 * n_exp], per_expert,
                         jnp.zeros((SPILL_CHUNK, ts), F32)).astype(BF16)


def _zero_fill(seg_ref, xs_hbm, stage, sem, n_exp, n_blocks):
    tail = SLOT_ROWS + EXPERT_BLOCK
    stage[0:tail, :] = jnp.zeros((tail, stage.shape[1]), stage.dtype)
    tails = []
    for e in range(n_exp):
        start = jnp.maximum(seg_ref[0, 0, e] - tail, 0)
        tails.append(pltpu.make_async_copy(stage.at[pl.ds(0, tail)],
                                           xs_hbm.at[pl.ds(pl.multiple_of(start, ROW_ALIGN), tail)], sem))
    for cp in tails:
        cp.start()
    n_used = seg_ref[0, 0, n_exp]

    def block_copy(i):
        row = pl.multiple_of(i * EXPERT_BLOCK, EXPERT_BLOCK)
        return pltpu.make_async_copy(stage.at[pl.ds(0, EXPERT_BLOCK)], xs_hbm.at[pl.ds(row, EXPERT_BLOCK)], sem)

    def start_block(i, carry):
        block_copy(i).start()
        return carry

    def wait_block(i, carry):
        block_copy(i).wait()
        return carry

    lax.fori_loop(n_used, n_blocks, start_block, 0)
    for cp in tails:
        cp.wait()
    lax.fori_loop(n_used, n_blocks, wait_block, 0)


def _dispatch_kernel(meta_ref, seg_ref, x_ref, rk_ref, xs_hbm, stage, spill, sem, sem_ov, *,
                     ts, n_exp, n_blocks):
    @pl.when(pl.program_id(0) == 0)
    def _():
        _zero_fill(seg_ref, xs_hbm, stage, sem_ov, n_exp, n_blocks)

    x = x_ref[...]
    slot_row = lax.broadcasted_iota(I32, (SLOT_ROWS, ts), 0).astype(F32)
    group_rows = SLOT_GROUP * SLOT_ROWS
    copies = []
    for g in range(n_exp // SLOT_GROUP):
        experts = range(g * SLOT_GROUP, (g + 1) * SLOT_GROUP)
        pick = jnp.concatenate(
            [jnp.where(rk_ref[e:e + 1, :] == slot_row, 1.0, 0.0) for e in experts], axis=0).astype(BF16)
        stage[g * group_rows:(g + 1) * group_rows, :] = _dot(pick, x).astype(BF16)
        started = _slot_copies(meta_ref, stage, xs_hbm, sem, experts, True)
        for cp in started:
            cp.start()
        copies += started
    n_spill = meta_ref[0, 0, 4 * n_exp + 1]

    def spill_chunk(ci, carry):
        base = pl.multiple_of(ci * SPILL_CHUNK, SPILL_CHUNK)
        spill[pl.ds(base, SPILL_CHUNK), :] = _dot(
            _spill_matrix_rows(meta_ref, rk_ref, None, base, ts, n_exp), x).astype(BF16)
        return carry

    lax.fori_loop(0, n_spill, spill_chunk, 0)
    _for_overflow_pieces(meta_ref, n_exp,
                         lambda e, k: _overflow_copy(meta_ref, spill, xs_hbm, sem_ov, n_exp, e, k, True).start())
    _for_overflow_pieces(meta_ref, n_exp,
                         lambda e, k: _overflow_copy(meta_ref, spill, xs_hbm, sem_ov, n_exp, e, k, True).wait())
    for cp in copies:
        cp.wait()


def _dispatch(h2, rank, meta, seg, p_rows, ts, n_exp):
    t_all, d = h2.shape
    nt = t_all // ts
    kern = functools.partial(_dispatch_kernel, ts=ts, n_exp=n_exp, n_blocks=p_rows // EXPERT_BLOCK)
    return pl.pallas_call(
        kern,
        grid=(nt,),
        in_specs=[
            pl.BlockSpec((1, 1, meta.shape[2]), lambda i: (i, 0, 0), memory_space=pltpu.SMEM),
            pl.BlockSpec((1, 1, seg.shape[2]), lambda i: (0, 0, 0), memory_space=pltpu.SMEM),
            pl.BlockSpec((ts, d), lambda i: (i, 0)),
            pl.BlockSpec((n_exp, ts), lambda i: (0, i)),
        ],
        out_specs=pl.BlockSpec(memory_space=pl.ANY),
        out_shape=jax.ShapeDtypeStruct((p_rows, d), BF16),
        scratch_shapes=[pltpu.VMEM((n_exp * SLOT_ROWS, d), BF16),
                        pltpu.VMEM((ts * TOP_K, d), BF16),
                        pltpu.SemaphoreType.DMA(()), pltpu.SemaphoreType.DMA(())],
        compiler_params=pltpu.CompilerParams(dimension_semantics=("arbitrary",), has_side_effects=True),
        name="dispatch",
    )(meta, seg, h2, rank)


def _expert_kernel(be_ref, nu_ref, xs_ref, wg_ref, wu_ref, wd_ref, ys_ref):
    del be_ref
    i = pl.program_id(0)

    @pl.when(i < nu_ref[0])
    def _():
        xb = xs_ref[...]
        a = _silu(_dot(xb, wg_ref[0])) * _dot(xb, wu_ref[0])
        ys_ref[...] = _dot(a.astype(BF16), wd_ref[0]).astype(BF16)

    @pl.when(i >= nu_ref[0])
    def _():
        ys_ref[...] = jnp.zeros_like(ys_ref)


def _experts(xs, block_e, n_used, w_gate, w_up, w_down, blk):
    p_rows, dw = xs.shape
    n_exp, d, ff = w_gate.shape
    nb = p_rows // blk
    used = lambda i, nu: jnp.minimum(i, nu[0] - 1)
    grid_spec = pltpu.PrefetchScalarGridSpec(
        num_scalar_prefetch=2,
        grid=(nb,),
        in_specs=[
            pl.BlockSpec((blk, dw), lambda i, be, nu: (used(i, nu), 0)),
            pl.BlockSpec((1, d, ff), lambda i, be, nu: (be[used(i, nu)], 0, 0)),
            pl.BlockSpec((1, d, ff), lambda i, be, nu: (be[used(i, nu)], 0, 0)),
            pl.BlockSpec((1, ff, d), lambda i, be, nu: (be[used(i, nu)], 0, 0)),
        ],
        out_specs=pl.BlockSpec((blk, dw), lambda i, be, nu: (i, 0)),
    )
    return pl.pallas_call(
        _expert_kernel,
        grid_spec=grid_spec,
        out_shape=jax.ShapeDtypeStruct((p_rows, dw), BF16),
        compiler_params=pltpu.CompilerParams(dimension_semantics=("arbitrary",)),
        name="experts",
    )(block_e, n_used, xs, w_gate, w_up, w_down)


def _combine_kernel(meta_ref, ys_hbm, rkt_ref, wdt_ref, rk_ref, wd_ref, ex_ref, rp_ref, x1_ref, h2_ref,
                    mod_ref, ng_ref, sg_ref, su_ref, sd_ref, o_ref, stage, spill, acc, sems, sem_ov, *,
                    ts, n_exp):
    n_spill = meta_ref[0, 0, 4 * n_exp + 1]
    n_groups = n_exp // SLOT_GROUP
    groups = [_slot_copies(meta_ref, stage, ys_hbm, sems.at[g],
                           range(g * SLOT_GROUP, (g + 1) * SLOT_GROUP), False) for g in range(n_groups)]
    for copies in groups:
        for cp in copies:
            cp.start()

    def clear_chunk(ci, carry):
        base = pl.multiple_of(ci * SPILL_CHUNK, SPILL_CHUNK)
        spill[pl.ds(base, SPILL_CHUNK), :] = jnp.zeros((SPILL_CHUNK, spill.shape[1]), spill.dtype)
        return carry

    lax.fori_loop(0, n_spill, clear_chunk, 0)
    _for_overflow_pieces(meta_ref, n_exp,
                         lambda e, k: _overflow_copy(meta_ref, spill, ys_hbm, sem_ov, n_exp, e, k, False).start())

    xb = h2_ref[...]
    a = _silu(_dot(xb, sg_ref[...])) * _dot(xb, su_ref[...])
    tot = _dot(a.astype(BF16), sd_ref[...])

    rank_lanes = _dot(rkt_ref[...].astype(BF16), ex_ref[...])
    weight_lanes = _dot(wdt_ref[...].astype(BF16), ex_ref[...])

    group_rows = SLOT_GROUP * SLOT_ROWS
    for g in range(n_groups):
        cols = slice(g * group_rows, (g + 1) * group_rows)
        unmix = jnp.where(rank_lanes[:, cols] == rp_ref[:, cols], weight_lanes[:, cols], 0.0).astype(BF16)
        for cp in groups[g]:
            cp.wait()
        tot = tot + _dot(unmix, stage[g * group_rows:(g + 1) * group_rows, :])
    acc[...] = tot

    _for_overflow_pieces(meta_ref, n_exp,
                         lambda e, k: _overflow_copy(meta_ref, spill, ys_hbm, sem_ov, n_exp, e, k, False).wait())

    def spill_chunk(ci, carry):
        base = pl.multiple_of(ci * SPILL_CHUNK, SPILL_CHUNK)
        acc[...] += _dot_tn(_spill_matrix_rows(meta_ref, rk_ref, wd_ref, base, ts, n_exp),
                            spill[pl.ds(base, SPILL_CHUNK), :])
        return carry

    lax.fori_loop(0, n_spill, spill_chunk, 0)
    g2 = mod_ref[0, 5:6, :]
    o_ref[...] = x1_ref[...] + g2 * _rms(acc[...], ng_ref[3:4, :])


def _combine(ys, meta, rank_tm, wd_tm, rank, wd, x1_flat, h2, mod3, norm_g, ws_gate, ws_up, ws_down,
             n_seq, ts, n_exp):
    t_all, d = x1_flat.shape
    nt = t_all // ts
    per_b = n_seq // ts
    ff = ws_gate.shape[1]
    lanes = n_exp * SLOT_ROWS
    lane = lax.broadcasted_iota(I32, (n_exp, lanes), 1)
    expand = jnp.where(lane // SLOT_ROWS == lax.broadcasted_iota(I32, (n_exp, lanes), 0), 1.0, 0.0).astype(BF16)
    slot_rank = (jnp.arange(lanes, dtype=I32) % SLOT_ROWS).astype(F32).reshape(1, lanes)
    c2 = lambda i: (0, 0)
    kern = functools.partial(_combine_kernel, ts=ts, n_exp=n_exp)
    return pl.pallas_call(
        kern,
        grid=(nt,),
        in_specs=[
            pl.BlockSpec((1, 1, meta.shape[2]), lambda i: (i, 0, 0), memory_space=pltpu.SMEM),
            pl.BlockSpec(memory_space=pl.ANY),
            pl.BlockSpec((ts, n_exp), lambda i: (i, 0)),
            pl.BlockSpec((ts, n_exp), lambda i: (i, 0)),
            pl.BlockSpec((n_exp, ts), lambda i: (0, i)),
            pl.BlockSpec((n_exp, ts), lambda i: (0, i)),
            pl.BlockSpec((n_exp, lanes), c2),
            pl.BlockSpec((1, lanes), c2),
            pl.BlockSpec((ts, d), lambda i: (i, 0)),
            pl.BlockSpec((ts, d), lambda i: (i, 0)),
            pl.BlockSpec((1, N_MOD, d), lambda i: (i // per_b, 0, 0)),
            pl.BlockSpec((4, d), c2),
            pl.BlockSpec((d, ff), c2),
            pl.BlockSpec((d, ff), c2),
            pl.BlockSpec((ff, d), c2),
        ],
        out_specs=pl.BlockSpec((ts, d), lambda i: (i, 0)),
        out_shape=jax.ShapeDtypeStruct((t_all, d), F32),
        scratch_shapes=[pltpu.VMEM((lanes, d), BF16), pltpu.VMEM((ts * TOP_K, d), BF16),
                        pltpu.VMEM((ts, d), F32),
                        pltpu.SemaphoreType.DMA((n_exp // SLOT_GROUP,)), pltpu.SemaphoreType.DMA(())],
        compiler_params=pltpu.CompilerParams(dimension_semantics=("arbitrary",)),
        name="combine",
    )(meta, ys, rank_tm, wd_tm, rank, wd, expand, slot_rank, x1_flat, h2, mod3, norm_g,
      ws_gate, ws_up, ws_down)


def _rope_tables(n):
    rows = jnp.repeat(jnp.arange(n // GRID_W, dtype=F32), GRID_W)
    cols = jnp.tile(jnp.arange(GRID_W, dtype=F32), n // GRID_W)
    quarter = LANES // 4
    freqs = ROPE_BASE ** (-jnp.arange(quarter, dtype=F32) / quarter)
    ang = jnp.concatenate([rows[:, None] * freqs, cols[:, None] * freqs], axis=-1)
    cos, sin = jnp.cos(ang), jnp.sin(ang)
    return jnp.concatenate([cos, cos], axis=-1), jnp.concatenate([-sin, sin], axis=-1)


def _retention_tables(log_decay):
    lg = -jnp.exp(log_decay.astype(F32))
    idx = jnp.arange(CHUNK, dtype=F32)
    rel = idx[:, None] - idx[None, :]
    lg3 = lg[:, :, None, None]
    intra_f = jnp.where(rel >= 0, jnp.exp(jnp.maximum(rel, 0.0) * lg3[0]), 0.0)
    intra_b = jnp.where(rel <= 0, jnp.exp(jnp.maximum(-rel, 0.0) * lg3[1]), 0.0)
    kd_f = jnp.exp((CHUNK - 1 - idx)[None, :] * lg[0][:, None])
    kd_b = jnp.exp(idx[None, :] * lg[1][:, None])
    qd_f = jnp.exp((idx + 1)[None, :] * lg[0][:, None])
    qd_b = jnp.exp((CHUNK - idx)[None, :] * lg[1][:, None])
    bc = lambda t: jnp.broadcast_to(t[:, :, None], (HEADS, CHUNK, LANES))
    intra = jnp.concatenate([intra_f, intra_b], axis=0)
    kd = jnp.concatenate([bc(kd_f), bc(kd_b)], axis=0)
    qd = jnp.concatenate([bc(qd_f), bc(qd_b)], axis=0)
    cd = jnp.broadcast_to(jnp.exp(CHUNK * lg).reshape(2 * HEADS, 1), (2 * HEADS, LANES))
    return intra, kd, qd, cd


def kernel(x, c, ctx, c_ctx, w_mod, b_mod, norm_g, w_in, ret_log_decay, ret_norm_g, mlstm_conv_w,
           mlstm_conv_b, mlstm_gate_b, mlstm_norm_g, w_out, w_router, router_bias, w_gate, w_up, w_down,
           ws_gate, ws_up, ws_down):
    b, n, d = x.shape
    n_ctx = ctx.shape[1]
    depth = w_mod.shape[0]
    assert depth == 1, "only the single-layer configuration is implemented"
    assert d // 2 // HEADS == LANES
    n_exp = w_router.shape[2]
    t_all = b * n
    r_w = d // 2
    main_cols = 8 * r_w
    l = 0

    pad = (-(b + 1)) % 8
    cc = jnp.concatenate([c, c_ctx[None, :], jnp.zeros((pad, d), F32)], axis=0)
    mod3 = _modulation(cc, w_mod[l], b_mod[l]).reshape(b + 1 + pad, N_MOD, d)

    w_main = w_in[l, :, :main_cols].astype(BF16)
    wg = w_in[l, :, main_cols:].astype(BF16)
    wgt = wg.T
    gb = mlstm_gate_b[l].reshape(-1).astype(F32)
    tabs = _retention_tables(ret_log_decay[l])
    head_g = jnp.concatenate([ret_norm_g[l], mlstm_norm_g[l]]).reshape(1, d).astype(F32)
    wr = w_router[l].T.astype(F32)
    wr_hi = wr.astype(BF16)
    wr_lo = (wr - wr_hi.astype(F32)).astype(BF16)

    def inproj(seq, mod_row, ts):
        cos2, sin2 = _rope_tables(n) if mod_row is None else (
            jnp.ones((seq.shape[1], LANES), F32), jnp.zeros((seq.shape[1], LANES), F32))
        return _inproj(seq, mod3, mod_row, norm_g[l, 0:1], w_main, wgt, wg, mlstm_conv_w[l],
                       mlstm_conv_b[l].reshape(1, -1), gb.reshape(16, 1), gb.reshape(1, 16), cos2, sin2, ts)

    nst = 2 * HEADS
    zero_states = (jnp.zeros((b, nst, LANES, LANES), F32), jnp.zeros((b, nst, LANES, 2 * LANES), F32),
                   jnp.zeros((b, nst, LANES), F32))
    p_c, gr_c, gc_c = inproj(ctx, b, min(n_ctx, 512))
    ctx_states = _scan(p_c, gr_c, gc_c, tabs, zero_states, with_output=False)

    ts = min(n, 512)
    p_l, gr_l, gc_l = inproj(x, None, ts)
    o_f, o_b, _, _, _ = _scan(p_l, gr_l, gc_l, tabs, tuple(ctx_states), with_output=True)
    ts_moe = MOE_TILE
    x1, h2, rank, wdense, tile_cnt = _post(
        x, o_f, o_b, p_l, mod3, norm_g[l], head_g, w_out[l].astype(BF16), wr_hi, wr_lo,
        router_bias[l].reshape(n_exp, 1).astype(F32), ts_moe)

    nt = t_all // ts_moe
    cnt = tile_cnt[:, 0, :].astype(I32)
    run_rows = (cnt + ROW_ALIGN - 1) // ROW_ALIGN * ROW_ALIGN
    seg_cap = (jnp.sum(run_rows, axis=0) + SLOT_ROWS + EXPERT_BLOCK - 1) // EXPERT_BLOCK * EXPERT_BLOCK
    seg_end = jnp.cumsum(seg_cap)
    run_start = (seg_end - seg_cap)[None, :] + jnp.cumsum(run_rows, axis=0) - run_rows
    ov_rows = jnp.maximum(run_rows - SLOT_ROWS, 0)
    ov_off = jnp.cumsum(ov_rows, axis=1) - ov_rows
    n_spill = (jnp.sum(ov_rows, axis=1, keepdims=True) + SPILL_CHUNK - 1) // SPILL_CHUNK
    spills = ov_rows > 0
    n_ov = jnp.sum(spills.astype(I32), axis=1, keepdims=True)
    nth = jnp.cumsum(spills.astype(I32), axis=1) - 1
    is_jth = spills[:, None, :] & (nth[:, None, :] == jnp.arange(n_exp, dtype=I32)[None, :, None])
    compact = lambda v: jnp.sum(jnp.where(is_jth, v[:, None, :], 0), axis=2)
    ov_e = compact(jnp.broadcast_to(jnp.arange(n_exp, dtype=I32)[None, :], cnt.shape))
    meta = jnp.concatenate([run_start, ov_e, compact(ov_rows // ROW_ALIGN), compact(ov_off), n_ov, n_spill],
                           axis=1).astype(I32)
    meta = jnp.pad(meta, ((0, 0), (0, (-meta.shape[1]) % LANES))).reshape(nt, 1, -1)
    p_rows = -(-(t_all * TOP_K + nt * n_exp * (ROW_ALIGN - 1) + n_exp * (SLOT_ROWS + EXPERT_BLOCK - 1))
               // EXPERT_BLOCK) * EXPERT_BLOCK
    nb = p_rows // EXPERT_BLOCK
    blk_first = jnp.arange(nb, dtype=I32) * EXPERT_BLOCK
    block_e = jnp.minimum(jnp.sum((seg_end[None, :] <= blk_first[:, None]).astype(I32), axis=1), n_exp - 1)
    n_used = (seg_end[-1:] // EXPERT_BLOCK).astype(I32)
    seg = jnp.concatenate([seg_end.astype(I32), n_used])
    seg = jnp.pad(seg, (0, (-seg.shape[0]) % LANES)).reshape(1, 1, -1)

    xs = _dispatch(h2, rank, meta, seg, p_rows, ts_moe, n_exp)
    ys = _experts(xs, block_e, n_used, w_gate[l].astype(BF16), w_up[l].astype(BF16), w_down[l].astype(BF16),
                  EXPERT_BLOCK)
    out = _combine(ys, meta, rank.T, wdense.T, rank, wdense, x1.reshape(t_all, d), h2, mod3, norm_g[l],
                   ws_gate[l].astype(BF16), ws_up[l].astype(BF16), ws_down[l].astype(BF16), n, ts_moe, n_exp)
    return out.reshape(b, n, d)
```

```python
import functools

import jax
import jax.numpy as jnp
from jax import lax
from jax.experimental import pallas as pl
from jax.experimental.pallas import tpu as pltpu

F32 = jnp.float32
BF16 = jnp.bfloat16
I32 = jnp.int32

EPS = 1e-6
LANES = 128
CHUNK = 128
HEADS = 4
GRID_W = 64
ROPE_BASE = 10000.0
N_GROUPS = 8
TOPK_GROUPS = 4
TOP_K = 8
ROUTED_SCALE = 2.5
N_MOD = 6
MOE_TILE = 256
SPILL_CHUNK = 64
ROW_ALIGN = 16
SLOT_ROWS = 48
SLOT_GROUP = 8
EXPERT_BLOCK = 512
NEG_INF = float("-inf")


def _sigmoid(v):
    return 1.0 / (1.0 + jnp.exp(-v))


def _silu(v):
    return v * _sigmoid(v)


def _log_sigmoid(v):
    return jnp.minimum(v, 0.0) - jnp.log(1.0 + jnp.exp(-jnp.abs(v)))


def _dot(a, b):
    return jnp.dot(a, b, preferred_element_type=F32)


def _dot_nt(a, b):
    return lax.dot_general(a, b, (((1,), (1,)), ((), ())), preferred_element_type=F32)


def _dot_tn(a, b):
    return lax.dot_general(a, b, (((0,), (0,)), ((), ())), preferred_element_type=F32)


def _split3(a):
    hi = a.astype(BF16)
    r = a - hi.astype(F32)
    mid = r.astype(BF16)
    lo = (r - mid.astype(F32)).astype(BF16)
    return hi, mid, lo


def _rms(v, g):
    ms = jnp.mean(v * v, axis=-1, keepdims=True)
    return v * lax.rsqrt(ms + EPS) * g


def _mod_kernel(c_ref, w_ref, b_ref, o_ref):
    a = _silu(c_ref[...])
    o_ref[...] = jnp.dot(a, w_ref[...], preferred_element_type=F32,
                         precision=lax.Precision.HIGHEST) + b_ref[...]


def _modulation(cc, w_mod, b_mod):
    rows, d = cc.shape
    cols = w_mod.shape[1]
    tn = d
    return pl.pallas_call(
        _mod_kernel,
        grid=(cols // tn,),
        in_specs=[pl.BlockSpec((rows, d), lambda j: (0, 0)),
                  pl.BlockSpec((d, tn), lambda j: (0, j)),
                  pl.BlockSpec((1, tn), lambda j: (0, j))],
        out_specs=pl.BlockSpec((rows, tn), lambda j: (0, j)),
        out_shape=jax.ShapeDtypeStruct((rows, cols), F32),
        name="mod",
    )(cc, w_mod, b_mod.reshape(1, cols))


def _inproj_kernel(x_ref, xp_ref, xn_ref, mod_ref, g_ref, w_ref, wgt_ref, wg_ref, cw_ref, cb_ref,
                   gbr_ref, gbc_ref, cos_ref, sin_ref, p_ref, gr_ref, gc_ref, *, ts, d):
    i = pl.program_id(1)
    last = pl.num_programs(1) - 1
    r_w = d // 2
    shift = mod_ref[0, 0:1, :]
    scale = mod_ref[0, 1:2, :]
    g = g_ref[...]

    def normmod(v):
        return _rms(v, g) * (1.0 + scale) + shift

    hb = normmod(x_ref[0]).astype(BF16)
    halo = jnp.concatenate([xp_ref[0], xn_ref[0]], axis=0)
    ph = _dot(normmod(halo).astype(BF16), w_ref[:, 3 * r_w:5 * r_w])
    prev_row = jnp.where(i == 0, 0.0, ph[7:8, :])
    next_row = jnp.where(i == last, 0.0, ph[8:9, :])

    cos2 = cos_ref[...]
    sin2 = sin_ref[...]
    rows = lax.broadcasted_iota(I32, (ts, r_w), 0)
    qscale = LANES ** -0.5

    for j in range(8):
        acc = _dot(hb, w_ref[:, j * r_w:(j + 1) * r_w])
        if j in (0, 1):
            if j == 0:
                acc = acc * qscale
            parts = []
            for h in range(HEADS):
                t = acc[:, h * LANES:(h + 1) * LANES]
                parts.append(t * cos2 + pltpu.roll(t, LANES // 2, axis=1) * sin2)
            acc = jnp.concatenate(parts, axis=1)
        elif j in (3, 4):
            c0 = (j - 3) * r_w
            pr = prev_row[:, c0:c0 + r_w]
            nx = next_row[:, c0:c0 + r_w]
            down = jnp.where(rows == 0, pr, pltpu.roll(acc, 1, axis=0))
            up = jnp.where(rows == ts - 1, nx, pltpu.roll(acc, ts - 1, axis=0))
            cw = cw_ref[:, c0:c0 + r_w]
            acc = down * cw[0:1, :] + acc * cw[1:2, :] + up * cw[2:3, :] + cb_ref[:, c0:c0 + r_w]
            acc = _silu(acc)
            if j == 4:
                acc = acc * qscale
        p_ref[0, :, j * r_w:(j + 1) * r_w] = acc.astype(BF16)

    gr = _dot_nt(wgt_ref[...], hb) + gbr_ref[...]
    ch_r = lax.broadcasted_iota(I32, gr.shape, 0)
    gr_ref[0] = jnp.where((ch_r // HEADS) % 2 == 1, _log_sigmoid(gr), gr)
    gc = _dot(hb, wg_ref[...]) + gbc_ref[...]
    ch_c = lax.broadcasted_iota(I32, gc.shape, 1)
    gc_ref[0] = jnp.where((ch_c // HEADS) % 2 == 1, _log_sigmoid(gc), gc)


def _inproj(x, mod3, mod_row, g, w_main, wgt, wg, conv_w, conv_b, gb_col, gb_row, cos2, sin2, ts):
    b, n, d = x.shape
    nt = n // ts
    nb8 = n // 8
    hb = ts // 8
    cols = w_main.shape[1]
    if mod_row is None:
        mod_map = lambda bi, i: (bi, 0, 0)
    else:
        mod_map = lambda bi, i: (mod_row, 0, 0)
    const2 = lambda bi, i: (0, 0)
    kern = functools.partial(_inproj_kernel, ts=ts, d=d)
    return pl.pallas_call(
        kern,
        grid=(b, nt),
        in_specs=[
            pl.BlockSpec((1, ts, d), lambda bi, i: (bi, i, 0)),
            pl.BlockSpec((1, 8, d), lambda bi, i: (bi, jnp.maximum(i * hb - 1, 0), 0)),
            pl.BlockSpec((1, 8, d), lambda bi, i: (bi, jnp.minimum((i + 1) * hb, nb8 - 1), 0)),
            pl.BlockSpec((1, N_MOD, d), mod_map),
            pl.BlockSpec((1, d), const2),
            pl.BlockSpec((d, cols), const2),
            pl.BlockSpec((16, d), const2),
            pl.BlockSpec((d, 16), const2),
            pl.BlockSpec((3, d), const2),
            pl.BlockSpec((1, d), const2),
            pl.BlockSpec((16, 1), const2),
            pl.BlockSpec((1, 16), const2),
            pl.BlockSpec((ts, LANES), lambda bi, i: (i, 0)),
            pl.BlockSpec((ts, LANES), lambda bi, i: (i, 0)),
        ],
        out_specs=[
            pl.BlockSpec((1, ts, cols), lambda bi, i: (bi, i, 0)),
            pl.BlockSpec((1, 16, ts), lambda bi, i: (bi, 0, i)),
            pl.BlockSpec((1, ts, 16), lambda bi, i: (bi, i, 0)),
        ],
        out_shape=[
            jax.ShapeDtypeStruct((b, n, cols), BF16),
            jax.ShapeDtypeStruct((b, 16, n), F32),
            jax.ShapeDtypeStruct((b, n, 16), F32),
        ],
        compiler_params=pltpu.CompilerParams(dimension_semantics=("parallel", "parallel")),
        name="inproj",
    )(x, x, x, mod3, g, w_main, wgt, wg, conv_w, conv_b, gb_col, gb_row, cos2, sin2)


def _scan_kernel(pf_ref, pb_ref, grf_ref, grb_ref, gcf_ref, gcb_ref, intra_ref, kd_ref, qd_ref, cd_ref,
                 rs0_ref, mc0_ref, mm0_ref, *out_refs, with_output, r_w):
    if with_output:
        of_ref, ob_ref, rs_ref, mc_ref, mm_ref = out_refs
    else:
        rs_ref, mc_ref, mm_ref = out_refs
    j = pl.program_id(1)

    @pl.when(j == 0)
    def _():
        rs_ref[...] = rs0_ref[...]
        mc_ref[...] = mc0_ref[...]
        mm_ref[...] = mm0_ref[...]

    c = CHUNK
    row = lax.broadcasted_iota(I32, (c, c), 0)
    col = lax.broadcasted_iota(I32, (c, c), 1)
    tri_le = (row <= col)
    tri_ge = (row >= col)
    ones_ext = jnp.where(lax.broadcasted_iota(I32, (c, LANES), 1) == 0, 1.0, 0.0).astype(BF16)

    def cumsums(gr, gc, fwd):
        m_row = jnp.where(tri_le if fwd else tri_ge, 1.0, 0.0).astype(BF16)
        m_col = jnp.where(tri_ge if fwd else tri_le, 1.0, 0.0).astype(BF16)
        b_row = sum(_dot(piece, m_row) for piece in _split3(gr))
        b_col = sum(_dot(m_col, piece) for piece in _split3(gc))
        return b_row, b_col

    for dr in range(2):
        fwd = dr == 0
        p_ref = pf_ref if fwd else pb_ref
        o_ref = None
        if with_output:
            o_ref = of_ref if fwd else ob_ref
        gr = (grf_ref if fwd else grb_ref)[0]
        gc = (gcf_ref if fwd else gcb_ref)[0]
        cs_row, cs_col = cumsums(gr, gc, fwd)
        causal = tri_ge if fwd else tri_le

        for h in range(HEADS):
            hs = slice(h * LANES, (h + 1) * LANES)
            st = dr * HEADS + h
            k = p_ref[0, :, r_w + h * LANES:r_w + (h + 1) * LANES]
            v = p_ref[0, :, 2 * r_w + h * LANES:2 * r_w + (h + 1) * LANES]
            s_prev = rs_ref[0, st]
            ks = (k.astype(F32) * kd_ref[st]).astype(BF16)
            upd = _dot_tn(ks, v)
            if with_output:
                q = p_ref[0, :, hs]
                sc = _dot_nt(q, k) * intra_ref[st]
                out = _dot(sc.astype(BF16), v) + qd_ref[st] * _dot(q, s_prev.astype(BF16))
                o_ref[0, :, hs] = out.astype(BF16)
            rs_ref[0, st] = s_prev * cd_ref[st:st + 1, :] + upd

            o0 = 3 * r_w
            mk = p_ref[0, :, o0 + r_w + h * LANES:o0 + r_w + (h + 1) * LANES]
            mv = p_ref[0, :, o0 + 2 * r_w + h * LANES:o0 + 2 * r_w + (h + 1) * LANES]
            ci = dr * 2 * HEADS + h
            cf = ci + HEADS
            ib_row = gr[ci:ci + 1, :]
            b_row = cs_row[cf:cf + 1, :]
            ib_col = gc[:, ci:ci + 1]
            b_col = cs_col[:, cf:cf + 1]
            b_tot = b_row[:, c - 1:c] if fwd else b_row[:, 0:1]
            m_prev = mm_ref[0, st:st + 1, 0:1]
            log_ws_row = b_tot - b_row + ib_row
            m_next = jnp.maximum(b_tot + m_prev, jnp.max(log_ws_row, axis=1, keepdims=True))
            decay_prev = jnp.exp(b_tot + m_prev - m_next)
            ws_col = jnp.exp(b_tot - b_col + ib_col - m_next)
            kw = (mk.astype(F32) * ws_col).astype(BF16)
            v_ext = jnp.concatenate([mv, ones_ext], axis=1)
            upd_c = _dot_tn(kw, v_ext)
            c_prev = mc_ref[0, st]
            if with_output:
                mq = p_ref[0, :, o0 + h * LANES:o0 + (h + 1) * LANES]
                dm = jnp.where(causal, b_col - b_row + ib_row, NEG_INF)
                log_inter = b_col + m_prev
                m_i = jnp.maximum(log_inter, jnp.max(dm, axis=1, keepdims=True))
                w = jnp.exp(dm - m_i)
                inter = jnp.exp(log_inter - m_i)
                sc = _dot_nt(mq, mk) * w
                hx = _dot(sc.astype(BF16), v_ext) + inter * _dot(mq, c_prev.astype(BF16))
                den = hx[:, LANES:LANES + 1]
                hout = hx[:, :LANES] / jnp.maximum(jnp.abs(den), jnp.exp(-m_i))
                o_ref[0, :, r_w + h * LANES:r_w + (h + 1) * LANES] = hout.astype(BF16)
            mc_ref[0, st] = decay_prev * c_prev + upd_c
            mm_ref[0, st:st + 1, :] = jnp.broadcast_to(m_next, (1, LANES))


def _scan(p, g_row, g_col, tabs, states, with_output):
    b, n, cols = p.shape
    nch = n // CHUNK
    r_w = cols // 8
    intra, kd, qd, cd = tabs
    rs0, mc0, mm0 = states
    nst = 2 * HEADS
    fwd3 = lambda bi, j: (bi, j, 0)
    bwd3 = lambda bi, j: (bi, nch - 1 - j, 0)
    c3 = lambda bi, j: (0, 0, 0)
    st4 = lambda bi, j: (bi, 0, 0, 0)
    in_specs = [
        pl.BlockSpec((1, CHUNK, 6 * r_w), fwd3),
        pl.BlockSpec((1, CHUNK, 6 * r_w), bwd3),
        pl.BlockSpec((1, 16, CHUNK), lambda bi, j: (bi, 0, j)),
        pl.BlockSpec((1, 16, CHUNK), lambda bi, j: (bi, 0, nch - 1 - j)),
        pl.BlockSpec((1, CHUNK, 16), fwd3),
        pl.BlockSpec((1, CHUNK, 16), bwd3),
        pl.BlockSpec((nst, CHUNK, LANES), c3),
        pl.BlockSpec((nst, CHUNK, LANES), c3),
        pl.BlockSpec((nst, CHUNK, LANES), c3),
        pl.BlockSpec((nst, LANES), lambda bi, j: (0, 0)),
        pl.BlockSpec((1, nst, LANES, LANES), st4),
        pl.BlockSpec((1, nst, LANES, 2 * LANES), st4),
        pl.BlockSpec((1, nst, LANES), lambda bi, j: (bi, 0, 0)),
    ]
    st_specs = [
        pl.BlockSpec((1, nst, LANES, LANES), st4),
        pl.BlockSpec((1, nst, LANES, 2 * LANES), st4),
        pl.BlockSpec((1, nst, LANES), lambda bi, j: (bi, 0, 0)),
    ]
    st_shapes = [
        jax.ShapeDtypeStruct((b, nst, LANES, LANES), F32),
        jax.ShapeDtypeStruct((b, nst, LANES, 2 * LANES), F32),
        jax.ShapeDtypeStruct((b, nst, LANES), F32),
    ]
    if with_output:
        out_specs = [pl.BlockSpec((1, CHUNK, 2 * r_w), fwd3), pl.BlockSpec((1, CHUNK, 2 * r_w), bwd3)] + st_specs
        out_shape = [jax.ShapeDtypeStruct((b, n, 2 * r_w), BF16)] * 2 + st_shapes
    else:
        out_specs, out_shape = st_specs, st_shapes
    kern = functools.partial(_scan_kernel, with_output=with_output, r_w=r_w)
    return pl.pallas_call(
        kern,
        grid=(b, nch),
        in_specs=in_specs,
        out_specs=out_specs,
        out_shape=out_shape,
        compiler_params=pltpu.CompilerParams(dimension_semantics=("parallel", "arbitrary")),
        name="scan_out" if with_output else "scan_state",
    )(p, p, g_row, g_row, g_col, g_col, intra, kd, qd, cd, rs0, mc0, mm0)


def _post_kernel(x_ref, of_ref, ob_ref, rg_ref, mo_ref, mod_ref, ng_ref, hg_ref, wo_ref, wrh_ref, wrl_ref,
                 rb_ref, su_ref, x1_ref, h2_ref, rk_ref, wd_ref, cnt_ref, *, ts, d, n_exp):
    s = of_ref[0].astype(F32) + ob_ref[0].astype(F32)
    parts = []
    for gi in range(2 * HEADS):
        sl = s[:, gi * LANES:(gi + 1) * LANES]
        mu = jnp.mean(sl, axis=-1, keepdims=True)
        dv = sl - mu
        var = jnp.mean(dv * dv, axis=-1, keepdims=True)
        y = dv * lax.rsqrt(var + EPS) * hg_ref[:, gi * LANES:(gi + 1) * LANES]
        if gi < HEADS:
            gate = _silu(rg_ref[0, :, gi * LANES:(gi + 1) * LANES].astype(F32))
        else:
            gate = _sigmoid(mo_ref[0, :, (gi - HEADS) * LANES:(gi - HEADS + 1) * LANES].astype(F32))
        parts.append((y * gate).astype(BF16))
    mixed = jnp.concatenate(parts, axis=1)
    y = _dot(mixed, wo_ref[...])
    g1 = mod_ref[0, 2:3, :]
    sh2 = mod_ref[0, 3:4, :]
    sc2 = mod_ref[0, 4:5, :]
    x1 = x_ref[0] + g1 * _rms(y, ng_ref[1:2, :])
    x1_ref[0] = x1
    h2 = _rms(x1, ng_ref[2:3, :]) * (1.0 + sc2) + sh2
    h_hi = h2.astype(BF16)
    h2_ref[...] = h_hi

    h_lo = (h2 - h_hi.astype(F32)).astype(BF16)
    logits = _dot_nt(wrh_ref[...], h_hi) + _dot_nt(wrh_ref[...], h_lo) + _dot_nt(wrl_ref[...], h_hi)
    scores = _sigmoid(logits)
    sel = scores + rb_ref[...]
    gsz = n_exp // N_GROUPS
    iota_g = lax.broadcasted_iota(I32, (gsz, ts), 0).astype(F32)
    grp = []
    for gi in range(N_GROUPS):
        blk = sel[gi * gsz:(gi + 1) * gsz, :]
        m1 = jnp.max(blk, axis=0, keepdims=True)
        i1 = jnp.min(jnp.where(blk == m1, iota_g, float(gsz)), axis=0, keepdims=True)
        m2 = jnp.max(jnp.where(iota_g == i1, NEG_INF, blk), axis=0, keepdims=True)
        grp.append(m1 + m2)
    masked_parts = []
    for gi in range(N_GROUPS):
        rank = jnp.zeros((1, ts), F32)
        for gj in range(N_GROUPS):
            if gj == gi:
                continue
            beats = (grp[gj] >= grp[gi]) if gj < gi else (grp[gj] > grp[gi])
            rank = rank + jnp.where(beats, 1.0, 0.0)
        keep = rank < float(TOPK_GROUPS)
        masked_parts.append(jnp.where(keep, sel[gi * gsz:(gi + 1) * gsz, :], NEG_INF))
    masked = jnp.concatenate(masked_parts, axis=0)

    iota_e = lax.broadcasted_iota(I32, (n_exp, ts), 0).astype(F32)
    selmask = jnp.zeros((n_exp, ts), F32)
    for _ in range(TOP_K):
        mx = jnp.max(masked, axis=0, keepdims=True)
        ei = jnp.min(jnp.where(masked == mx, iota_e, float(n_exp)), axis=0, keepdims=True)
        hit = iota_e == ei
        selmask = jnp.where(hit, 1.0, selmask)
        masked = jnp.where(hit, NEG_INF, masked)
    picked = selmask > 0.0
    wsel = jnp.where(picked, scores, 0.0)
    wd_ref[...] = wsel / jnp.sum(wsel, axis=0, keepdims=True) * ROUTED_SCALE
    rank = _dot(selmask.astype(BF16), su_ref[...])
    rk_ref[...] = jnp.where(picked, rank, -1.0)
    cnt_ref[0] = _dot_nt(jnp.ones((8, ts), BF16), selmask.astype(BF16))


def _post(x, o_f, o_b, p, mod3, norm_g, head_g, w_out, wr_hi, wr_lo, rbias, ts):
    b, n, d = x.shape
    nt = n // ts
    t_all = b * n
    n_exp = wr_hi.shape[0]
    r_w = d // 2
    su = jnp.where(lax.broadcasted_iota(I32, (ts, ts), 0) < lax.broadcasted_iota(I32, (ts, ts), 1),
                   1.0, 0.0).astype(BF16)
    tok3 = lambda bi, i: (bi, i, 0)
    c2 = lambda bi, i: (0, 0)
    flat = lambda bi, i: (0, bi * nt + i)
    kern = functools.partial(_post_kernel, ts=ts, d=d, n_exp=n_exp)
    return pl.pallas_call(
        kern,
        grid=(b, nt),
        in_specs=[
            pl.BlockSpec((1, ts, d), tok3),
            pl.BlockSpec((1, ts, d), tok3),
            pl.BlockSpec((1, ts, d), tok3),
            pl.BlockSpec((1, ts, r_w), lambda bi, i: (bi, i, 6)),
            pl.BlockSpec((1, ts, r_w), lambda bi, i: (bi, i, 7)),
            pl.BlockSpec((1, N_MOD, d), lambda bi, i: (bi, 0, 0)),
            pl.BlockSpec((4, d), c2),
            pl.BlockSpec((1, d), c2),
            pl.BlockSpec((d, d), c2),
            pl.BlockSpec((n_exp, d), c2),
            pl.BlockSpec((n_exp, d), c2),
            pl.BlockSpec((n_exp, 1), c2),
            pl.BlockSpec((ts, ts), c2),
        ],
        out_specs=[
            pl.BlockSpec((1, ts, d), tok3),
            pl.BlockSpec((ts, d), lambda bi, i: (bi * nt + i, 0)),
            pl.BlockSpec((n_exp, ts), flat),
            pl.BlockSpec((n_exp, ts), flat),
            pl.BlockSpec((1, 8, n_exp), lambda bi, i: (bi * nt + i, 0, 0)),
        ],
        out_shape=[
            jax.ShapeDtypeStruct((b, n, d), F32),
            jax.ShapeDtypeStruct((t_all, d), BF16),
            jax.ShapeDtypeStruct((n_exp, t_all), F32),
            jax.ShapeDtypeStruct((n_exp, t_all), F32),
            jax.ShapeDtypeStruct((b * nt, 8, n_exp), F32),
        ],
        compiler_params=pltpu.CompilerParams(dimension_semantics=("parallel", "parallel")),
        name="post",
    )(x, o_f, o_b, p, p, mod3, norm_g, head_g, w_out, wr_hi, wr_lo, rbias, su)


def _slot_copies(meta_ref, stage, buf, hbm, sems, n_exp, to_hbm):
    copies = []
    for e in range(n_exp):
        rows = hbm.at[pl.ds(pl.multiple_of(meta_ref[0, 0, e], ROW_ALIGN), SLOT_ROWS)]
        slot = stage.at[buf, pl.ds(e * SLOT_ROWS, SLOT_ROWS)]
        copies.append(pltpu.make_async_copy(slot, rows, sems.at[buf]) if to_hbm
                      else pltpu.make_async_copy(rows, slot, sems.at[buf]))
    return copies


def _overflow_copy(meta_ref, spill, hbm, sem, n_exp, j, i, to_hbm):
    e = meta_ref[0, 0, n_exp + j]
    src = meta_ref[0, 0, 3 * n_exp + j] + ROW_ALIGN * i
    dst = meta_ref[0, 0, e] + SLOT_ROWS + ROW_ALIGN * i
    piece = spill.at[pl.ds(pl.multiple_of(src, ROW_ALIGN), ROW_ALIGN)]
    rows = hbm.at[pl.ds(pl.multiple_of(dst, ROW_ALIGN), ROW_ALIGN)]
    return pltpu.make_async_copy(piece, rows, sem) if to_hbm else pltpu.make_async_copy(rows, piece, sem)


def _for_overflow_pieces(meta_ref, n_exp, fn):
    def per_expert(j, carry):
        def per_piece(i, c2):
            fn(j, i)
            return c2
        return lax.fori_loop(0, meta_ref[0, 0, 2 * n_exp + j], per_piece, carry)
    lax.fori_loop(0, meta_ref[0, 0, 4 * n_exp], per_expert, 0)


def _spill_matrix_rows(meta_ref, rk_ref, wd_ref, base, ts, n_exp):
    rows = (lax.broadcasted_iota(I32, (SPILL_CHUNK, ts), 0) + base).astype(F32)

    def per_expert(j, hit):
        e = meta_ref[0, 0, n_exp + j]
        rk = rk_ref[pl.ds(e, 1), :]
        val = 1.0 if wd_ref is None else wd_ref[pl.ds(e, 1), :]
        target = jnp.where(rk >= SLOT_ROWS, rk - SLOT_ROWS + meta_ref[0, 0, 3 * n_exp + j].astype(F32), -1.0)
        return jnp.where(target == rows, val, hit)

    return lax.fori_loop(0, meta_ref[0, 0, 4 * n_exp], per_expert,
                         jnp.zeros((SPILL_CHUNK, ts), F32)).astype(BF16)


def _zero_fill(seg_ref, xs_hbm, stage, sem, n_exp, n_blocks):
    tail = SLOT_ROWS + EXPERT_BLOCK
    stage[0, 0:tail, :] = jnp.zeros((tail, stage.shape[2]), stage.dtype)
    tails = []
    for e in range(n_exp):
        start = jnp.maximum(seg_ref[0, 0, e] - tail, 0)
        tails.append(pltpu.make_async_copy(stage.at[0, pl.ds(0, tail)],
                                           xs_hbm.at[pl.ds(pl.multiple_of(start, ROW_ALIGN), tail)], sem))
    for cp in tails:
        cp.start()
    n_used = seg_ref[0, 0, n_exp]

    def block_copy(i):
        row = pl.multiple_of(i * EXPERT_BLOCK, EXPERT_BLOCK)
        return pltpu.make_async_copy(stage.at[0, pl.ds(0, EXPERT_BLOCK)], xs_hbm.at[pl.ds(row, EXPERT_BLOCK)], sem)

    def start_block(i, carry):
        block_copy(i).start()
        return carry

    def wait_block(i, carry):
        block_copy(i).wait()
        return carry

    lax.fori_loop(n_used, n_blocks, start_block, 0)
    for cp in tails:
        cp.wait()
    lax.fori_loop(n_used, n_blocks, wait_block, 0)


def _dispatch_kernel(meta_ref, seg_ref, x_ref, rk_ref, xs_hbm, stage, spill, sems, sem_ov, *,
                     ts, n_exp, n_blocks):
    i = pl.program_id(0)
    buf = i % 2

    @pl.when(i == 0)
    def _():
        _zero_fill(seg_ref, xs_hbm, stage, sem_ov, n_exp, n_blocks)

    x = x_ref[...]
    slot_row = lax.broadcasted_iota(I32, (SLOT_ROWS, ts), 0).astype(F32)
    group_rows = SLOT_GROUP * SLOT_ROWS
    for g in range(n_exp // SLOT_GROUP):
        pick = jnp.concatenate(
            [jnp.where(rk_ref[e:e + 1, :] == slot_row, 1.0, 0.0)
             for e in range(g * SLOT_GROUP, (g + 1) * SLOT_GROUP)], axis=0).astype(BF16)
        stage[buf, g * group_rows:(g + 1) * group_rows, :] = _dot(pick, x).astype(BF16)
    n_spill = meta_ref[0, 0, 4 * n_exp + 1]

    def spill_chunk(ci, carry):
        base = pl.multiple_of(ci * SPILL_CHUNK, SPILL_CHUNK)
        spill[pl.ds(base, SPILL_CHUNK), :] = _dot(
            _spill_matrix_rows(meta_ref, rk_ref, None, base, ts, n_exp), x).astype(BF16)
        return carry

    lax.fori_loop(0, n_spill, spill_chunk, 0)

    @pl.when(i > 0)
    def _():
        for cp in _slot_copies(meta_ref, stage, 1 - buf, xs_hbm, sems, n_exp, True):
            cp.wait()

    copies = _slot_copies(meta_ref, stage, buf, xs_hbm, sems, n_exp, True)
    for cp in copies:
        cp.start()
    _for_overflow_pieces(meta_ref, n_exp,
                         lambda e, k: _overflow_copy(meta_ref, spill, xs_hbm, sem_ov, n_exp, e, k, True).start())
    _for_overflow_pieces(meta_ref, n_exp,
                         lambda e, k: _overflow_copy(meta_ref, spill, xs_hbm, sem_ov, n_exp, e, k, True).wait())

    @pl.when(i == pl.num_programs(0) - 1)
    def _():
        for cp in copies:
            cp.wait()


def _dispatch(h2, rank, meta, seg, p_rows, ts, n_exp):
    t_all, d = h2.shape
    nt = t_all // ts
    kern = functools.partial(_dispatch_kernel, ts=ts, n_exp=n_exp, n_blocks=p_rows // EXPERT_BLOCK)
    return pl.pallas_call(
        kern,
        grid=(nt,),
        in_specs=[
            pl.BlockSpec((1, 1, meta.shape[2]), lambda i: (i, 0, 0), memory_space=pltpu.SMEM),
            pl.BlockSpec((1, 1, seg.shape[2]), lambda i: (0, 0, 0), memory_space=pltpu.SMEM),
            pl.BlockSpec((ts, d), lambda i: (i, 0)),
            pl.BlockSpec((n_exp, ts), lambda i: (0, i)),
        ],
        out_specs=pl.BlockSpec(memory_space=pl.ANY),
        out_shape=jax.ShapeDtypeStruct((p_rows, d), BF16),
        scratch_shapes=[pltpu.VMEM((2, n_exp * SLOT_ROWS, d), BF16),
                        pltpu.VMEM((ts * TOP_K, d), BF16),
                        pltpu.SemaphoreType.DMA((2,)), pltpu.SemaphoreType.DMA(())],
        compiler_params=pltpu.CompilerParams(dimension_semantics=("arbitrary",), has_side_effects=True),
        name="dispatch",
    )(meta, seg, h2, rank)


def _expert_kernel(be_ref, nu_ref, xs_ref, wg_ref, wu_ref, wd_ref, ys_ref):
    del be_ref
    i = pl.program_id(0)

    @pl.when(i < nu_ref[0])
    def _():
        xb = xs_ref[...]
        a = _silu(_dot(xb, wg_ref[0])) * _dot(xb, wu_ref[0])
        ys_ref[...] = _dot(a.astype(BF16), wd_ref[0]).astype(BF16)

    @pl.when(i >= nu_ref[0])
    def _():
        ys_ref[...] = jnp.zeros_like(ys_ref)


def _experts(xs, block_e, n_used, w_gate, w_up, w_down, blk):
    p_rows, dw = xs.shape
    n_exp, d, ff = w_gate.shape
    nb = p_rows // blk
    used = lambda i, nu: jnp.minimum(i, nu[0] - 1)
    grid_spec = pltpu.PrefetchScalarGridSpec(
        num_scalar_prefetch=2,
        grid=(nb,),
        in_specs=[
            pl.BlockSpec((blk, dw), lambda i, be, nu: (used(i, nu), 0)),
            pl.BlockSpec((1, d, ff), lambda i, be, nu: (be[used(i, nu)], 0, 0)),
            pl.BlockSpec((1, d, ff), lambda i, be, nu: (be[used(i, nu)], 0, 0)),
            pl.BlockSpec((1, ff, d), lambda i, be, nu: (be[used(i, nu)], 0, 0)),
        ],
        out_specs=pl.BlockSpec((blk, dw), lambda i, be, nu: (i, 0)),
    )
    return pl.pallas_call(
        _expert_kernel,
        grid_spec=grid_spec,
        out_shape=jax.ShapeDtypeStruct((p_rows, dw), BF16),
        compiler_params=pltpu.CompilerParams(dimension_semantics=("arbitrary",)),
        name="experts",
    )(block_e, n_used, xs, w_gate, w_up, w_down)


def _combine_kernel(meta_ref, nxt_ref, ys_hbm, rkt_ref, wdt_ref, rk_ref, wd_ref, ex_ref, rp_ref, x1_ref, h2_ref,
                    mod_ref, ng_ref, sg_ref, su_ref, sd_ref, o_ref, stage, spill, acc, sems, sem_ov, *,
                    ts, n_exp):
    i = pl.program_id(0)
    buf = i % 2
    n_spill = meta_ref[0, 0, 4 * n_exp + 1]
    n_groups = n_exp // SLOT_GROUP

    @pl.when(i == 0)
    def _():
        for cp in _slot_copies(meta_ref, stage, 0, ys_hbm, sems, n_exp, False):
            cp.start()

    @pl.when(i < pl.num_programs(0) - 1)
    def _():
        for cp in _slot_copies(nxt_ref, stage, 1 - buf, ys_hbm, sems, n_exp, False):
            cp.start()

    def clear_chunk(ci, carry):
        base = pl.multiple_of(ci * SPILL_CHUNK, SPILL_CHUNK)
        spill[pl.ds(base, SPILL_CHUNK), :] = jnp.zeros((SPILL_CHUNK, spill.shape[1]), spill.dtype)
        return carry

    lax.fori_loop(0, n_spill, clear_chunk, 0)
    _for_overflow_pieces(meta_ref, n_exp,
                         lambda e, k: _overflow_copy(meta_ref, spill, ys_hbm, sem_ov, n_exp, e, k, False).start())

    xb = h2_ref[...]
    a = _silu(_dot(xb, sg_ref[...])) * _dot(xb, su_ref[...])
    tot = _dot(a.astype(BF16), sd_ref[...])

    rank_lanes = _dot(rkt_ref[...].astype(BF16), ex_ref[...])
    weight_lanes = _dot(wdt_ref[...].astype(BF16), ex_ref[...])

    for cp in _slot_copies(meta_ref, stage, buf, ys_hbm, sems, n_exp, False):
        cp.wait()
    group_rows = SLOT_GROUP * SLOT_ROWS
    for g in range(n_groups):
        cols = slice(g * group_rows, (g + 1) * group_rows)
        unmix = jnp.where(rank_lanes[:, cols] == rp_ref[:, cols], weight_lanes[:, cols], 0.0).astype(BF16)
        tot = tot + _dot(unmix, stage[buf, g * group_rows:(g + 1) * group_rows, :])
    acc[...] = tot

    _for_overflow_pieces(meta_ref, n_exp,
                         lambda e, k: _overflow_copy(meta_ref, spill, ys_hbm, sem_ov, n_exp, e, k, False).wait())

    def spill_chunk(ci, carry):
        base = pl.multiple_of(ci * SPILL_CHUNK, SPILL_CHUNK)
        acc[...] += _dot_tn(_spill_matrix_rows(meta_ref, rk_ref, wd_ref, base, ts, n_exp),
                            spill[pl.ds(base, SPILL_CHUNK), :])
        return carry

    lax.fori_loop(0, n_spill, spill_chunk, 0)
    g2 = mod_ref[0, 5:6, :]
    o_ref[...] = x1_ref[...] + g2 * _rms(acc[...], ng_ref[3:4, :])


def _combine(ys, meta, rank_tm, wd_tm, rank, wd, x1_flat, h2, mod3, norm_g, ws_gate, ws_up, ws_down,
             n_seq, ts, n_exp):
    t_all, d = x1_flat.shape
    nt = t_all // ts
    per_b = n_seq // ts
    ff = ws_gate.shape[1]
    lanes = n_exp * SLOT_ROWS
    lane = lax.broadcasted_iota(I32, (n_exp, lanes), 1)
    expand = jnp.where(lane // SLOT_ROWS == lax.broadcasted_iota(I32, (n_exp, lanes), 0), 1.0, 0.0).astype(BF16)
    slot_rank = (jnp.arange(lanes, dtype=I32) % SLOT_ROWS).astype(F32).reshape(1, lanes)
    c2 = lambda i: (0, 0)
    kern = functools.partial(_combine_kernel, ts=ts, n_exp=n_exp)
    return pl.pallas_call(
        kern,
        grid=(nt,),
        in_specs=[
            pl.BlockSpec((1, 1, meta.shape[2]), lambda i: (i, 0, 0), memory_space=pltpu.SMEM),
            pl.BlockSpec((1, 1, meta.shape[2]), lambda i: (jnp.minimum(i + 1, nt - 1), 0, 0),
                         memory_space=pltpu.SMEM),
            pl.BlockSpec(memory_space=pl.ANY),
            pl.BlockSpec((ts, n_exp), lambda i: (i, 0)),
            pl.BlockSpec((ts, n_exp), lambda i: (i, 0)),
            pl.BlockSpec((n_exp, ts), lambda i: (0, i)),
            pl.BlockSpec((n_exp, ts), lambda i: (0, i)),
            pl.BlockSpec((n_exp, lanes), c2),
            pl.BlockSpec((1, lanes), c2),
            pl.BlockSpec((ts, d), lambda i: (i, 0)),
            pl.BlockSpec((ts, d), lambda i: (i, 0)),
            pl.BlockSpec((1, N_MOD, d), lambda i: (i // per_b, 0, 0)),
            pl.BlockSpec((4, d), c2),
            pl.BlockSpec((d, ff), c2),
            pl.BlockSpec((d, ff), c2),
            pl.BlockSpec((ff, d), c2),
        ],
        out_specs=pl.BlockSpec((ts, d), lambda i: (i, 0)),
        out_shape=jax.ShapeDtypeStruct((t_all, d), F32),
        scratch_shapes=[pltpu.VMEM((2, lanes, d), BF16), pltpu.VMEM((ts * TOP_K, d), BF16),
                        pltpu.VMEM((ts, d), F32),
                        pltpu.SemaphoreType.DMA((2,)), pltpu.SemaphoreType.DMA(())],
        compiler_params=pltpu.CompilerParams(dimension_semantics=("arbitrary",)),
        name="combine",
    )(meta, meta, ys, rank_tm, wd_tm, rank, wd, expand, slot_rank, x1_flat, h2, mod3, norm_g,
      ws_gate, ws_up, ws_down)


def _rope_tables(n):
    rows = jnp.repeat(jnp.arange(n // GRID_W, dtype=F32), GRID_W)
    cols = jnp.tile(jnp.arange(GRID_W, dtype=F32), n // GRID_W)
    quarter = LANES // 4
    freqs = ROPE_BASE ** (-jnp.arange(quarter, dtype=F32) / quarter)
    ang = jnp.concatenate([rows[:, None] * freqs, cols[:, None] * freqs], axis=-1)
    cos, sin = jnp.cos(ang), jnp.sin(ang)
    return jnp.concatenate([cos, cos], axis=-1), jnp.concatenate([-sin, sin], axis=-1)


def _retention_tables(log_decay):
    lg = -jnp.exp(log_decay.astype(F32))
    idx = jnp.arange(CHUNK, dtype=F32)
    rel = idx[:, None] - idx[None, :]
    lg3 = lg[:, :, None, None]
    intra_f = jnp.where(rel >= 0, jnp.exp(jnp.maximum(rel, 0.0) * lg3[0]), 0.0)
    intra_b = jnp.where(rel <= 0, jnp.exp(jnp.maximum(-rel, 0.0) * lg3[1]), 0.0)
    kd_f = jnp.exp((CHUNK - 1 - idx)[None, :] * lg[0][:, None])
    kd_b = jnp.exp(idx[None, :] * lg[1][:, None])
    qd_f = jnp.exp((idx + 1)[None, :] * lg[0][:, None])
    qd_b = jnp.exp((CHUNK - idx)[None, :] * lg[1][:, None])
    bc = lambda t: jnp.broadcast_to(t[:, :, None], (HEADS, CHUNK, LANES))
    intra = jnp.concatenate([intra_f, intra_b], axis=0)
    kd = jnp.concatenate([bc(kd_f), bc(kd_b)], axis=0)
    qd = jnp.concatenate([bc(qd_f), bc(qd_b)], axis=0)
    cd = jnp.broadcast_to(jnp.exp(CHUNK * lg).reshape(2 * HEADS, 1), (2 * HEADS, LANES))
    return intra, kd, qd, cd


def kernel(x, c, ctx, c_ctx, w_mod, b_mod, norm_g, w_in, ret_log_decay, ret_norm_g, mlstm_conv_w,
           mlstm_conv_b, mlstm_gate_b, mlstm_norm_g, w_out, w_router, router_bias, w_gate, w_up, w_down,
           ws_gate, ws_up, ws_down):
    b, n, d = x.shape
    n_ctx = ctx.shape[1]
    depth = w_mod.shape[0]
    assert depth == 1, "only the single-layer configuration is implemented"
    assert d // 2 // HEADS == LANES
    n_exp = w_router.shape[2]
    t_all = b * n
    r_w = d // 2
    main_cols = 8 * r_w
    l = 0

    pad = (-(b + 1)) % 8
    cc = jnp.concatenate([c, c_ctx[None, :], jnp.zeros((pad, d), F32)], axis=0)
    mod3 = _modulation(cc, w_mod[l], b_mod[l]).reshape(b + 1 + pad, N_MOD, d)

    w_groups = w_in[l, :, :main_cols].astype(BF16).reshape(d, 8, r_w)
    w_main = w_groups[:, jnp.array([0, 1, 2, 4, 5, 6, 3, 7]), :].reshape(d, main_cols)
    wg = w_in[l, :, main_cols:].astype(BF16)
    wgt = wg.T
    gb = mlstm_gate_b[l].reshape(-1).astype(F32)
    tabs = _retention_tables(ret_log_decay[l])
    head_g = jnp.concatenate([ret_norm_g[l], mlstm_norm_g[l]]).reshape(1, d).astype(F32)
    wr = w_router[l].T.astype(F32)
    wr_hi = wr.astype(BF16)
    wr_lo = (wr - wr_hi.astype(F32)).astype(BF16)

    def inproj(seq, mod_row, ts):
        cos2, sin2 = _rope_tables(n) if mod_row is None else (
            jnp.ones((seq.shape[1], LANES), F32), jnp.zeros((seq.shape[1], LANES), F32))
        return _inproj(seq, mod3, mod_row, norm_g[l, 0:1], w_main, wgt, wg, mlstm_conv_w[l],
                       mlstm_conv_b[l].reshape(1, -1), gb.reshape(16, 1), gb.reshape(1, 16), cos2, sin2, ts)

    nst = 2 * HEADS
    zero_states = (jnp.zeros((b, nst, LANES, LANES), F32), jnp.zeros((b, nst, LANES, 2 * LANES), F32),
                   jnp.zeros((b, nst, LANES), F32))
    p_c, gr_c, gc_c = inproj(ctx, b, min(n_ctx, 512))
    ctx_states = _scan(p_c, gr_c, gc_c, tabs, zero_states, with_output=False)

    ts = min(n, 512)
    p_l, gr_l, gc_l = inproj(x, None, ts)
    o_f, o_b, _, _, _ = _scan(p_l, gr_l, gc_l, tabs, tuple(ctx_states), with_output=True)
    ts_moe = MOE_TILE
    x1, h2, rank, wdense, tile_cnt = _post(
        x, o_f, o_b, p_l, mod3, norm_g[l], head_g, w_out[l].astype(BF16), wr_hi, wr_lo,
        router_bias[l].reshape(n_exp, 1).astype(F32), ts_moe)

    nt = t_all // ts_moe
    cnt = tile_cnt[:, 0, :].astype(I32)
    run_rows = (cnt + ROW_ALIGN - 1) // ROW_ALIGN * ROW_ALIGN
    seg_cap = (jnp.sum(run_rows, axis=0) + SLOT_ROWS + EXPERT_BLOCK - 1) // EXPERT_BLOCK * EXPERT_BLOCK
    seg_end = jnp.cumsum(seg_cap)
    run_start = (seg_end - seg_cap)[None, :] + jnp.cumsum(run_rows, axis=0) - run_rows
    ov_rows = jnp.maximum(run_rows - SLOT_ROWS, 0)
    ov_off = jnp.cumsum(ov_rows, axis=1) - ov_rows
    n_spill = (jnp.sum(ov_rows, axis=1, keepdims=True) + SPILL_CHUNK - 1) // SPILL_CHUNK
    spills = ov_rows > 0
    n_ov = jnp.sum(spills.astype(I32), axis=1, keepdims=True)
    nth = jnp.cumsum(spills.astype(I32), axis=1) - 1
    is_jth = spills[:, None, :] & (nth[:, None, :] == jnp.arange(n_exp, dtype=I32)[None, :, None])
    compact = lambda v: jnp.sum(jnp.where(is_jth, v[:, None, :], 0), axis=2)
    ov_e = compact(jnp.broadcast_to(jnp.arange(n_exp, dtype=I32)[None, :], cnt.shape))
    meta = jnp.concatenate([run_start, ov_e, compact(ov_rows // ROW_ALIGN), compact(ov_off), n_ov, n_spill],
                           axis=1).astype(I32)
    meta = jnp.pad(meta, ((0, 0), (0, (-meta.shape[1]) % LANES))).reshape(nt, 1, -1)
    p_rows = -(-(t_all * TOP_K + nt * n_exp * (ROW_ALIGN - 1) + n_exp * (SLOT_ROWS + EXPERT_BLOCK - 1))
               // EXPERT_BLOCK) * EXPERT_BLOCK
    nb = p_rows // EXPERT_BLOCK
    blk_first = jnp.arange(nb, dtype=I32) * EXPERT_BLOCK
    block_e = jnp.minimum(jnp.sum((seg_end[None, :] <= blk_first[:, None]).astype(I32), axis=1), n_exp - 1)
    n_used = (seg_end[-1:] // EXPERT_BLOCK).astype(I32)
    seg = jnp.concatenate([seg_end.astype(I32), n_used])
    seg = jnp.pad(seg, (0, (-seg.shape[0]) % LANES)).reshape(1, 1, -1)

    xs = _dispatch(h2, rank, meta, seg, p_rows, ts_moe, n_exp)
    ys = _experts(xs, block_e, n_used, w_gate[l].astype(BF16), w_up[l].astype(BF16), w_down[l].astype(BF16),
                  EXPERT_BLOCK)
    out = _combine(ys, meta, rank.T, wdense.T, rank, wdense, x1.reshape(t_all, d), h2, mod3, norm_g[l],
                   ws_gate[l].astype(BF16), ws_up[l].astype(BF16), ws_down[l].astype(BF16), n, ts_moe, n_exp)
    return out.reshape(b, n, d)
```

```python
import functools

import jax
import jax.numpy as jnp
from jax import lax
from jax.experimental import pallas as pl
from jax.experimental.pallas import tpu as pltpu

F32 = jnp.float32
BF16 = jnp.bfloat16
I32 = jnp.int32

EPS = 1e-6
LANES = 128
CHUNK = 128
HEADS = 4
GRID_W = 64
ROPE_BASE = 10000.0
N_GROUPS = 8
TOPK_GROUPS = 4
TOP_K = 8
ROUTED_SCALE = 2.5
N_MOD = 6
MOE_TILE = 256
SPILL_CHUNK = 64
ROW_ALIGN = 16
SLOT_ROWS = 48
SLOT_GROUP = 8
EXPERT_BLOCK = 512
NEG_INF = float("-inf")


def _sigmoid(v):
    return 1.0 / (1.0 + jnp.exp(-v))


def _silu(v):
    return v * _sigmoid(v)


def _log_sigmoid(v):
    return jnp.minimum(v, 0.0) - jnp.log(1.0 + jnp.exp(-jnp.abs(v)))


def _dot(a, b):
    return jnp.dot(a, b, preferred_element_type=F32)


def _dot_nt(a, b):
    return lax.dot_general(a, b, (((1,), (1,)), ((), ())), preferred_element_type=F32)


def _dot_tn(a, b):
    return lax.dot_general(a, b, (((0,), (0,)), ((), ())), preferred_element_type=F32)


def _split3(a):
    hi = a.astype(BF16)
    r = a - hi.astype(F32)
    mid = r.astype(BF16)
    lo = (r - mid.astype(F32)).astype(BF16)
    return hi, mid, lo


def _rms(v, g):
    ms = jnp.mean(v * v, axis=-1, keepdims=True)
    return v * lax.rsqrt(ms + EPS) * g


def _mod_kernel(c_ref, w_ref, b_ref, o_ref):
    a = _silu(c_ref[...])
    o_ref[...] = jnp.dot(a, w_ref[...], preferred_element_type=F32,
                         precision=lax.Precision.HIGHEST) + b_ref[...]


def _modulation(cc, w_mod, b_mod):
    rows, d = cc.shape
    cols = w_mod.shape[1]
    tn = d
    return pl.pallas_call(
        _mod_kernel,
        grid=(cols // tn,),
        in_specs=[pl.BlockSpec((rows, d), lambda j: (0, 0)),
                  pl.BlockSpec((d, tn), lambda j: (0, j)),
                  pl.BlockSpec((1, tn), lambda j: (0, j))],
        out_specs=pl.BlockSpec((rows, tn), lambda j: (0, j)),
        out_shape=jax.ShapeDtypeStruct((rows, cols), F32),
        name="mod",
    )(cc, w_mod, b_mod.reshape(1, cols))


def _inproj_kernel(x_ref, xp_ref, xn_ref, mod_ref, g_ref, w_ref, wgt_ref, wg_ref, cw_ref, cb_ref,
                   gbr_ref, gbc_ref, cos_ref, sin_ref, p_ref, gr_ref, gc_ref, *, ts, d):
    i = pl.program_id(1)
    last = pl.num_programs(1) - 1
    r_w = d // 2
    shift = mod_ref[0, 0:1, :]
    scale = mod_ref[0, 1:2, :]
    g = g_ref[...]

    def normmod(v):
        return _rms(v, g) * (1.0 + scale) + shift

    hb = normmod(x_ref[0]).astype(BF16)
    halo = jnp.concatenate([xp_ref[0], xn_ref[0]], axis=0)
    ph = _dot(normmod(halo).astype(BF16), w_ref[:, 3 * r_w:5 * r_w])
    prev_row = jnp.where(i == 0, 0.0, ph[7:8, :])
    next_row = jnp.where(i == last, 0.0, ph[8:9, :])

    cos2 = cos_ref[...]
    sin2 = sin_ref[...]
    rows = lax.broadcasted_iota(I32, (ts, r_w), 0)
    qscale = LANES ** -0.5

    for j in range(8):
        acc = _dot(hb, w_ref[:, j * r_w:(j + 1) * r_w])
        if j in (0, 1):
            if j == 0:
                acc = acc * qscale
            parts = []
            for h in range(HEADS):
                t = acc[:, h * LANES:(h + 1) * LANES]
                parts.append(t * cos2 + pltpu.roll(t, LANES // 2, axis=1) * sin2)
            acc = jnp.concatenate(parts, axis=1)
        elif j in (3, 4):
            c0 = (j - 3) * r_w
            pr = prev_row[:, c0:c0 + r_w]
            nx = next_row[:, c0:c0 + r_w]
            down = jnp.where(rows == 0, pr, pltpu.roll(acc, 1, axis=0))
            up = jnp.where(rows == ts - 1, nx, pltpu.roll(acc, ts - 1, axis=0))
            cw = cw_ref[:, c0:c0 + r_w]
            acc = down * cw[0:1, :] + acc * cw[1:2, :] + up * cw[2:3, :] + cb_ref[:, c0:c0 + r_w]
            acc = _silu(acc)
            if j == 4:
                acc = acc * qscale
        p_ref[0, :, j * r_w:(j + 1) * r_w] = acc.astype(BF16)

    gr = _dot_nt(wgt_ref[...], hb) + gbr_ref[...]
    ch_r = lax.broadcasted_iota(I32, gr.shape, 0)
    gr_ref[0] = jnp.where((ch_r // HEADS) % 2 == 1, _log_sigmoid(gr), gr)
    gc = _dot(hb, wg_ref[...]) + gbc_ref[...]
    ch_c = lax.broadcasted_iota(I32, gc.shape, 1)
    gc_ref[0] = jnp.where((ch_c // HEADS) % 2 == 1, _log_sigmoid(gc), gc)


def _inproj(x, mod3, mod_row, g, w_main, wgt, wg, conv_w, conv_b, gb_col, gb_row, cos2, sin2, ts):
    b, n, d = x.shape
    nt = n // ts
    nb8 = n // 8
    hb = ts // 8
    cols = w_main.shape[1]
    if mod_row is None:
        mod_map = lambda bi, i: (bi, 0, 0)
    else:
        mod_map = lambda bi, i: (mod_row, 0, 0)
    const2 = lambda bi, i: (0, 0)
    kern = functools.partial(_inproj_kernel, ts=ts, d=d)
    return pl.pallas_call(
        kern,
        grid=(b, nt),
        in_specs=[
            pl.BlockSpec((1, ts, d), lambda bi, i: (bi, i, 0)),
            pl.BlockSpec((1, 8, d), lambda bi, i: (bi, jnp.maximum(i * hb - 1, 0), 0)),
            pl.BlockSpec((1, 8, d), lambda bi, i: (bi, jnp.minimum((i + 1) * hb, nb8 - 1), 0)),
            pl.BlockSpec((1, N_MOD, d), mod_map),
            pl.BlockSpec((1, d), const2),
            pl.BlockSpec((d, cols), const2),
            pl.BlockSpec((16, d), const2),
            pl.BlockSpec((d, 16), const2),
            pl.BlockSpec((3, d), const2),
            pl.BlockSpec((1, d), const2),
            pl.BlockSpec((16, 1), const2),
            pl.BlockSpec((1, 16), const2),
            pl.BlockSpec((ts, LANES), lambda bi, i: (i, 0)),
            pl.BlockSpec((ts, LANES), lambda bi, i: (i, 0)),
        ],
        out_specs=[
            pl.BlockSpec((1, ts, cols), lambda bi, i: (bi, i, 0)),
            pl.BlockSpec((1, 16, ts), lambda bi, i: (bi, 0, i)),
            pl.BlockSpec((1, ts, 16), lambda bi, i: (bi, i, 0)),
        ],
        out_shape=[
            jax.ShapeDtypeStruct((b, n, cols), BF16),
            jax.ShapeDtypeStruct((b, 16, n), F32),
            jax.ShapeDtypeStruct((b, n, 16), F32),
        ],
        compiler_params=pltpu.CompilerParams(dimension_semantics=("parallel", "parallel")),
        name="inproj",
    )(x, x, x, mod3, g, w_main, wgt, wg, conv_w, conv_b, gb_col, gb_row, cos2, sin2)


def _scan_kernel(pf_ref, pb_ref, grf_ref, grb_ref, gcf_ref, gcb_ref, intra_ref, kd_ref, qd_ref, cd_ref,
                 rs0_ref, mc0_ref, mm0_ref, *out_refs, with_output, r_w):
    if with_output:
        of_ref, ob_ref, rs_ref, mc_ref, mm_ref = out_refs
    else:
        rs_ref, mc_ref, mm_ref = out_refs
    j = pl.program_id(1)

    @pl.when(j == 0)
    def _():
        rs_ref[...] = rs0_ref[...]
        mc_ref[...] = mc0_ref[...]
        mm_ref[...] = mm0_ref[...]

    c = CHUNK
    row = lax.broadcasted_iota(I32, (c, c), 0)
    col = lax.broadcasted_iota(I32, (c, c), 1)
    tri_le = (row <= col)
    tri_ge = (row >= col)
    ones_ext = jnp.where(lax.broadcasted_iota(I32, (c, LANES), 1) == 0, 1.0, 0.0).astype(BF16)

    def cumsums(gr, gc, fwd):
        m_row = jnp.where(tri_le if fwd else tri_ge, 1.0, 0.0).astype(BF16)
        m_col = jnp.where(tri_ge if fwd else tri_le, 1.0, 0.0).astype(BF16)
        b_row = sum(_dot(piece, m_row) for piece in _split3(gr))
        b_col = sum(_dot(m_col, piece) for piece in _split3(gc))
        return b_row, b_col

    for dr in range(2):
        fwd = dr == 0
        p_ref = pf_ref if fwd else pb_ref
        o_ref = None
        if with_output:
            o_ref = of_ref if fwd else ob_ref
        gr = (grf_ref if fwd else grb_ref)[0]
        gc = (gcf_ref if fwd else gcb_ref)[0]
        cs_row, cs_col = cumsums(gr, gc, fwd)
        causal = tri_ge if fwd else tri_le

        for h in range(HEADS):
            hs = slice(h * LANES, (h + 1) * LANES)
            st = dr * HEADS + h
            k = p_ref[0, :, r_w + h * LANES:r_w + (h + 1) * LANES]
            v = p_ref[0, :, 2 * r_w + h * LANES:2 * r_w + (h + 1) * LANES]
            s_prev = rs_ref[0, st]
            ks = (k.astype(F32) * kd_ref[st]).astype(BF16)
            upd = _dot_tn(ks, v)
            if with_output:
                q = p_ref[0, :, hs]
                sc = _dot_nt(q, k) * intra_ref[st]
                out = _dot(sc.astype(BF16), v) + qd_ref[st] * _dot(q, s_prev.astype(BF16))
                o_ref[0, :, hs] = out.astype(BF16)
            rs_ref[0, st] = s_prev * cd_ref[st:st + 1, :] + upd

            o0 = 3 * r_w
            mk = p_ref[0, :, o0 + r_w + h * LANES:o0 + r_w + (h + 1) * LANES]
            mv = p_ref[0, :, o0 + 2 * r_w + h * LANES:o0 + 2 * r_w + (h + 1) * LANES]
            ci = dr * 2 * HEADS + h
            cf = ci + HEADS
            ib_row = gr[ci:ci + 1, :]
            b_row = cs_row[cf:cf + 1, :]
            ib_col = gc[:, ci:ci + 1]
            b_col = cs_col[:, cf:cf + 1]
            b_tot = b_row[:, c - 1:c] if fwd else b_row[:, 0:1]
            m_prev = mm_ref[0, st:st + 1, 0:1]
            log_ws_row = b_tot - b_row + ib_row
            m_next = jnp.maximum(b_tot + m_prev, jnp.max(log_ws_row, axis=1, keepdims=True))
            decay_prev = jnp.exp(b_tot + m_prev - m_next)
            ws_col = jnp.exp(b_tot - b_col + ib_col - m_next)
            kw = (mk.astype(F32) * ws_col).astype(BF16)
            v_ext = jnp.concatenate([mv, ones_ext], axis=1)
            upd_c = _dot_tn(kw, v_ext)
            c_prev = mc_ref[0, st]
            if with_output:
                mq = p_ref[0, :, o0 + h * LANES:o0 + (h + 1) * LANES]
                dm = jnp.where(causal, b_col - b_row + ib_row, NEG_INF)
                log_inter = b_col + m_prev
                m_i = jnp.maximum(log_inter, jnp.max(dm, axis=1, keepdims=True))
                w = jnp.exp(dm - m_i)
                inter = jnp.exp(log_inter - m_i)
                sc = _dot_nt(mq, mk) * w
                hx = _dot(sc.astype(BF16), v_ext) + inter * _dot(mq, c_prev.astype(BF16))
                den = hx[:, LANES:LANES + 1]
                hout = hx[:, :LANES] / jnp.maximum(jnp.abs(den), jnp.exp(-m_i))
                o_ref[0, :, r_w + h * LANES:r_w + (h + 1) * LANES] = hout.astype(BF16)
            mc_ref[0, st] = decay_prev * c_prev + upd_c
            mm_ref[0, st:st + 1, :] = jnp.broadcast_to(m_next, (1, LANES))


def _scan(p, g_row, g_col, tabs, states, with_output):
    b, n, cols = p.shape
    nch = n // CHUNK
    r_w = cols // 8
    intra, kd, qd, cd = tabs
    rs0, mc0, mm0 = states
    nst = 2 * HEADS
    fwd3 = lambda bi, j: (bi, j, 0)
    bwd3 = lambda bi, j: (bi, nch - 1 - j, 0)
    c3 = lambda bi, j: (0, 0, 0)
    st4 = lambda bi, j: (bi, 0, 0, 0)
    in_specs = [
        pl.BlockSpec((1, CHUNK, 6 * r_w), fwd3),
        pl.BlockSpec((1, CHUNK, 6 * r_w), bwd3),
        pl.BlockSpec((1, 16, CHUNK), lambda bi, j: (bi, 0, j)),
        pl.BlockSpec((1, 16, CHUNK), lambda bi, j: (bi, 0, nch - 1 - j)),
        pl.BlockSpec((1, CHUNK, 16), fwd3),
        pl.BlockSpec((1, CHUNK, 16), bwd3),
        pl.BlockSpec((nst, CHUNK, LANES), c3),
        pl.BlockSpec((nst, CHUNK, LANES), c3),
        pl.BlockSpec((nst, CHUNK, LANES), c3),
        pl.BlockSpec((nst, LANES), lambda bi, j: (0, 0)),
        pl.BlockSpec((1, nst, LANES, LANES), st4),
        pl.BlockSpec((1, nst, LANES, 2 * LANES), st4),
        pl.BlockSpec((1, nst, LANES), lambda bi, j: (bi, 0, 0)),
    ]
    st_specs = [
        pl.BlockSpec((1, nst, LANES, LANES), st4),
        pl.BlockSpec((1, nst, LANES, 2 * LANES), st4),
        pl.BlockSpec((1, nst, LANES), lambda bi, j: (bi, 0, 0)),
    ]
    st_shapes = [
        jax.ShapeDtypeStruct((b, nst, LANES, LANES), F32),
        jax.ShapeDtypeStruct((b, nst, LANES, 2 * LANES), F32),
        jax.ShapeDtypeStruct((b, nst, LANES), F32),
    ]
    if with_output:
        out_specs = [pl.BlockSpec((1, CHUNK, 2 * r_w), fwd3), pl.BlockSpec((1, CHUNK, 2 * r_w), bwd3)] + st_specs
        out_shape = [jax.ShapeDtypeStruct((b, n, 2 * r_w), BF16)] * 2 + st_shapes
    else:
        out_specs, out_shape = st_specs, st_shapes
    kern = functools.partial(_scan_kernel, with_output=with_output, r_w=r_w)
    return pl.pallas_call(
        kern,
        grid=(b, nch),
        in_specs=in_specs,
        out_specs=out_specs,
        out_shape=out_shape,
        compiler_params=pltpu.CompilerParams(dimension_semantics=("parallel", "arbitrary")),
        name="scan_out" if with_output else "scan_state",
    )(p, p, g_row, g_row, g_col, g_col, intra, kd, qd, cd, rs0, mc0, mm0)


def _post_kernel(x_ref, of_ref, ob_ref, rg_ref, mo_ref, mod_ref, ng_ref, hg_ref, wo_ref, wrh_ref, wrl_ref,
                 rb_ref, su_ref, x1_ref, h2_ref, rk_ref, wd_ref, cnt_ref, *, ts, d, n_exp):
    s = of_ref[0].astype(F32) + ob_ref[0].astype(F32)
    parts = []
    for gi in range(2 * HEADS):
        sl = s[:, gi * LANES:(gi + 1) * LANES]
        mu = jnp.mean(sl, axis=-1, keepdims=True)
        dv = sl - mu
        var = jnp.mean(dv * dv, axis=-1, keepdims=True)
        y = dv * lax.rsqrt(var + EPS) * hg_ref[:, gi * LANES:(gi + 1) * LANES]
        if gi < HEADS:
            gate = _silu(rg_ref[0, :, gi * LANES:(gi + 1) * LANES].astype(F32))
        else:
            gate = _sigmoid(mo_ref[0, :, (gi - HEADS) * LANES:(gi - HEADS + 1) * LANES].astype(F32))
        parts.append((y * gate).astype(BF16))
    mixed = jnp.concatenate(parts, axis=1)
    y = _dot(mixed, wo_ref[...])
    g1 = mod_ref[0, 2:3, :]
    sh2 = mod_ref[0, 3:4, :]
    sc2 = mod_ref[0, 4:5, :]
    x1 = x_ref[0] + g1 * _rms(y, ng_ref[1:2, :])
    x1_ref[0] = x1
    h2 = _rms(x1, ng_ref[2:3, :]) * (1.0 + sc2) + sh2
    h_hi = h2.astype(BF16)
    h2_ref[...] = h_hi

    h_lo = (h2 - h_hi.astype(F32)).astype(BF16)
    logits = _dot_nt(wrh_ref[...], h_hi) + _dot_nt(wrh_ref[...], h_lo) + _dot_nt(wrl_ref[...], h_hi)
    scores = _sigmoid(logits)
    sel = scores + rb_ref[...]
    gsz = n_exp // N_GROUPS
    iota_g = lax.broadcasted_iota(I32, (gsz, ts), 0).astype(F32)
    grp = []
    for gi in range(N_GROUPS):
        blk = sel[gi * gsz:(gi + 1) * gsz, :]
        m1 = jnp.max(blk, axis=0, keepdims=True)
        i1 = jnp.min(jnp.where(blk == m1, iota_g, float(gsz)), axis=0, keepdims=True)
        m2 = jnp.max(jnp.where(iota_g == i1, NEG_INF, blk), axis=0, keepdims=True)
        grp.append(m1 + m2)
    masked_parts = []
    for gi in range(N_GROUPS):
        rank = jnp.zeros((1, ts), F32)
        for gj in range(N_GROUPS):
            if gj == gi:
                continue
            beats = (grp[gj] >= grp[gi]) if gj < gi else (grp[gj] > grp[gi])
            rank = rank + jnp.where(beats, 1.0, 0.0)
        keep = rank < float(TOPK_GROUPS)
        masked_parts.append(jnp.where(keep, sel[gi * gsz:(gi + 1) * gsz, :], NEG_INF))
    masked = jnp.concatenate(masked_parts, axis=0)

    iota_e = lax.broadcasted_iota(I32, (n_exp, ts), 0).astype(F32)
    selmask = jnp.zeros((n_exp, ts), F32)
    for _ in range(TOP_K):
        mx = jnp.max(masked, axis=0, keepdims=True)
        ei = jnp.min(jnp.where(masked == mx, iota_e, float(n_exp)), axis=0, keepdims=True)
        hit = iota_e == ei
        selmask = jnp.where(hit, 1.0, selmask)
        masked = jnp.where(hit, NEG_INF, masked)
    picked = selmask > 0.0
    wsel = jnp.where(picked, scores, 0.0)
    wd_ref[...] = wsel / jnp.sum(wsel, axis=0, keepdims=True) * ROUTED_SCALE
    rank = _dot(selmask.astype(BF16), su_ref[...])
    rk_ref[...] = jnp.where(picked, rank, -1.0)
    cnt_ref[0] = _dot_nt(jnp.ones((8, ts), BF16), selmask.astype(BF16))


def _post(x, o_f, o_b, p, mod3, norm_g, head_g, w_out, wr_hi, wr_lo, rbias, ts):
    b, n, d = x.shape
    nt = n // ts
    t_all = b * n
    n_exp = wr_hi.shape[0]
    r_w = d // 2
    su = jnp.where(lax.broadcasted_iota(I32, (ts, ts), 0) < lax.broadcasted_iota(I32, (ts, ts), 1),
                   1.0, 0.0).astype(BF16)
    tok3 = lambda bi, i: (bi, i, 0)
    c2 = lambda bi, i: (0, 0)
    flat = lambda bi, i: (0, bi * nt + i)
    kern = functools.partial(_post_kernel, ts=ts, d=d, n_exp=n_exp)
    return pl.pallas_call(
        kern,
        grid=(b, nt),
        in_specs=[
            pl.BlockSpec((1, ts, d), tok3),
            pl.BlockSpec((1, ts, d), tok3),
            pl.BlockSpec((1, ts, d), tok3),
            pl.BlockSpec((1, ts, r_w), lambda bi, i: (bi, i, 6)),
            pl.BlockSpec((1, ts, r_w), lambda bi, i: (bi, i, 7)),
            pl.BlockSpec((1, N_MOD, d), lambda bi, i: (bi, 0, 0)),
            pl.BlockSpec((4, d), c2),
            pl.BlockSpec((1, d), c2),
            pl.BlockSpec((d, d), c2),
            pl.BlockSpec((n_exp, d), c2),
            pl.BlockSpec((n_exp, d), c2),
            pl.BlockSpec((n_exp, 1), c2),
            pl.BlockSpec((ts, ts), c2),
        ],
        out_specs=[
            pl.BlockSpec((1, ts, d), tok3),
            pl.BlockSpec((ts, d), lambda bi, i: (bi * nt + i, 0)),
            pl.BlockSpec((n_exp, ts), flat),
            pl.BlockSpec((n_exp, ts), flat),
            pl.BlockSpec((1, 8, n_exp), lambda bi, i: (bi * nt + i, 0, 0)),
        ],
        out_shape=[
            jax.ShapeDtypeStruct((b, n, d), F32),
            jax.ShapeDtypeStruct((t_all, d), BF16),
            jax.ShapeDtypeStruct((n_exp, t_all), F32),
            jax.ShapeDtypeStruct((n_exp, t_all), F32),
            jax.ShapeDtypeStruct((b * nt, 8, n_exp), F32),
        ],
        compiler_params=pltpu.CompilerParams(dimension_semantics=("parallel", "parallel")),
        name="post",
    )(x, o_f, o_b, p, p, mod3, norm_g, head_g, w_out, wr_hi, wr_lo, rbias, su)


def _slot_copies(meta_ref, stage, buf, hbm, sems, n_exp, to_hbm):
    copies = []
    for e in range(n_exp):
        rows = hbm.at[pl.ds(pl.multiple_of(meta_ref[0, 0, e], ROW_ALIGN), SLOT_ROWS)]
        slot = stage.at[buf, pl.ds(e * SLOT_ROWS, SLOT_ROWS)]
        copies.append(pltpu.make_async_copy(slot, rows, sems.at[buf]) if to_hbm
                      else pltpu.make_async_copy(rows, slot, sems.at[buf]))
    return copies


def _overflow_copy(meta_ref, spill, hbm, sem, n_exp, j, i, to_hbm):
    e = meta_ref[0, 0, n_exp + j]
    src = meta_ref[0, 0, 3 * n_exp + j] + ROW_ALIGN * i
    dst = meta_ref[0, 0, e] + SLOT_ROWS + ROW_ALIGN * i
    piece = spill.at[pl.ds(pl.multiple_of(src, ROW_ALIGN), ROW_ALIGN)]
    rows = hbm.at[pl.ds(pl.multiple_of(dst, ROW_ALIGN), ROW_ALIGN)]
    return pltpu.make_async_copy(piece, rows, sem) if to_hbm else pltpu.make_async_copy(rows, piece, sem)


def _for_overflow_pieces(meta_ref, n_exp, fn):
    def per_expert(j, carry):
        def per_piece(i, c2):
            fn(j, i)
            return c2
        return lax.fori_loop(0, meta_ref[0, 0, 2 * n_exp + j], per_piece, carry)
    lax.fori_loop(0, meta_ref[0, 0, 4 * n_exp], per_expert, 0)


def _spill_matrix_rows(meta_ref, rk_ref, wd_ref, base, ts, n_exp):
    rows = (lax.broadcasted_iota(I32, (SPILL_CHUNK, ts), 0) + base).astype(F32)

    def per_expert(j, hit):
        e = meta_ref[0, 0, n_exp + j]
        rk = rk_ref[pl.ds(e, 1), :]
        val = 1.0 if wd_ref is None else wd_ref[pl.ds(e, 1), :]
        target = jnp.where(rk >= SLOT_ROWS, rk - SLOT_ROWS + meta_ref[0, 0, 3 * n_exp + j].astype(F32), -1.0)
        return jnp.where(target == rows, val, hit)

    return lax.fori_loop(0, meta_ref[0, 0, 4 * n_exp], per_expert,
                         jnp.zeros((SPILL_CHUNK, ts), F32)).astype(BF16)


def _zero_fill(seg_ref, xs_hbm, stage, sem, n_exp, n_blocks):
    tail = SLOT_ROWS + EXPERT_BLOCK
    stage[0, 0:tail, :] = jnp.zeros((tail, stage.shape[2]), stage.dtype)
    tails = []
    for e in range(n_exp):
        start = jnp.maximum(seg_ref[0, 0, e] - tail, 0)
        tails.append(pltpu.make_async_copy(stage.at[0, pl.ds(0, tail)],
                                           xs_hbm.at[pl.ds(pl.multiple_of(start, ROW_ALIGN), tail)], sem))
    for cp in tails:
        cp.start()
    n_used = seg_ref[0, 0, n_exp]

    def block_copy(i):
        row = pl.multiple_of(i * EXPERT_BLOCK, EXPERT_BLOCK)
        return pltpu.make_async_copy(stage.at[0, pl.ds(0, EXPERT_BLOCK)], xs_hbm.at[pl.ds(row, EXPERT_BLOCK)], sem)

    def start_block(i, carry):
        block_copy(i).start()
        return carry

    def wait_block(i, carry):
        block_copy(i).wait()
        return carry

    lax.fori_loop(n_used, n_blocks, start_block, 0)
    for cp in tails:
        cp.wait()
    lax.fori_loop(n_used, n_blocks, wait_block, 0)


def _dispatch_kernel(meta_ref, seg_ref, x_ref, rk_ref, xs_hbm, stage, spill, sems, sem_ov, *,
                     ts, n_exp, n_blocks):
    i = pl.program_id(0)
    buf = i % 2

    @pl.when(i == 0)
    def _():
        _zero_fill(seg_ref, xs_hbm, stage, sem_ov, n_exp, n_blocks)

    x = x_ref[...]
    slot_row = lax.broadcasted_iota(I32, (SLOT_ROWS, ts), 0).astype(F32)
    group_rows = SLOT_GROUP * SLOT_ROWS
    for g in range(n_exp // SLOT_GROUP):
        pick = jnp.concatenate(
            [jnp.where(rk_ref[e:e + 1, :] == slot_row, 1.0, 0.0)
             for e in range(g * SLOT_GROUP, (g + 1) * SLOT_GROUP)], axis=0).astype(BF16)
        stage[buf, g * group_rows:(g + 1) * group_rows, :] = _dot(pick, x).astype(BF16)
    n_spill = meta_ref[0, 0, 4 * n_exp + 1]

    def spill_chunk(ci, carry):
        base = pl.multiple_of(ci * SPILL_CHUNK, SPILL_CHUNK)
        spill[pl.ds(base, SPILL_CHUNK), :] = _dot(
            _spill_matrix_rows(meta_ref, rk_ref, None, base, ts, n_exp), x).astype(BF16)
        return carry

    lax.fori_loop(0, n_spill, spill_chunk, 0)

    @pl.when(i > 0)
    def _():
        for cp in _slot_copies(meta_ref, stage, 1 - buf, xs_hbm, sems, n_exp, True):
            cp.wait()

    copies = _slot_copies(meta_ref, stage, buf, xs_hbm, sems, n_exp, True)
    for e, cp in enumerate(copies):
        cp.start(priority=e % 2)
    _for_overflow_pieces(meta_ref, n_exp,
                         lambda e, k: _overflow_copy(meta_ref, spill, xs_hbm, sem_ov, n_exp, e, k, True).start())
    _for_overflow_pieces(meta_ref, n_exp,
                         lambda e, k: _overflow_copy(meta_ref, spill, xs_hbm, sem_ov, n_exp, e, k, True).wait())

    @pl.when(i == pl.num_programs(0) - 1)
    def _():
        for cp in copies:
            cp.wait()


def _dispatch(h2, rank, meta, seg, p_rows, ts, n_exp):
    t_all, d = h2.shape
    nt = t_all // ts
    kern = functools.partial(_dispatch_kernel, ts=ts, n_exp=n_exp, n_blocks=p_rows // EXPERT_BLOCK)
    return pl.pallas_call(
        kern,
        grid=(nt,),
        in_specs=[
            pl.BlockSpec((1, 1, meta.shape[2]), lambda i: (i, 0, 0), memory_space=pltpu.SMEM),
            pl.BlockSpec((1, 1, seg.shape[2]), lambda i: (0, 0, 0), memory_space=pltpu.SMEM),
            pl.BlockSpec((ts, d), lambda i: (i, 0)),
            pl.BlockSpec((n_exp, ts), lambda i: (0, i)),
        ],
        out_specs=pl.BlockSpec(memory_space=pl.ANY),
        out_shape=jax.ShapeDtypeStruct((p_rows, d), BF16),
        scratch_shapes=[pltpu.VMEM((2, n_exp * SLOT_ROWS, d), BF16),
                        pltpu.VMEM((ts * TOP_K, d), BF16),
                        pltpu.SemaphoreType.DMA((2,)), pltpu.SemaphoreType.DMA(())],
        compiler_params=pltpu.CompilerParams(dimension_semantics=("arbitrary",), has_side_effects=True),
        name="dispatch",
    )(meta, seg, h2, rank)


def _expert_kernel(be_ref, nu_ref, xs_ref, wg_ref, wu_ref, wd_ref, ys_ref):
    del be_ref
    i = pl.program_id(0)

    @pl.when(i < nu_ref[0])
    def _():
        xb = xs_ref[...]
        a = _silu(_dot(xb, wg_ref[0])) * _dot(xb, wu_ref[0])
        ys_ref[...] = _dot(a.astype(BF16), wd_ref[0]).astype(BF16)


def _experts(xs, block_e, n_used, w_gate, w_up, w_down, blk):
    p_rows, dw = xs.shape
    n_exp, d, ff = w_gate.shape
    nb = p_rows // blk
    used = lambda i, nu: jnp.minimum(i, nu[0] - 1)
    grid_spec = pltpu.PrefetchScalarGridSpec(
        num_scalar_prefetch=2,
        grid=(nb,),
        in_specs=[
            pl.BlockSpec((blk, dw), lambda i, be, nu: (used(i, nu), 0)),
            pl.BlockSpec((1, d, ff), lambda i, be, nu: (be[used(i, nu)], 0, 0)),
            pl.BlockSpec((1, d, ff), lambda i, be, nu: (be[used(i, nu)], 0, 0)),
            pl.BlockSpec((1, ff, d), lambda i, be, nu: (be[used(i, nu)], 0, 0)),
        ],
        out_specs=pl.BlockSpec((blk, dw), lambda i, be, nu: (used(i, nu), 0)),
    )
    return pl.pallas_call(
        _expert_kernel,
        grid_spec=grid_spec,
        out_shape=jax.ShapeDtypeStruct((p_rows, dw), BF16),
        input_output_aliases={2: 0},
        compiler_params=pltpu.CompilerParams(dimension_semantics=("arbitrary",)),
        name="experts",
    )(block_e, n_used, xs, w_gate, w_up, w_down)


def _combine_kernel(meta_ref, nxt_ref, ys_hbm, rkt_ref, wdt_ref, rk_ref, wd_ref, ex_ref, rp_ref, x1_ref, h2_ref,
                    mod_ref, ng_ref, sg_ref, su_ref, sd_ref, o_ref, stage, spill, acc, sems, sem_ov, *,
                    ts, n_exp):
    i = pl.program_id(0)
    buf = i % 2
    n_spill = meta_ref[0, 0, 4 * n_exp + 1]
    n_groups = n_exp // SLOT_GROUP

    @pl.when(i == 0)
    def _():
        for e, cp in enumerate(_slot_copies(meta_ref, stage, 0, ys_hbm, sems, n_exp, False)):
            cp.start(priority=e % 2)

    @pl.when(i < pl.num_programs(0) - 1)
    def _():
        for e, cp in enumerate(_slot_copies(nxt_ref, stage, 1 - buf, ys_hbm, sems, n_exp, False)):
            cp.start(priority=e % 2)

    def clear_chunk(ci, carry):
        base = pl.multiple_of(ci * SPILL_CHUNK, SPILL_CHUNK)
        spill[pl.ds(base, SPILL_CHUNK), :] = jnp.zeros((SPILL_CHUNK, spill.shape[1]), spill.dtype)
        return carry

    lax.fori_loop(0, n_spill, clear_chunk, 0)
    _for_overflow_pieces(meta_ref, n_exp,
                         lambda e, k: _overflow_copy(meta_ref, spill, ys_hbm, sem_ov, n_exp, e, k, False).start())

    xb = h2_ref[...]
    a = _silu(_dot(xb, sg_ref[...])) * _dot(xb, su_ref[...])
    tot = _dot(a.astype(BF16), sd_ref[...])

    rank_lanes = _dot(rkt_ref[...].astype(BF16), ex_ref[...])
    weight_lanes = _dot(wdt_ref[...].astype(BF16), ex_ref[...])

    for cp in _slot_copies(meta_ref, stage, buf, ys_hbm, sems, n_exp, False):
        cp.wait()
    group_rows = SLOT_GROUP * SLOT_ROWS
    for g in range(n_groups):
        cols = slice(g * group_rows, (g + 1) * group_rows)
        unmix = jnp.where(rank_lanes[:, cols] == rp_ref[:, cols], weight_lanes[:, cols], 0.0).astype(BF16)
        tot = tot + _dot(unmix, stage[buf, g * group_rows:(g + 1) * group_rows, :])
    acc[...] = tot

    _for_overflow_pieces(meta_ref, n_exp,
                         lambda e, k: _overflow_copy(meta_ref, spill, ys_hbm, sem_ov, n_exp, e, k, False).wait())

    def spill_chunk(ci, carry):
        base = pl.multiple_of(ci * SPILL_CHUNK, SPILL_CHUNK)
        acc[...] += _dot_tn(_spill_matrix_rows(meta_ref, rk_ref, wd_ref, base, ts, n_exp),
                            spill[pl.ds(base, SPILL_CHUNK), :])
        return carry

    lax.fori_loop(0, n_spill, spill_chunk, 0)
    g2 = mod_ref[0, 5:6, :]
    o_ref[...] = x1_ref[...] + g2 * _rms(acc[...], ng_ref[3:4, :])


def _combine(ys, meta, rank_tm, wd_tm, rank, wd, x1_flat, h2, mod3, norm_g, ws_gate, ws_up, ws_down,
             n_seq, ts, n_exp):
    t_all, d = x1_flat.shape
    nt = t_all // ts
    per_b = n_seq // ts
    ff = ws_gate.shape[1]
    lanes = n_exp * SLOT_ROWS
    lane = lax.broadcasted_iota(I32, (n_exp, lanes), 1)
    expand = jnp.where(lane // SLOT_ROWS == lax.broadcasted_iota(I32, (n_exp, lanes), 0), 1.0, 0.0).astype(BF16)
    slot_rank = (jnp.arange(lanes, dtype=I32) % SLOT_ROWS).astype(F32).reshape(1, lanes)
    c2 = lambda i: (0, 0)
    kern = functools.partial(_combine_kernel, ts=ts, n_exp=n_exp)
    return pl.pallas_call(
        kern,
        grid=(nt,),
        in_specs=[
            pl.BlockSpec((1, 1, meta.shape[2]), lambda i: (i, 0, 0), memory_space=pltpu.SMEM),
            pl.BlockSpec((1, 1, meta.shape[2]), lambda i: (jnp.minimum(i + 1, nt - 1), 0, 0),
                         memory_space=pltpu.SMEM),
            pl.BlockSpec(memory_space=pl.ANY),
            pl.BlockSpec((ts, n_exp), lambda i: (i, 0)),
            pl.BlockSpec((ts, n_exp), lambda i: (i, 0)),
            pl.BlockSpec((n_exp, ts), lambda i: (0, i)),
            pl.BlockSpec((n_exp, ts), lambda i: (0, i)),
            pl.BlockSpec((n_exp, lanes), c2),
            pl.BlockSpec((1, lanes), c2),
            pl.BlockSpec((ts, d), lambda i: (i, 0)),
            pl.BlockSpec((ts, d), lambda i: (i, 0)),
            pl.BlockSpec((1, N_MOD, d), lambda i: (i // per_b, 0, 0)),
            pl.BlockSpec((4, d), c2),
            pl.BlockSpec((d, ff), c2),
            pl.BlockSpec((d, ff), c2),
            pl.BlockSpec((ff, d), c2),
        ],
        out_specs=pl.BlockSpec((ts, d), lambda i: (i, 0)),
        out_shape=jax.ShapeDtypeStruct((t_all, d), F32),
        scratch_shapes=[pltpu.VMEM((2, lanes, d), BF16), pltpu.VMEM((ts * TOP_K, d), BF16),
                        pltpu.VMEM((ts, d), F32),
                        pltpu.SemaphoreType.DMA((2,)), pltpu.SemaphoreType.DMA(())],
        compiler_params=pltpu.CompilerParams(dimension_semantics=("arbitrary",)),
        name="combine",
    )(meta, meta, ys, rank_tm, wd_tm, rank, wd, expand, slot_rank, x1_flat, h2, mod3, norm_g,
      ws_gate, ws_up, ws_down)


def _rope_tables(n):
    rows = jnp.repeat(jnp.arange(n // GRID_W, dtype=F32), GRID_W)
    cols = jnp.tile(jnp.arange(GRID_W, dtype=F32), n // GRID_W)
    quarter = LANES // 4
    freqs = ROPE_BASE ** (-jnp.arange(quarter, dtype=F32) / quarter)
    ang = jnp.concatenate([rows[:, None] * freqs, cols[:, None] * freqs], axis=-1)
    cos, sin = jnp.cos(ang), jnp.sin(ang)
    return jnp.concatenate([cos, cos], axis=-1), jnp.concatenate([-sin, sin], axis=-1)


def _retention_tables(log_decay):
    lg = -jnp.exp(log_decay.astype(F32))
    idx = jnp.arange(CHUNK, dtype=F32)
    rel = idx[:, None] - idx[None, :]
    lg3 = lg[:, :, None, None]
    intra_f = jnp.where(rel >= 0, jnp.exp(jnp.maximum(rel, 0.0) * lg3[0]), 0.0)
    intra_b = jnp.where(rel <= 0, jnp.exp(jnp.maximum(-rel, 0.0) * lg3[1]), 0.0)
    kd_f = jnp.exp((CHUNK - 1 - idx)[None, :] * lg[0][:, None])
    kd_b = jnp.exp(idx[None, :] * lg[1][:, None])
    qd_f = jnp.exp((idx + 1)[None, :] * lg[0][:, None])
    qd_b = jnp.exp((CHUNK - idx)[None, :] * lg[1][:, None])
    bc = lambda t: jnp.broadcast_to(t[:, :, None], (HEADS, CHUNK, LANES))
    intra = jnp.concatenate([intra_f, intra_b], axis=0)
    kd = jnp.concatenate([bc(kd_f), bc(kd_b)], axis=0)
    qd = jnp.concatenate([bc(qd_f), bc(qd_b)], axis=0)
    cd = jnp.broadcast_to(jnp.exp(CHUNK * lg).reshape(2 * HEADS, 1), (2 * HEADS, LANES))
    return intra, kd, qd, cd


def kernel(x, c, ctx, c_ctx, w_mod, b_mod, norm_g, w_in, ret_log_decay, ret_norm_g, mlstm_conv_w,
           mlstm_conv_b, mlstm_gate_b, mlstm_norm_g, w_out, w_router, router_bias, w_gate, w_up, w_down,
           ws_gate, ws_up, ws_down):
    b, n, d = x.shape
    n_ctx = ctx.shape[1]
    depth = w_mod.shape[0]
    assert depth == 1, "only the single-layer configuration is implemented"
    assert d // 2 // HEADS == LANES
    n_exp = w_router.shape[2]
    t_all = b * n
    r_w = d // 2
    main_cols = 8 * r_w
    l = 0

    pad = (-(b + 1)) % 8
    cc = jnp.concatenate([c, c_ctx[None, :], jnp.zeros((pad, d), F32)], axis=0)
    mod3 = _modulation(cc, w_mod[l], b_mod[l]).reshape(b + 1 + pad, N_MOD, d)

    w_groups = w_in[l, :, :main_cols].astype(BF16).reshape(d, 8, r_w)
    w_main = w_groups[:, jnp.array([0, 1, 2, 4, 5, 6, 3, 7]), :].reshape(d, main_cols)
    wg = w_in[l, :, main_cols:].astype(BF16)
    wgt = wg.T
    gb = mlstm_gate_b[l].reshape(-1).astype(F32)
    tabs = _retention_tables(ret_log_decay[l])
    head_g = jnp.concatenate([ret_norm_g[l], mlstm_norm_g[l]]).reshape(1, d).astype(F32)
    wr = w_router[l].T.astype(F32)
    wr_hi = wr.astype(BF16)
    wr_lo = (wr - wr_hi.astype(F32)).astype(BF16)

    def inproj(seq, mod_row, ts):
        cos2, sin2 = _rope_tables(n) if mod_row is None else (
            jnp.ones((seq.shape[1], LANES), F32), jnp.zeros((seq.shape[1], LANES), F32))
        return _inproj(seq, mod3, mod_row, norm_g[l, 0:1], w_main, wgt, wg, mlstm_conv_w[l],
                       mlstm_conv_b[l].reshape(1, -1), gb.reshape(16, 1), gb.reshape(1, 16), cos2, sin2, ts)

    nst = 2 * HEADS
    zero_states = (jnp.zeros((b, nst, LANES, LANES), F32), jnp.zeros((b, nst, LANES, 2 * LANES), F32),
                   jnp.zeros((b, nst, LANES), F32))
    p_c, gr_c, gc_c = inproj(ctx, b, min(n_ctx, 512))
    ctx_states = _scan(p_c, gr_c, gc_c, tabs, zero_states, with_output=False)

    ts = min(n, 512)
    p_l, gr_l, gc_l = inproj(x, None, ts)
    o_f, o_b, _, _, _ = _scan(p_l, gr_l, gc_l, tabs, tuple(ctx_states), with_output=True)
    ts_moe = MOE_TILE
    x1, h2, rank, wdense, tile_cnt = _post(
        x, o_f, o_b, p_l, mod3, norm_g[l], head_g, w_out[l].astype(BF16), wr_hi, wr_lo,
        router_bias[l].reshape(n_exp, 1).astype(F32), ts_moe)

    nt = t_all // ts_moe
    cnt = tile_cnt[:, 0, :].astype(I32)
    run_rows = (cnt + ROW_ALIGN - 1) // ROW_ALIGN * ROW_ALIGN
    seg_cap = (jnp.sum(run_rows, axis=0) + SLOT_ROWS + EXPERT_BLOCK - 1) // EXPERT_BLOCK * EXPERT_BLOCK
    seg_end = jnp.cumsum(seg_cap)
    run_start = (seg_end - seg_cap)[None, :] + jnp.cumsum(run_rows, axis=0) - run_rows
    ov_rows = jnp.maximum(run_rows - SLOT_ROWS, 0)
    ov_off = jnp.cumsum(ov_rows, axis=1) - ov_rows
    n_spill = (jnp.sum(ov_rows, axis=1, keepdims=True) + SPILL_CHUNK - 1) // SPILL_CHUNK
    spills = ov_rows > 0
    n_ov = jnp.sum(spills.astype(I32), axis=1, keepdims=True)
    nth = jnp.cumsum(spills.astype(I32), axis=1) - 1
    is_jth = spills[:, None, :] & (nth[:, None, :] == jnp.arange(n_exp, dtype=I32)[None, :, None])
    compact = lambda v: jnp.sum(jnp.where(is_jth, v[:, None, :], 0), axis=2)
    ov_e = compact(jnp.broadcast_to(jnp.arange(n_exp, dtype=I32)[None, :], cnt.shape))
    meta = jnp.concatenate([run_start, ov_e, compact(ov_rows // ROW_ALIGN), compact(ov_off), n_ov, n_spill],
                           axis=1).astype(I32)
    meta = jnp.pad(meta, ((0, 0), (0, (-meta.shape[1]) % LANES))).reshape(nt, 1, -1)
    p_rows = -(-(t_all * TOP_K + nt * n_exp * (ROW_ALIGN - 1) + n_exp * (SLOT_ROWS + EXPERT_BLOCK - 1))
               // EXPERT_BLOCK) * EXPERT_BLOCK
    nb = p_rows // EXPERT_BLOCK
    blk_first = jnp.arange(nb, dtype=I32) * EXPERT_BLOCK
    block_e = jnp.minimum(jnp.sum((seg_end[None, :] <= blk_first[:, None]).astype(I32), axis=1), n_exp - 1)
    n_used = (seg_end[-1:] // EXPERT_BLOCK).astype(I32)
    seg = jnp.concatenate([seg_end.astype(I32), n_used])
    seg = jnp.pad(seg, (0, (-seg.shape[0]) % LANES)).reshape(1, 1, -1)

    xs = _dispatch(h2, rank, meta, seg, p_rows, ts_moe, n_exp)
    ys = _experts(xs, block_e, n_used, w_gate[l].astype(BF16), w_up[l].astype(BF16), w_down[l].astype(BF16),
                  EXPERT_BLOCK)
    out = _combine(ys, meta, rank.T, wdense.T, rank, wdense, x1.reshape(t_all, d), h2, mod3, norm_g[l],
                   ws_gate[l].astype(BF16), ws_up[l].astype(BF16), ws_down[l].astype(BF16), n, ts_moe, n_exp)
    return out.reshape(b, n, d)
```

```python
import functools

import jax
import jax.numpy as jnp
from jax import lax
from jax.experimental import pallas as pl
from jax.experimental.pallas import tpu as pltpu

F32 = jnp.float32
BF16 = jnp.bfloat16
I32 = jnp.int32

EPS = 1e-6
LANES = 128
CHUNK = 128
HEADS = 4
GRID_W = 64
ROPE_BASE = 10000.0
N_GROUPS = 8
TOPK_GROUPS = 4
TOP_K = 8
ROUTED_SCALE = 2.5
N_MOD = 6
MOE_TILE = 256
SPILL_CHUNK = 64
ROW_ALIGN = 16
SLOT_ROWS = 48
SLOT_GROUP = 8
EXPERT_BLOCK = 512
NEG_INF = float("-inf")


def _sigmoid(v):
    return 1.0 / (1.0 + jnp.exp(-v))


def _silu(v):
    return v * _sigmoid(v)


def _log_sigmoid(v):
    return jnp.minimum(v, 0.0) - jnp.log(1.0 + jnp.exp(-jnp.abs(v)))


def _dot(a, b):
    return jnp.dot(a, b, preferred_element_type=F32)


def _dot_nt(a, b):
    return lax.dot_general(a, b, (((1,), (1,)), ((), ())), preferred_element_type=F32)


def _dot_tn(a, b):
    return lax.dot_general(a, b, (((0,), (0,)), ((), ())), preferred_element_type=F32)


def _split3(a):
    hi = a.astype(BF16)
    r = a - hi.astype(F32)
    mid = r.astype(BF16)
    lo = (r - mid.astype(F32)).astype(BF16)
    return hi, mid, lo


def _rms(v, g):
    ms = jnp.mean(v * v, axis=-1, keepdims=True)
    return v * lax.rsqrt(ms + EPS) * g


def _mod_kernel(c_ref, w_ref, b_ref, o_ref):
    a = _silu(c_ref[...])
    o_ref[...] = jnp.dot(a, w_ref[...], preferred_element_type=F32,
                         precision=lax.Precision.HIGHEST) + b_ref[...]


def _modulation(cc, w_mod, b_mod):
    rows, d = cc.shape
    cols = w_mod.shape[1]
    tn = d
    return pl.pallas_call(
        _mod_kernel,
        grid=(cols // tn,),
        in_specs=[pl.BlockSpec((rows, d), lambda j: (0, 0)),
                  pl.BlockSpec((d, tn), lambda j: (0, j)),
                  pl.BlockSpec((1, tn), lambda j: (0, j))],
        out_specs=pl.BlockSpec((rows, tn), lambda j: (0, j)),
        out_shape=jax.ShapeDtypeStruct((rows, cols), F32),
        name="mod",
    )(cc, w_mod, b_mod.reshape(1, cols))


def _inproj_kernel(x_ref, xp_ref, xn_ref, mod_ref, g_ref, w_ref, wgt_ref, cw_ref, cb_ref,
                   gbr_ref, cos_ref, sin_ref, p_ref, kt_ref, gr_ref, *, ts, d):
    i = pl.program_id(1)
    last = pl.num_programs(1) - 1
    r_w = d // 2
    shift = mod_ref[0, 0:1, :]
    scale = mod_ref[0, 1:2, :]
    g = g_ref[...]

    def normmod(v):
        return _rms(v, g) * (1.0 + scale) + shift

    hb = normmod(x_ref[0]).astype(BF16)
    halo = jnp.concatenate([xp_ref[0], xn_ref[0]], axis=0)
    ph = _dot(normmod(halo).astype(BF16), w_ref[:, 3 * r_w:5 * r_w])
    prev_row = jnp.where(i == 0, 0.0, ph[7:8, :])
    next_row = jnp.where(i == last, 0.0, ph[8:9, :])

    cos2 = cos_ref[...]
    sin2 = sin_ref[...]
    rows = lax.broadcasted_iota(I32, (ts, r_w), 0)
    qscale = LANES ** -0.5

    for j in range(8):
        acc = _dot(hb, w_ref[:, j * r_w:(j + 1) * r_w])
        if j in (0, 1):
            if j == 0:
                acc = acc * qscale
            parts = []
            for h in range(HEADS):
                t = acc[:, h * LANES:(h + 1) * LANES]
                parts.append(t * cos2 + pltpu.roll(t, LANES // 2, axis=1) * sin2)
            acc = jnp.concatenate(parts, axis=1)
        elif j in (3, 4):
            c0 = (j - 3) * r_w
            pr = prev_row[:, c0:c0 + r_w]
            nx = next_row[:, c0:c0 + r_w]
            down = jnp.where(rows == 0, pr, pltpu.roll(acc, 1, axis=0))
            up = jnp.where(rows == ts - 1, nx, pltpu.roll(acc, ts - 1, axis=0))
            cw = cw_ref[:, c0:c0 + r_w]
            acc = down * cw[0:1, :] + acc * cw[1:2, :] + up * cw[2:3, :] + cb_ref[:, c0:c0 + r_w]
            acc = _silu(acc)
            if j == 4:
                acc = acc * qscale
        p_ref[0, :, j * r_w:(j + 1) * r_w] = acc.astype(BF16)
        if j in (1, 4):
            kt_ref[0, (j // 4) * r_w:(j // 4 + 1) * r_w, :] = acc.T.astype(BF16)

    gr = _dot_nt(wgt_ref[...], hb) + gbr_ref[...]
    ch_r = lax.broadcasted_iota(I32, gr.shape, 0)
    gr_ref[0] = jnp.where((ch_r // HEADS) % 2 == 1, _log_sigmoid(gr), gr)


def _inproj(x, mod3, mod_row, g, w_main, wgt, conv_w, conv_b, gb_col, cos2, sin2, ts):
    b, n, d = x.shape
    nt = n // ts
    nb8 = n // 8
    hb = ts // 8
    cols = w_main.shape[1]
    if mod_row is None:
        mod_map = lambda bi, i: (bi, 0, 0)
    else:
        mod_map = lambda bi, i: (mod_row, 0, 0)
    const2 = lambda bi, i: (0, 0)
    kern = functools.partial(_inproj_kernel, ts=ts, d=d)
    return pl.pallas_call(
        kern,
        grid=(b, nt),
        in_specs=[
            pl.BlockSpec((1, ts, d), lambda bi, i: (bi, i, 0)),
            pl.BlockSpec((1, 8, d), lambda bi, i: (bi, jnp.maximum(i * hb - 1, 0), 0)),
            pl.BlockSpec((1, 8, d), lambda bi, i: (bi, jnp.minimum((i + 1) * hb, nb8 - 1), 0)),
            pl.BlockSpec((1, N_MOD, d), mod_map),
            pl.BlockSpec((1, d), const2),
            pl.BlockSpec((d, cols), const2),
            pl.BlockSpec((16, d), const2),
            pl.BlockSpec((3, d), const2),
            pl.BlockSpec((1, d), const2),
            pl.BlockSpec((16, 1), const2),
            pl.BlockSpec((ts, LANES), lambda bi, i: (i, 0)),
            pl.BlockSpec((ts, LANES), lambda bi, i: (i, 0)),
        ],
        out_specs=[
            pl.BlockSpec((1, ts, cols), lambda bi, i: (bi, i, 0)),
            pl.BlockSpec((1, d, ts), lambda bi, i: (bi, 0, i)),
            pl.BlockSpec((1, 16, ts), lambda bi, i: (bi, 0, i)),
        ],
        out_shape=[
            jax.ShapeDtypeStruct((b, n, cols), BF16),
            jax.ShapeDtypeStruct((b, d, n), BF16),
            jax.ShapeDtypeStruct((b, 16, n), F32),
        ],
        compiler_params=pltpu.CompilerParams(dimension_semantics=("parallel", "parallel")),
        name="inproj",
    )(x, x, x, mod3, g, w_main, wgt, conv_w, conv_b, gb_col, cos2, sin2)


def _scan_kernel(pf_ref, pb_ref, ktf_ref, ktb_ref, grf_ref, grb_ref, intra_ref, kd_ref, qd_ref, cd_ref,
                 rs0_ref, mc0_ref, mm0_ref, *out_refs, with_output, r_w):
    if with_output:
        of_ref, ob_ref, rs_ref, mc_ref, mm_ref = out_refs
    else:
        rs_ref, mc_ref, mm_ref = out_refs
    j = pl.program_id(1)

    @pl.when(j == 0)
    def _():
        rs_ref[...] = rs0_ref[...]
        mc_ref[...] = mc0_ref[...]
        mm_ref[...] = mm0_ref[...]

    c = CHUNK
    row = lax.broadcasted_iota(I32, (c, c), 0)
    col = lax.broadcasted_iota(I32, (c, c), 1)
    tri_le = (row <= col)
    tri_ge = (row >= col)
    eye = row == col
    ones_blk = jnp.ones((c, LANES), BF16)
    lane = lax.broadcasted_iota(I32, (HEADS, c), 1)

    def spread_rows(vecs):
        diag = jnp.concatenate([jnp.where(eye, v, 0.0) for v in vecs], axis=0)
        hi = diag.astype(BF16)
        lo = (diag - hi.astype(F32)).astype(BF16)
        out = _dot(hi, ones_blk) + _dot(lo, ones_blk)
        return [out[n * c:(n + 1) * c, :] for n in range(len(vecs))]

    def running_max(a, fwd):
        pm = a
        s = 1
        while s < c:
            if fwd:
                pm = jnp.where(lane >= s, jnp.maximum(pm, pltpu.roll(pm, s, axis=1)), pm)
            else:
                pm = jnp.where(lane < c - s, jnp.maximum(pm, pltpu.roll(pm, c - s, axis=1)), pm)
            s *= 2
        return pm

    n_st = 2 * HEADS
    rs_prev = [rs_ref[0, st] for st in range(n_st)]
    mc_prev = [mc_ref[0, st] for st in range(n_st)]
    mm_prev = [mm_ref[0, st:st + 1, 0:1] for st in range(n_st)]
    heads = [(dr, h) for dr in range(2) for h in range(HEADS)]
    o0 = 3 * r_w

    def cols(ref, base, h):
        return ref[0, :, base + h * LANES:base + (h + 1) * LANES]

    cs_rows = []
    for dr in range(2):
        gr = (grf_ref, grb_ref)[dr][0]
        tri = jnp.where(tri_le if dr == 0 else tri_ge, 1.0, 0.0).astype(BF16)
        cs_rows.append((gr, sum(_dot(piece, tri) for piece in _split3(gr))))
    ret_upd, ret_sc, ret_in, ml_sc, ml_in = [], [], [], [], []
    for dr, h in heads:
        st = dr * HEADS + h
        p_ref, kt_ref = (pf_ref, pb_ref)[dr], (ktf_ref, ktb_ref)[dr]
        ks = (kt_ref[0, h * LANES:(h + 1) * LANES, :].astype(F32) * kd_ref[st:st + 1, :]).astype(BF16)
        ret_upd.append(_dot(ks, cols(p_ref, 2 * r_w, h)))
        if with_output:
            q, mq = cols(p_ref, 0, h), cols(p_ref, o0, h)
            ret_sc.append(_dot_nt(q, cols(p_ref, r_w, h)))
            ret_in.append(_dot(q, rs_prev[st].astype(BF16)))
            ml_sc.append(_dot_nt(mq, cols(p_ref, o0 + r_w, h)))
            ml_in.append(_dot(mq, mc_prev[st].astype(BF16)))

    gate = []
    for dr in range(2):
        gr, cs_row = cs_rows[dr]
        g0 = dr * 2 * HEADS
        a_rows = gr[g0:g0 + HEADS, :] - cs_row[g0 + HEADS:g0 + 2 * HEADS, :]
        pm_rows = running_max(a_rows, dr == 0)
        last = c - 1 if dr == 0 else 0
        for h in range(HEADS):
            st = dr * HEADS + h
            b_row = cs_row[g0 + HEADS + h:g0 + HEADS + h + 1, :]
            a_row, pm_row = a_rows[h:h + 1, :], pm_rows[h:h + 1, :]
            b_tot = b_row[:, last:last + 1]
            m_prev = mm_prev[st]
            m_next = b_tot + jnp.maximum(m_prev, pm_row[:, last:last + 1])
            gate.append(dict(b_row=b_row, a_row=a_row, pm_row=pm_row, m_prev=m_prev, m_next=m_next,
                             decay_prev=jnp.exp(b_tot + m_prev - m_next),
                             ws_row=jnp.exp(b_tot + a_row - m_next)))
    ml_upd, spread = [], []
    for dr, h in heads:
        st = dr * HEADS + h
        p_ref, kt_ref = (pf_ref, pb_ref)[dr], (ktf_ref, ktb_ref)[dr]
        mkt = kt_ref[0, r_w + h * LANES:r_w + (h + 1) * LANES, :]
        kw = (mkt.astype(F32) * gate[st]["ws_row"]).astype(BF16)
        v_ext = jnp.concatenate([cols(p_ref, o0 + 2 * r_w, h), ones_blk], axis=1)
        ml_upd.append(_dot(kw, v_ext))
        if with_output:
            spread.append(spread_rows([gate[st]["b_row"], gate[st]["pm_row"]]))

    if with_output:
        ret_out, ml_out, stab = [], [], []
        for dr, h in heads:
            st = dr * HEADS + h
            p_ref = (pf_ref, pb_ref)[dr]
            sc = (ret_sc[st] * intra_ref[st]).astype(BF16)
            ret_out.append(_dot(sc, cols(p_ref, 2 * r_w, h)))
            b_sp, pm_sp = spread[st]
            c_sp = jnp.maximum(gate[st]["m_prev"], pm_sp)
            causal = tri_ge if dr == 0 else tri_le
            w = jnp.where(causal, jnp.exp(gate[st]["a_row"] - c_sp), 0.0)
            v_ext = jnp.concatenate([cols(p_ref, o0 + 2 * r_w, h), ones_blk], axis=1)
            ml_out.append(_dot((ml_sc[st] * w).astype(BF16), v_ext))
            stab.append((jnp.exp(gate[st]["m_prev"] - c_sp), jnp.exp(-(b_sp + c_sp))))

    for dr, h in heads:
        st = dr * HEADS + h
        if with_output:
            o_ref = (of_ref, ob_ref)[dr]
            o_ref[0, :, h * LANES:(h + 1) * LANES] = (ret_out[st] + qd_ref[st] * ret_in[st]).astype(BF16)
            inter, floor = stab[st]
            hx = ml_out[st] + jnp.concatenate([inter, inter], axis=1) * ml_in[st]
            hout = hx[:, :LANES] / jnp.maximum(jnp.abs(hx[:, LANES:]), floor)
            o_ref[0, :, r_w + h * LANES:r_w + (h + 1) * LANES] = hout.astype(BF16)
    for dr, h in heads:
        st = dr * HEADS + h
        rs_ref[0, st] = rs_prev[st] * cd_ref[st:st + 1, :] + ret_upd[st]
        mc_ref[0, st] = gate[st]["decay_prev"] * mc_prev[st] + ml_upd[st]
        mm_ref[0, st:st + 1, :] = jnp.broadcast_to(gate[st]["m_next"], (1, LANES))


def _scan(p, kt, g_row, tabs, states, with_output):
    b, n, cols = p.shape
    nch = n // CHUNK
    r_w = cols // 8
    intra, kd, qd, cd = tabs
    rs0, mc0, mm0 = states
    nst = 2 * HEADS
    fwd3 = lambda bi, j: (bi, j, 0)
    bwd3 = lambda bi, j: (bi, nch - 1 - j, 0)
    c3 = lambda bi, j: (0, 0, 0)
    st4 = lambda bi, j: (bi, 0, 0, 0)
    in_specs = [
        pl.BlockSpec((1, CHUNK, 6 * r_w), fwd3),
        pl.BlockSpec((1, CHUNK, 6 * r_w), bwd3),
        pl.BlockSpec((1, 2 * r_w, CHUNK), lambda bi, j: (bi, 0, j)),
        pl.BlockSpec((1, 2 * r_w, CHUNK), lambda bi, j: (bi, 0, nch - 1 - j)),
        pl.BlockSpec((1, 16, CHUNK), lambda bi, j: (bi, 0, j)),
        pl.BlockSpec((1, 16, CHUNK), lambda bi, j: (bi, 0, nch - 1 - j)),
        pl.BlockSpec((nst, CHUNK, LANES), c3),
        pl.BlockSpec((nst, CHUNK), lambda bi, j: (0, 0)),
        pl.BlockSpec((nst, CHUNK, LANES), c3),
        pl.BlockSpec((nst, LANES), lambda bi, j: (0, 0)),
        pl.BlockSpec((1, nst, LANES, LANES), st4),
        pl.BlockSpec((1, nst, LANES, 2 * LANES), st4),
        pl.BlockSpec((1, nst, LANES), lambda bi, j: (bi, 0, 0)),
    ]
    st_specs = [
        pl.BlockSpec((1, nst, LANES, LANES), st4),
        pl.BlockSpec((1, nst, LANES, 2 * LANES), st4),
        pl.BlockSpec((1, nst, LANES), lambda bi, j: (bi, 0, 0)),
    ]
    st_shapes = [
        jax.ShapeDtypeStruct((b, nst, LANES, LANES), F32),
        jax.ShapeDtypeStruct((b, nst, LANES, 2 * LANES), F32),
        jax.ShapeDtypeStruct((b, nst, LANES), F32),
    ]
    if with_output:
        out_specs = [pl.BlockSpec((1, CHUNK, 2 * r_w), fwd3), pl.BlockSpec((1, CHUNK, 2 * r_w), bwd3)] + st_specs
        out_shape = [jax.ShapeDtypeStruct((b, n, 2 * r_w), BF16)] * 2 + st_shapes
    else:
        out_specs, out_shape = st_specs, st_shapes
    kern = functools.partial(_scan_kernel, with_output=with_output, r_w=r_w)
    return pl.pallas_call(
        kern,
        grid=(b, nch),
        in_specs=in_specs,
        out_specs=out_specs,
        out_shape=out_shape,
        compiler_params=pltpu.CompilerParams(dimension_semantics=("parallel", "arbitrary")),
        name="scan_out" if with_output else "scan_state",
    )(p, p, kt, kt, g_row, g_row, intra, kd, qd, cd, rs0, mc0, mm0)


def _post_kernel(x_ref, of_ref, ob_ref, rg_ref, mo_ref, mod_ref, ng_ref, hg_ref, wo_ref, wrh_ref, wrl_ref,
                 rb_ref, su_ref, x1_ref, h2_ref, rk_ref, wd_ref, cnt_ref, *, ts, d, n_exp):
    s = of_ref[0].astype(F32) + ob_ref[0].astype(F32)
    parts = []
    for gi in range(2 * HEADS):
        sl = s[:, gi * LANES:(gi + 1) * LANES]
        mu = jnp.mean(sl, axis=-1, keepdims=True)
        dv = sl - mu
        var = jnp.mean(dv * dv, axis=-1, keepdims=True)
        y = dv * lax.rsqrt(var + EPS) * hg_ref[:, gi * LANES:(gi + 1) * LANES]
        if gi < HEADS:
            gate = _silu(rg_ref[0, :, gi * LANES:(gi + 1) * LANES].astype(F32))
        else:
            gate = _sigmoid(mo_ref[0, :, (gi - HEADS) * LANES:(gi - HEADS + 1) * LANES].astype(F32))
        parts.append((y * gate).astype(BF16))
    mixed = jnp.concatenate(parts, axis=1)
    y = _dot(mixed, wo_ref[...])
    g1 = mod_ref[0, 2:3, :]
    sh2 = mod_ref[0, 3:4, :]
    sc2 = mod_ref[0, 4:5, :]
    x1 = x_ref[0] + g1 * _rms(y, ng_ref[1:2, :])
    x1_ref[0] = x1
    h2 = _rms(x1, ng_ref[2:3, :]) * (1.0 + sc2) + sh2
    h_hi = h2.astype(BF16)
    h2_ref[...] = h_hi

    h_lo = (h2 - h_hi.astype(F32)).astype(BF16)
    logits = _dot_nt(wrh_ref[...], h_hi) + _dot_nt(wrh_ref[...], h_lo) + _dot_nt(wrl_ref[...], h_hi)
    scores = _sigmoid(logits)
    sel = scores + rb_ref[...]
    gsz = n_exp // N_GROUPS
    iota_g = lax.broadcasted_iota(I32, (gsz, ts), 0).astype(F32)
    grp = []
    for gi in range(N_GROUPS):
        blk = sel[gi * gsz:(gi + 1) * gsz, :]
        m1 = jnp.max(blk, axis=0, keepdims=True)
        i1 = jnp.min(jnp.where(blk == m1, iota_g, float(gsz)), axis=0, keepdims=True)
        m2 = jnp.max(jnp.where(iota_g == i1, NEG_INF, blk), axis=0, keepdims=True)
        grp.append(m1 + m2)
    masked_parts = []
    for gi in range(N_GROUPS):
        rank = jnp.zeros((1, ts), F32)
        for gj in range(N_GROUPS):
            if gj == gi:
                continue
            beats = (grp[gj] >= grp[gi]) if gj < gi else (grp[gj] > grp[gi])
            rank = rank + jnp.where(beats, 1.0, 0.0)
        keep = rank < float(TOPK_GROUPS)
        masked_parts.append(jnp.where(keep, sel[gi * gsz:(gi + 1) * gsz, :], NEG_INF))
    masked = jnp.concatenate(masked_parts, axis=0)

    iota_e = lax.broadcasted_iota(I32, (n_exp, ts), 0).astype(F32)
    selmask = jnp.zeros((n_exp, ts), F32)
    for _ in range(TOP_K):
        mx = jnp.max(masked, axis=0, keepdims=True)
        ei = jnp.min(jnp.where(masked == mx, iota_e, float(n_exp)), axis=0, keepdims=True)
        hit = iota_e == ei
        selmask = jnp.where(hit, 1.0, selmask)
        masked = jnp.where(hit, NEG_INF, masked)
    picked = selmask > 0.0
    wsel = jnp.where(picked, scores, 0.0)
    wd_ref[...] = wsel / jnp.sum(wsel, axis=0, keepdims=True) * ROUTED_SCALE
    rank = _dot(selmask.astype(BF16), su_ref[...])
    rk_ref[...] = jnp.where(picked, rank, -1.0)
    cnt_ref[0] = _dot_nt(jnp.ones((8, ts), BF16), selmask.astype(BF16))


def _post(x, o_f, o_b, p, mod3, norm_g, head_g, w_out, wr_hi, wr_lo, rbias, ts):
    b, n, d = x.shape
    nt = n // ts
    t_all = b * n
    n_exp = wr_hi.shape[0]
    r_w = d // 2
    su = jnp.where(lax.broadcasted_iota(I32, (ts, ts), 0) < lax.broadcasted_iota(I32, (ts, ts), 1),
                   1.0, 0.0).astype(BF16)
    tok3 = lambda bi, i: (bi, i, 0)
    c2 = lambda bi, i: (0, 0)
    flat = lambda bi, i: (0, bi * nt + i)
    kern = functools.partial(_post_kernel, ts=ts, d=d, n_exp=n_exp)
    return pl.pallas_call(
        kern,
        grid=(b, nt),
        in_specs=[
            pl.BlockSpec((1, ts, d), tok3),
            pl.BlockSpec((1, ts, d), tok3),
            pl.BlockSpec((1, ts, d), tok3),
            pl.BlockSpec((1, ts, r_w), lambda bi, i: (bi, i, 6)),
            pl.BlockSpec((1, ts, r_w), lambda bi, i: (bi, i, 7)),
            pl.BlockSpec((1, N_MOD, d), lambda bi, i: (bi, 0, 0)),
            pl.BlockSpec((4, d), c2),
            pl.BlockSpec((1, d), c2),
            pl.BlockSpec((d, d), c2),
            pl.BlockSpec((n_exp, d), c2),
            pl.BlockSpec((n_exp, d), c2),
            pl.BlockSpec((n_exp, 1), c2),
            pl.BlockSpec((ts, ts), c2),
        ],
        out_specs=[
            pl.BlockSpec((1, ts, d), tok3),
            pl.BlockSpec((ts, d), lambda bi, i: (bi * nt + i, 0)),
            pl.BlockSpec((n_exp, ts), flat),
            pl.BlockSpec((n_exp, ts), flat),
            pl.BlockSpec((1, 8, n_exp), lambda bi, i: (bi * nt + i, 0, 0)),
        ],
        out_shape=[
            jax.ShapeDtypeStruct((b, n, d), F32),
            jax.ShapeDtypeStruct((t_all, d), BF16),
            jax.ShapeDtypeStruct((n_exp, t_all), F32),
            jax.ShapeDtypeStruct((n_exp, t_all), F32),
            jax.ShapeDtypeStruct((b * nt, 8, n_exp), F32),
        ],
        compiler_params=pltpu.CompilerParams(dimension_semantics=("parallel", "parallel")),
        name="post",
    )(x, o_f, o_b, p, p, mod3, norm_g, head_g, w_out, wr_hi, wr_lo, rbias, su)


def _slot_copies(meta_ref, stage, buf, hbm, sems, n_exp, to_hbm):
    copies = []
    for e in range(n_exp):
        rows = hbm.at[pl.ds(pl.multiple_of(meta_ref[0, 0, e], ROW_ALIGN), SLOT_ROWS)]
        slot = stage.at[buf, pl.ds(e * SLOT_ROWS, SLOT_ROWS)]
        copies.append(pltpu.make_async_copy(slot, rows, sems.at[buf]) if to_hbm
                      else pltpu.make_async_copy(rows, slot, sems.at[buf]))
    return copies


def _overflow_copy(meta_ref, spill, hbm, sem, n_exp, j, i, to_hbm):
    e = meta_ref[0, 0, n_exp + j]
    src = meta_ref[0, 0, 3 * n_exp + j] + ROW_ALIGN * i
    dst = meta_ref[0, 0, e] + SLOT_ROWS + ROW_ALIGN * i
    piece = spill.at[pl.ds(pl.multiple_of(src, ROW_ALIGN), ROW_ALIGN)]
    rows = hbm.at[pl.ds(pl.multiple_of(dst, ROW_ALIGN), ROW_ALIGN)]
    return pltpu.make_async_copy(piece, rows, sem) if to_hbm else pltpu.make_async_copy(rows, piece, sem)


def _for_overflow_pieces(meta_ref, n_exp, fn):
    def per_expert(j, carry):
        def per_piece(i, c2):
            fn(j, i)
            return c2
        return lax.fori_loop(0, meta_ref[0, 0, 2 * n_exp + j], per_piece, carry)
    lax.fori_loop(0, meta_ref[0, 0, 4 * n_exp], per_expert, 0)


def _spill_matrix_rows(meta_ref, rk_ref, wd_ref, base, ts, n_exp):
    rows = (lax.broadcasted_iota(I32, (SPILL_CHUNK, ts), 0) + base).astype(F32)

    def per_expert(j, hit):
        e = meta_ref[0, 0, n_exp + j]
        rk = rk_ref[pl.ds(e, 1), :]
        val = 1.0 if wd_ref is None else wd_ref[pl.ds(e, 1), :]
        target = jnp.where(rk >= SLOT_ROWS, rk - SLOT_ROWS + meta_ref[0, 0, 3 * n_exp + j].astype(F32), -1.0)
        return jnp.where(target == rows, val, hit)

    return lax.fori_loop(0, meta_ref[0, 0, 4 * n_exp], per_expert,
                         jnp.zeros((SPILL_CHUNK, ts), F32)).astype(BF16)


def _zero_fill(seg_ref, xs_hbm, stage, sem, n_exp, n_blocks):
    tail = SLOT_ROWS + EXPERT_BLOCK
    stage[0, 0:tail, :] = jnp.zeros((tail, stage.shape[2]), stage.dtype)
    tails = []
    for e in range(n_exp):
        start = jnp.maximum(seg_ref[0, 0, e] - tail, 0)
        tails.append(pltpu.make_async_copy(stage.at[0, pl.ds(0, tail)],
                                           xs_hbm.at[pl.ds(pl.multiple_of(start, ROW_ALIGN), tail)], sem))
    for cp in tails:
        cp.start()
    n_used = seg_ref[0, 0, n_exp]

    def block_copy(i):
        row = pl.multiple_of(i * EXPERT_BLOCK, EXPERT_BLOCK)
        return pltpu.make_async_copy(stage.at[0, pl.ds(0, EXPERT_BLOCK)], xs_hbm.at[pl.ds(row, EXPERT_BLOCK)], sem)

    def start_block(i, carry):
        block_copy(i).start()
        return carry

    def wait_block(i, carry):
        block_copy(i).wait()
        return carry

    lax.fori_loop(n_used, n_blocks, start_block, 0)
    for cp in tails:
        cp.wait()
    lax.fori_loop(n_used, n_blocks, wait_block, 0)


def _dispatch_kernel(meta_ref, seg_ref, x_ref, rk_ref, xs_hbm, stage, spill, sems, sem_ov, *,
                     ts, n_exp, n_blocks):
    i = pl.program_id(0)
    buf = i % 2

    @pl.when(i == 0)
    def _():
        _zero_fill(seg_ref, xs_hbm, stage, sem_ov, n_exp, n_blocks)

    x = x_ref[...]
    slot_row = lax.broadcasted_iota(I32, (SLOT_ROWS, ts), 0).astype(F32)
    group_rows = SLOT_GROUP * SLOT_ROWS
    for g in range(n_exp // SLOT_GROUP):
        pick = jnp.concatenate(
            [jnp.where(rk_ref[e:e + 1, :] == slot_row, 1.0, 0.0)
             for e in range(g * SLOT_GROUP, (g + 1) * SLOT_GROUP)], axis=0).astype(BF16)
        stage[buf, g * group_rows:(g + 1) * group_rows, :] = _dot(pick, x).astype(BF16)
    n_spill = meta_ref[0, 0, 4 * n_exp + 1]

    def spill_chunk(ci, carry):
        base = pl.multiple_of(ci * SPILL_CHUNK, SPILL_CHUNK)
        spill[pl.ds(base, SPILL_CHUNK), :] = _dot(
            _spill_matrix_rows(meta_ref, rk_ref, None, base, ts, n_exp), x).astype(BF16)
        return carry

    lax.fori_loop(0, n_spill, spill_chunk, 0)

    @pl.when(i > 0)
    def _():
        for cp in _slot_copies(meta_ref, stage, 1 - buf, xs_hbm, sems, n_exp, True):
            cp.wait()

    copies = _slot_copies(meta_ref, stage, buf, xs_hbm, sems, n_exp, True)
    for e, cp in enumerate(copies):
        cp.start(priority=e % 2)
    _for_overflow_pieces(meta_ref, n_exp,
                         lambda e, k: _overflow_copy(meta_ref, spill, xs_hbm, sem_ov, n_exp, e, k, True).start())
    _for_overflow_pieces(meta_ref, n_exp,
                         lambda e, k: _overflow_copy(meta_ref, spill, xs_hbm, sem_ov, n_exp, e, k, True).wait())

    @pl.when(i == pl.num_programs(0) - 1)
    def _():
        for cp in copies:
            cp.wait()


def _dispatch(h2, rank, meta, seg, p_rows, ts, n_exp):
    t_all, d = h2.shape
    nt = t_all // ts
    kern = functools.partial(_dispatch_kernel, ts=ts, n_exp=n_exp, n_blocks=p_rows // EXPERT_BLOCK)
    return pl.pallas_call(
        kern,
        grid=(nt,),
        in_specs=[
            pl.BlockSpec((1, 1, meta.shape[2]), lambda i: (i, 0, 0), memory_space=pltpu.SMEM),
            pl.BlockSpec((1, 1, seg.shape[2]), lambda i: (0, 0, 0), memory_space=pltpu.SMEM),
            pl.BlockSpec((ts, d), lambda i: (i, 0)),
            pl.BlockSpec((n_exp, ts), lambda i: (0, i)),
        ],
        out_specs=pl.BlockSpec(memory_space=pl.ANY),
        out_shape=jax.ShapeDtypeStruct((p_rows, d), BF16),
        scratch_shapes=[pltpu.VMEM((2, n_exp * SLOT_ROWS, d), BF16),
                        pltpu.VMEM((ts * TOP_K, d), BF16),
                        pltpu.SemaphoreType.DMA((2,)), pltpu.SemaphoreType.DMA(())],
        compiler_params=pltpu.CompilerParams(dimension_semantics=("arbitrary",), has_side_effects=True),
        name="dispatch",
    )(meta, seg, h2, rank)


def _expert_kernel(be_ref, nu_ref, xs_ref, wg_ref, wu_ref, wd_ref, ys_ref):
    del be_ref
    i = pl.program_id(0)

    @pl.when(i < nu_ref[0])
    def _():
        xb = xs_ref[...]
        a = _silu(_dot(xb, wg_ref[0])) * _dot(xb, wu_ref[0])
        ys_ref[...] = _dot(a.astype(BF16), wd_ref[0]).astype(BF16)


def _experts(xs, block_e, n_used, w_gate, w_up, w_down, blk):
    p_rows, dw = xs.shape
    n_exp, d, ff = w_gate.shape
    nb = p_rows // blk
    used = lambda i, nu: jnp.minimum(i, nu[0] - 1)
    grid_spec = pltpu.PrefetchScalarGridSpec(
        num_scalar_prefetch=2,
        grid=(nb,),
        in_specs=[
            pl.BlockSpec((blk, dw), lambda i, be, nu: (used(i, nu), 0)),
            pl.BlockSpec((1, d, ff), lambda i, be, nu: (be[used(i, nu)], 0, 0)),
            pl.BlockSpec((1, d, ff), lambda i, be, nu: (be[used(i, nu)], 0, 0)),
            pl.BlockSpec((1, ff, d), lambda i, be, nu: (be[used(i, nu)], 0, 0)),
        ],
        out_specs=pl.BlockSpec((blk, dw), lambda i, be, nu: (used(i, nu), 0)),
    )
    return pl.pallas_call(
        _expert_kernel,
        grid_spec=grid_spec,
        out_shape=jax.ShapeDtypeStruct((p_rows, dw), BF16),
        input_output_aliases={2: 0},
        compiler_params=pltpu.CompilerParams(dimension_semantics=("arbitrary",)),
        name="experts",
    )(block_e, n_used, xs, w_gate, w_up, w_down)


def _combine_kernel(meta_ref, nxt_ref, ys_hbm, rkt_ref, wdt_ref, rk_ref, wd_ref, ex_ref, rp_ref, x1_ref, h2_ref,
                    mod_ref, ng_ref, sg_ref, su_ref, sd_ref, o_ref, stage, spill, acc, sems, sem_ov, *,
                    ts, n_exp):
    i = pl.program_id(0)
    buf = i % 2
    n_spill = meta_ref[0, 0, 4 * n_exp + 1]
    n_groups = n_exp // SLOT_GROUP

    @pl.when(i == 0)
    def _():
        for e, cp in enumerate(_slot_copies(meta_ref, stage, 0, ys_hbm, sems, n_exp, False)):
            cp.start(priority=e % 2)

    @pl.when(i < pl.num_programs(0) - 1)
    def _():
        for e, cp in enumerate(_slot_copies(nxt_ref, stage, 1 - buf, ys_hbm, sems, n_exp, False)):
            cp.start(priority=e % 2)

    def clear_chunk(ci, carry):
        base = pl.multiple_of(ci * SPILL_CHUNK, SPILL_CHUNK)
        spill[pl.ds(base, SPILL_CHUNK), :] = jnp.zeros((SPILL_CHUNK, spill.shape[1]), spill.dtype)
        return carry

    lax.fori_loop(0, n_spill, clear_chunk, 0)
    _for_overflow_pieces(meta_ref, n_exp,
                         lambda e, k: _overflow_copy(meta_ref, spill, ys_hbm, sem_ov, n_exp, e, k, False).start())

    xb = h2_ref[...]
    a = _silu(_dot(xb, sg_ref[...])) * _dot(xb, su_ref[...])
    tot = _dot(a.astype(BF16), sd_ref[...])

    rank_lanes = _dot(rkt_ref[...].astype(BF16), ex_ref[...])
    weight_lanes = _dot(wdt_ref[...].astype(BF16), ex_ref[...])

    for cp in _slot_copies(meta_ref, stage, buf, ys_hbm, sems, n_exp, False):
        cp.wait()
    group_rows = SLOT_GROUP * SLOT_ROWS
    for g in range(n_groups):
        cols = slice(g * group_rows, (g + 1) * group_rows)
        unmix = jnp.where(rank_lanes[:, cols] == rp_ref[:, cols], weight_lanes[:, cols], 0.0).astype(BF16)
        tot = tot + _dot(unmix, stage[buf, g * group_rows:(g + 1) * group_rows, :])
    acc[...] = tot

    _for_overflow_pieces(meta_ref, n_exp,
                         lambda e, k: _overflow_copy(meta_ref, spill, ys_hbm, sem_ov, n_exp, e, k, False).wait())

    def spill_chunk(ci, carry):
        base = pl.multiple_of(ci * SPILL_CHUNK, SPILL_CHUNK)
        acc[...] += _dot_tn(_spill_matrix_rows(meta_ref, rk_ref, wd_ref, base, ts, n_exp),
                            spill[pl.ds(base, SPILL_CHUNK), :])
        return carry

    lax.fori_loop(0, n_spill, spill_chunk, 0)
    g2 = mod_ref[0, 5:6, :]
    o_ref[...] = x1_ref[...] + g2 * _rms(acc[...], ng_ref[3:4, :])


def _combine(ys, meta, rank_tm, wd_tm, rank, wd, x1_flat, h2, mod3, norm_g, ws_gate, ws_up, ws_down,
             n_seq, ts, n_exp):
    t_all, d = x1_flat.shape
    nt = t_all // ts
    per_b = n_seq // ts
    ff = ws_gate.shape[1]
    lanes = n_exp * SLOT_ROWS
    lane = lax.broadcasted_iota(I32, (n_exp, lanes), 1)
    expand = jnp.where(lane // SLOT_ROWS == lax.broadcasted_iota(I32, (n_exp, lanes), 0), 1.0, 0.0).astype(BF16)
    slot_rank = (jnp.arange(lanes, dtype=I32) % SLOT_ROWS).astype(F32).reshape(1, lanes)
    c2 = lambda i: (0, 0)
    kern = functools.partial(_combine_kernel, ts=ts, n_exp=n_exp)
    return pl.pallas_call(
        kern,
        grid=(nt,),
        in_specs=[
            pl.BlockSpec((1, 1, meta.shape[2]), lambda i: (i, 0, 0), memory_space=pltpu.SMEM),
            pl.BlockSpec((1, 1, meta.shape[2]), lambda i: (jnp.minimum(i + 1, nt - 1), 0, 0),
                         memory_space=pltpu.SMEM),
            pl.BlockSpec(memory_space=pl.ANY),
            pl.BlockSpec((ts, n_exp), lambda i: (i, 0)),
            pl.BlockSpec((ts, n_exp), lambda i: (i, 0)),
            pl.BlockSpec((n_exp, ts), lambda i: (0, i)),
            pl.BlockSpec((n_exp, ts), lambda i: (0, i)),
            pl.BlockSpec((n_exp, lanes), c2),
            pl.BlockSpec((1, lanes), c2),
            pl.BlockSpec((ts, d), lambda i: (i, 0)),
            pl.BlockSpec((ts, d), lambda i: (i, 0)),
            pl.BlockSpec((1, N_MOD, d), lambda i: (i // per_b, 0, 0)),
            pl.BlockSpec((4, d), c2),
            pl.BlockSpec((d, ff), c2),
            pl.BlockSpec((d, ff), c2),
            pl.BlockSpec((ff, d), c2),
        ],
        out_specs=pl.BlockSpec((ts, d), lambda i: (i, 0)),
        out_shape=jax.ShapeDtypeStruct((t_all, d), F32),
        scratch_shapes=[pltpu.VMEM((2, lanes, d), BF16), pltpu.VMEM((ts * TOP_K, d), BF16),
                        pltpu.VMEM((ts, d), F32),
                        pltpu.SemaphoreType.DMA((2,)), pltpu.SemaphoreType.DMA(())],
        compiler_params=pltpu.CompilerParams(dimension_semantics=("arbitrary",)),
        name="combine",
    )(meta, meta, ys, rank_tm, wd_tm, rank, wd, expand, slot_rank, x1_flat, h2, mod3, norm_g,
      ws_gate, ws_up, ws_down)


def _rope_tables(n):
    rows = jnp.repeat(jnp.arange(n // GRID_W, dtype=F32), GRID_W)
    cols = jnp.tile(jnp.arange(GRID_W, dtype=F32), n // GRID_W)
    quarter = LANES // 4
    freqs = ROPE_BASE ** (-jnp.arange(quarter, dtype=F32) / quarter)
    ang = jnp.concatenate([rows[:, None] * freqs, cols[:, None] * freqs], axis=-1)
    cos, sin = jnp.cos(ang), jnp.sin(ang)
    return jnp.concatenate([cos, cos], axis=-1), jnp.concatenate([-sin, sin], axis=-1)


def _retention_tables(log_decay):
    lg = -jnp.exp(log_decay.astype(F32))
    idx = jnp.arange(CHUNK, dtype=F32)
    rel = idx[:, None] - idx[None, :]
    lg3 = lg[:, :, None, None]
    intra_f = jnp.where(rel >= 0, jnp.exp(jnp.maximum(rel, 0.0) * lg3[0]), 0.0)
    intra_b = jnp.where(rel <= 0, jnp.exp(jnp.maximum(-rel, 0.0) * lg3[1]), 0.0)
    kd_f = jnp.exp((CHUNK - 1 - idx)[None, :] * lg[0][:, None])
    kd_b = jnp.exp(idx[None, :] * lg[1][:, None])
    qd_f = jnp.exp((idx + 1)[None, :] * lg[0][:, None])
    qd_b = jnp.exp((CHUNK - idx)[None, :] * lg[1][:, None])
    bc = lambda t: jnp.broadcast_to(t[:, :, None], (HEADS, CHUNK, LANES))
    intra = jnp.concatenate([intra_f, intra_b], axis=0)
    kd = jnp.concatenate([kd_f, kd_b], axis=0)
    qd = jnp.concatenate([bc(qd_f), bc(qd_b)], axis=0)
    cd = jnp.broadcast_to(jnp.exp(CHUNK * lg).reshape(2 * HEADS, 1), (2 * HEADS, LANES))
    return intra, kd, qd, cd


def kernel(x, c, ctx, c_ctx, w_mod, b_mod, norm_g, w_in, ret_log_decay, ret_norm_g, mlstm_conv_w,
           mlstm_conv_b, mlstm_gate_b, mlstm_norm_g, w_out, w_router, router_bias, w_gate, w_up, w_down,
           ws_gate, ws_up, ws_down):
    b, n, d = x.shape
    n_ctx = ctx.shape[1]
    depth = w_mod.shape[0]
    assert depth == 1, "only the single-layer configuration is implemented"
    assert d // 2 // HEADS == LANES
    n_exp = w_router.shape[2]
    t_all = b * n
    r_w = d // 2
    main_cols = 8 * r_w
    l = 0

    pad = (-(b + 1)) % 8
    cc = jnp.concatenate([c, c_ctx[None, :], jnp.zeros((pad, d), F32)], axis=0)
    mod3 = _modulation(cc, w_mod[l], b_mod[l]).reshape(b + 1 + pad, N_MOD, d)

    w_groups = w_in[l, :, :main_cols].astype(BF16).reshape(d, 8, r_w)
    w_main = w_groups[:, jnp.array([0, 1, 2, 4, 5, 6, 3, 7]), :].reshape(d, main_cols)
    wg = w_in[l, :, main_cols:].astype(BF16)
    wgt = wg.T
    gb = mlstm_gate_b[l].reshape(-1).astype(F32)
    tabs = _retention_tables(ret_log_decay[l])
    head_g = jnp.concatenate([ret_norm_g[l], mlstm_norm_g[l]]).reshape(1, d).astype(F32)
    wr = w_router[l].T.astype(F32)
    wr_hi = wr.astype(BF16)
    wr_lo = (wr - wr_hi.astype(F32)).astype(BF16)

    def inproj(seq, mod_row, ts):
        cos2, sin2 = _rope_tables(n) if mod_row is None else (
            jnp.ones((seq.shape[1], LANES), F32), jnp.zeros((seq.shape[1], LANES), F32))
        return _inproj(seq, mod3, mod_row, norm_g[l, 0:1], w_main, wgt, mlstm_conv_w[l],
                       mlstm_conv_b[l].reshape(1, -1), gb.reshape(16, 1), cos2, sin2, ts)

    nst = 2 * HEADS
    zero_states = (jnp.zeros((b, nst, LANES, LANES), F32), jnp.zeros((b, nst, LANES, 2 * LANES), F32),
                   jnp.zeros((b, nst, LANES), F32))
    p_c, kt_c, gr_c = inproj(ctx, b, min(n_ctx, 512))
    ctx_states = _scan(p_c, kt_c, gr_c, tabs, zero_states, with_output=False)

    ts = min(n, 512)
    p_l, kt_l, gr_l = inproj(x, None, ts)
    o_f, o_b, _, _, _ = _scan(p_l, kt_l, gr_l, tabs, tuple(ctx_states), with_output=True)
    ts_moe = MOE_TILE
    x1, h2, rank, wdense, tile_cnt = _post(
        x, o_f, o_b, p_l, mod3, norm_g[l], head_g, w_out[l].astype(BF16), wr_hi, wr_lo,
        router_bias[l].reshape(n_exp, 1).astype(F32), ts_moe)

    nt = t_all // ts_moe
    cnt = tile_cnt[:, 0, :].astype(I32)
    run_rows = (cnt + ROW_ALIGN - 1) // ROW_ALIGN * ROW_ALIGN
    seg_cap = (jnp.sum(run_rows, axis=0) + SLOT_ROWS + EXPERT_BLOCK - 1) // EXPERT_BLOCK * EXPERT_BLOCK
    seg_end = jnp.cumsum(seg_cap)
    run_start = (seg_end - seg_cap)[None, :] + jnp.cumsum(run_rows, axis=0) - run_rows
    ov_rows = jnp.maximum(run_rows - SLOT_ROWS, 0)
    ov_off = jnp.cumsum(ov_rows, axis=1) - ov_rows
    n_spill = (jnp.sum(ov_rows, axis=1, keepdims=True) + SPILL_CHUNK - 1) // SPILL_CHUNK
    spills = ov_rows > 0
    n_ov = jnp.sum(spills.astype(I32), axis=1, keepdims=True)
    nth = jnp.cumsum(spills.astype(I32), axis=1) - 1
    is_jth = spills[:, None, :] & (nth[:, None, :] == jnp.arange(n_exp, dtype=I32)[None, :, None])
    compact = lambda v: jnp.sum(jnp.where(is_jth, v[:, None, :], 0), axis=2)
    ov_e = compact(jnp.broadcast_to(jnp.arange(n_exp, dtype=I32)[None, :], cnt.shape))
    meta = jnp.concatenate([run_start, ov_e, compact(ov_rows // ROW_ALIGN), compact(ov_off), n_ov, n_spill],
                           axis=1).astype(I32)
    meta = jnp.pad(meta, ((0, 0), (0, (-meta.shape[1]) % LANES))).reshape(nt, 1, -1)
    p_rows = -(-(t_all * TOP_K + nt * n_exp * (ROW_ALIGN - 1) + n_exp * (SLOT_ROWS + EXPERT_BLOCK - 1))
               // EXPERT_BLOCK) * EXPERT_BLOCK
    nb = p_rows // EXPERT_BLOCK
    blk_first = jnp.arange(nb, dtype=I32) * EXPERT_BLOCK
    block_e = jnp.minimum(jnp.sum((seg_end[None, :] <= blk_first[:, None]).astype(I32), axis=1), n_exp - 1)
    n_used = (seg_end[-1:] // EXPERT_BLOCK).astype(I32)
    seg = jnp.concatenate([seg_end.astype(I32), n_used])
    seg = jnp.pad(seg, (0, (-seg.shape[0]) % LANES)).reshape(1, 1, -1)

    xs = _dispatch(h2, rank, meta, seg, p_rows, ts_moe, n_exp)
    ys = _experts(xs, block_e, n_used, w_gate[l].astype(BF16), w_up[l].astype(BF16), w_down[l].astype(BF16),
                  EXPERT_BLOCK)
    out = _combine(ys, meta, rank.T, wdense.T, rank, wdense, x1.reshape(t_all, d), h2, mod3, norm_g[l],
                   ws_gate[l].astype(BF16), ws_up[l].astype(BF16), ws_down[l].astype(BF16), n, ts_moe, n_exp)
    return out.reshape(b, n, d)
```

```python
import functools

import jax
import jax.numpy as jnp
from jax import lax
from jax.experimental import pallas as pl
from jax.experimental.pallas import tpu as pltpu

F32 = jnp.float32
BF16 = jnp.bfloat16
I32 = jnp.int32

EPS = 1e-6
LANES = 128
CHUNK = 128
HEADS = 4
GRID_W = 64
ROPE_BASE = 10000.0
N_GROUPS = 8
TOPK_GROUPS = 4
TOP_K = 8
ROUTED_SCALE = 2.5
N_MOD = 6
MOE_TILE = 256
SPILL_CHUNK = 64
ROW_ALIGN = 16
SLOT_ROWS = 64
SLOT_GROUP = 8
EXPERT_BLOCK = 512
NEG_INF = float("-inf")


def _sigmoid(v):
    return 1.0 / (1.0 + jnp.exp(-v))


def _silu(v):
    return v * _sigmoid(v)


def _log_sigmoid(v):
    return jnp.minimum(v, 0.0) - jnp.log(1.0 + jnp.exp(-jnp.abs(v)))


def _dot(a, b):
    return jnp.dot(a, b, preferred_element_type=F32)


def _dot_nt(a, b):
    return lax.dot_general(a, b, (((1,), (1,)), ((), ())), preferred_element_type=F32)


def _dot_tn(a, b):
    return lax.dot_general(a, b, (((0,), (0,)), ((), ())), preferred_element_type=F32)


def _split3(a):
    hi = a.astype(BF16)
    r = a - hi.astype(F32)
    mid = r.astype(BF16)
    lo = (r - mid.astype(F32)).astype(BF16)
    return hi, mid, lo


def _rms(v, g):
    ms = jnp.mean(v * v, axis=-1, keepdims=True)
    return v * lax.rsqrt(ms + EPS) * g


def _mod_kernel(c_ref, w_ref, b_ref, o_ref):
    a = _silu(c_ref[...])
    o_ref[...] = jnp.dot(a, w_ref[...], preferred_element_type=F32,
                         precision=lax.Precision.HIGHEST) + b_ref[...]


def _modulation(cc, w_mod, b_mod):
    rows, d = cc.shape
    cols = w_mod.shape[1]
    tn = d
    return pl.pallas_call(
        _mod_kernel,
        grid=(cols // tn,),
        in_specs=[pl.BlockSpec((rows, d), lambda j: (0, 0)),
                  pl.BlockSpec((d, tn), lambda j: (0, j)),
                  pl.BlockSpec((1, tn), lambda j: (0, j))],
        out_specs=pl.BlockSpec((rows, tn), lambda j: (0, j)),
        out_shape=jax.ShapeDtypeStruct((rows, cols), F32),
        name="mod",
    )(cc, w_mod, b_mod.reshape(1, cols))


def _inproj_kernel(x_ref, xp_ref, xn_ref, mod_ref, g_ref, w_ref, wgt_ref, cw_ref, cb_ref,
                   gbr_ref, cos_ref, sin_ref, p_ref, kt_ref, gr_ref, *, ts, d):
    i = pl.program_id(1)
    last = pl.num_programs(1) - 1
    r_w = d // 2
    shift = mod_ref[0, 0:1, :]
    scale = mod_ref[0, 1:2, :]
    g = g_ref[...]

    def normmod(v):
        return _rms(v, g) * (1.0 + scale) + shift

    hb = normmod(x_ref[0]).astype(BF16)
    halo = jnp.concatenate([xp_ref[0], xn_ref[0]], axis=0)
    ph = _dot(normmod(halo).astype(BF16), w_ref[:, 3 * r_w:5 * r_w])
    prev_row = jnp.where(i == 0, 0.0, ph[7:8, :])
    next_row = jnp.where(i == last, 0.0, ph[8:9, :])

    cos2 = cos_ref[...]
    sin2 = sin_ref[...]
    rows = lax.broadcasted_iota(I32, (ts, r_w), 0)
    qscale = LANES ** -0.5

    for j in range(8):
        acc = _dot(hb, w_ref[:, j * r_w:(j + 1) * r_w])
        if j in (0, 1):
            if j == 0:
                acc = acc * qscale
            parts = []
            for h in range(HEADS):
                t = acc[:, h * LANES:(h + 1) * LANES]
                parts.append(t * cos2 + pltpu.roll(t, LANES // 2, axis=1) * sin2)
            acc = jnp.concatenate(parts, axis=1)
        elif j in (3, 4):
            c0 = (j - 3) * r_w
            pr = prev_row[:, c0:c0 + r_w]
            nx = next_row[:, c0:c0 + r_w]
            down = jnp.where(rows == 0, pr, pltpu.roll(acc, 1, axis=0))
            up = jnp.where(rows == ts - 1, nx, pltpu.roll(acc, ts - 1, axis=0))
            cw = cw_ref[:, c0:c0 + r_w]
            acc = down * cw[0:1, :] + acc * cw[1:2, :] + up * cw[2:3, :] + cb_ref[:, c0:c0 + r_w]
            acc = _silu(acc)
            if j == 4:
                acc = acc * qscale
        p_ref[0, :, j * r_w:(j + 1) * r_w] = acc.astype(BF16)
        if j in (1, 4):
            kt_ref[0, (j // 4) * r_w:(j // 4 + 1) * r_w, :] = acc.T.astype(BF16)

    gr = _dot_nt(wgt_ref[...], hb) + gbr_ref[...]
    ch_r = lax.broadcasted_iota(I32, gr.shape, 0)
    gr_ref[0] = jnp.where((ch_r // HEADS) % 2 == 1, _log_sigmoid(gr), gr)


def _inproj(x, mod3, mod_row, g, w_main, wgt, conv_w, conv_b, gb_col, cos2, sin2, ts):
    b, n, d = x.shape
    nt = n // ts
    nb8 = n // 8
    hb = ts // 8
    cols = w_main.shape[1]
    if mod_row is None:
        mod_map = lambda bi, i: (bi, 0, 0)
    else:
        mod_map = lambda bi, i: (mod_row, 0, 0)
    const2 = lambda bi, i: (0, 0)
    kern = functools.partial(_inproj_kernel, ts=ts, d=d)
    return pl.pallas_call(
        kern,
        grid=(b, nt),
        in_specs=[
            pl.BlockSpec((1, ts, d), lambda bi, i: (bi, i, 0)),
            pl.BlockSpec((1, 8, d), lambda bi, i: (bi, jnp.maximum(i * hb - 1, 0), 0)),
            pl.BlockSpec((1, 8, d), lambda bi, i: (bi, jnp.minimum((i + 1) * hb, nb8 - 1), 0)),
            pl.BlockSpec((1, N_MOD, d), mod_map),
            pl.BlockSpec((1, d), const2),
            pl.BlockSpec((d, cols), const2),
            pl.BlockSpec((16, d), const2),
            pl.BlockSpec((3, d), const2),
            pl.BlockSpec((1, d), const2),
            pl.BlockSpec((16, 1), const2),
            pl.BlockSpec((ts, LANES), lambda bi, i: (i, 0)),
            pl.BlockSpec((ts, LANES), lambda bi, i: (i, 0)),
        ],
        out_specs=[
            pl.BlockSpec((1, ts, cols), lambda bi, i: (bi, i, 0)),
            pl.BlockSpec((1, d, ts), lambda bi, i: (bi, 0, i)),
            pl.BlockSpec((1, 16, ts), lambda bi, i: (bi, 0, i)),
        ],
        out_shape=[
            jax.ShapeDtypeStruct((b, n, cols), BF16),
            jax.ShapeDtypeStruct((b, d, n), BF16),
            jax.ShapeDtypeStruct((b, 16, n), F32),
        ],
        compiler_params=pltpu.CompilerParams(dimension_semantics=("parallel", "parallel")),
        name="inproj",
    )(x, x, x, mod3, g, w_main, wgt, conv_w, conv_b, gb_col, cos2, sin2)


def _scan_kernel(pf_ref, pb_ref, ktf_ref, ktb_ref, grf_ref, grb_ref, intra_ref, kd_ref, qd_ref, cd_ref,
                 rs0_ref, mc0_ref, mm0_ref, *out_refs, with_output, r_w):
    if with_output:
        of_ref, ob_ref, rs_ref, mc_ref, mm_ref = out_refs
    else:
        rs_ref, mc_ref, mm_ref = out_refs
    j = pl.program_id(1)

    @pl.when(j == 0)
    def _():
        rs_ref[...] = rs0_ref[...]
        mc_ref[...] = mc0_ref[...]
        mm_ref[...] = mm0_ref[...]

    c = CHUNK
    row = lax.broadcasted_iota(I32, (c, c), 0)
    col = lax.broadcasted_iota(I32, (c, c), 1)
    tri_le = (row <= col)
    tri_ge = (row >= col)
    eye = row == col
    ones_blk = jnp.ones((c, LANES), BF16)
    lane = lax.broadcasted_iota(I32, (HEADS, c), 1)

    def spread_rows(vecs):
        diag = jnp.concatenate([jnp.where(eye, v, 0.0) for v in vecs], axis=0)
        hi = diag.astype(BF16)
        lo = (diag - hi.astype(F32)).astype(BF16)
        out = _dot(hi, ones_blk) + _dot(lo, ones_blk)
        return [out[n * c:(n + 1) * c, :] for n in range(len(vecs))]

    def running_max(a, fwd):
        pm = a
        s = 1
        while s < c:
            if fwd:
                pm = jnp.where(lane >= s, jnp.maximum(pm, pltpu.roll(pm, s, axis=1)), pm)
            else:
                pm = jnp.where(lane < c - s, jnp.maximum(pm, pltpu.roll(pm, c - s, axis=1)), pm)
            s *= 2
        return pm

    n_st = 2 * HEADS
    rs_prev = [rs_ref[0, st] for st in range(n_st)]
    mc_prev = [mc_ref[0, st] for st in range(n_st)]
    mm_prev = [mm_ref[0, st:st + 1, 0:1] for st in range(n_st)]
    heads = [(dr, h) for dr in range(2) for h in range(HEADS)]
    o0 = 3 * r_w

    def cols(ref, base, h):
        return ref[0, :, base + h * LANES:base + (h + 1) * LANES]

    cs_rows = []
    for dr in range(2):
        gr = (grf_ref, grb_ref)[dr][0]
        tri = jnp.where(tri_le if dr == 0 else tri_ge, 1.0, 0.0).astype(BF16)
        cs_rows.append((gr, sum(_dot(piece, tri) for piece in _split3(gr))))
    ret_upd, ret_sc, ret_in, ml_sc, ml_in = [], [], [], [], []
    for dr, h in heads:
        st = dr * HEADS + h
        p_ref, kt_ref = (pf_ref, pb_ref)[dr], (ktf_ref, ktb_ref)[dr]
        ks = (kt_ref[0, h * LANES:(h + 1) * LANES, :].astype(F32) * kd_ref[st:st + 1, :]).astype(BF16)
        ret_upd.append(_dot(ks, cols(p_ref, 2 * r_w, h)))
        if with_output:
            q, mq = cols(p_ref, 0, h), cols(p_ref, o0, h)
            ret_sc.append(_dot_nt(q, cols(p_ref, r_w, h)))
            ret_in.append(_dot(q, rs_prev[st].astype(BF16)))
            ml_sc.append(_dot_nt(mq, cols(p_ref, o0 + r_w, h)))
            ml_in.append(_dot(mq, mc_prev[st].astype(BF16)))

    gate = []
    for dr in range(2):
        gr, cs_row = cs_rows[dr]
        g0 = dr * 2 * HEADS
        a_rows = gr[g0:g0 + HEADS, :] - cs_row[g0 + HEADS:g0 + 2 * HEADS, :]
        pm_rows = running_max(a_rows, dr == 0)
        last = c - 1 if dr == 0 else 0
        for h in range(HEADS):
            st = dr * HEADS + h
            b_row = cs_row[g0 + HEADS + h:g0 + HEADS + h + 1, :]
            a_row, pm_row = a_rows[h:h + 1, :], pm_rows[h:h + 1, :]
            b_tot = b_row[:, last:last + 1]
            m_prev = mm_prev[st]
            m_next = b_tot + jnp.maximum(m_prev, pm_row[:, last:last + 1])
            gate.append(dict(b_row=b_row, a_row=a_row, pm_row=pm_row, m_prev=m_prev, m_next=m_next,
                             decay_prev=jnp.exp(b_tot + m_prev - m_next),
                             ws_row=jnp.exp(b_tot + a_row - m_next)))
    ml_upd, spread = [], []
    for dr, h in heads:
        st = dr * HEADS + h
        p_ref, kt_ref = (pf_ref, pb_ref)[dr], (ktf_ref, ktb_ref)[dr]
        mkt = kt_ref[0, r_w + h * LANES:r_w + (h + 1) * LANES, :]
        kw = (mkt.astype(F32) * gate[st]["ws_row"]).astype(BF16)
        v_ext = jnp.concatenate([cols(p_ref, o0 + 2 * r_w, h), ones_blk], axis=1)
        ml_upd.append(_dot(kw, v_ext))
        if with_output:
            spread.append(spread_rows([gate[st]["b_row"], gate[st]["pm_row"]]))

    if with_output:
        ret_out, ml_out, stab = [], [], []
        for dr, h in heads:
            st = dr * HEADS + h
            p_ref = (pf_ref, pb_ref)[dr]
            sc = (ret_sc[st] * intra_ref[st]).astype(BF16)
            ret_out.append(_dot(sc, cols(p_ref, 2 * r_w, h)))
            b_sp, pm_sp = spread[st]
            c_sp = jnp.maximum(gate[st]["m_prev"], pm_sp)
            causal = tri_ge if dr == 0 else tri_le
            w = jnp.where(causal, jnp.exp(gate[st]["a_row"] - c_sp), 0.0)
            v_ext = jnp.concatenate([cols(p_ref, o0 + 2 * r_w, h), ones_blk], axis=1)
            ml_out.append(_dot((ml_sc[st] * w).astype(BF16), v_ext))
            stab.append((jnp.exp(gate[st]["m_prev"] - c_sp), jnp.exp(-(b_sp + c_sp))))

    for dr, h in heads:
        st = dr * HEADS + h
        if with_output:
            o_ref = (of_ref, ob_ref)[dr]
            o_ref[0, :, h * LANES:(h + 1) * LANES] = (ret_out[st] + qd_ref[st] * ret_in[st]).astype(BF16)
            inter, floor = stab[st]
            hx = ml_out[st] + jnp.concatenate([inter, inter], axis=1) * ml_in[st]
            hout = hx[:, :LANES] / jnp.maximum(jnp.abs(hx[:, LANES:]), floor)
            o_ref[0, :, r_w + h * LANES:r_w + (h + 1) * LANES] = hout.astype(BF16)
    for dr, h in heads:
        st = dr * HEADS + h
        rs_ref[0, st] = rs_prev[st] * cd_ref[st:st + 1, :] + ret_upd[st]
        mc_ref[0, st] = gate[st]["decay_prev"] * mc_prev[st] + ml_upd[st]
        mm_ref[0, st:st + 1, :] = jnp.broadcast_to(gate[st]["m_next"], (1, LANES))


def _scan(p, kt, g_row, tabs, states, with_output):
    b, n, cols = p.shape
    nch = n // CHUNK
    r_w = cols // 8
    intra, kd, qd, cd = tabs
    rs0, mc0, mm0 = states
    nst = 2 * HEADS
    fwd3 = lambda bi, j: (bi, j, 0)
    bwd3 = lambda bi, j: (bi, nch - 1 - j, 0)
    c3 = lambda bi, j: (0, 0, 0)
    st4 = lambda bi, j: (bi, 0, 0, 0)
    in_specs = [
        pl.BlockSpec((1, CHUNK, 6 * r_w), fwd3),
        pl.BlockSpec((1, CHUNK, 6 * r_w), bwd3),
        pl.BlockSpec((1, 2 * r_w, CHUNK), lambda bi, j: (bi, 0, j)),
        pl.BlockSpec((1, 2 * r_w, CHUNK), lambda bi, j: (bi, 0, nch - 1 - j)),
        pl.BlockSpec((1, 16, CHUNK), lambda bi, j: (bi, 0, j)),
        pl.BlockSpec((1, 16, CHUNK), lambda bi, j: (bi, 0, nch - 1 - j)),
        pl.BlockSpec((nst, CHUNK, LANES), c3),
        pl.BlockSpec((nst, CHUNK), lambda bi, j: (0, 0)),
        pl.BlockSpec((nst, CHUNK, LANES), c3),
        pl.BlockSpec((nst, LANES), lambda bi, j: (0, 0)),
        pl.BlockSpec((1, nst, LANES, LANES), st4),
        pl.BlockSpec((1, nst, LANES, 2 * LANES), st4),
        pl.BlockSpec((1, nst, LANES), lambda bi, j: (bi, 0, 0)),
    ]
    st_specs = [
        pl.BlockSpec((1, nst, LANES, LANES), st4),
        pl.BlockSpec((1, nst, LANES, 2 * LANES), st4),
        pl.BlockSpec((1, nst, LANES), lambda bi, j: (bi, 0, 0)),
    ]
    st_shapes = [
        jax.ShapeDtypeStruct((b, nst, LANES, LANES), F32),
        jax.ShapeDtypeStruct((b, nst, LANES, 2 * LANES), F32),
        jax.ShapeDtypeStruct((b, nst, LANES), F32),
    ]
    if with_output:
        out_specs = [pl.BlockSpec((1, CHUNK, 2 * r_w), fwd3), pl.BlockSpec((1, CHUNK, 2 * r_w), bwd3)] + st_specs
        out_shape = [jax.ShapeDtypeStruct((b, n, 2 * r_w), BF16)] * 2 + st_shapes
    else:
        out_specs, out_shape = st_specs, st_shapes
    kern = functools.partial(_scan_kernel, with_output=with_output, r_w=r_w)
    return pl.pallas_call(
        kern,
        grid=(b, nch),
        in_specs=in_specs,
        out_specs=out_specs,
        out_shape=out_shape,
        compiler_params=pltpu.CompilerParams(dimension_semantics=("parallel", "arbitrary")),
        name="scan_out" if with_output else "scan_state",
    )(p, p, kt, kt, g_row, g_row, intra, kd, qd, cd, rs0, mc0, mm0)


def _post_kernel(x_ref, of_ref, ob_ref, rg_ref, mo_ref, mod_ref, ng_ref, hg_ref, wo_ref, wrh_ref, wrl_ref,
                 rb_ref, su_ref, x1_ref, h2_ref, rk_ref, wd_ref, cnt_ref, *, ts, d, n_exp):
    s = of_ref[0].astype(F32) + ob_ref[0].astype(F32)
    parts = []
    for gi in range(2 * HEADS):
        sl = s[:, gi * LANES:(gi + 1) * LANES]
        mu = jnp.mean(sl, axis=-1, keepdims=True)
        dv = sl - mu
        var = jnp.mean(dv * dv, axis=-1, keepdims=True)
        y = dv * lax.rsqrt(var + EPS) * hg_ref[:, gi * LANES:(gi + 1) * LANES]
        if gi < HEADS:
            gate = _silu(rg_ref[0, :, gi * LANES:(gi + 1) * LANES].astype(F32))
        else:
            gate = _sigmoid(mo_ref[0, :, (gi - HEADS) * LANES:(gi - HEADS + 1) * LANES].astype(F32))
        parts.append((y * gate).astype(BF16))
    mixed = jnp.concatenate(parts, axis=1)
    y = _dot(mixed, wo_ref[...])
    g1 = mod_ref[0, 2:3, :]
    sh2 = mod_ref[0, 3:4, :]
    sc2 = mod_ref[0, 4:5, :]
    x1 = x_ref[0] + g1 * _rms(y, ng_ref[1:2, :])
    x1_ref[0] = x1
    h2 = _rms(x1, ng_ref[2:3, :]) * (1.0 + sc2) + sh2
    h_hi = h2.astype(BF16)
    h2_ref[...] = h_hi

    h_lo = (h2 - h_hi.astype(F32)).astype(BF16)
    logits = _dot_nt(wrh_ref[...], h_hi) + _dot_nt(wrh_ref[...], h_lo) + _dot_nt(wrl_ref[...], h_hi)
    scores = _sigmoid(logits)
    sel = scores + rb_ref[...]
    gsz = n_exp // N_GROUPS
    iota_g = lax.broadcasted_iota(I32, (gsz, ts), 0).astype(F32)
    grp = []
    for gi in range(N_GROUPS):
        blk = sel[gi * gsz:(gi + 1) * gsz, :]
        m1 = jnp.max(blk, axis=0, keepdims=True)
        i1 = jnp.min(jnp.where(blk == m1, iota_g, float(gsz)), axis=0, keepdims=True)
        m2 = jnp.max(jnp.where(iota_g == i1, NEG_INF, blk), axis=0, keepdims=True)
        grp.append(m1 + m2)
    masked_parts = []
    for gi in range(N_GROUPS):
        rank = jnp.zeros((1, ts), F32)
        for gj in range(N_GROUPS):
            if gj == gi:
                continue
            beats = (grp[gj] >= grp[gi]) if gj < gi else (grp[gj] > grp[gi])
            rank = rank + jnp.where(beats, 1.0, 0.0)
        keep = rank < float(TOPK_GROUPS)
        masked_parts.append(jnp.where(keep, sel[gi * gsz:(gi + 1) * gsz, :], NEG_INF))
    masked = jnp.concatenate(masked_parts, axis=0)

    iota_e = lax.broadcasted_iota(I32, (n_exp, ts), 0).astype(F32)
    selmask = jnp.zeros((n_exp, ts), F32)
    for _ in range(TOP_K):
        mx = jnp.max(masked, axis=0, keepdims=True)
        ei = jnp.min(jnp.where(masked == mx, iota_e, float(n_exp)), axis=0, keepdims=True)
        hit = iota_e == ei
        selmask = jnp.where(hit, 1.0, selmask)
        masked = jnp.where(hit, NEG_INF, masked)
    picked = selmask > 0.0
    wsel = jnp.where(picked, scores, 0.0)
    wd_ref[...] = wsel / jnp.sum(wsel, axis=0, keepdims=True) * ROUTED_SCALE
    rank = _dot(selmask.astype(BF16), su_ref[...])
    rk_ref[...] = jnp.where(picked, rank, -1.0)
    cnt_ref[0] = _dot_nt(jnp.ones((8, ts), BF16), selmask.astype(BF16))


def _post(x, o_f, o_b, p, mod3, norm_g, head_g, w_out, wr_hi, wr_lo, rbias, ts):
    b, n, d = x.shape
    nt = n // ts
    t_all = b * n
    n_exp = wr_hi.shape[0]
    r_w = d // 2
    su = jnp.where(lax.broadcasted_iota(I32, (ts, ts), 0) < lax.broadcasted_iota(I32, (ts, ts), 1),
                   1.0, 0.0).astype(BF16)
    tok3 = lambda bi, i: (bi, i, 0)
    c2 = lambda bi, i: (0, 0)
    flat = lambda bi, i: (0, bi * nt + i)
    kern = functools.partial(_post_kernel, ts=ts, d=d, n_exp=n_exp)
    return pl.pallas_call(
        kern,
        grid=(b, nt),
        in_specs=[
            pl.BlockSpec((1, ts, d), tok3),
            pl.BlockSpec((1, ts, d), tok3),
            pl.BlockSpec((1, ts, d), tok3),
            pl.BlockSpec((1, ts, r_w), lambda bi, i: (bi, i, 6)),
            pl.BlockSpec((1, ts, r_w), lambda bi, i: (bi, i, 7)),
            pl.BlockSpec((1, N_MOD, d), lambda bi, i: (bi, 0, 0)),
            pl.BlockSpec((4, d), c2),
            pl.BlockSpec((1, d), c2),
            pl.BlockSpec((d, d), c2),
            pl.BlockSpec((n_exp, d), c2),
            pl.BlockSpec((n_exp, d), c2),
            pl.BlockSpec((n_exp, 1), c2),
            pl.BlockSpec((ts, ts), c2),
        ],
        out_specs=[
            pl.BlockSpec((1, ts, d), tok3),
            pl.BlockSpec((ts, d), lambda bi, i: (bi * nt + i, 0)),
            pl.BlockSpec((n_exp, ts), flat),
            pl.BlockSpec((n_exp, ts), flat),
            pl.BlockSpec((1, 8, n_exp), lambda bi, i: (bi * nt + i, 0, 0)),
        ],
        out_shape=[
            jax.ShapeDtypeStruct((b, n, d), F32),
            jax.ShapeDtypeStruct((t_all, d), BF16),
            jax.ShapeDtypeStruct((n_exp, t_all), F32),
            jax.ShapeDtypeStruct((n_exp, t_all), F32),
            jax.ShapeDtypeStruct((b * nt, 8, n_exp), F32),
        ],
        compiler_params=pltpu.CompilerParams(dimension_semantics=("parallel", "parallel")),
        name="post",
    )(x, o_f, o_b, p, p, mod3, norm_g, head_g, w_out, wr_hi, wr_lo, rbias, su)


def _slot_copies(meta_ref, stage, buf, hbm, sems, n_exp, to_hbm):
    copies = []
    for e in range(n_exp):
        rows = hbm.at[pl.ds(pl.multiple_of(meta_ref[0, 0, e], ROW_ALIGN), SLOT_ROWS)]
        slot = stage.at[buf, pl.ds(e * SLOT_ROWS, SLOT_ROWS)]
        copies.append(pltpu.make_async_copy(slot, rows, sems.at[buf]) if to_hbm
                      else pltpu.make_async_copy(rows, slot, sems.at[buf]))
    return copies


def _overflow_copy(meta_ref, spill, hbm, sem, n_exp, j, i, to_hbm):
    e = meta_ref[0, 0, n_exp + j]
    src = meta_ref[0, 0, 3 * n_exp + j] + ROW_ALIGN * i
    dst = meta_ref[0, 0, e] + SLOT_ROWS + ROW_ALIGN * i
    piece = spill.at[pl.ds(pl.multiple_of(src, ROW_ALIGN), ROW_ALIGN)]
    rows = hbm.at[pl.ds(pl.multiple_of(dst, ROW_ALIGN), ROW_ALIGN)]
    return pltpu.make_async_copy(piece, rows, sem) if to_hbm else pltpu.make_async_copy(rows, piece, sem)


def _for_overflow_pieces(meta_ref, n_exp, fn):
    def per_expert(j, carry):
        def per_piece(i, c2):
            fn(j, i)
            return c2
        return lax.fori_loop(0, meta_ref[0, 0, 2 * n_exp + j], per_piece, carry)
    lax.fori_loop(0, meta_ref[0, 0, 4 * n_exp], per_expert, 0)


def _spill_matrix_rows(meta_ref, rk_ref, wd_ref, base, ts, n_exp):
    rows = (lax.broadcasted_iota(I32, (SPILL_CHUNK, ts), 0) + base).astype(F32)

    def per_expert(j, hit):
        e = meta_ref[0, 0, n_exp + j]
        rk = rk_ref[pl.ds(e, 1), :]
        val = 1.0 if wd_ref is None else wd_ref[pl.ds(e, 1), :]
        target = jnp.where(rk >= SLOT_ROWS, rk - SLOT_ROWS + meta_ref[0, 0, 3 * n_exp + j].astype(F32), -1.0)
        return jnp.where(target == rows, val, hit)

    return lax.fori_loop(0, meta_ref[0, 0, 4 * n_exp], per_expert,
                         jnp.zeros((SPILL_CHUNK, ts), F32)).astype(BF16)


def _zero_fill(seg_ref, xs_hbm, stage, sem, n_exp, n_blocks):
    tail = SLOT_ROWS + EXPERT_BLOCK
    stage[0, 0:tail, :] = jnp.zeros((tail, stage.shape[2]), stage.dtype)
    tails = []
    for e in range(n_exp):
        start = jnp.maximum(seg_ref[0, 0, e] - tail, 0)
        tails.append(pltpu.make_async_copy(stage.at[0, pl.ds(0, tail)],
                                           xs_hbm.at[pl.ds(pl.multiple_of(start, ROW_ALIGN), tail)], sem))
    for cp in tails:
        cp.start()
    n_used = seg_ref[0, 0, n_exp]

    def block_copy(i):
        row = pl.multiple_of(i * EXPERT_BLOCK, EXPERT_BLOCK)
        return pltpu.make_async_copy(stage.at[0, pl.ds(0, EXPERT_BLOCK)], xs_hbm.at[pl.ds(row, EXPERT_BLOCK)], sem)

    def start_block(i, carry):
        block_copy(i).start()
        return carry

    def wait_block(i, carry):
        block_copy(i).wait()
        return carry

    lax.fori_loop(n_used, n_blocks, start_block, 0)
    for cp in tails:
        cp.wait()
    lax.fori_loop(n_used, n_blocks, wait_block, 0)


def _dispatch_kernel(meta_ref, seg_ref, x_ref, rk_ref, xs_hbm, stage, spill, sems, sem_ov, *,
                     ts, n_exp, n_blocks):
    i = pl.program_id(0)
    buf = i % 2

    @pl.when(i == 0)
    def _():
        _zero_fill(seg_ref, xs_hbm, stage, sem_ov, n_exp, n_blocks)

    x = x_ref[...]
    slot_row = lax.broadcasted_iota(I32, (SLOT_ROWS, ts), 0).astype(F32)
    group_rows = SLOT_GROUP * SLOT_ROWS
    for g in range(n_exp // SLOT_GROUP):
        pick = jnp.concatenate(
            [jnp.where(rk_ref[e:e + 1, :] == slot_row, 1.0, 0.0)
             for e in range(g * SLOT_GROUP, (g + 1) * SLOT_GROUP)], axis=0).astype(BF16)
        stage[buf, g * group_rows:(g + 1) * group_rows, :] = _dot(pick, x).astype(BF16)
    n_spill = meta_ref[0, 0, 4 * n_exp + 1]

    def spill_chunk(ci, carry):
        base = pl.multiple_of(ci * SPILL_CHUNK, SPILL_CHUNK)
        spill[pl.ds(base, SPILL_CHUNK), :] = _dot(
            _spill_matrix_rows(meta_ref, rk_ref, None, base, ts, n_exp), x).astype(BF16)
        return carry

    lax.fori_loop(0, n_spill, spill_chunk, 0)

    @pl.when(i > 0)
    def _():
        for cp in _slot_copies(meta_ref, stage, 1 - buf, xs_hbm, sems, n_exp, True):
            cp.wait()

    copies = _slot_copies(meta_ref, stage, buf, xs_hbm, sems, n_exp, True)
    for e, cp in enumerate(copies):
        cp.start(priority=e % 2)
    _for_overflow_pieces(meta_ref, n_exp,
                         lambda e, k: _overflow_copy(meta_ref, spill, xs_hbm, sem_ov, n_exp, e, k, True).start())
    _for_overflow_pieces(meta_ref, n_exp,
                         lambda e, k: _overflow_copy(meta_ref, spill, xs_hbm, sem_ov, n_exp, e, k, True).wait())

    @pl.when(i == pl.num_programs(0) - 1)
    def _():
        for cp in copies:
            cp.wait()


def _dispatch(h2, rank, meta, seg, p_rows, ts, n_exp):
    t_all, d = h2.shape
    nt = t_all // ts
    kern = functools.partial(_dispatch_kernel, ts=ts, n_exp=n_exp, n_blocks=p_rows // EXPERT_BLOCK)
    return pl.pallas_call(
        kern,
        grid=(nt,),
        in_specs=[
            pl.BlockSpec((1, 1, meta.shape[2]), lambda i: (i, 0, 0), memory_space=pltpu.SMEM),
            pl.BlockSpec((1, 1, seg.shape[2]), lambda i: (0, 0, 0), memory_space=pltpu.SMEM),
            pl.BlockSpec((ts, d), lambda i: (i, 0)),
            pl.BlockSpec((n_exp, ts), lambda i: (0, i)),
        ],
        out_specs=pl.BlockSpec(memory_space=pl.ANY),
        out_shape=jax.ShapeDtypeStruct((p_rows, d), BF16),
        scratch_shapes=[pltpu.VMEM((2, n_exp * SLOT_ROWS, d), BF16),
                        pltpu.VMEM((ts * TOP_K, d), BF16),
                        pltpu.SemaphoreType.DMA((2,)), pltpu.SemaphoreType.DMA(())],
        compiler_params=pltpu.CompilerParams(dimension_semantics=("arbitrary",), has_side_effects=True),
        name="dispatch",
    )(meta, seg, h2, rank)


def _expert_kernel(be_ref, nu_ref, xs_ref, wg_ref, wu_ref, wd_ref, ys_ref):
    del be_ref
    i = pl.program_id(0)

    @pl.when(i < nu_ref[0])
    def _():
        xb = xs_ref[...]
        a = _silu(_dot(xb, wg_ref[0])) * _dot(xb, wu_ref[0])
        ys_ref[...] = _dot(a.astype(BF16), wd_ref[0]).astype(BF16)


def _experts(xs, block_e, n_used, w_gate, w_up, w_down, blk):
    p_rows, dw = xs.shape
    n_exp, d, ff = w_gate.shape
    nb = p_rows // blk
    used = lambda i, nu: jnp.minimum(i, nu[0] - 1)
    grid_spec = pltpu.PrefetchScalarGridSpec(
        num_scalar_prefetch=2,
        grid=(nb,),
        in_specs=[
            pl.BlockSpec((blk, dw), lambda i, be, nu: (used(i, nu), 0)),
            pl.BlockSpec((1, d, ff), lambda i, be, nu: (be[used(i, nu)], 0, 0)),
            pl.BlockSpec((1, d, ff), lambda i, be, nu: (be[used(i, nu)], 0, 0)),
            pl.BlockSpec((1, ff, d), lambda i, be, nu: (be[used(i, nu)], 0, 0)),
        ],
        out_specs=pl.BlockSpec((blk, dw), lambda i, be, nu: (used(i, nu), 0)),
    )
    return pl.pallas_call(
        _expert_kernel,
        grid_spec=grid_spec,
        out_shape=jax.ShapeDtypeStruct((p_rows, dw), BF16),
        input_output_aliases={2: 0},
        compiler_params=pltpu.CompilerParams(dimension_semantics=("arbitrary",)),
        name="experts",
    )(block_e, n_used, xs, w_gate, w_up, w_down)


def _combine_kernel(meta_ref, nxt_ref, ys_hbm, rkt_ref, wdt_ref, rk_ref, wd_ref, ex_ref, rp_ref, x1_ref, h2_ref,
                    mod_ref, ng_ref, sg_ref, su_ref, sd_ref, o_ref, stage, spill, acc, sems, sem_ov, *,
                    ts, n_exp):
    i = pl.program_id(0)
    buf = i % 2
    n_spill = meta_ref[0, 0, 4 * n_exp + 1]
    n_groups = n_exp // SLOT_GROUP

    @pl.when(i == 0)
    def _():
        for e, cp in enumerate(_slot_copies(meta_ref, stage, 0, ys_hbm, sems, n_exp, False)):
            cp.start(priority=e % 2)

    @pl.when(i < pl.num_programs(0) - 1)
    def _():
        for e, cp in enumerate(_slot_copies(nxt_ref, stage, 1 - buf, ys_hbm, sems, n_exp, False)):
            cp.start(priority=e % 2)

    def clear_chunk(ci, carry):
        base = pl.multiple_of(ci * SPILL_CHUNK, SPILL_CHUNK)
        spill[pl.ds(base, SPILL_CHUNK), :] = jnp.zeros((SPILL_CHUNK, spill.shape[1]), spill.dtype)
        return carry

    lax.fori_loop(0, n_spill, clear_chunk, 0)
    _for_overflow_pieces(meta_ref, n_exp,
                         lambda e, k: _overflow_copy(meta_ref, spill, ys_hbm, sem_ov, n_exp, e, k, False).start())

    xb = h2_ref[...]
    a = _silu(_dot(xb, sg_ref[...])) * _dot(xb, su_ref[...])
    tot = _dot(a.astype(BF16), sd_ref[...])

    rank_lanes = _dot(rkt_ref[...].astype(BF16), ex_ref[...])
    weight_lanes = _dot(wdt_ref[...].astype(BF16), ex_ref[...])

    for cp in _slot_copies(meta_ref, stage, buf, ys_hbm, sems, n_exp, False):
        cp.wait()
    group_rows = SLOT_GROUP * SLOT_ROWS
    for g in range(n_groups):
        cols = slice(g * group_rows, (g + 1) * group_rows)
        unmix = jnp.where(rank_lanes[:, cols] == rp_ref[:, cols], weight_lanes[:, cols], 0.0).astype(BF16)
        tot = tot + _dot(unmix, stage[buf, g * group_rows:(g + 1) * group_rows, :])
    acc[...] = tot

    _for_overflow_pieces(meta_ref, n_exp,
                         lambda e, k: _overflow_copy(meta_ref, spill, ys_hbm, sem_ov, n_exp, e, k, False).wait())

    def spill_chunk(ci, carry):
        base = pl.multiple_of(ci * SPILL_CHUNK, SPILL_CHUNK)
        acc[...] += _dot_tn(_spill_matrix_rows(meta_ref, rk_ref, wd_ref, base, ts, n_exp),
                            spill[pl.ds(base, SPILL_CHUNK), :])
        return carry

    lax.fori_loop(0, n_spill, spill_chunk, 0)
    g2 = mod_ref[0, 5:6, :]
    o_ref[...] = x1_ref[...] + g2 * _rms(acc[...], ng_ref[3:4, :])


def _combine(ys, meta, rank_tm, wd_tm, rank, wd, x1_flat, h2, mod3, norm_g, ws_gate, ws_up, ws_down,
             n_seq, ts, n_exp):
    t_all, d = x1_flat.shape
    nt = t_all // ts
    per_b = n_seq // ts
    ff = ws_gate.shape[1]
    lanes = n_exp * SLOT_ROWS
    lane = lax.broadcasted_iota(I32, (n_exp, lanes), 1)
    expand = jnp.where(lane // SLOT_ROWS == lax.broadcasted_iota(I32, (n_exp, lanes), 0), 1.0, 0.0).astype(BF16)
    slot_rank = (jnp.arange(lanes, dtype=I32) % SLOT_ROWS).astype(F32).reshape(1, lanes)
    c2 = lambda i: (0, 0)
    kern = functools.partial(_combine_kernel, ts=ts, n_exp=n_exp)
    return pl.pallas_call(
        kern,
        grid=(nt,),
        in_specs=[
            pl.BlockSpec((1, 1, meta.shape[2]), lambda i: (i, 0, 0), memory_space=pltpu.SMEM),
            pl.BlockSpec((1, 1, meta.shape[2]), lambda i: (jnp.minimum(i + 1, nt - 1), 0, 0),
                         memory_space=pltpu.SMEM),
            pl.BlockSpec(memory_space=pl.ANY),
            pl.BlockSpec((ts, n_exp), lambda i: (i, 0)),
            pl.BlockSpec((ts, n_exp), lambda i: (i, 0)),
            pl.BlockSpec((n_exp, ts), lambda i: (0, i)),
            pl.BlockSpec((n_exp, ts), lambda i: (0, i)),
            pl.BlockSpec((n_exp, lanes), c2),
            pl.BlockSpec((1, lanes), c2),
            pl.BlockSpec((ts, d), lambda i: (i, 0)),
            pl.BlockSpec((ts, d), lambda i: (i, 0)),
            pl.BlockSpec((1, N_MOD, d), lambda i: (i // per_b, 0, 0)),
            pl.BlockSpec((4, d), c2),
            pl.BlockSpec((d, ff), c2),
            pl.BlockSpec((d, ff), c2),
            pl.BlockSpec((ff, d), c2),
        ],
        out_specs=pl.BlockSpec((ts, d), lambda i: (i, 0)),
        out_shape=jax.ShapeDtypeStruct((t_all, d), F32),
        scratch_shapes=[pltpu.VMEM((2, lanes, d), BF16), pltpu.VMEM((ts * TOP_K, d), BF16),
                        pltpu.VMEM((ts, d), F32),
                        pltpu.SemaphoreType.DMA((2,)), pltpu.SemaphoreType.DMA(())],
        compiler_params=pltpu.CompilerParams(dimension_semantics=("arbitrary",)),
        name="combine",
    )(meta, meta, ys, rank_tm, wd_tm, rank, wd, expand, slot_rank, x1_flat, h2, mod3, norm_g,
      ws_gate, ws_up, ws_down)


def _rope_tables(n):
    rows = jnp.repeat(jnp.arange(n // GRID_W, dtype=F32), GRID_W)
    cols = jnp.tile(jnp.arange(GRID_W, dtype=F32), n // GRID_W)
    quarter = LANES // 4
    freqs = ROPE_BASE ** (-jnp.arange(quarter, dtype=F32) / quarter)
    ang = jnp.concatenate([rows[:, None] * freqs, cols[:, None] * freqs], axis=-1)
    cos, sin = jnp.cos(ang), jnp.sin(ang)
    return jnp.concatenate([cos, cos], axis=-1), jnp.concatenate([-sin, sin], axis=-1)


def _retention_tables(log_decay):
    lg = -jnp.exp(log_decay.astype(F32))
    idx = jnp.arange(CHUNK, dtype=F32)
    rel = idx[:, None] - idx[None, :]
    lg3 = lg[:, :, None, None]
    intra_f = jnp.where(rel >= 0, jnp.exp(jnp.maximum(rel, 0.0) * lg3[0]), 0.0)
    intra_b = jnp.where(rel <= 0, jnp.exp(jnp.maximum(-rel, 0.0) * lg3[1]), 0.0)
    kd_f = jnp.exp((CHUNK - 1 - idx)[None, :] * lg[0][:, None])
    kd_b = jnp.exp(idx[None, :] * lg[1][:, None])
    qd_f = jnp.exp((idx + 1)[None, :] * lg[0][:, None])
    qd_b = jnp.exp((CHUNK - idx)[None, :] * lg[1][:, None])
    bc = lambda t: jnp.broadcast_to(t[:, :, None], (HEADS, CHUNK, LANES))
    intra = jnp.concatenate([intra_f, intra_b], axis=0)
    kd = jnp.concatenate([kd_f, kd_b], axis=0)
    qd = jnp.concatenate([bc(qd_f), bc(qd_b)], axis=0)
    cd = jnp.broadcast_to(jnp.exp(CHUNK * lg).reshape(2 * HEADS, 1), (2 * HEADS, LANES))
    return intra, kd, qd, cd


def kernel(x, c, ctx, c_ctx, w_mod, b_mod, norm_g, w_in, ret_log_decay, ret_norm_g, mlstm_conv_w,
           mlstm_conv_b, mlstm_gate_b, mlstm_norm_g, w_out, w_router, router_bias, w_gate, w_up, w_down,
           ws_gate, ws_up, ws_down):
    b, n, d = x.shape
    n_ctx = ctx.shape[1]
    depth = w_mod.shape[0]
    assert depth == 1, "only the single-layer configuration is implemented"
    assert d // 2 // HEADS == LANES
    n_exp = w_router.shape[2]
    t_all = b * n
    r_w = d // 2
    main_cols = 8 * r_w
    l = 0

    pad = (-(b + 1)) % 8
    cc = jnp.concatenate([c, c_ctx[None, :], jnp.zeros((pad, d), F32)], axis=0)
    mod3 = _modulation(cc, w_mod[l], b_mod[l]).reshape(b + 1 + pad, N_MOD, d)

    w_groups = w_in[l, :, :main_cols].astype(BF16).reshape(d, 8, r_w)
    w_main = w_groups[:, jnp.array([0, 1, 2, 4, 5, 6, 3, 7]), :].reshape(d, main_cols)
    wg = w_in[l, :, main_cols:].astype(BF16)
    wgt = wg.T
    gb = mlstm_gate_b[l].reshape(-1).astype(F32)
    tabs = _retention_tables(ret_log_decay[l])
    head_g = jnp.concatenate([ret_norm_g[l], mlstm_norm_g[l]]).reshape(1, d).astype(F32)
    wr = w_router[l].T.astype(F32)
    wr_hi = wr.astype(BF16)
    wr_lo = (wr - wr_hi.astype(F32)).astype(BF16)

    def inproj(seq, mod_row, ts):
        cos2, sin2 = _rope_tables(n) if mod_row is None else (
            jnp.ones((seq.shape[1], LANES), F32), jnp.zeros((seq.shape[1], LANES), F32))
        return _inproj(seq, mod3, mod_row, norm_g[l, 0:1], w_main, wgt, mlstm_conv_w[l],
                       mlstm_conv_b[l].reshape(1, -1), gb.reshape(16, 1), cos2, sin2, ts)

    nst = 2 * HEADS
    zero_states = (jnp.zeros((b, nst, LANES, LANES), F32), jnp.zeros((b, nst, LANES, 2 * LANES), F32),
                   jnp.zeros((b, nst, LANES), F32))
    p_c, kt_c, gr_c = inproj(ctx, b, min(n_ctx, 512))
    ctx_states = _scan(p_c, kt_c, gr_c, tabs, zero_states, with_output=False)

    ts = min(n, 512)
    p_l, kt_l, gr_l = inproj(x, None, ts)
    o_f, o_b, _, _, _ = _scan(p_l, kt_l, gr_l, tabs, tuple(ctx_states), with_output=True)
    ts_moe = MOE_TILE
    x1, h2, rank, wdense, tile_cnt = _post(
        x, o_f, o_b, p_l, mod3, norm_g[l], head_g, w_out[l].astype(BF16), wr_hi, wr_lo,
        router_bias[l].reshape(n_exp, 1).astype(F32), ts_moe)

    nt = t_all // ts_moe
    cnt = tile_cnt[:, 0, :].astype(I32)
    run_rows = (cnt + ROW_ALIGN - 1) // ROW_ALIGN * ROW_ALIGN
    seg_cap = (jnp.sum(run_rows, axis=0) + SLOT_ROWS + EXPERT_BLOCK - 1) // EXPERT_BLOCK * EXPERT_BLOCK
    seg_end = jnp.cumsum(seg_cap)
    run_start = (seg_end - seg_cap)[None, :] + jnp.cumsum(run_rows, axis=0) - run_rows
    ov_rows = jnp.maximum(run_rows - SLOT_ROWS, 0)
    ov_off = jnp.cumsum(ov_rows, axis=1) - ov_rows
    n_spill = (jnp.sum(ov_rows, axis=1, keepdims=True) + SPILL_CHUNK - 1) // SPILL_CHUNK
    spills = ov_rows > 0
    n_ov = jnp.sum(spills.astype(I32), axis=1, keepdims=True)
    nth = jnp.cumsum(spills.astype(I32), axis=1) - 1
    is_jth = spills[:, None, :] & (nth[:, None, :] == jnp.arange(n_exp, dtype=I32)[None, :, None])
    compact = lambda v: jnp.sum(jnp.where(is_jth, v[:, None, :], 0), axis=2)
    ov_e = compact(jnp.broadcast_to(jnp.arange(n_exp, dtype=I32)[None, :], cnt.shape))
    meta = jnp.concatenate([run_start, ov_e, compact(ov_rows // ROW_ALIGN), compact(ov_off), n_ov, n_spill],
                           axis=1).astype(I32)
    meta = jnp.pad(meta, ((0, 0), (0, (-meta.shape[1]) % LANES))).reshape(nt, 1, -1)
    p_rows = -(-(t_all * TOP_K + nt * n_exp * (ROW_ALIGN - 1) + n_exp * (SLOT_ROWS + EXPERT_BLOCK - 1))
               // EXPERT_BLOCK) * EXPERT_BLOCK
    nb = p_rows // EXPERT_BLOCK
    blk_first = jnp.arange(nb, dtype=I32) * EXPERT_BLOCK
    block_e = jnp.minimum(jnp.sum((seg_end[None, :] <= blk_first[:, None]).astype(I32), axis=1), n_exp - 1)
    n_used = (seg_end[-1:] // EXPERT_BLOCK).astype(I32)
    seg = jnp.concatenate([seg_end.astype(I32), n_used])
    seg = jnp.pad(seg, (0, (-seg.shape[0]) % LANES)).reshape(1, 1, -1)

    xs = _dispatch(h2, rank, meta, seg, p_rows, ts_moe, n_exp)
    ys = _experts(xs, block_e, n_used, w_gate[l].astype(BF16), w_up[l].astype(BF16), w_down[l].astype(BF16),
                  EXPERT_BLOCK)
    out = _combine(ys, meta, rank.T, wdense.T, rank, wdense, x1.reshape(t_all, d), h2, mod3, norm_g[l],
                   ws_gate[l].astype(BF16), ws_up[l].astype(BF16), ws_down[l].astype(BF16), n, ts_moe, n_exp)
    return out.reshape(b, n, d)
```

```python
import functools

import jax
import jax.numpy as jnp
from jax import lax
from jax.experimental import pallas as pl
from jax.experimental.pallas import tpu as pltpu

F32 = jnp.float32
BF16 = jnp.bfloat16
I32 = jnp.int32

EPS = 1e-6
LANES = 128
CHUNK = 128
HEADS = 4
GRID_W = 64
ROPE_BASE = 10000.0
N_GROUPS = 8
TOPK_GROUPS = 4
TOP_K = 8
ROUTED_SCALE = 2.5
N_MOD = 6
MOE_TILE = 256
SPILL_CHUNK = 64
ROW_ALIGN = 16
SLOT_ROWS = 64
SLOT_GROUP = 8
EXPERT_BLOCK = 512
NEG_INF = float("-inf")
P_SLOT = {0: 0, 2: 1, 3: 2, 5: 3, 6: 4, 7: 5}


def _sigmoid(v):
    return 1.0 / (1.0 + jnp.exp(-v))


def _silu(v):
    return v * _sigmoid(v)


def _log_sigmoid(v):
    return jnp.minimum(v, 0.0) - jnp.log(1.0 + jnp.exp(-jnp.abs(v)))


def _dot(a, b):
    return jnp.dot(a, b, preferred_element_type=F32)


def _dot_nt(a, b):
    return lax.dot_general(a, b, (((1,), (1,)), ((), ())), preferred_element_type=F32)


def _dot_tn(a, b):
    return lax.dot_general(a, b, (((0,), (0,)), ((), ())), preferred_element_type=F32)


def _split3(a):
    hi = a.astype(BF16)
    r = a - hi.astype(F32)
    mid = r.astype(BF16)
    lo = (r - mid.astype(F32)).astype(BF16)
    return hi, mid, lo


def _rms(v, g):
    ms = jnp.mean(v * v, axis=-1, keepdims=True)
    return v * lax.rsqrt(ms + EPS) * g


def _mod_kernel(c_ref, w_ref, b_ref, o_ref):
    a = _silu(c_ref[...])
    o_ref[...] = jnp.dot(a, w_ref[...], preferred_element_type=F32,
                         precision=lax.Precision.HIGHEST) + b_ref[...]


def _modulation(cc, w_mod, b_mod):
    rows, d = cc.shape
    cols = w_mod.shape[1]
    tn = d
    return pl.pallas_call(
        _mod_kernel,
        grid=(cols // tn,),
        in_specs=[pl.BlockSpec((rows, d), lambda j: (0, 0)),
                  pl.BlockSpec((d, tn), lambda j: (0, j)),
                  pl.BlockSpec((1, tn), lambda j: (0, j))],
        out_specs=pl.BlockSpec((rows, tn), lambda j: (0, j)),
        out_shape=jax.ShapeDtypeStruct((rows, cols), F32),
        name="mod",
    )(cc, w_mod, b_mod.reshape(1, cols))


def _inproj_kernel(x_ref, xp_ref, xn_ref, mod_ref, g_ref, w_ref, wgt_ref, cw_ref, cb_ref,
                   gbr_ref, cos_ref, sin_ref, p_ref, kt_ref, gr_ref, *, ts, d):
    i = pl.program_id(1)
    last = pl.num_programs(1) - 1
    r_w = d // 2
    shift = mod_ref[0, 0:1, :]
    scale = mod_ref[0, 1:2, :]
    g = g_ref[...]

    def normmod(v):
        return _rms(v, g) * (1.0 + scale) + shift

    hb = normmod(x_ref[0]).astype(BF16)
    halo = jnp.concatenate([xp_ref[0], xn_ref[0]], axis=0)
    ph = _dot(normmod(halo).astype(BF16), w_ref[:, 3 * r_w:5 * r_w])
    prev_row = jnp.where(i == 0, 0.0, ph[7:8, :])
    next_row = jnp.where(i == last, 0.0, ph[8:9, :])

    cos2 = cos_ref[...]
    sin2 = sin_ref[...]
    rows = lax.broadcasted_iota(I32, (ts, r_w), 0)
    qscale = LANES ** -0.5

    for j in range(8):
        acc = _dot(hb, w_ref[:, j * r_w:(j + 1) * r_w])
        if j in (0, 1):
            if j == 0:
                acc = acc * qscale
            parts = []
            for h in range(HEADS):
                t = acc[:, h * LANES:(h + 1) * LANES]
                parts.append(t * cos2 + pltpu.roll(t, LANES // 2, axis=1) * sin2)
            acc = jnp.concatenate(parts, axis=1)
        elif j in (3, 4):
            c0 = (j - 3) * r_w
            pr = prev_row[:, c0:c0 + r_w]
            nx = next_row[:, c0:c0 + r_w]
            down = jnp.where(rows == 0, pr, pltpu.roll(acc, 1, axis=0))
            up = jnp.where(rows == ts - 1, nx, pltpu.roll(acc, ts - 1, axis=0))
            cw = cw_ref[:, c0:c0 + r_w]
            acc = down * cw[0:1, :] + acc * cw[1:2, :] + up * cw[2:3, :] + cb_ref[:, c0:c0 + r_w]
            acc = _silu(acc)
            if j == 4:
                acc = acc * qscale
        if j in (1, 4):
            kt_ref[0, (j // 4) * r_w:(j // 4 + 1) * r_w, :] = acc.T.astype(BF16)
        else:
            slot = P_SLOT[j]
            p_ref[0, :, slot * r_w:(slot + 1) * r_w] = acc.astype(BF16)

    gr = _dot_nt(wgt_ref[...], hb) + gbr_ref[...]
    ch_r = lax.broadcasted_iota(I32, gr.shape, 0)
    gr_ref[0] = jnp.where((ch_r // HEADS) % 2 == 1, _log_sigmoid(gr), gr)


def _inproj(x, mod3, mod_row, g, w_main, wgt, conv_w, conv_b, gb_col, cos2, sin2, ts):
    b, n, d = x.shape
    nt = n // ts
    nb8 = n // 8
    hb = ts // 8
    cols = w_main.shape[1]
    p_cols = cols // 8 * 6
    if mod_row is None:
        mod_map = lambda bi, i: (bi, 0, 0)
    else:
        mod_map = lambda bi, i: (mod_row, 0, 0)
    const2 = lambda bi, i: (0, 0)
    kern = functools.partial(_inproj_kernel, ts=ts, d=d)
    return pl.pallas_call(
        kern,
        grid=(b, nt),
        in_specs=[
            pl.BlockSpec((1, ts, d), lambda bi, i: (bi, i, 0)),
            pl.BlockSpec((1, 8, d), lambda bi, i: (bi, jnp.maximum(i * hb - 1, 0), 0)),
            pl.BlockSpec((1, 8, d), lambda bi, i: (bi, jnp.minimum((i + 1) * hb, nb8 - 1), 0)),
            pl.BlockSpec((1, N_MOD, d), mod_map),
            pl.BlockSpec((1, d), const2),
            pl.BlockSpec((d, cols), const2),
            pl.BlockSpec((16, d), const2),
            pl.BlockSpec((3, d), const2),
            pl.BlockSpec((1, d), const2),
            pl.BlockSpec((16, 1), const2),
            pl.BlockSpec((ts, LANES), lambda bi, i: (i, 0)),
            pl.BlockSpec((ts, LANES), lambda bi, i: (i, 0)),
        ],
        out_specs=[
            pl.BlockSpec((1, ts, p_cols), lambda bi, i: (bi, i, 0)),
            pl.BlockSpec((1, d, ts), lambda bi, i: (bi, 0, i)),
            pl.BlockSpec((1, 16, ts), lambda bi, i: (bi, 0, i)),
        ],
        out_shape=[
            jax.ShapeDtypeStruct((b, n, p_cols), BF16),
            jax.ShapeDtypeStruct((b, d, n), BF16),
            jax.ShapeDtypeStruct((b, 16, n), F32),
        ],
        compiler_params=pltpu.CompilerParams(dimension_semantics=("parallel", "parallel")),
        name="inproj",
    )(x, x, x, mod3, g, w_main, wgt, conv_w, conv_b, gb_col, cos2, sin2)


def _scan_kernel(pf_ref, pb_ref, ktf_ref, ktb_ref, grf_ref, grb_ref, intra_ref, kd_ref, qd_ref, cd_ref,
                 rs0_ref, mc0_ref, mm0_ref, *out_refs, with_output, r_w):
    if with_output:
        of_ref, ob_ref, rs_ref, mc_ref, mm_ref = out_refs
    else:
        rs_ref, mc_ref, mm_ref = out_refs
    j = pl.program_id(1)

    @pl.when(j == 0)
    def _():
        rs_ref[...] = rs0_ref[...]
        mc_ref[...] = mc0_ref[...]
        mm_ref[...] = mm0_ref[...]

    c = CHUNK
    row = lax.broadcasted_iota(I32, (c, c), 0)
    col = lax.broadcasted_iota(I32, (c, c), 1)
    tri_le = (row <= col)
    tri_ge = (row >= col)
    eye = row == col
    ones_blk = jnp.ones((c, LANES), BF16)
    lane = lax.broadcasted_iota(I32, (HEADS, c), 1)

    def spread_rows(vecs):
        diag = jnp.concatenate([jnp.where(eye, v, 0.0) for v in vecs], axis=0)
        hi = diag.astype(BF16)
        lo = (diag - hi.astype(F32)).astype(BF16)
        out = _dot(hi, ones_blk) + _dot(lo, ones_blk)
        return [out[n * c:(n + 1) * c, :] for n in range(len(vecs))]

    def running_max(a, fwd):
        pm = a
        s = 1
        while s < c:
            if fwd:
                pm = jnp.where(lane >= s, jnp.maximum(pm, pltpu.roll(pm, s, axis=1)), pm)
            else:
                pm = jnp.where(lane < c - s, jnp.maximum(pm, pltpu.roll(pm, c - s, axis=1)), pm)
            s *= 2
        return pm

    n_st = 2 * HEADS
    rs_prev = [rs_ref[0, st] for st in range(n_st)]
    mc_prev = [mc_ref[0, st] for st in range(n_st)]
    mm_prev = [mm_ref[0, st:st + 1, 0:1] for st in range(n_st)]
    heads = [(dr, h) for dr in range(2) for h in range(HEADS)]
    o0 = 2 * r_w

    def cols(ref, base, h):
        return ref[0, :, base + h * LANES:base + (h + 1) * LANES]

    cs_rows = []
    for dr in range(2):
        gr = (grf_ref, grb_ref)[dr][0]
        tri = jnp.where(tri_le if dr == 0 else tri_ge, 1.0, 0.0).astype(BF16)
        cs_rows.append((gr, sum(_dot(piece, tri) for piece in _split3(gr))))
    ret_upd, ret_sc, ret_in, ml_sc, ml_in = [], [], [], [], []
    for dr, h in heads:
        st = dr * HEADS + h
        p_ref, kt_ref = (pf_ref, pb_ref)[dr], (ktf_ref, ktb_ref)[dr]
        kt = kt_ref[0, h * LANES:(h + 1) * LANES, :]
        ks = (kt.astype(F32) * kd_ref[st:st + 1, :]).astype(BF16)
        ret_upd.append(_dot(ks, cols(p_ref, r_w, h)))
        if with_output:
            q, mq = cols(p_ref, 0, h), cols(p_ref, o0, h)
            ret_sc.append(_dot(q, kt))
            ret_in.append(_dot(q, rs_prev[st].astype(BF16)))
            ml_sc.append(_dot(mq, kt_ref[0, r_w + h * LANES:r_w + (h + 1) * LANES, :]))
            ml_in.append(_dot(mq, mc_prev[st].astype(BF16)))

    gate = []
    for dr in range(2):
        gr, cs_row = cs_rows[dr]
        g0 = dr * 2 * HEADS
        a_rows = gr[g0:g0 + HEADS, :] - cs_row[g0 + HEADS:g0 + 2 * HEADS, :]
        pm_rows = running_max(a_rows, dr == 0)
        last = c - 1 if dr == 0 else 0
        for h in range(HEADS):
            st = dr * HEADS + h
            b_row = cs_row[g0 + HEADS + h:g0 + HEADS + h + 1, :]
            a_row, pm_row = a_rows[h:h + 1, :], pm_rows[h:h + 1, :]
            b_tot = b_row[:, last:last + 1]
            m_prev = mm_prev[st]
            m_next = b_tot + jnp.maximum(m_prev, pm_row[:, last:last + 1])
            gate.append(dict(b_row=b_row, a_row=a_row, pm_row=pm_row, m_prev=m_prev, m_next=m_next,
                             decay_prev=jnp.exp(b_tot + m_prev - m_next),
                             ws_row=jnp.exp(b_tot + a_row - m_next)))
    ml_upd, spread = [], []
    for dr, h in heads:
        st = dr * HEADS + h
        p_ref, kt_ref = (pf_ref, pb_ref)[dr], (ktf_ref, ktb_ref)[dr]
        mkt = kt_ref[0, r_w + h * LANES:r_w + (h + 1) * LANES, :]
        kw = (mkt.astype(F32) * gate[st]["ws_row"]).astype(BF16)
        v_ext = jnp.concatenate([cols(p_ref, o0 + r_w, h), ones_blk], axis=1)
        ml_upd.append(_dot(kw, v_ext))
        if with_output:
            spread.append(spread_rows([gate[st]["b_row"], gate[st]["pm_row"]]))

    if with_output:
        ret_out, ml_out, stab = [], [], []
        for dr, h in heads:
            st = dr * HEADS + h
            p_ref = (pf_ref, pb_ref)[dr]
            sc = (ret_sc[st] * intra_ref[st]).astype(BF16)
            ret_out.append(_dot(sc, cols(p_ref, r_w, h)))
            b_sp, pm_sp = spread[st]
            c_sp = jnp.maximum(gate[st]["m_prev"], pm_sp)
            causal = tri_ge if dr == 0 else tri_le
            w = jnp.where(causal, jnp.exp(gate[st]["a_row"] - c_sp), 0.0)
            v_ext = jnp.concatenate([cols(p_ref, o0 + r_w, h), ones_blk], axis=1)
            ml_out.append(_dot((ml_sc[st] * w).astype(BF16), v_ext))
            stab.append((jnp.exp(gate[st]["m_prev"] - c_sp), jnp.exp(-(b_sp + c_sp))))

    for dr, h in heads:
        st = dr * HEADS + h
        if with_output:
            o_ref = (of_ref, ob_ref)[dr]
            o_ref[0, :, h * LANES:(h + 1) * LANES] = (ret_out[st] + qd_ref[st] * ret_in[st]).astype(BF16)
            inter, floor = stab[st]
            hx = ml_out[st] + jnp.concatenate([inter, inter], axis=1) * ml_in[st]
            hout = hx[:, :LANES] / jnp.maximum(jnp.abs(hx[:, LANES:]), floor)
            o_ref[0, :, r_w + h * LANES:r_w + (h + 1) * LANES] = hout.astype(BF16)
    for dr, h in heads:
        st = dr * HEADS + h
        rs_ref[0, st] = rs_prev[st] * cd_ref[st:st + 1, :] + ret_upd[st]
        mc_ref[0, st] = gate[st]["decay_prev"] * mc_prev[st] + ml_upd[st]
        mm_ref[0, st:st + 1, :] = jnp.broadcast_to(gate[st]["m_next"], (1, LANES))


def _scan(p, kt, g_row, tabs, states, with_output):
    b, n, cols = p.shape
    nch = n // CHUNK
    r_w = cols // 6
    intra, kd, qd, cd = tabs
    rs0, mc0, mm0 = states
    nst = 2 * HEADS
    fwd3 = lambda bi, j: (bi, j, 0)
    bwd3 = lambda bi, j: (bi, nch - 1 - j, 0)
    c3 = lambda bi, j: (0, 0, 0)
    st4 = lambda bi, j: (bi, 0, 0, 0)
    in_specs = [
        pl.BlockSpec((1, CHUNK, 4 * r_w), fwd3),
        pl.BlockSpec((1, CHUNK, 4 * r_w), bwd3),
        pl.BlockSpec((1, 2 * r_w, CHUNK), lambda bi, j: (bi, 0, j)),
        pl.BlockSpec((1, 2 * r_w, CHUNK), lambda bi, j: (bi, 0, nch - 1 - j)),
        pl.BlockSpec((1, 16, CHUNK), lambda bi, j: (bi, 0, j)),
        pl.BlockSpec((1, 16, CHUNK), lambda bi, j: (bi, 0, nch - 1 - j)),
        pl.BlockSpec((nst, CHUNK, LANES), c3),
        pl.BlockSpec((nst, CHUNK), lambda bi, j: (0, 0)),
        pl.BlockSpec((nst, CHUNK, LANES), c3),
        pl.BlockSpec((nst, LANES), lambda bi, j: (0, 0)),
        pl.BlockSpec((1, nst, LANES, LANES), st4),
        pl.BlockSpec((1, nst, LANES, 2 * LANES), st4),
        pl.BlockSpec((1, nst, LANES), lambda bi, j: (bi, 0, 0)),
    ]
    st_specs = [
        pl.BlockSpec((1, nst, LANES, LANES), st4),
        pl.BlockSpec((1, nst, LANES, 2 * LANES), st4),
        pl.BlockSpec((1, nst, LANES), lambda bi, j: (bi, 0, 0)),
    ]
    st_shapes = [
        jax.ShapeDtypeStruct((b, nst, LANES, LANES), F32),
        jax.ShapeDtypeStruct((b, nst, LANES, 2 * LANES), F32),
        jax.ShapeDtypeStruct((b, nst, LANES), F32),
    ]
    if with_output:
        out_specs = [pl.BlockSpec((1, CHUNK, 2 * r_w), fwd3), pl.BlockSpec((1, CHUNK, 2 * r_w), bwd3)] + st_specs
        out_shape = [jax.ShapeDtypeStruct((b, n, 2 * r_w), BF16)] * 2 + st_shapes
    else:
        out_specs, out_shape = st_specs, st_shapes
    kern = functools.partial(_scan_kernel, with_output=with_output, r_w=r_w)
    return pl.pallas_call(
        kern,
        grid=(b, nch),
        in_specs=in_specs,
        out_specs=out_specs,
        out_shape=out_shape,
        compiler_params=pltpu.CompilerParams(dimension_semantics=("parallel", "arbitrary")),
        name="scan_out" if with_output else "scan_state",
    )(p, p, kt, kt, g_row, g_row, intra, kd, qd, cd, rs0, mc0, mm0)


def _post_kernel(x_ref, of_ref, ob_ref, rg_ref, mo_ref, mod_ref, ng_ref, hg_ref, wo_ref, wrh_ref, wrl_ref,
                 rb_ref, su_ref, x1_ref, h2_ref, rk_ref, wd_ref, cnt_ref, *, ts, d, n_exp):
    s = of_ref[0].astype(F32) + ob_ref[0].astype(F32)
    parts = []
    for gi in range(2 * HEADS):
        sl = s[:, gi * LANES:(gi + 1) * LANES]
        mu = jnp.mean(sl, axis=-1, keepdims=True)
        dv = sl - mu
        var = jnp.mean(dv * dv, axis=-1, keepdims=True)
        y = dv * lax.rsqrt(var + EPS) * hg_ref[:, gi * LANES:(gi + 1) * LANES]
        if gi < HEADS:
            gate = _silu(rg_ref[0, :, gi * LANES:(gi + 1) * LANES].astype(F32))
        else:
            gate = _sigmoid(mo_ref[0, :, (gi - HEADS) * LANES:(gi - HEADS + 1) * LANES].astype(F32))
        parts.append((y * gate).astype(BF16))
    mixed = jnp.concatenate(parts, axis=1)
    y = _dot(mixed, wo_ref[...])
    g1 = mod_ref[0, 2:3, :]
    sh2 = mod_ref[0, 3:4, :]
    sc2 = mod_ref[0, 4:5, :]
    x1 = x_ref[0] + g1 * _rms(y, ng_ref[1:2, :])
    x1_ref[0] = x1
    h2 = _rms(x1, ng_ref[2:3, :]) * (1.0 + sc2) + sh2
    h_hi = h2.astype(BF16)
    h2_ref[...] = h_hi

    h_lo = (h2 - h_hi.astype(F32)).astype(BF16)
    logits = _dot_nt(wrh_ref[...], h_hi) + _dot_nt(wrh_ref[...], h_lo) + _dot_nt(wrl_ref[...], h_hi)
    scores = _sigmoid(logits)
    sel = scores + rb_ref[...]
    gsz = n_exp // N_GROUPS
    iota_g = lax.broadcasted_iota(I32, (gsz, ts), 0).astype(F32)
    grp = []
    for gi in range(N_GROUPS):
        blk = sel[gi * gsz:(gi + 1) * gsz, :]
        m1 = jnp.max(blk, axis=0, keepdims=True)
        i1 = jnp.min(jnp.where(blk == m1, iota_g, float(gsz)), axis=0, keepdims=True)
        m2 = jnp.max(jnp.where(iota_g == i1, NEG_INF, blk), axis=0, keepdims=True)
        grp.append(m1 + m2)
    masked_parts = []
    for gi in range(N_GROUPS):
        rank = jnp.zeros((1, ts), F32)
        for gj in range(N_GROUPS):
            if gj == gi:
                continue
            beats = (grp[gj] >= grp[gi]) if gj < gi else (grp[gj] > grp[gi])
            rank = rank + jnp.where(beats, 1.0, 0.0)
        keep = rank < float(TOPK_GROUPS)
        masked_parts.append(jnp.where(keep, sel[gi * gsz:(gi + 1) * gsz, :], NEG_INF))
    masked = jnp.concatenate(masked_parts, axis=0)

    iota_e = lax.broadcasted_iota(I32, (n_exp, ts), 0).astype(F32)
    selmask = jnp.zeros((n_exp, ts), F32)
    for _ in range(TOP_K):
        mx = jnp.max(masked, axis=0, keepdims=True)
        ei = jnp.min(jnp.where(masked == mx, iota_e, float(n_exp)), axis=0, keepdims=True)
        hit = iota_e == ei
        selmask = jnp.where(hit, 1.0, selmask)
        masked = jnp.where(hit, NEG_INF, masked)
    picked = selmask > 0.0
    wsel = jnp.where(picked, scores, 0.0)
    wd_ref[...] = wsel / jnp.sum(wsel, axis=0, keepdims=True) * ROUTED_SCALE
    rank = _dot(selmask.astype(BF16), su_ref[...])
    rk_ref[...] = jnp.where(picked, rank, -1.0)
    cnt_ref[0] = _dot_nt(jnp.ones((8, ts), BF16), selmask.astype(BF16))


def _post(x, o_f, o_b, p, mod3, norm_g, head_g, w_out, wr_hi, wr_lo, rbias, ts):
    b, n, d = x.shape
    nt = n // ts
    t_all = b * n
    n_exp = wr_hi.shape[0]
    r_w = d // 2
    su = jnp.where(lax.broadcasted_iota(I32, (ts, ts), 0) < lax.broadcasted_iota(I32, (ts, ts), 1),
                   1.0, 0.0).astype(BF16)
    tok3 = lambda bi, i: (bi, i, 0)
    c2 = lambda bi, i: (0, 0)
    flat = lambda bi, i: (0, bi * nt + i)
    kern = functools.partial(_post_kernel, ts=ts, d=d, n_exp=n_exp)
    return pl.pallas_call(
        kern,
        grid=(b, nt),
        in_specs=[
            pl.BlockSpec((1, ts, d), tok3),
            pl.BlockSpec((1, ts, d), tok3),
            pl.BlockSpec((1, ts, d), tok3),
            pl.BlockSpec((1, ts, r_w), lambda bi, i: (bi, i, 4)),
            pl.BlockSpec((1, ts, r_w), lambda bi, i: (bi, i, 5)),
            pl.BlockSpec((1, N_MOD, d), lambda bi, i: (bi, 0, 0)),
            pl.BlockSpec((4, d), c2),
            pl.BlockSpec((1, d), c2),
            pl.BlockSpec((d, d), c2),
            pl.BlockSpec((n_exp, d), c2),
            pl.BlockSpec((n_exp, d), c2),
            pl.BlockSpec((n_exp, 1), c2),
            pl.BlockSpec((ts, ts), c2),
        ],
        out_specs=[
            pl.BlockSpec((1, ts, d), tok3),
            pl.BlockSpec((ts, d), lambda bi, i: (bi * nt + i, 0)),
            pl.BlockSpec((n_exp, ts), flat),
            pl.BlockSpec((n_exp, ts), flat),
            pl.BlockSpec((1, 8, n_exp), lambda bi, i: (bi * nt + i, 0, 0)),
        ],
        out_shape=[
            jax.ShapeDtypeStruct((b, n, d), F32),
            jax.ShapeDtypeStruct((t_all, d), BF16),
            jax.ShapeDtypeStruct((n_exp, t_all), F32),
            jax.ShapeDtypeStruct((n_exp, t_all), F32),
            jax.ShapeDtypeStruct((b * nt, 8, n_exp), F32),
        ],
        compiler_params=pltpu.CompilerParams(dimension_semantics=("parallel", "parallel")),
        name="post",
    )(x, o_f, o_b, p, p, mod3, norm_g, head_g, w_out, wr_hi, wr_lo, rbias, su)


def _slot_copies(meta_ref, stage, buf, hbm, sems, n_exp, to_hbm):
    copies = []
    for e in range(n_exp):
        rows = hbm.at[pl.ds(pl.multiple_of(meta_ref[0, 0, e], ROW_ALIGN), SLOT_ROWS)]
        slot = stage.at[buf, pl.ds(e * SLOT_ROWS, SLOT_ROWS)]
        copies.append(pltpu.make_async_copy(slot, rows, sems.at[buf]) if to_hbm
                      else pltpu.make_async_copy(rows, slot, sems.at[buf]))
    return copies


def _overflow_copy(meta_ref, spill, hbm, sem, n_exp, j, i, to_hbm):
    e = meta_ref[0, 0, n_exp + j]
    src = meta_ref[0, 0, 3 * n_exp + j] + ROW_ALIGN * i
    dst = meta_ref[0, 0, e] + SLOT_ROWS + ROW_ALIGN * i
    piece = spill.at[pl.ds(pl.multiple_of(src, ROW_ALIGN), ROW_ALIGN)]
    rows = hbm.at[pl.ds(pl.multiple_of(dst, ROW_ALIGN), ROW_ALIGN)]
    return pltpu.make_async_copy(piece, rows, sem) if to_hbm else pltpu.make_async_copy(rows, piece, sem)


def _for_overflow_pieces(meta_ref, n_exp, fn):
    def per_expert(j, carry):
        def per_piece(i, c2):
            fn(j, i)
            return c2
        return lax.fori_loop(0, meta_ref[0, 0, 2 * n_exp + j], per_piece, carry)
    lax.fori_loop(0, meta_ref[0, 0, 4 * n_exp], per_expert, 0)


def _spill_matrix_rows(meta_ref, rk_ref, wd_ref, base, ts, n_exp):
    rows = (lax.broadcasted_iota(I32, (SPILL_CHUNK, ts), 0) + base).astype(F32)

    def per_expert(j, hit):
        e = meta_ref[0, 0, n_exp + j]
        rk = rk_ref[pl.ds(e, 1), :]
        val = 1.0 if wd_ref is None else wd_ref[pl.ds(e, 1), :]
        target = jnp.where(rk >= SLOT_ROWS, rk - SLOT_ROWS + meta_ref[0, 0, 3 * n_exp + j].astype(F32), -1.0)
        return jnp.where(target == rows, val, hit)

    return lax.fori_loop(0, meta_ref[0, 0, 4 * n_exp], per_expert,
                         jnp.zeros((SPILL_CHUNK, ts), F32)).astype(BF16)


def _zero_fill(seg_ref, xs_hbm, stage, sem, n_exp, n_blocks):
    tail = SLOT_ROWS + EXPERT_BLOCK
    stage[0, 0:tail, :] = jnp.zeros((tail, stage.shape[2]), stage.dtype)
    tails = []
    for e in range(n_exp):
        start = jnp.maximum(seg_ref[0, 0, e] - tail, 0)
        tails.append(pltpu.make_async_copy(stage.at[0, pl.ds(0, tail)],
                                           xs_hbm.at[pl.ds(pl.multiple_of(start, ROW_ALIGN), tail)], sem))
    for cp in tails:
        cp.start()
    n_used = seg_ref[0, 0, n_exp]

    def block_copy(i):
        row = pl.multiple_of(i * EXPERT_BLOCK, EXPERT_BLOCK)
        return pltpu.make_async_copy(stage.at[0, pl.ds(0, EXPERT_BLOCK)], xs_hbm.at[pl.ds(row, EXPERT_BLOCK)], sem)

    def start_block(i, carry):
        block_copy(i).start()
        return carry

    def wait_block(i, carry):
        block_copy(i).wait()
        return carry

    lax.fori_loop(n_used, n_blocks, start_block, 0)
    for cp in tails:
        cp.wait()
    lax.fori_loop(n_used, n_blocks, wait_block, 0)


def _dispatch_kernel(meta_ref, seg_ref, x_ref, rk_ref, xs_hbm, stage, spill, sems, sem_ov, *,
                     ts, n_exp, n_blocks):
    i = pl.program_id(0)
    buf = i % 2

    @pl.when(i == 0)
    def _():
        _zero_fill(seg_ref, xs_hbm, stage, sem_ov, n_exp, n_blocks)

    x = x_ref[...]
    slot_row = lax.broadcasted_iota(I32, (SLOT_ROWS, ts), 0).astype(F32)
    group_rows = SLOT_GROUP * SLOT_ROWS
    for g in range(n_exp // SLOT_GROUP):
        pick = jnp.concatenate(
            [jnp.where(rk_ref[e:e + 1, :] == slot_row, 1.0, 0.0)
             for e in range(g * SLOT_GROUP, (g + 1) * SLOT_GROUP)], axis=0).astype(BF16)
        stage[buf, g * group_rows:(g + 1) * group_rows, :] = _dot(pick, x).astype(BF16)
    n_spill = meta_ref[0, 0, 4 * n_exp + 1]

    def spill_chunk(ci, carry):
        base = pl.multiple_of(ci * SPILL_CHUNK, SPILL_CHUNK)
        spill[pl.ds(base, SPILL_CHUNK), :] = _dot(
            _spill_matrix_rows(meta_ref, rk_ref, None, base, ts, n_exp), x).astype(BF16)
        return carry

    lax.fori_loop(0, n_spill, spill_chunk, 0)

    @pl.when(i > 0)
    def _():
        for cp in _slot_copies(meta_ref, stage, 1 - buf, xs_hbm, sems, n_exp, True):
            cp.wait()

    copies = _slot_copies(meta_ref, stage, buf, xs_hbm, sems, n_exp, True)
    for e, cp in enumerate(copies):
        cp.start(priority=e % 2)
    _for_overflow_pieces(meta_ref, n_exp,
                         lambda e, k: _overflow_copy(meta_ref, spill, xs_hbm, sem_ov, n_exp, e, k, True).start())
    _for_overflow_pieces(meta_ref, n_exp,
                         lambda e, k: _overflow_copy(meta_ref, spill, xs_hbm, sem_ov, n_exp, e, k, True).wait())

    @pl.when(i == pl.num_programs(0) - 1)
    def _():
        for cp in copies:
            cp.wait()


def _dispatch(h2, rank, meta, seg, p_rows, ts, n_exp):
    t_all, d = h2.shape
    nt = t_all // ts
    kern = functools.partial(_dispatch_kernel, ts=ts, n_exp=n_exp, n_blocks=p_rows // EXPERT_BLOCK)
    return pl.pallas_call(
        kern,
        grid=(nt,),
        in_specs=[
            pl.BlockSpec((1, 1, meta.shape[2]), lambda i: (i, 0, 0), memory_space=pltpu.SMEM),
            pl.BlockSpec((1, 1, seg.shape[2]), lambda i: (0, 0, 0), memory_space=pltpu.SMEM),
            pl.BlockSpec((ts, d), lambda i: (i, 0)),
            pl.BlockSpec((n_exp, ts), lambda i: (0, i)),
        ],
        out_specs=pl.BlockSpec(memory_space=pl.ANY),
        out_shape=jax.ShapeDtypeStruct((p_rows, d), BF16),
        scratch_shapes=[pltpu.VMEM((2, n_exp * SLOT_ROWS, d), BF16),
                        pltpu.VMEM((ts * TOP_K, d), BF16),
                        pltpu.SemaphoreType.DMA((2,)), pltpu.SemaphoreType.DMA(())],
        compiler_params=pltpu.CompilerParams(dimension_semantics=("arbitrary",), has_side_effects=True),
        name="dispatch",
    )(meta, seg, h2, rank)


def _expert_kernel(be_ref, nu_ref, xs_ref, wg_ref, wu_ref, wd_ref, ys_ref):
    del be_ref
    i = pl.program_id(0)

    @pl.when(i < nu_ref[0])
    def _():
        xb = xs_ref[...]
        a = _silu(_dot(xb, wg_ref[0])) * _dot(xb, wu_ref[0])
        ys_ref[...] = _dot(a.astype(BF16), wd_ref[0]).astype(BF16)


def _experts(xs, block_e, n_used, w_gate, w_up, w_down, blk):
    p_rows, dw = xs.shape
    n_exp, d, ff = w_gate.shape
    nb = p_rows // blk
    used = lambda i, nu: jnp.minimum(i, nu[0] - 1)
    grid_spec = pltpu.PrefetchScalarGridSpec(
        num_scalar_prefetch=2,
        grid=(nb,),
        in_specs=[
            pl.BlockSpec((blk, dw), lambda i, be, nu: (used(i, nu), 0)),
            pl.BlockSpec((1, d, ff), lambda i, be, nu: (be[used(i, nu)], 0, 0)),
            pl.BlockSpec((1, d, ff), lambda i, be, nu: (be[used(i, nu)], 0, 0)),
            pl.BlockSpec((1, ff, d), lambda i, be, nu: (be[used(i, nu)], 0, 0)),
        ],
        out_specs=pl.BlockSpec((blk, dw), lambda i, be, nu: (used(i, nu), 0)),
    )
    return pl.pallas_call(
        _expert_kernel,
        grid_spec=grid_spec,
        out_shape=jax.ShapeDtypeStruct((p_rows, dw), BF16),
        input_output_aliases={2: 0},
        compiler_params=pltpu.CompilerParams(dimension_semantics=("arbitrary",)),
        name="experts",
    )(block_e, n_used, xs, w_gate, w_up, w_down)


def _combine_kernel(meta_ref, nxt_ref, ys_hbm, rkt_ref, wdt_ref, rk_ref, wd_ref, ex_ref, rp_ref, x1_ref, h2_ref,
                    mod_ref, ng_ref, sg_ref, su_ref, sd_ref, o_ref, stage, spill, acc, sems, sem_ov, *,
                    ts, n_exp):
    i = pl.program_id(0)
    buf = i % 2
    n_spill = meta_ref[0, 0, 4 * n_exp + 1]
    n_groups = n_exp // SLOT_GROUP

    @pl.when(i == 0)
    def _():
        for e, cp in enumerate(_slot_copies(meta_ref, stage, 0, ys_hbm, sems, n_exp, False)):
            cp.start(priority=e % 2)

    @pl.when(i < pl.num_programs(0) - 1)
    def _():
        for e, cp in enumerate(_slot_copies(nxt_ref, stage, 1 - buf, ys_hbm, sems, n_exp, False)):
            cp.start(priority=e % 2)

    def clear_chunk(ci, carry):
        base = pl.multiple_of(ci * SPILL_CHUNK, SPILL_CHUNK)
        spill[pl.ds(base, SPILL_CHUNK), :] = jnp.zeros((SPILL_CHUNK, spill.shape[1]), spill.dtype)
        return carry

    lax.fori_loop(0, n_spill, clear_chunk, 0)
    _for_overflow_pieces(meta_ref, n_exp,
                         lambda e, k: _overflow_copy(meta_ref, spill, ys_hbm, sem_ov, n_exp, e, k, False).start())

    xb = h2_ref[...]
    a = _silu(_dot(xb, sg_ref[...])) * _dot(xb, su_ref[...])
    tot = _dot(a.astype(BF16), sd_ref[...])

    rank_lanes = _dot(rkt_ref[...].astype(BF16), ex_ref[...])
    weight_lanes = _dot(wdt_ref[...].astype(BF16), ex_ref[...])

    for cp in _slot_copies(meta_ref, stage, buf, ys_hbm, sems, n_exp, False):
        cp.wait()
    group_rows = SLOT_GROUP * SLOT_ROWS
    for g in range(n_groups):
        cols = slice(g * group_rows, (g + 1) * group_rows)
        unmix = jnp.where(rank_lanes[:, cols] == rp_ref[:, cols], weight_lanes[:, cols], 0.0).astype(BF16)
        tot = tot + _dot(unmix, stage[buf, g * group_rows:(g + 1) * group_rows, :])
    acc[...] = tot

    _for_overflow_pieces(meta_ref, n_exp,
                         lambda e, k: _overflow_copy(meta_ref, spill, ys_hbm, sem_ov, n_exp, e, k, False).wait())

    def spill_chunk(ci, carry):
        base = pl.multiple_of(ci * SPILL_CHUNK, SPILL_CHUNK)
        acc[...] += _dot_tn(_spill_matrix_rows(meta_ref, rk_ref, wd_ref, base, ts, n_exp),
                            spill[pl.ds(base, SPILL_CHUNK), :])
        return carry

    lax.fori_loop(0, n_spill, spill_chunk, 0)
    g2 = mod_ref[0, 5:6, :]
    o_ref[...] = x1_ref[...] + g2 * _rms(acc[...], ng_ref[3:4, :])


def _combine(ys, meta, rank_tm, wd_tm, rank, wd, x1_flat, h2, mod3, norm_g, ws_gate, ws_up, ws_down,
             n_seq, ts, n_exp):
    t_all, d = x1_flat.shape
    nt = t_all // ts
    per_b = n_seq // ts
    ff = ws_gate.shape[1]
    lanes = n_exp * SLOT_ROWS
    lane = lax.broadcasted_iota(I32, (n_exp, lanes), 1)
    expand = jnp.where(lane // SLOT_ROWS == lax.broadcasted_iota(I32, (n_exp, lanes), 0), 1.0, 0.0).astype(BF16)
    slot_rank = (jnp.arange(lanes, dtype=I32) % SLOT_ROWS).astype(F32).reshape(1, lanes)
    c2 = lambda i: (0, 0)
    kern = functools.partial(_combine_kernel, ts=ts, n_exp=n_exp)
    return pl.pallas_call(
        kern,
        grid=(nt,),
        in_specs=[
            pl.BlockSpec((1, 1, meta.shape[2]), lambda i: (i, 0, 0), memory_space=pltpu.SMEM),
            pl.BlockSpec((1, 1, meta.shape[2]), lambda i: (jnp.minimum(i + 1, nt - 1), 0, 0),
                         memory_space=pltpu.SMEM),
            pl.BlockSpec(memory_space=pl.ANY),
            pl.BlockSpec((ts, n_exp), lambda i: (i, 0)),
            pl.BlockSpec((ts, n_exp), lambda i: (i, 0)),
            pl.BlockSpec((n_exp, ts), lambda i: (0, i)),
            pl.BlockSpec((n_exp, ts), lambda i: (0, i)),
            pl.BlockSpec((n_exp, lanes), c2),
            pl.BlockSpec((1, lanes), c2),
            pl.BlockSpec((ts, d), lambda i: (i, 0)),
            pl.BlockSpec((ts, d), lambda i: (i, 0)),
            pl.BlockSpec((1, N_MOD, d), lambda i: (i // per_b, 0, 0)),
            pl.BlockSpec((4, d), c2),
            pl.BlockSpec((d, ff), c2),
            pl.BlockSpec((d, ff), c2),
            pl.BlockSpec((ff, d), c2),
        ],
        out_specs=pl.BlockSpec((ts, d), lambda i: (i, 0)),
        out_shape=jax.ShapeDtypeStruct((t_all, d), F32),
        scratch_shapes=[pltpu.VMEM((2, lanes, d), BF16), pltpu.VMEM((ts * TOP_K, d), BF16),
                        pltpu.VMEM((ts, d), F32),
                        pltpu.SemaphoreType.DMA((2,)), pltpu.SemaphoreType.DMA(())],
        compiler_params=pltpu.CompilerParams(dimension_semantics=("arbitrary",)),
        name="combine",
    )(meta, meta, ys, rank_tm, wd_tm, rank, wd, expand, slot_rank, x1_flat, h2, mod3, norm_g,
      ws_gate, ws_up, ws_down)


def _rope_tables(n):
    rows = jnp.repeat(jnp.arange(n // GRID_W, dtype=F32), GRID_W)
    cols = jnp.tile(jnp.arange(GRID_W, dtype=F32), n // GRID_W)
    quarter = LANES // 4
    freqs = ROPE_BASE ** (-jnp.arange(quarter, dtype=F32) / quarter)
    ang = jnp.concatenate([rows[:, None] * freqs, cols[:, None] * freqs], axis=-1)
    cos, sin = jnp.cos(ang), jnp.sin(ang)
    return jnp.concatenate([cos, cos], axis=-1), jnp.concatenate([-sin, sin], axis=-1)


def _retention_tables(log_decay):
    lg = -jnp.exp(log_decay.astype(F32))
    idx = jnp.arange(CHUNK, dtype=F32)
    rel = idx[:, None] - idx[None, :]
    lg3 = lg[:, :, None, None]
    intra_f = jnp.where(rel >= 0, jnp.exp(jnp.maximum(rel, 0.0) * lg3[0]), 0.0)
    intra_b = jnp.where(rel <= 0, jnp.exp(jnp.maximum(-rel, 0.0) * lg3[1]), 0.0)
    kd_f = jnp.exp((CHUNK - 1 - idx)[None, :] * lg[0][:, None])
    kd_b = jnp.exp(idx[None, :] * lg[1][:, None])
    qd_f = jnp.exp((idx + 1)[None, :] * lg[0][:, None])
    qd_b = jnp.exp((CHUNK - idx)[None, :] * lg[1][:, None])
    bc = lambda t: jnp.broadcast_to(t[:, :, None], (HEADS, CHUNK, LANES))
    intra = jnp.concatenate([intra_f, intra_b], axis=0)
    kd = jnp.concatenate([kd_f, kd_b], axis=0)
    qd = jnp.concatenate([bc(qd_f), bc(qd_b)], axis=0)
    cd = jnp.broadcast_to(jnp.exp(CHUNK * lg).reshape(2 * HEADS, 1), (2 * HEADS, LANES))
    return intra, kd, qd, cd


def kernel(x, c, ctx, c_ctx, w_mod, b_mod, norm_g, w_in, ret_log_decay, ret_norm_g, mlstm_conv_w,
           mlstm_conv_b, mlstm_gate_b, mlstm_norm_g, w_out, w_router, router_bias, w_gate, w_up, w_down,
           ws_gate, ws_up, ws_down):
    b, n, d = x.shape
    n_ctx = ctx.shape[1]
    depth = w_mod.shape[0]
    assert depth == 1, "only the single-layer configuration is implemented"
    assert d // 2 // HEADS == LANES
    n_exp = w_router.shape[2]
    t_all = b * n
    r_w = d // 2
    main_cols = 8 * r_w
    l = 0

    pad = (-(b + 1)) % 8
    cc = jnp.concatenate([c, c_ctx[None, :], jnp.zeros((pad, d), F32)], axis=0)
    mod3 = _modulation(cc, w_mod[l], b_mod[l]).reshape(b + 1 + pad, N_MOD, d)

    w_groups = w_in[l, :, :main_cols].astype(BF16).reshape(d, 8, r_w)
    w_main = w_groups[:, jnp.array([0, 1, 2, 4, 5, 6, 3, 7]), :].reshape(d, main_cols)
    wg = w_in[l, :, main_cols:].astype(BF16)
    wgt = wg.T
    gb = mlstm_gate_b[l].reshape(-1).astype(F32)
    tabs = _retention_tables(ret_log_decay[l])
    head_g = jnp.concatenate([ret_norm_g[l], mlstm_norm_g[l]]).reshape(1, d).astype(F32)
    wr = w_router[l].T.astype(F32)
    wr_hi = wr.astype(BF16)
    wr_lo = (wr - wr_hi.astype(F32)).astype(BF16)

    def inproj(seq, mod_row, ts):
        cos2, sin2 = _rope_tables(n) if mod_row is None else (
            jnp.ones((seq.shape[1], LANES), F32), jnp.zeros((seq.shape[1], LANES), F32))
        return _inproj(seq, mod3, mod_row, norm_g[l, 0:1], w_main, wgt, mlstm_conv_w[l],
                       mlstm_conv_b[l].reshape(1, -1), gb.reshape(16, 1), cos2, sin2, ts)

    nst = 2 * HEADS
    zero_states = (jnp.zeros((b, nst, LANES, LANES), F32), jnp.zeros((b, nst, LANES, 2 * LANES), F32),
                   jnp.zeros((b, nst, LANES), F32))
    p_c, kt_c, gr_c = inproj(ctx, b, min(n_ctx, 512))
    ctx_states = _scan(p_c, kt_c, gr_c, tabs, zero_states, with_output=False)

    ts = min(n, 512)
    p_l, kt_l, gr_l = inproj(x, None, ts)
    o_f, o_b, _, _, _ = _scan(p_l, kt_l, gr_l, tabs, tuple(ctx_states), with_output=True)
    ts_moe = MOE_TILE
    x1, h2, rank, wdense, tile_cnt = _post(
        x, o_f, o_b, p_l, mod3, norm_g[l], head_g, w_out[l].astype(BF16), wr_hi, wr_lo,
        router_bias[l].reshape(n_exp, 1).astype(F32), ts_moe)

    nt = t_all // ts_moe
    cnt = tile_cnt[:, 0, :].astype(I32)
    run_rows = (cnt + ROW_ALIGN - 1) // ROW_ALIGN * ROW_ALIGN
    seg_cap = (jnp.sum(run_rows, axis=0) + SLOT_ROWS + EXPERT_BLOCK - 1) // EXPERT_BLOCK * EXPERT_BLOCK
    seg_end = jnp.cumsum(seg_cap)
    run_start = (seg_end - seg_cap)[None, :] + jnp.cumsum(run_rows, axis=0) - run_rows
    ov_rows = jnp.maximum(run_rows - SLOT_ROWS, 0)
    ov_off = jnp.cumsum(ov_rows, axis=1) - ov_rows
    n_spill = (jnp.sum(ov_rows, axis=1, keepdims=True) + SPILL_CHUNK - 1) // SPILL_CHUNK
    spills = ov_rows > 0
    n_ov = jnp.sum(spills.astype(I32), axis=1, keepdims=True)
    nth = jnp.cumsum(spills.astype(I32), axis=1) - 1
    is_jth = spills[:, None, :] & (nth[:, None, :] == jnp.arange(n_exp, dtype=I32)[None, :, None])
    compact = lambda v: jnp.sum(jnp.where(is_jth, v[:, None, :], 0), axis=2)
    ov_e = compact(jnp.broadcast_to(jnp.arange(n_exp, dtype=I32)[None, :], cnt.shape))
    meta = jnp.concatenate([run_start, ov_e, compact(ov_rows // ROW_ALIGN), compact(ov_off), n_ov, n_spill],
                           axis=1).astype(I32)
    meta = jnp.pad(meta, ((0, 0), (0, (-meta.shape[1]) % LANES))).reshape(nt, 1, -1)
    p_rows = -(-(t_all * TOP_K + nt * n_exp * (ROW_ALIGN - 1) + n_exp * (SLOT_ROWS + EXPERT_BLOCK - 1))
               // EXPERT_BLOCK) * EXPERT_BLOCK
    nb = p_rows // EXPERT_BLOCK
    blk_first = jnp.arange(nb, dtype=I32) * EXPERT_BLOCK
    block_e = jnp.minimum(jnp.sum((seg_end[None, :] <= blk_first[:, None]).astype(I32), axis=1), n_exp - 1)
    n_used = (seg_end[-1:] // EXPERT_BLOCK).astype(I32)
    seg = jnp.concatenate([seg_end.astype(I32), n_used])
    seg = jnp.pad(seg, (0, (-seg.shape[0]) % LANES)).reshape(1, 1, -1)

    xs = _dispatch(h2, rank, meta, seg, p_rows, ts_moe, n_exp)
    ys = _experts(xs, block_e, n_used, w_gate[l].astype(BF16), w_up[l].astype(BF16), w_down[l].astype(BF16),
                  EXPERT_BLOCK)
    out = _combine(ys, meta, rank.T, wdense.T, rank, wdense, x1.reshape(t_all, d), h2, mod3, norm_g[l],
                   ws_gate[l].astype(BF16), ws_up[l].astype(BF16), ws_down[l].astype(BF16), n, ts_moe, n_exp)
    return out.reshape(b, n, d)
```

```python
import functools

import jax
import jax.numpy as jnp
from jax import lax
from jax.experimental import pallas as pl
from jax.experimental.pallas import tpu as pltpu

F32 = jnp.float32
BF16 = jnp.bfloat16
I32 = jnp.int32

EPS = 1e-6
LANES = 128
CHUNK = 128
HEADS = 4
GRID_W = 64
ROPE_BASE = 10000.0
N_GROUPS = 8
TOPK_GROUPS = 4
TOP_K = 8
ROUTED_SCALE = 2.5
N_MOD = 6
MOE_TILE = 256
SPILL_CHUNK = 64
ROW_ALIGN = 16
SLOT_ROWS = 64
SLOT_GROUP = 8
EXPERT_BLOCK = 512
NEG_INF = float("-inf")
P_SLOT = {0: 0, 2: 1, 3: 2, 5: 3, 6: 4, 7: 5}


def _sigmoid(v):
    return 1.0 / (1.0 + jnp.exp(-v))


def _silu(v):
    return v * _sigmoid(v)


def _log_sigmoid(v):
    return jnp.minimum(v, 0.0) - jnp.log(1.0 + jnp.exp(-jnp.abs(v)))


def _dot(a, b):
    return jnp.dot(a, b, preferred_element_type=F32)


def _dot_nt(a, b):
    return lax.dot_general(a, b, (((1,), (1,)), ((), ())), preferred_element_type=F32)


def _dot_tn(a, b):
    return lax.dot_general(a, b, (((0,), (0,)), ((), ())), preferred_element_type=F32)


def _split3(a):
    hi = a.astype(BF16)
    r = a - hi.astype(F32)
    mid = r.astype(BF16)
    lo = (r - mid.astype(F32)).astype(BF16)
    return hi, mid, lo


def _rms(v, g):
    ms = jnp.mean(v * v, axis=-1, keepdims=True)
    return v * lax.rsqrt(ms + EPS) * g


def _mod_kernel(c_ref, w_ref, b_ref, o_ref):
    a = _silu(c_ref[...])
    o_ref[...] = jnp.dot(a, w_ref[...], preferred_element_type=F32,
                         precision=lax.Precision.HIGHEST) + b_ref[...]


def _modulation(cc, w_mod, b_mod):
    rows, d = cc.shape
    cols = w_mod.shape[1]
    tn = d
    return pl.pallas_call(
        _mod_kernel,
        grid=(cols // tn,),
        in_specs=[pl.BlockSpec((rows, d), lambda j: (0, 0)),
                  pl.BlockSpec((d, tn), lambda j: (0, j)),
                  pl.BlockSpec((1, tn), lambda j: (0, j))],
        out_specs=pl.BlockSpec((rows, tn), lambda j: (0, j)),
        out_shape=jax.ShapeDtypeStruct((rows, cols), F32),
        name="mod",
    )(cc, w_mod, b_mod.reshape(1, cols))


def _inproj_kernel(x_ref, xp_ref, xn_ref, mod_ref, g_ref, w_ref, wgt_ref, cw_ref, cb_ref,
                   gbr_ref, cos_ref, sin_ref, p_ref, kt_ref, gr_ref, *, ts, d):
    i = pl.program_id(1)
    last = pl.num_programs(1) - 1
    r_w = d // 2
    shift = mod_ref[0, 0:1, :]
    scale = mod_ref[0, 1:2, :]
    g = g_ref[...]

    def normmod(v):
        return _rms(v, g) * (1.0 + scale) + shift

    hb = normmod(x_ref[0]).astype(BF16)
    halo = jnp.concatenate([xp_ref[0], xn_ref[0]], axis=0)
    ph = _dot(normmod(halo).astype(BF16), w_ref[:, 3 * r_w:5 * r_w])
    prev_row = jnp.where(i == 0, 0.0, ph[7:8, :])
    next_row = jnp.where(i == last, 0.0, ph[8:9, :])

    cos2 = cos_ref[...]
    sin2 = sin_ref[...]
    rows = lax.broadcasted_iota(I32, (ts, r_w), 0)
    qscale = LANES ** -0.5

    for j in range(8):
        acc = _dot(hb, w_ref[:, j * r_w:(j + 1) * r_w])
        if j in (0, 1):
            if j == 0:
                acc = acc * qscale
            parts = []
            for h in range(HEADS):
                t = acc[:, h * LANES:(h + 1) * LANES]
                parts.append(t * cos2 + pltpu.roll(t, LANES // 2, axis=1) * sin2)
            acc = jnp.concatenate(parts, axis=1)
        elif j in (3, 4):
            c0 = (j - 3) * r_w
            pr = prev_row[:, c0:c0 + r_w]
            nx = next_row[:, c0:c0 + r_w]
            down = jnp.where(rows == 0, pr, pltpu.roll(acc, 1, axis=0))
            up = jnp.where(rows == ts - 1, nx, pltpu.roll(acc, ts - 1, axis=0))
            cw = cw_ref[:, c0:c0 + r_w]
            acc = down * cw[0:1, :] + acc * cw[1:2, :] + up * cw[2:3, :] + cb_ref[:, c0:c0 + r_w]
            acc = _silu(acc)
            if j == 4:
                acc = acc * qscale
        if j in (1, 4):
            kt_ref[0, (j // 4) * r_w:(j // 4 + 1) * r_w, :] = acc.T.astype(BF16)
        else:
            slot = P_SLOT[j]
            p_ref[0, :, slot * r_w:(slot + 1) * r_w] = acc.astype(BF16)

    gr = _dot_nt(wgt_ref[...], hb) + gbr_ref[...]
    ch_r = lax.broadcasted_iota(I32, gr.shape, 0)
    gr_ref[0] = jnp.where((ch_r // HEADS) % 2 == 1, _log_sigmoid(gr), gr)


def _inproj(x, mod3, mod_row, g, w_main, wgt, conv_w, conv_b, gb_col, cos2, sin2, ts):
    b, n, d = x.shape
    nt = n // ts
    nb8 = n // 8
    hb = ts // 8
    cols = w_main.shape[1]
    p_cols = cols // 8 * 6
    if mod_row is None:
        mod_map = lambda bi, i: (bi, 0, 0)
    else:
        mod_map = lambda bi, i: (mod_row, 0, 0)
    const2 = lambda bi, i: (0, 0)
    kern = functools.partial(_inproj_kernel, ts=ts, d=d)
    return pl.pallas_call(
        kern,
        grid=(b, nt),
        in_specs=[
            pl.BlockSpec((1, ts, d), lambda bi, i: (bi, i, 0)),
            pl.BlockSpec((1, 8, d), lambda bi, i: (bi, jnp.maximum(i * hb - 1, 0), 0)),
            pl.BlockSpec((1, 8, d), lambda bi, i: (bi, jnp.minimum((i + 1) * hb, nb8 - 1), 0)),
            pl.BlockSpec((1, N_MOD, d), mod_map),
            pl.BlockSpec((1, d), const2),
            pl.BlockSpec((d, cols), const2),
            pl.BlockSpec((16, d), const2),
            pl.BlockSpec((3, d), const2),
            pl.BlockSpec((1, d), const2),
            pl.BlockSpec((16, 1), const2),
            pl.BlockSpec((ts, LANES), lambda bi, i: (i, 0)),
            pl.BlockSpec((ts, LANES), lambda bi, i: (i, 0)),
        ],
        out_specs=[
            pl.BlockSpec((1, ts, p_cols), lambda bi, i: (bi, i, 0)),
            pl.BlockSpec((1, d, ts), lambda bi, i: (bi, 0, i)),
            pl.BlockSpec((1, 16, ts), lambda bi, i: (bi, 0, i)),
        ],
        out_shape=[
            jax.ShapeDtypeStruct((b, n, p_cols), BF16),
            jax.ShapeDtypeStruct((b, d, n), BF16),
            jax.ShapeDtypeStruct((b, 16, n), F32),
        ],
        compiler_params=pltpu.CompilerParams(dimension_semantics=("parallel", "parallel")),
        name="inproj",
    )(x, x, x, mod3, g, w_main, wgt, conv_w, conv_b, gb_col, cos2, sin2)


def _scan_kernel(pf_ref, pb_ref, ktf_ref, ktb_ref, grf_ref, grb_ref, intra_ref, kd_ref, qd_ref, cd_ref,
                 rs0_ref, mc0_ref, mm0_ref, *out_refs, with_output, r_w):
    if with_output:
        of_ref, ob_ref, rs_ref, mc_ref, mm_ref = out_refs
    else:
        rs_ref, mc_ref, mm_ref = out_refs
    j = pl.program_id(1)

    @pl.when(j == 0)
    def _():
        rs_ref[...] = rs0_ref[...]
        mc_ref[...] = mc0_ref[...]
        mm_ref[...] = mm0_ref[...]

    c = CHUNK
    row = lax.broadcasted_iota(I32, (c, c), 0)
    col = lax.broadcasted_iota(I32, (c, c), 1)
    tri_le = (row <= col)
    tri_ge = (row >= col)
    eye = row == col
    ones_blk = jnp.ones((c, LANES), BF16)
    lane = lax.broadcasted_iota(I32, (HEADS, c), 1)

    def spread_rows(vecs):
        diag = jnp.concatenate([jnp.where(eye, v, 0.0) for v in vecs], axis=0)
        hi = diag.astype(BF16)
        lo = (diag - hi.astype(F32)).astype(BF16)
        out = _dot(hi, ones_blk) + _dot(lo, ones_blk)
        return [out[n * c:(n + 1) * c, :] for n in range(len(vecs))]

    def running_max(a, fwd):
        pm = a
        s = 1
        while s < c:
            if fwd:
                pm = jnp.where(lane >= s, jnp.maximum(pm, pltpu.roll(pm, s, axis=1)), pm)
            else:
                pm = jnp.where(lane < c - s, jnp.maximum(pm, pltpu.roll(pm, c - s, axis=1)), pm)
            s *= 2
        return pm

    n_st = 2 * HEADS
    rs_prev = [rs_ref[0, st] for st in range(n_st)]
    mc_prev = [mc_ref[0, st] for st in range(n_st)]
    mm_prev = [mm_ref[0, st:st + 1, 0:1] for st in range(n_st)]
    heads = [(dr, h) for dr in range(2) for h in range(HEADS)]
    o0 = 2 * r_w

    def cols(ref, base, h):
        return ref[0, :, base + h * LANES:base + (h + 1) * LANES]

    cs_rows = []
    for dr in range(2):
        gr = (grf_ref, grb_ref)[dr][0]
        tri = jnp.where(tri_le if dr == 0 else tri_ge, 1.0, 0.0).astype(BF16)
        cs_rows.append((gr, sum(_dot(piece, tri) for piece in _split3(gr))))
    ret_upd, ret_sc, ret_in, ml_sc, ml_in = [], [], [], [], []
    for dr, h in heads:
        st = dr * HEADS + h
        p_ref, kt_ref = (pf_ref, pb_ref)[dr], (ktf_ref, ktb_ref)[dr]
        kt = kt_ref[0, h * LANES:(h + 1) * LANES, :]
        ks = (kt.astype(F32) * kd_ref[st:st + 1, :]).astype(BF16)
        ret_upd.append(_dot(ks, cols(p_ref, r_w, h)))
        if with_output:
            q, mq = cols(p_ref, 0, h), cols(p_ref, o0, h)
            ret_sc.append(_dot(q, kt))
            ret_in.append(_dot(q, rs_prev[st].astype(BF16)))
            ml_sc.append(_dot(mq, kt_ref[0, r_w + h * LANES:r_w + (h + 1) * LANES, :]))
            ml_in.append(_dot(mq, mc_prev[st].astype(BF16)))

    gate = []
    for dr in range(2):
        gr, cs_row = cs_rows[dr]
        g0 = dr * 2 * HEADS
        a_rows = gr[g0:g0 + HEADS, :] - cs_row[g0 + HEADS:g0 + 2 * HEADS, :]
        pm_rows = running_max(a_rows, dr == 0)
        last = c - 1 if dr == 0 else 0
        for h in range(HEADS):
            st = dr * HEADS + h
            b_row = cs_row[g0 + HEADS + h:g0 + HEADS + h + 1, :]
            a_row, pm_row = a_rows[h:h + 1, :], pm_rows[h:h + 1, :]
            b_tot = b_row[:, last:last + 1]
            m_prev = mm_prev[st]
            m_next = b_tot + jnp.maximum(m_prev, pm_row[:, last:last + 1])
            gate.append(dict(b_row=b_row, a_row=a_row, pm_row=pm_row, m_prev=m_prev, m_next=m_next,
                             decay_prev=jnp.exp(b_tot + m_prev - m_next),
                             ws_row=jnp.exp(b_tot + a_row - m_next)))
    ml_upd, spread = [], []
    for dr, h in heads:
        st = dr * HEADS + h
        p_ref, kt_ref = (pf_ref, pb_ref)[dr], (ktf_ref, ktb_ref)[dr]
        mkt = kt_ref[0, r_w + h * LANES:r_w + (h + 1) * LANES, :]
        kw = (mkt.astype(F32) * gate[st]["ws_row"]).astype(BF16)
        v_ext = jnp.concatenate([cols(p_ref, o0 + r_w, h), ones_blk], axis=1)
        ml_upd.append(_dot(kw, v_ext))
        if with_output:
            spread.append(spread_rows([gate[st]["b_row"], gate[st]["pm_row"]]))

    if with_output:
        ret_out, ml_out, stab = [], [], []
        for dr, h in heads:
            st = dr * HEADS + h
            p_ref = (pf_ref, pb_ref)[dr]
            sc = (ret_sc[st] * intra_ref[st]).astype(BF16)
            ret_out.append(_dot(sc, cols(p_ref, r_w, h)))
            b_sp, pm_sp = spread[st]
            c_sp = jnp.maximum(gate[st]["m_prev"], pm_sp)
            causal = tri_ge if dr == 0 else tri_le
            w = jnp.where(causal, jnp.exp(gate[st]["a_row"] - c_sp), 0.0)
            v_ext = jnp.concatenate([cols(p_ref, o0 + r_w, h), ones_blk], axis=1)
            ml_out.append(_dot((ml_sc[st] * w).astype(BF16), v_ext))
            stab.append((jnp.exp(gate[st]["m_prev"] - c_sp), jnp.exp(-(b_sp + c_sp))))

    for dr, h in heads:
        st = dr * HEADS + h
        if with_output:
            o_ref = (of_ref, ob_ref)[dr]
            o_ref[0, :, h * LANES:(h + 1) * LANES] = (ret_out[st] + qd_ref[st] * ret_in[st]).astype(BF16)
            inter, floor = stab[st]
            hx = ml_out[st] + jnp.concatenate([inter, inter], axis=1) * ml_in[st]
            hout = hx[:, :LANES] / jnp.maximum(jnp.abs(hx[:, LANES:]), floor)
            o_ref[0, :, r_w + h * LANES:r_w + (h + 1) * LANES] = hout.astype(BF16)
    for dr, h in heads:
        st = dr * HEADS + h
        rs_ref[0, st] = rs_prev[st] * cd_ref[st:st + 1, :] + ret_upd[st]
        mc_ref[0, st] = gate[st]["decay_prev"] * mc_prev[st] + ml_upd[st]
        mm_ref[0, st:st + 1, :] = jnp.broadcast_to(gate[st]["m_next"], (1, LANES))


def _scan(p, kt, g_row, tabs, states, with_output):
    b, n, cols = p.shape
    nch = n // CHUNK
    r_w = cols // 6
    intra, kd, qd, cd = tabs
    rs0, mc0, mm0 = states
    nst = 2 * HEADS
    fwd3 = lambda bi, j: (bi, j, 0)
    bwd3 = lambda bi, j: (bi, nch - 1 - j, 0)
    c3 = lambda bi, j: (0, 0, 0)
    st4 = lambda bi, j: (bi, 0, 0, 0)
    in_specs = [
        pl.BlockSpec((1, CHUNK, 4 * r_w), fwd3),
        pl.BlockSpec((1, CHUNK, 4 * r_w), bwd3),
        pl.BlockSpec((1, 2 * r_w, CHUNK), lambda bi, j: (bi, 0, j)),
        pl.BlockSpec((1, 2 * r_w, CHUNK), lambda bi, j: (bi, 0, nch - 1 - j)),
        pl.BlockSpec((1, 16, CHUNK), lambda bi, j: (bi, 0, j)),
        pl.BlockSpec((1, 16, CHUNK), lambda bi, j: (bi, 0, nch - 1 - j)),
        pl.BlockSpec((nst, CHUNK, LANES), c3),
        pl.BlockSpec((nst, CHUNK), lambda bi, j: (0, 0)),
        pl.BlockSpec((nst, CHUNK, LANES), c3),
        pl.BlockSpec((nst, LANES), lambda bi, j: (0, 0)),
        pl.BlockSpec((1, nst, LANES, LANES), st4),
        pl.BlockSpec((1, nst, LANES, 2 * LANES), st4),
        pl.BlockSpec((1, nst, LANES), lambda bi, j: (bi, 0, 0)),
    ]
    st_specs = [
        pl.BlockSpec((1, nst, LANES, LANES), st4),
        pl.BlockSpec((1, nst, LANES, 2 * LANES), st4),
        pl.BlockSpec((1, nst, LANES), lambda bi, j: (bi, 0, 0)),
    ]
    st_shapes = [
        jax.ShapeDtypeStruct((b, nst, LANES, LANES), F32),
        jax.ShapeDtypeStruct((b, nst, LANES, 2 * LANES), F32),
        jax.ShapeDtypeStruct((b, nst, LANES), F32),
    ]
    if with_output:
        out_specs = [pl.BlockSpec((1, CHUNK, 2 * r_w), fwd3), pl.BlockSpec((1, CHUNK, 2 * r_w), bwd3)] + st_specs
        out_shape = [jax.ShapeDtypeStruct((b, n, 2 * r_w), BF16)] * 2 + st_shapes
    else:
        out_specs, out_shape = st_specs, st_shapes
    kern = functools.partial(_scan_kernel, with_output=with_output, r_w=r_w)
    return pl.pallas_call(
        kern,
        grid=(b, nch),
        in_specs=in_specs,
        out_specs=out_specs,
        out_shape=out_shape,
        compiler_params=pltpu.CompilerParams(dimension_semantics=("parallel", "arbitrary")),
        name="scan_out" if with_output else "scan_state",
    )(p, p, kt, kt, g_row, g_row, intra, kd, qd, cd, rs0, mc0, mm0)


def _post_kernel(x_ref, of_ref, ob_ref, rg_ref, mo_ref, mod_ref, ng_ref, hg_ref, wo_ref, wrh_ref, wrl_ref,
                 rb_ref, su_ref, x1_ref, h2_ref, rk_ref, wd_ref, cnt_ref, *, ts, d, n_exp):
    s = of_ref[0].astype(F32) + ob_ref[0].astype(F32)
    parts = []
    for gi in range(2 * HEADS):
        sl = s[:, gi * LANES:(gi + 1) * LANES]
        mu = jnp.mean(sl, axis=-1, keepdims=True)
        dv = sl - mu
        var = jnp.mean(dv * dv, axis=-1, keepdims=True)
        y = dv * lax.rsqrt(var + EPS) * hg_ref[:, gi * LANES:(gi + 1) * LANES]
        if gi < HEADS:
            gate = _silu(rg_ref[0, :, gi * LANES:(gi + 1) * LANES].astype(F32))
        else:
            gate = _sigmoid(mo_ref[0, :, (gi - HEADS) * LANES:(gi - HEADS + 1) * LANES].astype(F32))
        parts.append((y * gate).astype(BF16))
    mixed = jnp.concatenate(parts, axis=1)
    y = _dot(mixed, wo_ref[...])
    g1 = mod_ref[0, 2:3, :]
    sh2 = mod_ref[0, 3:4, :]
    sc2 = mod_ref[0, 4:5, :]
    x1 = x_ref[0] + g1 * _rms(y, ng_ref[1:2, :])
    x1_ref[0] = x1
    h2 = _rms(x1, ng_ref[2:3, :]) * (1.0 + sc2) + sh2
    h_hi = h2.astype(BF16)
    h2_ref[...] = h_hi

    h_lo = (h2 - h_hi.astype(F32)).astype(BF16)
    logits = _dot_nt(wrh_ref[...], h_hi) + _dot_nt(wrh_ref[...], h_lo) + _dot_nt(wrl_ref[...], h_hi)
    scores = _sigmoid(logits)
    sel = scores + rb_ref[...]
    gsz = n_exp // N_GROUPS
    iota_g = lax.broadcasted_iota(I32, (gsz, ts), 0).astype(F32)
    grp = []
    for gi in range(N_GROUPS):
        blk = sel[gi * gsz:(gi + 1) * gsz, :]
        m1 = jnp.max(blk, axis=0, keepdims=True)
        i1 = jnp.min(jnp.where(blk == m1, iota_g, float(gsz)), axis=0, keepdims=True)
        m2 = jnp.max(jnp.where(iota_g == i1, NEG_INF, blk), axis=0, keepdims=True)
        grp.append(m1 + m2)
    masked_parts = []
    for gi in range(N_GROUPS):
        rank = jnp.zeros((1, ts), F32)
        for gj in range(N_GROUPS):
            if gj == gi:
                continue
            beats = (grp[gj] >= grp[gi]) if gj < gi else (grp[gj] > grp[gi])
            rank = rank + jnp.where(beats, 1.0, 0.0)
        keep = rank < float(TOPK_GROUPS)
        masked_parts.append(jnp.where(keep, sel[gi * gsz:(gi + 1) * gsz, :], NEG_INF))
    masked = jnp.concatenate(masked_parts, axis=0)

    iota_e = lax.broadcasted_iota(I32, (n_exp, ts), 0).astype(F32)
    selmask = jnp.zeros((n_exp, ts), F32)
    for _ in range(TOP_K):
        mx = jnp.max(masked, axis=0, keepdims=True)
        ei = jnp.min(jnp.where(masked == mx, iota_e, float(n_exp)), axis=0, keepdims=True)
        hit = iota_e == ei
        selmask = jnp.where(hit, 1.0, selmask)
        masked = jnp.where(hit, NEG_INF, masked)
    picked = selmask > 0.0
    wsel = jnp.where(picked, scores, 0.0)
    wd_ref[...] = wsel / jnp.sum(wsel, axis=0, keepdims=True) * ROUTED_SCALE
    rank = _dot(selmask.astype(BF16), su_ref[...])
    rk_ref[...] = jnp.where(picked, rank, -1.0)
    cnt_ref[0] = _dot_nt(jnp.ones((8, ts), BF16), selmask.astype(BF16))


def _post(x, o_f, o_b, p, mod3, norm_g, head_g, w_out, wr_hi, wr_lo, rbias, ts):
    b, n, d = x.shape
    nt = n // ts
    t_all = b * n
    n_exp = wr_hi.shape[0]
    r_w = d // 2
    su = jnp.where(lax.broadcasted_iota(I32, (ts, ts), 0) < lax.broadcasted_iota(I32, (ts, ts), 1),
                   1.0, 0.0).astype(BF16)
    tok3 = lambda bi, i: (bi, i, 0)
    c2 = lambda bi, i: (0, 0)
    flat = lambda bi, i: (0, bi * nt + i)
    kern = functools.partial(_post_kernel, ts=ts, d=d, n_exp=n_exp)
    return pl.pallas_call(
        kern,
        grid=(b, nt),
        in_specs=[
            pl.BlockSpec((1, ts, d), tok3),
            pl.BlockSpec((1, ts, d), tok3),
            pl.BlockSpec((1, ts, d), tok3),
            pl.BlockSpec((1, ts, r_w), lambda bi, i: (bi, i, 4)),
            pl.BlockSpec((1, ts, r_w), lambda bi, i: (bi, i, 5)),
            pl.BlockSpec((1, N_MOD, d), lambda bi, i: (bi, 0, 0)),
            pl.BlockSpec((4, d), c2),
            pl.BlockSpec((1, d), c2),
            pl.BlockSpec((d, d), c2),
            pl.BlockSpec((n_exp, d), c2),
            pl.BlockSpec((n_exp, d), c2),
            pl.BlockSpec((n_exp, 1), c2),
            pl.BlockSpec((ts, ts), c2),
        ],
        out_specs=[
            pl.BlockSpec((1, ts, d), tok3),
            pl.BlockSpec((ts, d), lambda bi, i: (bi * nt + i, 0)),
            pl.BlockSpec((n_exp, ts), flat),
            pl.BlockSpec((n_exp, ts), flat),
            pl.BlockSpec((1, 8, n_exp), lambda bi, i: (bi * nt + i, 0, 0)),
        ],
        out_shape=[
            jax.ShapeDtypeStruct((b, n, d), F32),
            jax.ShapeDtypeStruct((t_all, d), BF16),
            jax.ShapeDtypeStruct((n_exp, t_all), F32),
            jax.ShapeDtypeStruct((n_exp, t_all), F32),
            jax.ShapeDtypeStruct((b * nt, 8, n_exp), F32),
        ],
        compiler_params=pltpu.CompilerParams(dimension_semantics=("parallel", "parallel")),
        name="post",
    )(x, o_f, o_b, p, p, mod3, norm_g, head_g, w_out, wr_hi, wr_lo, rbias, su)


def _slot_transfers(meta_ref, stage, buf, hbm, sems, n_exp, to_hbm, wait):
    for e in range(n_exp):
        start = pl.multiple_of(meta_ref[0, 0, e], ROW_ALIGN)
        fits_half = meta_ref[0, 0, 4 * n_exp + 2 + e] != 0
        for rows, cond in ((SLOT_ROWS // 2, fits_half), (SLOT_ROWS, jnp.logical_not(fits_half))):
            @pl.when(cond)
            def _():
                run = hbm.at[pl.ds(start, rows)]
                slot = stage.at[buf, pl.ds(e * SLOT_ROWS, rows)]
                cp = (pltpu.make_async_copy(slot, run, sems.at[buf]) if to_hbm
                      else pltpu.make_async_copy(run, slot, sems.at[buf]))
                if wait:
                    cp.wait()
                else:
                    cp.start(priority=e % 2)


def _overflow_copy(meta_ref, spill, hbm, sem, n_exp, j, i, to_hbm):
    e = meta_ref[0, 0, n_exp + j]
    src = meta_ref[0, 0, 3 * n_exp + j] + ROW_ALIGN * i
    dst = meta_ref[0, 0, e] + SLOT_ROWS + ROW_ALIGN * i
    piece = spill.at[pl.ds(pl.multiple_of(src, ROW_ALIGN), ROW_ALIGN)]
    rows = hbm.at[pl.ds(pl.multiple_of(dst, ROW_ALIGN), ROW_ALIGN)]
    return pltpu.make_async_copy(piece, rows, sem) if to_hbm else pltpu.make_async_copy(rows, piece, sem)


def _for_overflow_pieces(meta_ref, n_exp, fn):
    def per_expert(j, carry):
        def per_piece(i, c2):
            fn(j, i)
            return c2
        return lax.fori_loop(0, meta_ref[0, 0, 2 * n_exp + j], per_piece, carry)
    lax.fori_loop(0, meta_ref[0, 0, 4 * n_exp], per_expert, 0)


def _spill_matrix_rows(meta_ref, rk_ref, wd_ref, base, ts, n_exp):
    rows = (lax.broadcasted_iota(I32, (SPILL_CHUNK, ts), 0) + base).astype(F32)

    def per_expert(j, hit):
        e = meta_ref[0, 0, n_exp + j]
        rk = rk_ref[pl.ds(e, 1), :]
        val = 1.0 if wd_ref is None else wd_ref[pl.ds(e, 1), :]
        target = jnp.where(rk >= SLOT_ROWS, rk - SLOT_ROWS + meta_ref[0, 0, 3 * n_exp + j].astype(F32), -1.0)
        return jnp.where(target == rows, val, hit)

    return lax.fori_loop(0, meta_ref[0, 0, 4 * n_exp], per_expert,
                         jnp.zeros((SPILL_CHUNK, ts), F32)).astype(BF16)


def _zero_fill(seg_ref, xs_hbm, stage, sem, n_exp, n_blocks):
    tail = SLOT_ROWS + EXPERT_BLOCK
    stage[0, 0:tail, :] = jnp.zeros((tail, stage.shape[2]), stage.dtype)
    tails = []
    for e in range(n_exp):
        start = jnp.maximum(seg_ref[0, 0, e] - tail, 0)
        tails.append(pltpu.make_async_copy(stage.at[0, pl.ds(0, tail)],
                                           xs_hbm.at[pl.ds(pl.multiple_of(start, ROW_ALIGN), tail)], sem))
    for cp in tails:
        cp.start()
    n_used = seg_ref[0, 0, n_exp]

    def block_copy(i):
        row = pl.multiple_of(i * EXPERT_BLOCK, EXPERT_BLOCK)
        return pltpu.make_async_copy(stage.at[0, pl.ds(0, EXPERT_BLOCK)], xs_hbm.at[pl.ds(row, EXPERT_BLOCK)], sem)

    def start_block(i, carry):
        block_copy(i).start()
        return carry

    def wait_block(i, carry):
        block_copy(i).wait()
        return carry

    lax.fori_loop(n_used, n_blocks, start_block, 0)
    for cp in tails:
        cp.wait()
    lax.fori_loop(n_used, n_blocks, wait_block, 0)


def _dispatch_kernel(meta_ref, prv_ref, seg_ref, x_ref, rk_ref, xs_hbm, stage, spill, sems, sem_ov, *,
                     ts, n_exp, n_blocks):
    i = pl.program_id(0)
    buf = i % 2

    @pl.when(i == 0)
    def _():
        _zero_fill(seg_ref, xs_hbm, stage, sem_ov, n_exp, n_blocks)

    x = x_ref[...]
    slot_row = lax.broadcasted_iota(I32, (SLOT_ROWS, ts), 0).astype(F32)
    group_rows = SLOT_GROUP * SLOT_ROWS
    for g in range(n_exp // SLOT_GROUP):
        pick = jnp.concatenate(
            [jnp.where(rk_ref[e:e + 1, :] == slot_row, 1.0, 0.0)
             for e in range(g * SLOT_GROUP, (g + 1) * SLOT_GROUP)], axis=0).astype(BF16)
        stage[buf, g * group_rows:(g + 1) * group_rows, :] = _dot(pick, x).astype(BF16)
    n_spill = meta_ref[0, 0, 4 * n_exp + 1]

    def spill_chunk(ci, carry):
        base = pl.multiple_of(ci * SPILL_CHUNK, SPILL_CHUNK)
        spill[pl.ds(base, SPILL_CHUNK), :] = _dot(
            _spill_matrix_rows(meta_ref, rk_ref, None, base, ts, n_exp), x).astype(BF16)
        return carry

    lax.fori_loop(0, n_spill, spill_chunk, 0)

    @pl.when(i > 0)
    def _():
        _slot_transfers(prv_ref, stage, 1 - buf, xs_hbm, sems, n_exp, True, wait=True)

    _slot_transfers(meta_ref, stage, buf, xs_hbm, sems, n_exp, True, wait=False)
    _for_overflow_pieces(meta_ref, n_exp,
                         lambda e, k: _overflow_copy(meta_ref, spill, xs_hbm, sem_ov, n_exp, e, k, True).start())
    _for_overflow_pieces(meta_ref, n_exp,
                         lambda e, k: _overflow_copy(meta_ref, spill, xs_hbm, sem_ov, n_exp, e, k, True).wait())

    @pl.when(i == pl.num_programs(0) - 1)
    def _():
        _slot_transfers(meta_ref, stage, buf, xs_hbm, sems, n_exp, True, wait=True)


def _dispatch(h2, rank, meta, seg, p_rows, ts, n_exp):
    t_all, d = h2.shape
    nt = t_all // ts
    kern = functools.partial(_dispatch_kernel, ts=ts, n_exp=n_exp, n_blocks=p_rows // EXPERT_BLOCK)
    return pl.pallas_call(
        kern,
        grid=(nt,),
        in_specs=[
            pl.BlockSpec((1, 1, meta.shape[2]), lambda i: (i, 0, 0), memory_space=pltpu.SMEM),
            pl.BlockSpec((1, 1, meta.shape[2]), lambda i: (jnp.maximum(i - 1, 0), 0, 0), memory_space=pltpu.SMEM),
            pl.BlockSpec((1, 1, seg.shape[2]), lambda i: (0, 0, 0), memory_space=pltpu.SMEM),
            pl.BlockSpec((ts, d), lambda i: (i, 0)),
            pl.BlockSpec((n_exp, ts), lambda i: (0, i)),
        ],
        out_specs=pl.BlockSpec(memory_space=pl.ANY),
        out_shape=jax.ShapeDtypeStruct((p_rows, d), BF16),
        scratch_shapes=[pltpu.VMEM((2, n_exp * SLOT_ROWS, d), BF16),
                        pltpu.VMEM((ts * TOP_K, d), BF16),
                        pltpu.SemaphoreType.DMA((2,)), pltpu.SemaphoreType.DMA(())],
        compiler_params=pltpu.CompilerParams(dimension_semantics=("arbitrary",), has_side_effects=True),
        name="dispatch",
    )(meta, meta, seg, h2, rank)


def _expert_kernel(be_ref, nu_ref, xs_ref, wg_ref, wu_ref, wd_ref, ys_ref):
    del be_ref
    i = pl.program_id(0)

    @pl.when(i < nu_ref[0])
    def _():
        xb = xs_ref[...]
        a = _silu(_dot(xb, wg_ref[0])) * _dot(xb, wu_ref[0])
        ys_ref[...] = _dot(a.astype(BF16), wd_ref[0]).astype(BF16)


def _experts(xs, block_e, n_used, w_gate, w_up, w_down, blk):
    p_rows, dw = xs.shape
    n_exp, d, ff = w_gate.shape
    nb = p_rows // blk
    used = lambda i, nu: jnp.minimum(i, nu[0] - 1)
    grid_spec = pltpu.PrefetchScalarGridSpec(
        num_scalar_prefetch=2,
        grid=(nb,),
        in_specs=[
            pl.BlockSpec((blk, dw), lambda i, be, nu: (used(i, nu), 0)),
            pl.BlockSpec((1, d, ff), lambda i, be, nu: (be[used(i, nu)], 0, 0)),
            pl.BlockSpec((1, d, ff), lambda i, be, nu: (be[used(i, nu)], 0, 0)),
            pl.BlockSpec((1, ff, d), lambda i, be, nu: (be[used(i, nu)], 0, 0)),
        ],
        out_specs=pl.BlockSpec((blk, dw), lambda i, be, nu: (used(i, nu), 0)),
    )
    return pl.pallas_call(
        _expert_kernel,
        grid_spec=grid_spec,
        out_shape=jax.ShapeDtypeStruct((p_rows, dw), BF16),
        input_output_aliases={2: 0},
        compiler_params=pltpu.CompilerParams(dimension_semantics=("arbitrary",)),
        name="experts",
    )(block_e, n_used, xs, w_gate, w_up, w_down)


def _combine_kernel(meta_ref, nxt_ref, ys_hbm, rkt_ref, wdt_ref, rk_ref, wd_ref, ex_ref, rp_ref, x1_ref, h2_ref,
                    mod_ref, ng_ref, sg_ref, su_ref, sd_ref, o_ref, stage, spill, acc, sems, sem_ov, *,
                    ts, n_exp):
    i = pl.program_id(0)
    buf = i % 2
    n_spill = meta_ref[0, 0, 4 * n_exp + 1]
    n_groups = n_exp // SLOT_GROUP

    @pl.when(i == 0)
    def _():
        stage[...] = jnp.zeros(stage.shape, stage.dtype)
        _slot_transfers(meta_ref, stage, 0, ys_hbm, sems, n_exp, False, wait=False)

    @pl.when(i < pl.num_programs(0) - 1)
    def _():
        _slot_transfers(nxt_ref, stage, 1 - buf, ys_hbm, sems, n_exp, False, wait=False)

    def clear_chunk(ci, carry):
        base = pl.multiple_of(ci * SPILL_CHUNK, SPILL_CHUNK)
        spill[pl.ds(base, SPILL_CHUNK), :] = jnp.zeros((SPILL_CHUNK, spill.shape[1]), spill.dtype)
        return carry

    lax.fori_loop(0, n_spill, clear_chunk, 0)
    _for_overflow_pieces(meta_ref, n_exp,
                         lambda e, k: _overflow_copy(meta_ref, spill, ys_hbm, sem_ov, n_exp, e, k, False).start())

    xb = h2_ref[...]
    a = _silu(_dot(xb, sg_ref[...])) * _dot(xb, su_ref[...])
    tot = _dot(a.astype(BF16), sd_ref[...])

    rank_lanes = _dot(rkt_ref[...].astype(BF16), ex_ref[...])
    weight_lanes = _dot(wdt_ref[...].astype(BF16), ex_ref[...])

    _slot_transfers(meta_ref, stage, buf, ys_hbm, sems, n_exp, False, wait=True)
    group_rows = SLOT_GROUP * SLOT_ROWS
    for g in range(n_groups):
        cols = slice(g * group_rows, (g + 1) * group_rows)
        unmix = jnp.where(rank_lanes[:, cols] == rp_ref[:, cols], weight_lanes[:, cols], 0.0).astype(BF16)
        tot = tot + _dot(unmix, stage[buf, g * group_rows:(g + 1) * group_rows, :])
    acc[...] = tot

    _for_overflow_pieces(meta_ref, n_exp,
                         lambda e, k: _overflow_copy(meta_ref, spill, ys_hbm, sem_ov, n_exp, e, k, False).wait())

    def spill_chunk(ci, carry):
        base = pl.multiple_of(ci * SPILL_CHUNK, SPILL_CHUNK)
        acc[...] += _dot_tn(_spill_matrix_rows(meta_ref, rk_ref, wd_ref, base, ts, n_exp),
                            spill[pl.ds(base, SPILL_CHUNK), :])
        return carry

    lax.fori_loop(0, n_spill, spill_chunk, 0)
    g2 = mod_ref[0, 5:6, :]
    o_ref[...] = x1_ref[...] + g2 * _rms(acc[...], ng_ref[3:4, :])


def _combine(ys, meta, rank_tm, wd_tm, rank, wd, x1_flat, h2, mod3, norm_g, ws_gate, ws_up, ws_down,
             n_seq, ts, n_exp):
    t_all, d = x1_flat.shape
    nt = t_all // ts
    per_b = n_seq // ts
    ff = ws_gate.shape[1]
    lanes = n_exp * SLOT_ROWS
    lane = lax.broadcasted_iota(I32, (n_exp, lanes), 1)
    expand = jnp.where(lane // SLOT_ROWS == lax.broadcasted_iota(I32, (n_exp, lanes), 0), 1.0, 0.0).astype(BF16)
    slot_rank = (jnp.arange(lanes, dtype=I32) % SLOT_ROWS).astype(F32).reshape(1, lanes)
    c2 = lambda i: (0, 0)
    kern = functools.partial(_combine_kernel, ts=ts, n_exp=n_exp)
    return pl.pallas_call(
        kern,
        grid=(nt,),
        in_specs=[
            pl.BlockSpec((1, 1, meta.shape[2]), lambda i: (i, 0, 0), memory_space=pltpu.SMEM),
            pl.BlockSpec((1, 1, meta.shape[2]), lambda i: (jnp.minimum(i + 1, nt - 1), 0, 0),
                         memory_space=pltpu.SMEM),
            pl.BlockSpec(memory_space=pl.ANY),
            pl.BlockSpec((ts, n_exp), lambda i: (i, 0)),
            pl.BlockSpec((ts, n_exp), lambda i: (i, 0)),
            pl.BlockSpec((n_exp, ts), lambda i: (0, i)),
            pl.BlockSpec((n_exp, ts), lambda i: (0, i)),
            pl.BlockSpec((n_exp, lanes), c2),
            pl.BlockSpec((1, lanes), c2),
            pl.BlockSpec((ts, d), lambda i: (i, 0)),
            pl.BlockSpec((ts, d), lambda i: (i, 0)),
            pl.BlockSpec((1, N_MOD, d), lambda i: (i // per_b, 0, 0)),
            pl.BlockSpec((4, d), c2),
            pl.BlockSpec((d, ff), c2),
            pl.BlockSpec((d, ff), c2),
            pl.BlockSpec((ff, d), c2),
        ],
        out_specs=pl.BlockSpec((ts, d), lambda i: (i, 0)),
        out_shape=jax.ShapeDtypeStruct((t_all, d), F32),
        scratch_shapes=[pltpu.VMEM((2, lanes, d), BF16), pltpu.VMEM((ts * TOP_K, d), BF16),
                        pltpu.VMEM((ts, d), F32),
                        pltpu.SemaphoreType.DMA((2,)), pltpu.SemaphoreType.DMA(())],
        compiler_params=pltpu.CompilerParams(dimension_semantics=("arbitrary",)),
        name="combine",
    )(meta, meta, ys, rank_tm, wd_tm, rank, wd, expand, slot_rank, x1_flat, h2, mod3, norm_g,
      ws_gate, ws_up, ws_down)


def _rope_tables(n):
    rows = jnp.repeat(jnp.arange(n // GRID_W, dtype=F32), GRID_W)
    cols = jnp.tile(jnp.arange(GRID_W, dtype=F32), n // GRID_W)
    quarter = LANES // 4
    freqs = ROPE_BASE ** (-jnp.arange(quarter, dtype=F32) / quarter)
    ang = jnp.concatenate([rows[:, None] * freqs, cols[:, None] * freqs], axis=-1)
    cos, sin = jnp.cos(ang), jnp.sin(ang)
    return jnp.concatenate([cos, cos], axis=-1), jnp.concatenate([-sin, sin], axis=-1)


def _retention_tables(log_decay):
    lg = -jnp.exp(log_decay.astype(F32))
    idx = jnp.arange(CHUNK, dtype=F32)
    rel = idx[:, None] - idx[None, :]
    lg3 = lg[:, :, None, None]
    intra_f = jnp.where(rel >= 0, jnp.exp(jnp.maximum(rel, 0.0) * lg3[0]), 0.0)
    intra_b = jnp.where(rel <= 0, jnp.exp(jnp.maximum(-rel, 0.0) * lg3[1]), 0.0)
    kd_f = jnp.exp((CHUNK - 1 - idx)[None, :] * lg[0][:, None])
    kd_b = jnp.exp(idx[None, :] * lg[1][:, None])
    qd_f = jnp.exp((idx + 1)[None, :] * lg[0][:, None])
    qd_b = jnp.exp((CHUNK - idx)[None, :] * lg[1][:, None])
    bc = lambda t: jnp.broadcast_to(t[:, :, None], (HEADS, CHUNK, LANES))
    intra = jnp.concatenate([intra_f, intra_b], axis=0)
    kd = jnp.concatenate([kd_f, kd_b], axis=0)
    qd = jnp.concatenate([bc(qd_f), bc(qd_b)], axis=0)
    cd = jnp.broadcast_to(jnp.exp(CHUNK * lg).reshape(2 * HEADS, 1), (2 * HEADS, LANES))
    return intra, kd, qd, cd


def kernel(x, c, ctx, c_ctx, w_mod, b_mod, norm_g, w_in, ret_log_decay, ret_norm_g, mlstm_conv_w,
           mlstm_conv_b, mlstm_gate_b, mlstm_norm_g, w_out, w_router, router_bias, w_gate, w_up, w_down,
           ws_gate, ws_up, ws_down):
    b, n, d = x.shape
    n_ctx = ctx.shape[1]
    depth = w_mod.shape[0]
    assert depth == 1, "only the single-layer configuration is implemented"
    assert d // 2 // HEADS == LANES
    n_exp = w_router.shape[2]
    t_all = b * n
    r_w = d // 2
    main_cols = 8 * r_w
    l = 0

    pad = (-(b + 1)) % 8
    cc = jnp.concatenate([c, c_ctx[None, :], jnp.zeros((pad, d), F32)], axis=0)
    mod3 = _modulation(cc, w_mod[l], b_mod[l]).reshape(b + 1 + pad, N_MOD, d)

    w_groups = w_in[l, :, :main_cols].astype(BF16).reshape(d, 8, r_w)
    w_main = w_groups[:, jnp.array([0, 1, 2, 4, 5, 6, 3, 7]), :].reshape(d, main_cols)
    wg = w_in[l, :, main_cols:].astype(BF16)
    wgt = wg.T
    gb = mlstm_gate_b[l].reshape(-1).astype(F32)
    tabs = _retention_tables(ret_log_decay[l])
    head_g = jnp.concatenate([ret_norm_g[l], mlstm_norm_g[l]]).reshape(1, d).astype(F32)
    wr = w_router[l].T.astype(F32)
    wr_hi = wr.astype(BF16)
    wr_lo = (wr - wr_hi.astype(F32)).astype(BF16)

    def inproj(seq, mod_row, ts):
        cos2, sin2 = _rope_tables(n) if mod_row is None else (
            jnp.ones((seq.shape[1], LANES), F32), jnp.zeros((seq.shape[1], LANES), F32))
        return _inproj(seq, mod3, mod_row, norm_g[l, 0:1], w_main, wgt, mlstm_conv_w[l],
                       mlstm_conv_b[l].reshape(1, -1), gb.reshape(16, 1), cos2, sin2, ts)

    nst = 2 * HEADS
    zero_states = (jnp.zeros((b, nst, LANES, LANES), F32), jnp.zeros((b, nst, LANES, 2 * LANES), F32),
                   jnp.zeros((b, nst, LANES), F32))
    p_c, kt_c, gr_c = inproj(ctx, b, min(n_ctx, 512))
    ctx_states = _scan(p_c, kt_c, gr_c, tabs, zero_states, with_output=False)

    ts = min(n, 512)
    p_l, kt_l, gr_l = inproj(x, None, ts)
    o_f, o_b, _, _, _ = _scan(p_l, kt_l, gr_l, tabs, tuple(ctx_states), with_output=True)
    ts_moe = MOE_TILE
    x1, h2, rank, wdense, tile_cnt = _post(
        x, o_f, o_b, p_l, mod3, norm_g[l], head_g, w_out[l].astype(BF16), wr_hi, wr_lo,
        router_bias[l].reshape(n_exp, 1).astype(F32), ts_moe)

    nt = t_all // ts_moe
    cnt = tile_cnt[:, 0, :].astype(I32)
    run_rows = (cnt + ROW_ALIGN - 1) // ROW_ALIGN * ROW_ALIGN
    seg_cap = (jnp.sum(run_rows, axis=0) + SLOT_ROWS + EXPERT_BLOCK - 1) // EXPERT_BLOCK * EXPERT_BLOCK
    seg_end = jnp.cumsum(seg_cap)
    run_start = (seg_end - seg_cap)[None, :] + jnp.cumsum(run_rows, axis=0) - run_rows
    ov_rows = jnp.maximum(run_rows - SLOT_ROWS, 0)
    ov_off = jnp.cumsum(ov_rows, axis=1) - ov_rows
    n_spill = (jnp.sum(ov_rows, axis=1, keepdims=True) + SPILL_CHUNK - 1) // SPILL_CHUNK
    spills = ov_rows > 0
    n_ov = jnp.sum(spills.astype(I32), axis=1, keepdims=True)
    nth = jnp.cumsum(spills.astype(I32), axis=1) - 1
    is_jth = spills[:, None, :] & (nth[:, None, :] == jnp.arange(n_exp, dtype=I32)[None, :, None])
    compact = lambda v: jnp.sum(jnp.where(is_jth, v[:, None, :], 0), axis=2)
    ov_e = compact(jnp.broadcast_to(jnp.arange(n_exp, dtype=I32)[None, :], cnt.shape))
    fits_half = (run_rows <= SLOT_ROWS // 2).astype(I32)
    meta = jnp.concatenate([run_start, ov_e, compact(ov_rows // ROW_ALIGN), compact(ov_off), n_ov, n_spill,
                            fits_half], axis=1).astype(I32)
    meta = jnp.pad(meta, ((0, 0), (0, (-meta.shape[1]) % LANES))).reshape(nt, 1, -1)
    p_rows = -(-(t_all * TOP_K + nt * n_exp * (ROW_ALIGN - 1) + n_exp * (SLOT_ROWS + EXPERT_BLOCK - 1))
               // EXPERT_BLOCK) * EXPERT_BLOCK
    nb = p_rows // EXPERT_BLOCK
    blk_first = jnp.arange(nb, dtype=I32) * EXPERT_BLOCK
    block_e = jnp.minimum(jnp.sum((seg_end[None, :] <= blk_first[:, None]).astype(I32), axis=1), n_exp - 1)
    n_used = (seg_end[-1:] // EXPERT_BLOCK).astype(I32)
    seg = jnp.concatenate([seg_end.astype(I32), n_used])
    seg = jnp.pad(seg, (0, (-seg.shape[0]) % LANES)).reshape(1, 1, -1)

    xs = _dispatch(h2, rank, meta, seg, p_rows, ts_moe, n_exp)
    ys = _experts(xs, block_e, n_used, w_gate[l].astype(BF16), w_up[l].astype(BF16), w_down[l].astype(BF16),
                  EXPERT_BLOCK)
    out = _combine(ys, meta, rank.T, wdense.T, rank, wdense, x1.reshape(t_all, d), h2, mod3, norm_g[l],
                   ws_gate[l].astype(BF16), ws_up[l].astype(BF16), ws_down[l].astype(BF16), n, ts_moe, n_exp)
    return out.reshape(b, n, d)
```

```python
import functools

import jax
import jax.numpy as jnp
from jax import lax
from jax.experimental import pallas as pl
from jax.experimental.pallas import tpu as pltpu

F32 = jnp.float32
BF16 = jnp.bfloat16
I32 = jnp.int32

EPS = 1e-6
LANES = 128
CHUNK = 128
HEADS = 4
GRID_W = 64
ROPE_BASE = 10000.0
N_GROUPS = 8
TOPK_GROUPS = 4
TOP_K = 8
ROUTED_SCALE = 2.5
N_MOD = 6
MOE_TILE = 256
SPILL_CHUNK = 256
ROW_ALIGN = 16
SLOT_ROWS = 48
HALF_ROWS = 32
SLOT_GROUP = 8
EXPERT_BLOCK = 512
NEG_INF = float("-inf")
P_SLOT = {0: 0, 2: 1, 3: 2, 5: 3, 6: 4, 7: 5}


def _sigmoid(v):
    return 1.0 / (1.0 + jnp.exp(-v))


def _silu(v):
    return v * _sigmoid(v)


def _log_sigmoid(v):
    return jnp.minimum(v, 0.0) - jnp.log(1.0 + jnp.exp(-jnp.abs(v)))


def _dot(a, b):
    return jnp.dot(a, b, preferred_element_type=F32)


def _dot_nt(a, b):
    return lax.dot_general(a, b, (((1,), (1,)), ((), ())), preferred_element_type=F32)


def _dot_tn(a, b):
    return lax.dot_general(a, b, (((0,), (0,)), ((), ())), preferred_element_type=F32)


def _split3(a):
    hi = a.astype(BF16)
    r = a - hi.astype(F32)
    mid = r.astype(BF16)
    lo = (r - mid.astype(F32)).astype(BF16)
    return hi, mid, lo


def _rms(v, g):
    ms = jnp.mean(v * v, axis=-1, keepdims=True)
    return v * lax.rsqrt(ms + EPS) * g


def _mod_kernel(c_ref, w_ref, b_ref, o_ref):
    a = _silu(c_ref[...])
    o_ref[...] = jnp.dot(a, w_ref[...], preferred_element_type=F32,
                         precision=lax.Precision.HIGHEST) + b_ref[...]


def _modulation(cc, w_mod, b_mod):
    rows, d = cc.shape
    cols = w_mod.shape[1]
    tn = d
    return pl.pallas_call(
        _mod_kernel,
        grid=(cols // tn,),
        in_specs=[pl.BlockSpec((rows, d), lambda j: (0, 0)),
                  pl.BlockSpec((d, tn), lambda j: (0, j)),
                  pl.BlockSpec((1, tn), lambda j: (0, j))],
        out_specs=pl.BlockSpec((rows, tn), lambda j: (0, j)),
        out_shape=jax.ShapeDtypeStruct((rows, cols), F32),
        name="mod",
    )(cc, w_mod, b_mod.reshape(1, cols))


def _inproj_kernel(x_ref, xp_ref, xn_ref, mod_ref, g_ref, w_ref, wgt_ref, cw_ref, cb_ref,
                   gbr_ref, cos_ref, sin_ref, p_ref, kt_ref, gr_ref, *, ts, d):
    i = pl.program_id(1)
    last = pl.num_programs(1) - 1
    r_w = d // 2
    shift = mod_ref[0, 0:1, :]
    scale = mod_ref[0, 1:2, :]
    g = g_ref[...]

    def normmod(v):
        return _rms(v, g) * (1.0 + scale) + shift

    hb = normmod(x_ref[0]).astype(BF16)
    halo = jnp.concatenate([xp_ref[0], xn_ref[0]], axis=0)
    ph = _dot(normmod(halo).astype(BF16), w_ref[:, 3 * r_w:5 * r_w])
    prev_row = jnp.where(i == 0, 0.0, ph[7:8, :])
    next_row = jnp.where(i == last, 0.0, ph[8:9, :])

    cos2 = cos_ref[...]
    sin2 = sin_ref[...]
    rows = lax.broadcasted_iota(I32, (ts, r_w), 0)
    qscale = LANES ** -0.5

    for j in range(8):
        acc = _dot(hb, w_ref[:, j * r_w:(j + 1) * r_w])
        if j in (0, 1):
            if j == 0:
                acc = acc * qscale
            parts = []
            for h in range(HEADS):
                t = acc[:, h * LANES:(h + 1) * LANES]
                parts.append(t * cos2 + pltpu.roll(t, LANES // 2, axis=1) * sin2)
            acc = jnp.concatenate(parts, axis=1)
        elif j in (3, 4):
            c0 = (j - 3) * r_w
            pr = prev_row[:, c0:c0 + r_w]
            nx = next_row[:, c0:c0 + r_w]
            down = jnp.where(rows == 0, pr, pltpu.roll(acc, 1, axis=0))
            up = jnp.where(rows == ts - 1, nx, pltpu.roll(acc, ts - 1, axis=0))
            cw = cw_ref[:, c0:c0 + r_w]
            acc = down * cw[0:1, :] + acc * cw[1:2, :] + up * cw[2:3, :] + cb_ref[:, c0:c0 + r_w]
            acc = _silu(acc)
            if j == 4:
                acc = acc * qscale
        if j in (1, 4):
            kt_ref[0, (j // 4) * r_w:(j // 4 + 1) * r_w, :] = acc.T.astype(BF16)
        else:
            slot = P_SLOT[j]
            p_ref[0, :, slot * r_w:(slot + 1) * r_w] = acc.astype(BF16)

    gr = _dot_nt(wgt_ref[...], hb) + gbr_ref[...]
    ch_r = lax.broadcasted_iota(I32, gr.shape, 0)
    gr_ref[0] = jnp.where((ch_r // HEADS) % 2 == 1, _log_sigmoid(gr), gr)


def _inproj(x, mod3, mod_row, g, w_main, wgt, conv_w, conv_b, gb_col, cos2, sin2, ts):
    b, n, d = x.shape
    nt = n // ts
    nb8 = n // 8
    hb = ts // 8
    cols = w_main.shape[1]
    p_cols = cols // 8 * 6
    if mod_row is None:
        mod_map = lambda bi, i: (bi, 0, 0)
    else:
        mod_map = lambda bi, i: (mod_row, 0, 0)
    const2 = lambda bi, i: (0, 0)
    kern = functools.partial(_inproj_kernel, ts=ts, d=d)
    return pl.pallas_call(
        kern,
        grid=(b, nt),
        in_specs=[
            pl.BlockSpec((1, ts, d), lambda bi, i: (bi, i, 0)),
            pl.BlockSpec((1, 8, d), lambda bi, i: (bi, jnp.maximum(i * hb - 1, 0), 0)),
            pl.BlockSpec((1, 8, d), lambda bi, i: (bi, jnp.minimum((i + 1) * hb, nb8 - 1), 0)),
            pl.BlockSpec((1, N_MOD, d), mod_map),
            pl.BlockSpec((1, d), const2),
            pl.BlockSpec((d, cols), const2),
            pl.BlockSpec((16, d), const2),
            pl.BlockSpec((3, d), const2),
            pl.BlockSpec((1, d), const2),
            pl.BlockSpec((16, 1), const2),
            pl.BlockSpec((ts, LANES), lambda bi, i: (i, 0)),
            pl.BlockSpec((ts, LANES), lambda bi, i: (i, 0)),
        ],
        out_specs=[
            pl.BlockSpec((1, ts, p_cols), lambda bi, i: (bi, i, 0)),
            pl.BlockSpec((1, d, ts), lambda bi, i: (bi, 0, i)),
            pl.BlockSpec((1, 16, ts), lambda bi, i: (bi, 0, i)),
        ],
        out_shape=[
            jax.ShapeDtypeStruct((b, n, p_cols), BF16),
            jax.ShapeDtypeStruct((b, d, n), BF16),
            jax.ShapeDtypeStruct((b, 16, n), F32),
        ],
        compiler_params=pltpu.CompilerParams(dimension_semantics=("parallel", "parallel")),
        name="inproj",
    )(x, x, x, mod3, g, w_main, wgt, conv_w, conv_b, gb_col, cos2, sin2)


def _scan_kernel(pf_ref, pb_ref, ktf_ref, ktb_ref, grf_ref, grb_ref, intra_ref, kd_ref, qd_ref, cd_ref,
                 rs0_ref, mc0_ref, mm0_ref, *out_refs, with_output, r_w):
    if with_output:
        of_ref, ob_ref, rs_ref, mc_ref, mm_ref = out_refs
    else:
        rs_ref, mc_ref, mm_ref = out_refs
    j = pl.program_id(1)

    @pl.when(j == 0)
    def _():
        rs_ref[...] = rs0_ref[...]
        mc_ref[...] = mc0_ref[...]
        mm_ref[...] = mm0_ref[...]

    c = CHUNK
    row = lax.broadcasted_iota(I32, (c, c), 0)
    col = lax.broadcasted_iota(I32, (c, c), 1)
    tri_le = (row <= col)
    tri_ge = (row >= col)
    eye = row == col
    ones_blk = jnp.ones((c, LANES), BF16)
    lane = lax.broadcasted_iota(I32, (HEADS, c), 1)

    def spread_rows(vecs):
        diag = jnp.concatenate([jnp.where(eye, v, 0.0) for v in vecs], axis=0)
        hi = diag.astype(BF16)
        lo = (diag - hi.astype(F32)).astype(BF16)
        out = _dot(hi, ones_blk) + _dot(lo, ones_blk)
        return [out[n * c:(n + 1) * c, :] for n in range(len(vecs))]

    def running_max(a, fwd):
        pm = a
        s = 1
        while s < c:
            if fwd:
                pm = jnp.where(lane >= s, jnp.maximum(pm, pltpu.roll(pm, s, axis=1)), pm)
            else:
                pm = jnp.where(lane < c - s, jnp.maximum(pm, pltpu.roll(pm, c - s, axis=1)), pm)
            s *= 2
        return pm

    n_st = 2 * HEADS
    rs_prev = [rs_ref[0, st] for st in range(n_st)]
    mc_prev = [mc_ref[0, st] for st in range(n_st)]
    mm_prev = [mm_ref[0, st:st + 1, 0:1] for st in range(n_st)]
    heads = [(dr, h) for dr in range(2) for h in range(HEADS)]
    o0 = 2 * r_w

    def cols(ref, base, h):
        return ref[0, :, base + h * LANES:base + (h + 1) * LANES]

    cs_rows = []
    for dr in range(2):
        gr = (grf_ref, grb_ref)[dr][0]
        tri = jnp.where(tri_le if dr == 0 else tri_ge, 1.0, 0.0).astype(BF16)
        cs_rows.append((gr, sum(_dot(piece, tri) for piece in _split3(gr))))
    ret_upd, ret_sc, ret_in, ml_sc, ml_in = [], [], [], [], []
    for dr, h in heads:
        st = dr * HEADS + h
        p_ref, kt_ref = (pf_ref, pb_ref)[dr], (ktf_ref, ktb_ref)[dr]
        kt = kt_ref[0, h * LANES:(h + 1) * LANES, :]
        ks = (kt.astype(F32) * kd_ref[st:st + 1, :]).astype(BF16)
        ret_upd.append(_dot(ks, cols(p_ref, r_w, h)))
        if with_output:
            q, mq = cols(p_ref, 0, h), cols(p_ref, o0, h)
            ret_sc.append(_dot(q, kt))
            ret_in.append(_dot(q, rs_prev[st].astype(BF16)))
            ml_sc.append(_dot(mq, kt_ref[0, r_w + h * LANES:r_w + (h + 1) * LANES, :]))
            ml_in.append(_dot(mq, mc_prev[st].astype(BF16)))

    gate = []
    for dr in range(2):
        gr, cs_row = cs_rows[dr]
        g0 = dr * 2 * HEADS
        a_rows = gr[g0:g0 + HEADS, :] - cs_row[g0 + HEADS:g0 + 2 * HEADS, :]
        pm_rows = running_max(a_rows, dr == 0)
        last = c - 1 if dr == 0 else 0
        for h in range(HEADS):
            st = dr * HEADS + h
            b_row = cs_row[g0 + HEADS + h:g0 + HEADS + h + 1, :]
            a_row, pm_row = a_rows[h:h + 1, :], pm_rows[h:h + 1, :]
            b_tot = b_row[:, last:last + 1]
            m_prev = mm_prev[st]
            m_next = b_tot + jnp.maximum(m_prev, pm_row[:, last:last + 1])
            gate.append(dict(b_row=b_row, a_row=a_row, pm_row=pm_row, m_prev=m_prev, m_next=m_next,
                             decay_prev=jnp.exp(b_tot + m_prev - m_next),
                             ws_row=jnp.exp(b_tot + a_row - m_next)))
    ml_upd, spread = [], []
    for dr, h in heads:
        st = dr * HEADS + h
        p_ref, kt_ref = (pf_ref, pb_ref)[dr], (ktf_ref, ktb_ref)[dr]
        mkt = kt_ref[0, r_w + h * LANES:r_w + (h + 1) * LANES, :]
        kw = (mkt.astype(F32) * gate[st]["ws_row"]).astype(BF16)
        v_ext = jnp.concatenate([cols(p_ref, o0 + r_w, h), ones_blk], axis=1)
        ml_upd.append(_dot(kw, v_ext))
        if with_output:
            spread.append(spread_rows([gate[st]["b_row"], gate[st]["pm_row"]]))

    if with_output:
        ret_out, ml_out, stab = [], [], []
        for dr, h in heads:
            st = dr * HEADS + h
            p_ref = (pf_ref, pb_ref)[dr]
            sc = (ret_sc[st] * intra_ref[st]).astype(BF16)
            ret_out.append(_dot(sc, cols(p_ref, r_w, h)))
            b_sp, pm_sp = spread[st]
            c_sp = jnp.maximum(gate[st]["m_prev"], pm_sp)
            causal = tri_ge if dr == 0 else tri_le
            w = jnp.where(causal, jnp.exp(gate[st]["a_row"] - c_sp), 0.0)
            v_ext = jnp.concatenate([cols(p_ref, o0 + r_w, h), ones_blk], axis=1)
            ml_out.append(_dot((ml_sc[st] * w).astype(BF16), v_ext))
            stab.append((jnp.exp(gate[st]["m_prev"] - c_sp), jnp.exp(-(b_sp + c_sp))))

    for dr, h in heads:
        st = dr * HEADS + h
        if with_output:
            o_ref = (of_ref, ob_ref)[dr]
            o_ref[0, :, h * LANES:(h + 1) * LANES] = (ret_out[st] + qd_ref[st] * ret_in[st]).astype(BF16)
            inter, floor = stab[st]
            hx = ml_out[st] + jnp.concatenate([inter, inter], axis=1) * ml_in[st]
            hout = hx[:, :LANES] / jnp.maximum(jnp.abs(hx[:, LANES:]), floor)
            o_ref[0, :, r_w + h * LANES:r_w + (h + 1) * LANES] = hout.astype(BF16)
    for dr, h in heads:
        st = dr * HEADS + h
        rs_ref[0, st] = rs_prev[st] * cd_ref[st:st + 1, :] + ret_upd[st]
        mc_ref[0, st] = gate[st]["decay_prev"] * mc_prev[st] + ml_upd[st]
        mm_ref[0, st:st + 1, :] = jnp.broadcast_to(gate[st]["m_next"], (1, LANES))


def _scan(p, kt, g_row, tabs, states, with_output):
    b, n, cols = p.shape
    nch = n // CHUNK
    r_w = cols // 6
    intra, kd, qd, cd = tabs
    rs0, mc0, mm0 = states
    nst = 2 * HEADS
    fwd3 = lambda bi, j: (bi, j, 0)
    bwd3 = lambda bi, j: (bi, nch - 1 - j, 0)
    c3 = lambda bi, j: (0, 0, 0)
    st4 = lambda bi, j: (bi, 0, 0, 0)
    in_specs = [
        pl.BlockSpec((1, CHUNK, 4 * r_w), fwd3),
        pl.BlockSpec((1, CHUNK, 4 * r_w), bwd3),
        pl.BlockSpec((1, 2 * r_w, CHUNK), lambda bi, j: (bi, 0, j)),
        pl.BlockSpec((1, 2 * r_w, CHUNK), lambda bi, j: (bi, 0, nch - 1 - j)),
        pl.BlockSpec((1, 16, CHUNK), lambda bi, j: (bi, 0, j)),
        pl.BlockSpec((1, 16, CHUNK), lambda bi, j: (bi, 0, nch - 1 - j)),
        pl.BlockSpec((nst, CHUNK, LANES), c3),
        pl.BlockSpec((nst, CHUNK), lambda bi, j: (0, 0)),
        pl.BlockSpec((nst, CHUNK, LANES), c3),
        pl.BlockSpec((nst, LANES), lambda bi, j: (0, 0)),
        pl.BlockSpec((1, nst, LANES, LANES), st4),
        pl.BlockSpec((1, nst, LANES, 2 * LANES), st4),
        pl.BlockSpec((1, nst, LANES), lambda bi, j: (bi, 0, 0)),
    ]
    st_specs = [
        pl.BlockSpec((1, nst, LANES, LANES), st4),
        pl.BlockSpec((1, nst, LANES, 2 * LANES), st4),
        pl.BlockSpec((1, nst, LANES), lambda bi, j: (bi, 0, 0)),
    ]
    st_shapes = [
        jax.ShapeDtypeStruct((b, nst, LANES, LANES), F32),
        jax.ShapeDtypeStruct((b, nst, LANES, 2 * LANES), F32),
        jax.ShapeDtypeStruct((b, nst, LANES), F32),
    ]
    if with_output:
        out_specs = [pl.BlockSpec((1, CHUNK, 2 * r_w), fwd3), pl.BlockSpec((1, CHUNK, 2 * r_w), bwd3)] + st_specs
        out_shape = [jax.ShapeDtypeStruct((b, n, 2 * r_w), BF16)] * 2 + st_shapes
    else:
        out_specs, out_shape = st_specs, st_shapes
    kern = functools.partial(_scan_kernel, with_output=with_output, r_w=r_w)
    return pl.pallas_call(
        kern,
        grid=(b, nch),
        in_specs=in_specs,
        out_specs=out_specs,
        out_shape=out_shape,
        compiler_params=pltpu.CompilerParams(dimension_semantics=("parallel", "arbitrary")),
        name="scan_out" if with_output else "scan_state",
    )(p, p, kt, kt, g_row, g_row, intra, kd, qd, cd, rs0, mc0, mm0)


def _post_kernel(x_ref, of_ref, ob_ref, rg_ref, mo_ref, mod_ref, ng_ref, hg_ref, wo_ref, wrh_ref, wrl_ref,
                 rb_ref, su_ref, x1_ref, h2_ref, rk_ref, wd_ref, cnt_ref, *, ts, d, n_exp):
    s = of_ref[0].astype(F32) + ob_ref[0].astype(F32)
    parts = []
    for gi in range(2 * HEADS):
        sl = s[:, gi * LANES:(gi + 1) * LANES]
        mu = jnp.mean(sl, axis=-1, keepdims=True)
        dv = sl - mu
        var = jnp.mean(dv * dv, axis=-1, keepdims=True)
        y = dv * lax.rsqrt(var + EPS) * hg_ref[:, gi * LANES:(gi + 1) * LANES]
        if gi < HEADS:
            gate = _silu(rg_ref[0, :, gi * LANES:(gi + 1) * LANES].astype(F32))
        else:
            gate = _sigmoid(mo_ref[0, :, (gi - HEADS) * LANES:(gi - HEADS + 1) * LANES].astype(F32))
        parts.append((y * gate).astype(BF16))
    mixed = jnp.concatenate(parts, axis=1)
    y = _dot(mixed, wo_ref[...])
    g1 = mod_ref[0, 2:3, :]
    sh2 = mod_ref[0, 3:4, :]
    sc2 = mod_ref[0, 4:5, :]
    x1 = x_ref[0] + g1 * _rms(y, ng_ref[1:2, :])
    x1_ref[0] = x1
    h2 = _rms(x1, ng_ref[2:3, :]) * (1.0 + sc2) + sh2
    h_hi = h2.astype(BF16)
    h2_ref[...] = h_hi

    h_lo = (h2 - h_hi.astype(F32)).astype(BF16)
    logits = _dot_nt(wrh_ref[...], h_hi) + _dot_nt(wrh_ref[...], h_lo) + _dot_nt(wrl_ref[...], h_hi)
    scores = _sigmoid(logits)
    sel = scores + rb_ref[...]
    gsz = n_exp // N_GROUPS
    iota_g = lax.broadcasted_iota(I32, (gsz, ts), 0).astype(F32)
    grp = []
    for gi in range(N_GROUPS):
        blk = sel[gi * gsz:(gi + 1) * gsz, :]
        m1 = jnp.max(blk, axis=0, keepdims=True)
        i1 = jnp.min(jnp.where(blk == m1, iota_g, float(gsz)), axis=0, keepdims=True)
        m2 = jnp.max(jnp.where(iota_g == i1, NEG_INF, blk), axis=0, keepdims=True)
        grp.append(m1 + m2)
    masked_parts = []
    for gi in range(N_GROUPS):
        rank = jnp.zeros((1, ts), F32)
        for gj in range(N_GROUPS):
            if gj == gi:
                continue
            beats = (grp[gj] >= grp[gi]) if gj < gi else (grp[gj] > grp[gi])
            rank = rank + jnp.where(beats, 1.0, 0.0)
        keep = rank < float(TOPK_GROUPS)
        masked_parts.append(jnp.where(keep, sel[gi * gsz:(gi + 1) * gsz, :], NEG_INF))
    masked = jnp.concatenate(masked_parts, axis=0)

    iota_e = lax.broadcasted_iota(I32, (n_exp, ts), 0).astype(F32)
    selmask = jnp.zeros((n_exp, ts), F32)
    for _ in range(TOP_K):
        mx = jnp.max(masked, axis=0, keepdims=True)
        ei = jnp.min(jnp.where(masked == mx, iota_e, float(n_exp)), axis=0, keepdims=True)
        hit = iota_e == ei
        selmask = jnp.where(hit, 1.0, selmask)
        masked = jnp.where(hit, NEG_INF, masked)
    picked = selmask > 0.0
    wsel = jnp.where(picked, scores, 0.0)
    wd_ref[...] = wsel / jnp.sum(wsel, axis=0, keepdims=True) * ROUTED_SCALE
    rank = _dot(selmask.astype(BF16), su_ref[...])
    rk_ref[...] = jnp.where(picked, rank, -1.0)
    cnt_ref[0] = _dot_nt(jnp.ones((8, ts), BF16), selmask.astype(BF16))


def _post(x, o_f, o_b, p, mod3, norm_g, head_g, w_out, wr_hi, wr_lo, rbias, ts):
    b, n, d = x.shape
    nt = n // ts
    t_all = b * n
    n_exp = wr_hi.shape[0]
    r_w = d // 2
    su = jnp.where(lax.broadcasted_iota(I32, (ts, ts), 0) < lax.broadcasted_iota(I32, (ts, ts), 1),
                   1.0, 0.0).astype(BF16)
    tok3 = lambda bi, i: (bi, i, 0)
    c2 = lambda bi, i: (0, 0)
    flat = lambda bi, i: (0, bi * nt + i)
    kern = functools.partial(_post_kernel, ts=ts, d=d, n_exp=n_exp)
    return pl.pallas_call(
        kern,
        grid=(b, nt),
        in_specs=[
            pl.BlockSpec((1, ts, d), tok3),
            pl.BlockSpec((1, ts, d), tok3),
            pl.BlockSpec((1, ts, d), tok3),
            pl.BlockSpec((1, ts, r_w), lambda bi, i: (bi, i, 4)),
            pl.BlockSpec((1, ts, r_w), lambda bi, i: (bi, i, 5)),
            pl.BlockSpec((1, N_MOD, d), lambda bi, i: (bi, 0, 0)),
            pl.BlockSpec((4, d), c2),
            pl.BlockSpec((1, d), c2),
            pl.BlockSpec((d, d), c2),
            pl.BlockSpec((n_exp, d), c2),
            pl.BlockSpec((n_exp, d), c2),
            pl.BlockSpec((n_exp, 1), c2),
            pl.BlockSpec((ts, ts), c2),
        ],
        out_specs=[
            pl.BlockSpec((1, ts, d), tok3),
            pl.BlockSpec((ts, d), lambda bi, i: (bi * nt + i, 0)),
            pl.BlockSpec((n_exp, ts), flat),
            pl.BlockSpec((n_exp, ts), flat),
            pl.BlockSpec((1, 8, n_exp), lambda bi, i: (bi * nt + i, 0, 0)),
        ],
        out_shape=[
            jax.ShapeDtypeStruct((b, n, d), F32),
            jax.ShapeDtypeStruct((t_all, d), BF16),
            jax.ShapeDtypeStruct((n_exp, t_all), F32),
            jax.ShapeDtypeStruct((n_exp, t_all), F32),
            jax.ShapeDtypeStruct((b * nt, 8, n_exp), F32),
        ],
        compiler_params=pltpu.CompilerParams(dimension_semantics=("parallel", "parallel")),
        name="post",
    )(x, o_f, o_b, p, p, mod3, norm_g, head_g, w_out, wr_hi, wr_lo, rbias, su)


def _slot_transfers(meta_ref, stage, buf, hbm, sems, n_exp, to_hbm, wait):
    for e in range(n_exp):
        start = pl.multiple_of(meta_ref[0, 0, e], ROW_ALIGN)
        fits_half = meta_ref[0, 0, 4 * n_exp + 2 + e] != 0
        for rows, cond in ((HALF_ROWS, fits_half), (SLOT_ROWS, jnp.logical_not(fits_half))):
            @pl.when(cond)
            def _():
                run = hbm.at[pl.ds(start, rows)]
                slot = stage.at[buf, pl.ds(e * SLOT_ROWS, rows)]
                cp = (pltpu.make_async_copy(slot, run, sems.at[buf]) if to_hbm
                      else pltpu.make_async_copy(run, slot, sems.at[buf]))
                if wait:
                    cp.wait()
                else:
                    cp.start(priority=e % 2)


def _overflow_copy(meta_ref, spill, hbm, sem, n_exp, j, i, to_hbm):
    e = meta_ref[0, 0, n_exp + j]
    src = meta_ref[0, 0, 3 * n_exp + j] + ROW_ALIGN * i
    dst = meta_ref[0, 0, e] + SLOT_ROWS + ROW_ALIGN * i
    piece = spill.at[pl.ds(pl.multiple_of(src, ROW_ALIGN), ROW_ALIGN)]
    rows = hbm.at[pl.ds(pl.multiple_of(dst, ROW_ALIGN), ROW_ALIGN)]
    return pltpu.make_async_copy(piece, rows, sem) if to_hbm else pltpu.make_async_copy(rows, piece, sem)


def _for_overflow_pieces(meta_ref, n_exp, fn):
    def per_expert(j, carry):
        def per_piece(i, c2):
            fn(j, i)
            return c2
        return lax.fori_loop(0, meta_ref[0, 0, 2 * n_exp + j], per_piece, carry)
    lax.fori_loop(0, meta_ref[0, 0, 4 * n_exp], per_expert, 0)


def _spill_matrix_rows(meta_ref, rk_ref, wd_ref, base, ts, n_exp):
    rows = (lax.broadcasted_iota(I32, (SPILL_CHUNK, ts), 0) + base).astype(F32)

    def per_expert(j, hit):
        e = meta_ref[0, 0, n_exp + j]
        rk = rk_ref[pl.ds(e, 1), :]
        val = 1.0 if wd_ref is None else wd_ref[pl.ds(e, 1), :]
        target = jnp.where(rk >= SLOT_ROWS, rk - SLOT_ROWS + meta_ref[0, 0, 3 * n_exp + j].astype(F32), -1.0)
        return jnp.where(target == rows, val, hit)

    return lax.fori_loop(0, meta_ref[0, 0, 4 * n_exp], per_expert,
                         jnp.zeros((SPILL_CHUNK, ts), F32)).astype(BF16)


def _zero_fill(seg_ref, xs_hbm, stage, sem, n_exp, n_blocks):
    tail = SLOT_ROWS + EXPERT_BLOCK
    stage[0, 0:tail, :] = jnp.zeros((tail, stage.shape[2]), stage.dtype)
    tails = []
    for e in range(n_exp):
        start = jnp.maximum(seg_ref[0, 0, e] - tail, 0)
        tails.append(pltpu.make_async_copy(stage.at[0, pl.ds(0, tail)],
                                           xs_hbm.at[pl.ds(pl.multiple_of(start, ROW_ALIGN), tail)], sem))
    for cp in tails:
        cp.start()
    n_used = seg_ref[0, 0, n_exp]

    def block_copy(i):
        row = pl.multiple_of(i * EXPERT_BLOCK, EXPERT_BLOCK)
        return pltpu.make_async_copy(stage.at[0, pl.ds(0, EXPERT_BLOCK)], xs_hbm.at[pl.ds(row, EXPERT_BLOCK)], sem)

    def start_block(i, carry):
        block_copy(i).start()
        return carry

    def wait_block(i, carry):
        block_copy(i).wait()
        return carry

    lax.fori_loop(n_used, n_blocks, start_block, 0)
    for cp in tails:
        cp.wait()
    lax.fori_loop(n_used, n_blocks, wait_block, 0)


def _dispatch_kernel(meta_ref, prv_ref, seg_ref, x_ref, rk_ref, xs_hbm, stage, spill, sems, sem_ov, *,
                     ts, n_exp, n_blocks):
    i = pl.program_id(0)
    buf = i % 2

    @pl.when(i == 0)
    def _():
        _zero_fill(seg_ref, xs_hbm, stage, sem_ov, n_exp, n_blocks)

    x = x_ref[...]
    slot_row = lax.broadcasted_iota(I32, (SLOT_ROWS, ts), 0).astype(F32)
    group_rows = SLOT_GROUP * SLOT_ROWS
    for g in range(n_exp // SLOT_GROUP):
        pick = jnp.concatenate(
            [jnp.where(rk_ref[e:e + 1, :] == slot_row, 1.0, 0.0)
             for e in range(g * SLOT_GROUP, (g + 1) * SLOT_GROUP)], axis=0).astype(BF16)
        stage[buf, g * group_rows:(g + 1) * group_rows, :] = _dot(pick, x).astype(BF16)
    n_spill = meta_ref[0, 0, 4 * n_exp + 1]

    @pl.when(i > 0)
    def _():
        _for_overflow_pieces(prv_ref, n_exp,
                             lambda e, k: _overflow_copy(prv_ref, spill, xs_hbm, sem_ov, n_exp, e, k, True).wait())

    def spill_chunk(ci, carry):
        base = pl.multiple_of(ci * SPILL_CHUNK, SPILL_CHUNK)
        spill[pl.ds(base, SPILL_CHUNK), :] = _dot(
            _spill_matrix_rows(meta_ref, rk_ref, None, base, ts, n_exp), x).astype(BF16)
        return carry

    lax.fori_loop(0, n_spill, spill_chunk, 0)

    @pl.when(i > 0)
    def _():
        _slot_transfers(prv_ref, stage, 1 - buf, xs_hbm, sems, n_exp, True, wait=True)

    _slot_transfers(meta_ref, stage, buf, xs_hbm, sems, n_exp, True, wait=False)
    _for_overflow_pieces(meta_ref, n_exp,
                         lambda e, k: _overflow_copy(meta_ref, spill, xs_hbm, sem_ov, n_exp, e, k, True).start())

    @pl.when(i == pl.num_programs(0) - 1)
    def _():
        _slot_transfers(meta_ref, stage, buf, xs_hbm, sems, n_exp, True, wait=True)
        _for_overflow_pieces(meta_ref, n_exp,
                             lambda e, k: _overflow_copy(meta_ref, spill, xs_hbm, sem_ov, n_exp, e, k, True).wait())


def _dispatch(h2, rank, meta, seg, p_rows, ts, n_exp):
    t_all, d = h2.shape
    nt = t_all // ts
    kern = functools.partial(_dispatch_kernel, ts=ts, n_exp=n_exp, n_blocks=p_rows // EXPERT_BLOCK)
    return pl.pallas_call(
        kern,
        grid=(nt,),
        in_specs=[
            pl.BlockSpec((1, 1, meta.shape[2]), lambda i: (i, 0, 0), memory_space=pltpu.SMEM),
            pl.BlockSpec((1, 1, meta.shape[2]), lambda i: (jnp.maximum(i - 1, 0), 0, 0), memory_space=pltpu.SMEM),
            pl.BlockSpec((1, 1, seg.shape[2]), lambda i: (0, 0, 0), memory_space=pltpu.SMEM),
            pl.BlockSpec((ts, d), lambda i: (i, 0)),
            pl.BlockSpec((n_exp, ts), lambda i: (0, i)),
        ],
        out_specs=pl.BlockSpec(memory_space=pl.ANY),
        out_shape=jax.ShapeDtypeStruct((p_rows, d), BF16),
        scratch_shapes=[pltpu.VMEM((2, n_exp * SLOT_ROWS, d), BF16),
                        pltpu.VMEM((ts * TOP_K, d), BF16),
                        pltpu.SemaphoreType.DMA((2,)), pltpu.SemaphoreType.DMA(())],
        compiler_params=pltpu.CompilerParams(dimension_semantics=("arbitrary",), has_side_effects=True),
        name="dispatch",
    )(meta, meta, seg, h2, rank)


def _expert_kernel(be_ref, nu_ref, xs_ref, wg_ref, wu_ref, wd_ref, ys_ref):
    del be_ref
    i = pl.program_id(0)

    @pl.when(i < nu_ref[0])
    def _():
        xb = xs_ref[...]
        a = _silu(_dot(xb, wg_ref[0])) * _dot(xb, wu_ref[0])
        ys_ref[...] = _dot(a.astype(BF16), wd_ref[0]).astype(BF16)


def _experts(xs, block_e, n_used, w_gate, w_up, w_down, blk):
    p_rows, dw = xs.shape
    n_exp, d, ff = w_gate.shape
    nb = p_rows // blk
    used = lambda i, nu: jnp.minimum(i, nu[0] - 1)
    grid_spec = pltpu.PrefetchScalarGridSpec(
        num_scalar_prefetch=2,
        grid=(nb,),
        in_specs=[
            pl.BlockSpec((blk, dw), lambda i, be, nu: (used(i, nu), 0)),
            pl.BlockSpec((1, d, ff), lambda i, be, nu: (be[used(i, nu)], 0, 0)),
            pl.BlockSpec((1, d, ff), lambda i, be, nu: (be[used(i, nu)], 0, 0)),
            pl.BlockSpec((1, ff, d), lambda i, be, nu: (be[used(i, nu)], 0, 0)),
        ],
        out_specs=pl.BlockSpec((blk, dw), lambda i, be, nu: (used(i, nu), 0)),
    )
    return pl.pallas_call(
        _expert_kernel,
        grid_spec=grid_spec,
        out_shape=jax.ShapeDtypeStruct((p_rows, dw), BF16),
        input_output_aliases={2: 0},
        compiler_params=pltpu.CompilerParams(dimension_semantics=("arbitrary",)),
        name="experts",
    )(block_e, n_used, xs, w_gate, w_up, w_down)


def _combine_kernel(meta_ref, nxt_ref, ys_hbm, rkt_ref, wdt_ref, rk_ref, wd_ref, ex_ref, rp_ref, x1_ref, h2_ref,
                    mod_ref, ng_ref, sg_ref, su_ref, sd_ref, o_ref, stage, spill, acc, sems, sem_ov, *,
                    ts, n_exp):
    i = pl.program_id(0)
    buf = i % 2
    n_spill = meta_ref[0, 0, 4 * n_exp + 1]
    n_groups = n_exp // SLOT_GROUP

    @pl.when(i == 0)
    def _():
        stage[...] = jnp.zeros(stage.shape, stage.dtype)
        _slot_transfers(meta_ref, stage, 0, ys_hbm, sems, n_exp, False, wait=False)

    @pl.when(i < pl.num_programs(0) - 1)
    def _():
        _slot_transfers(nxt_ref, stage, 1 - buf, ys_hbm, sems, n_exp, False, wait=False)

    def clear_chunk(ci, carry):
        base = pl.multiple_of(ci * SPILL_CHUNK, SPILL_CHUNK)
        spill[pl.ds(base, SPILL_CHUNK), :] = jnp.zeros((SPILL_CHUNK, spill.shape[1]), spill.dtype)
        return carry

    lax.fori_loop(0, n_spill, clear_chunk, 0)
    _for_overflow_pieces(meta_ref, n_exp,
                         lambda e, k: _overflow_copy(meta_ref, spill, ys_hbm, sem_ov, n_exp, e, k, False).start())

    xb = h2_ref[...]
    a = _silu(_dot(xb, sg_ref[...])) * _dot(xb, su_ref[...])
    tot = _dot(a.astype(BF16), sd_ref[...])

    rank_lanes = _dot(rkt_ref[...].astype(BF16), ex_ref[...])
    weight_lanes = _dot(wdt_ref[...].astype(BF16), ex_ref[...])

    _slot_transfers(meta_ref, stage, buf, ys_hbm, sems, n_exp, False, wait=True)
    group_rows = SLOT_GROUP * SLOT_ROWS
    for g in range(n_groups):
        cols = slice(g * group_rows, (g + 1) * group_rows)
        unmix = jnp.where(rank_lanes[:, cols] == rp_ref[:, cols], weight_lanes[:, cols], 0.0).astype(BF16)
        tot = tot + _dot(unmix, stage[buf, g * group_rows:(g + 1) * group_rows, :])
    acc[...] = tot

    _for_overflow_pieces(meta_ref, n_exp,
                         lambda e, k: _overflow_copy(meta_ref, spill, ys_hbm, sem_ov, n_exp, e, k, False).wait())

    def spill_chunk(ci, carry):
        base = pl.multiple_of(ci * SPILL_CHUNK, SPILL_CHUNK)
        acc[...] += _dot_tn(_spill_matrix_rows(meta_ref, rk_ref, wd_ref, base, ts, n_exp),
                            spill[pl.ds(base, SPILL_CHUNK), :])
        return carry

    lax.fori_loop(0, n_spill, spill_chunk, 0)
    g2 = mod_ref[0, 5:6, :]
    o_ref[...] = x1_ref[...] + g2 * _rms(acc[...], ng_ref[3:4, :])


def _combine(ys, meta, rank_tm, wd_tm, rank, wd, x1_flat, h2, mod3, norm_g, ws_gate, ws_up, ws_down,
             n_seq, ts, n_exp):
    t_all, d = x1_flat.shape
    nt = t_all // ts
    per_b = n_seq // ts
    ff = ws_gate.shape[1]
    lanes = n_exp * SLOT_ROWS
    lane = lax.broadcasted_iota(I32, (n_exp, lanes), 1)
    expand = jnp.where(lane // SLOT_ROWS == lax.broadcasted_iota(I32, (n_exp, lanes), 0), 1.0, 0.0).astype(BF16)
    slot_rank = (jnp.arange(lanes, dtype=I32) % SLOT_ROWS).astype(F32).reshape(1, lanes)
    c2 = lambda i: (0, 0)
    kern = functools.partial(_combine_kernel, ts=ts, n_exp=n_exp)
    return pl.pallas_call(
        kern,
        grid=(nt,),
        in_specs=[
            pl.BlockSpec((1, 1, meta.shape[2]), lambda i: (i, 0, 0), memory_space=pltpu.SMEM),
            pl.BlockSpec((1, 1, meta.shape[2]), lambda i: (jnp.minimum(i + 1, nt - 1), 0, 0),
                         memory_space=pltpu.SMEM),
            pl.BlockSpec(memory_space=pl.ANY),
            pl.BlockSpec((ts, n_exp), lambda i: (i, 0)),
            pl.BlockSpec((ts, n_exp), lambda i: (i, 0)),
            pl.BlockSpec((n_exp, ts), lambda i: (0, i)),
            pl.BlockSpec((n_exp, ts), lambda i: (0, i)),
            pl.BlockSpec((n_exp, lanes), c2),
            pl.BlockSpec((1, lanes), c2),
            pl.BlockSpec((ts, d), lambda i: (i, 0)),
            pl.BlockSpec((ts, d), lambda i: (i, 0)),
            pl.BlockSpec((1, N_MOD, d), lambda i: (i // per_b, 0, 0)),
            pl.BlockSpec((4, d), c2),
            pl.BlockSpec((d, ff), c2),
            pl.BlockSpec((d, ff), c2),
            pl.BlockSpec((ff, d), c2),
        ],
        out_specs=pl.BlockSpec((ts, d), lambda i: (i, 0)),
        out_shape=jax.ShapeDtypeStruct((t_all, d), F32),
        scratch_shapes=[pltpu.VMEM((2, lanes, d), BF16), pltpu.VMEM((ts * TOP_K, d), BF16),
                        pltpu.VMEM((ts, d), F32),
                        pltpu.SemaphoreType.DMA((2,)), pltpu.SemaphoreType.DMA(())],
        compiler_params=pltpu.CompilerParams(dimension_semantics=("arbitrary",)),
        name="combine",
    )(meta, meta, ys, rank_tm, wd_tm, rank, wd, expand, slot_rank, x1_flat, h2, mod3, norm_g,
      ws_gate, ws_up, ws_down)


def _rope_tables(n):
    rows = jnp.repeat(jnp.arange(n // GRID_W, dtype=F32), GRID_W)
    cols = jnp.tile(jnp.arange(GRID_W, dtype=F32), n // GRID_W)
    quarter = LANES // 4
    freqs = ROPE_BASE ** (-jnp.arange(quarter, dtype=F32) / quarter)
    ang = jnp.concatenate([rows[:, None] * freqs, cols[:, None] * freqs], axis=-1)
    cos, sin = jnp.cos(ang), jnp.sin(ang)
    return jnp.concatenate([cos, cos], axis=-1), jnp.concatenate([-sin, sin], axis=-1)


def _retention_tables(log_decay):
    lg = -jnp.exp(log_decay.astype(F32))
    idx = jnp.arange(CHUNK, dtype=F32)
    rel = idx[:, None] - idx[None, :]
    lg3 = lg[:, :, None, None]
    intra_f = jnp.where(rel >= 0, jnp.exp(jnp.maximum(rel, 0.0) * lg3[0]), 0.0)
    intra_b = jnp.where(rel <= 0, jnp.exp(jnp.maximum(-rel, 0.0) * lg3[1]), 0.0)
    kd_f = jnp.exp((CHUNK - 1 - idx)[None, :] * lg[0][:, None])
    kd_b = jnp.exp(idx[None, :] * lg[1][:, None])
    qd_f = jnp.exp((idx + 1)[None, :] * lg[0][:, None])
    qd_b = jnp.exp((CHUNK - idx)[None, :] * lg[1][:, None])
    bc = lambda t: jnp.broadcast_to(t[:, :, None], (HEADS, CHUNK, LANES))
    intra = jnp.concatenate([intra_f, intra_b], axis=0)
    kd = jnp.concatenate([kd_f, kd_b], axis=0)
    qd = jnp.concatenate([bc(qd_f), bc(qd_b)], axis=0)
    cd = jnp.broadcast_to(jnp.exp(CHUNK * lg).reshape(2 * HEADS, 1), (2 * HEADS, LANES))
    return intra, kd, qd, cd


def kernel(x, c, ctx, c_ctx, w_mod, b_mod, norm_g, w_in, ret_log_decay, ret_norm_g, mlstm_conv_w,
           mlstm_conv_b, mlstm_gate_b, mlstm_norm_g, w_out, w_router, router_bias, w_gate, w_up, w_down,
           ws_gate, ws_up, ws_down):
    b, n, d = x.shape
    n_ctx = ctx.shape[1]
    depth = w_mod.shape[0]
    assert depth == 1, "only the single-layer configuration is implemented"
    assert d // 2 // HEADS == LANES
    n_exp = w_router.shape[2]
    t_all = b * n
    r_w = d // 2
    main_cols = 8 * r_w
    l = 0

    pad = (-(b + 1)) % 8
    cc = jnp.concatenate([c, c_ctx[None, :], jnp.zeros((pad, d), F32)], axis=0)
    mod3 = _modulation(cc, w_mod[l], b_mod[l]).reshape(b + 1 + pad, N_MOD, d)

    w_groups = w_in[l, :, :main_cols].astype(BF16).reshape(d, 8, r_w)
    w_main = w_groups[:, jnp.array([0, 1, 2, 4, 5, 6, 3, 7]), :].reshape(d, main_cols)
    wg = w_in[l, :, main_cols:].astype(BF16)
    wgt = wg.T
    gb = mlstm_gate_b[l].reshape(-1).astype(F32)
    tabs = _retention_tables(ret_log_decay[l])
    head_g = jnp.concatenate([ret_norm_g[l], mlstm_norm_g[l]]).reshape(1, d).astype(F32)
    wr = w_router[l].T.astype(F32)
    wr_hi = wr.astype(BF16)
    wr_lo = (wr - wr_hi.astype(F32)).astype(BF16)

    def inproj(seq, mod_row, ts):
        cos2, sin2 = _rope_tables(n) if mod_row is None else (
            jnp.ones((seq.shape[1], LANES), F32), jnp.zeros((seq.shape[1], LANES), F32))
        return _inproj(seq, mod3, mod_row, norm_g[l, 0:1], w_main, wgt, mlstm_conv_w[l],
                       mlstm_conv_b[l].reshape(1, -1), gb.reshape(16, 1), cos2, sin2, ts)

    nst = 2 * HEADS
    zero_states = (jnp.zeros((b, nst, LANES, LANES), F32), jnp.zeros((b, nst, LANES, 2 * LANES), F32),
                   jnp.zeros((b, nst, LANES), F32))
    p_c, kt_c, gr_c = inproj(ctx, b, min(n_ctx, 512))
    ctx_states = _scan(p_c, kt_c, gr_c, tabs, zero_states, with_output=False)

    ts = min(n, 512)
    p_l, kt_l, gr_l = inproj(x, None, ts)
    o_f, o_b, _, _, _ = _scan(p_l, kt_l, gr_l, tabs, tuple(ctx_states), with_output=True)
    ts_moe = MOE_TILE
    x1, h2, rank, wdense, tile_cnt = _post(
        x, o_f, o_b, p_l, mod3, norm_g[l], head_g, w_out[l].astype(BF16), wr_hi, wr_lo,
        router_bias[l].reshape(n_exp, 1).astype(F32), ts_moe)

    nt = t_all // ts_moe
    cnt = tile_cnt[:, 0, :].astype(I32)
    run_rows = (cnt + ROW_ALIGN - 1) // ROW_ALIGN * ROW_ALIGN
    seg_cap = (jnp.sum(run_rows, axis=0) + SLOT_ROWS + EXPERT_BLOCK - 1) // EXPERT_BLOCK * EXPERT_BLOCK
    seg_end = jnp.cumsum(seg_cap)
    run_start = (seg_end - seg_cap)[None, :] + jnp.cumsum(run_rows, axis=0) - run_rows
    ov_rows = jnp.maximum(run_rows - SLOT_ROWS, 0)
    ov_off = jnp.cumsum(ov_rows, axis=1) - ov_rows
    n_spill = (jnp.sum(ov_rows, axis=1, keepdims=True) + SPILL_CHUNK - 1) // SPILL_CHUNK
    spills = ov_rows > 0
    n_ov = jnp.sum(spills.astype(I32), axis=1, keepdims=True)
    nth = jnp.cumsum(spills.astype(I32), axis=1) - 1
    is_jth = spills[:, None, :] & (nth[:, None, :] == jnp.arange(n_exp, dtype=I32)[None, :, None])
    compact = lambda v: jnp.sum(jnp.where(is_jth, v[:, None, :], 0), axis=2)
    ov_e = compact(jnp.broadcast_to(jnp.arange(n_exp, dtype=I32)[None, :], cnt.shape))
    fits_half = (run_rows <= HALF_ROWS).astype(I32)
    meta = jnp.concatenate([run_start, ov_e, compact(ov_rows // ROW_ALIGN), compact(ov_off), n_ov, n_spill,
                            fits_half], axis=1).astype(I32)
    meta = jnp.pad(meta, ((0, 0), (0, (-meta.shape[1]) % LANES))).reshape(nt, 1, -1)
    p_rows = -(-(t_all * TOP_K + nt * n_exp * (ROW_ALIGN - 1) + n_exp * (SLOT_ROWS + EXPERT_BLOCK - 1))
               // EXPERT_BLOCK) * EXPERT_BLOCK
    nb = p_rows // EXPERT_BLOCK
    blk_first = jnp.arange(nb, dtype=I32) * EXPERT_BLOCK
    block_e = jnp.minimum(jnp.sum((seg_end[None, :] <= blk_first[:, None]).astype(I32), axis=1), n_exp - 1)
    n_used = (seg_end[-1:] // EXPERT_BLOCK).astype(I32)
    seg = jnp.concatenate([seg_end.astype(I32), n_used])
    seg = jnp.pad(seg, (0, (-seg.shape[0]) % LANES)).reshape(1, 1, -1)

    xs = _dispatch(h2, rank, meta, seg, p_rows, ts_moe, n_exp)
    ys = _experts(xs, block_e, n_used, w_gate[l].astype(BF16), w_up[l].astype(BF16), w_down[l].astype(BF16),
                  EXPERT_BLOCK)
    out = _combine(ys, meta, rank.T, wdense.T, rank, wdense, x1.reshape(t_all, d), h2, mod3, norm_g[l],
                   ws_gate[l].astype(BF16), ws_up[l].astype(BF16), ws_down[l].astype(BF16), n, ts_moe, n_exp)
    return out.reshape(b, n, d)
```

```python
import functools

import jax
import jax.numpy as jnp
from jax import lax
from jax.experimental import pallas as pl
from jax.experimental.pallas import tpu as pltpu

F32 = jnp.float32
BF16 = jnp.bfloat16
I32 = jnp.int32

EPS = 1e-6
LANES = 128
CHUNK = 128
HEADS = 4
GRID_W = 64
ROPE_BASE = 10000.0
N_GROUPS = 8
TOPK_GROUPS = 4
TOP_K = 8
ROUTED_SCALE = 2.5
N_MOD = 6
MOE_TILE = 256
SPILL_CHUNK = 256
ROW_ALIGN = 16
SLOT_ROWS = 64
HALF_ROWS = 32
SLOT_GROUP = 8
EXPERT_BLOCK = 512
NEG_INF = float("-inf")
P_SLOT = {0: 0, 2: 1, 3: 2, 5: 3, 6: 4, 7: 5}


def _sigmoid(v):
    return 1.0 / (1.0 + jnp.exp(-v))


def _silu(v):
    return v * _sigmoid(v)


def _log_sigmoid(v):
    return jnp.minimum(v, 0.0) - jnp.log(1.0 + jnp.exp(-jnp.abs(v)))


def _dot(a, b):
    return jnp.dot(a, b, preferred_element_type=F32)


def _dot_nt(a, b):
    return lax.dot_general(a, b, (((1,), (1,)), ((), ())), preferred_element_type=F32)


def _dot_tn(a, b):
    return lax.dot_general(a, b, (((0,), (0,)), ((), ())), preferred_element_type=F32)


def _split3(a):
    hi = a.astype(BF16)
    r = a - hi.astype(F32)
    mid = r.astype(BF16)
    lo = (r - mid.astype(F32)).astype(BF16)
    return hi, mid, lo


def _rms(v, g):
    ms = jnp.mean(v * v, axis=-1, keepdims=True)
    return v * lax.rsqrt(ms + EPS) * g


def _mod_kernel(c_ref, w_ref, b_ref, o_ref):
    a = _silu(c_ref[...])
    o_ref[...] = jnp.dot(a, w_ref[...], preferred_element_type=F32,
                         precision=lax.Precision.HIGHEST) + b_ref[...]


def _modulation(cc, w_mod, b_mod):
    rows, d = cc.shape
    cols = w_mod.shape[1]
    tn = d
    return pl.pallas_call(
        _mod_kernel,
        grid=(cols // tn,),
        in_specs=[pl.BlockSpec((rows, d), lambda j: (0, 0)),
                  pl.BlockSpec((d, tn), lambda j: (0, j)),
                  pl.BlockSpec((1, tn), lambda j: (0, j))],
        out_specs=pl.BlockSpec((rows, tn), lambda j: (0, j)),
        out_shape=jax.ShapeDtypeStruct((rows, cols), F32),
        name="mod",
    )(cc, w_mod, b_mod.reshape(1, cols))


def _inproj_kernel(x_ref, xp_ref, xn_ref, mod_ref, g_ref, w_ref, wgt_ref, cw_ref, cb_ref,
                   gbr_ref, cos_ref, sin_ref, p_ref, kt_ref, gr_ref, *, ts, d):
    i = pl.program_id(1)
    last = pl.num_programs(1) - 1
    r_w = d // 2
    shift = mod_ref[0, 0:1, :]
    scale = mod_ref[0, 1:2, :]
    g = g_ref[...]

    def normmod(v):
        return _rms(v, g) * (1.0 + scale) + shift

    hb = normmod(x_ref[0]).astype(BF16)
    halo = jnp.concatenate([xp_ref[0], xn_ref[0]], axis=0)
    ph = _dot(normmod(halo).astype(BF16), w_ref[:, 3 * r_w:5 * r_w])
    prev_row = jnp.where(i == 0, 0.0, ph[7:8, :])
    next_row = jnp.where(i == last, 0.0, ph[8:9, :])

    cos2 = cos_ref[...]
    sin2 = sin_ref[...]
    rows = lax.broadcasted_iota(I32, (ts, r_w), 0)
    qscale = LANES ** -0.5

    for j in range(8):
        acc = _dot(hb, w_ref[:, j * r_w:(j + 1) * r_w])
        if j in (0, 1):
            if j == 0:
                acc = acc * qscale
            parts = []
            for h in range(HEADS):
                t = acc[:, h * LANES:(h + 1) * LANES]
                parts.append(t * cos2 + pltpu.roll(t, LANES // 2, axis=1) * sin2)
            acc = jnp.concatenate(parts, axis=1)
        elif j in (3, 4):
            c0 = (j - 3) * r_w
            pr = prev_row[:, c0:c0 + r_w]
            nx = next_row[:, c0:c0 + r_w]
            down = jnp.where(rows == 0, pr, pltpu.roll(acc, 1, axis=0))
            up = jnp.where(rows == ts - 1, nx, pltpu.roll(acc, ts - 1, axis=0))
            cw = cw_ref[:, c0:c0 + r_w]
            acc = down * cw[0:1, :] + acc * cw[1:2, :] + up * cw[2:3, :] + cb_ref[:, c0:c0 + r_w]
            acc = _silu(acc)
            if j == 4:
                acc = acc * qscale
        if j in (1, 4):
            kt_ref[0, (j // 4) * r_w:(j // 4 + 1) * r_w, :] = acc.T.astype(BF16)
        else:
            slot = P_SLOT[j]
            p_ref[0, :, slot * r_w:(slot + 1) * r_w] = acc.astype(BF16)

    gr = _dot_nt(wgt_ref[...], hb) + gbr_ref[...]
    ch_r = lax.broadcasted_iota(I32, gr.shape, 0)
    gr_ref[0] = jnp.where((ch_r // HEADS) % 2 == 1, _log_sigmoid(gr), gr)


def _inproj(x, mod3, mod_row, g, w_main, wgt, conv_w, conv_b, gb_col, cos2, sin2, ts):
    b, n, d = x.shape
    nt = n // ts
    nb8 = n // 8
    hb = ts // 8
    cols = w_main.shape[1]
    p_cols = cols // 8 * 6
    if mod_row is None:
        mod_map = lambda bi, i: (bi, 0, 0)
    else:
        mod_map = lambda bi, i: (mod_row, 0, 0)
    const2 = lambda bi, i: (0, 0)
    kern = functools.partial(_inproj_kernel, ts=ts, d=d)
    return pl.pallas_call(
        kern,
        grid=(b, nt),
        in_specs=[
            pl.BlockSpec((1, ts, d), lambda bi, i: (bi, i, 0)),
            pl.BlockSpec((1, 8, d), lambda bi, i: (bi, jnp.maximum(i * hb - 1, 0), 0)),
            pl.BlockSpec((1, 8, d), lambda bi, i: (bi, jnp.minimum((i + 1) * hb, nb8 - 1), 0)),
            pl.BlockSpec((1, N_MOD, d), mod_map),
            pl.BlockSpec((1, d), const2),
            pl.BlockSpec((d, cols), const2),
            pl.BlockSpec((16, d), const2),
            pl.BlockSpec((3, d), const2),
            pl.BlockSpec((1, d), const2),
            pl.BlockSpec((16, 1), const2),
            pl.BlockSpec((ts, LANES), lambda bi, i: (i, 0)),
            pl.BlockSpec((ts, LANES), lambda bi, i: (i, 0)),
        ],
        out_specs=[
            pl.BlockSpec((1, ts, p_cols), lambda bi, i: (bi, i, 0)),
            pl.BlockSpec((1, d, ts), lambda bi, i: (bi, 0, i)),
            pl.BlockSpec((1, 16, ts), lambda bi, i: (bi, 0, i)),
        ],
        out_shape=[
            jax.ShapeDtypeStruct((b, n, p_cols), BF16),
            jax.ShapeDtypeStruct((b, d, n), BF16),
            jax.ShapeDtypeStruct((b, 16, n), F32),
        ],
        compiler_params=pltpu.CompilerParams(dimension_semantics=("parallel", "parallel")),
        name="inproj",
    )(x, x, x, mod3, g, w_main, wgt, conv_w, conv_b, gb_col, cos2, sin2)


def _scan_kernel(pf_ref, pb_ref, ktf_ref, ktb_ref, grf_ref, grb_ref, intra_ref, kd_ref, qd_ref, cd_ref,
                 rs0_ref, mc0_ref, mm0_ref, *out_refs, with_output, r_w):
    if with_output:
        of_ref, ob_ref, rs_ref, mc_ref, mm_ref = out_refs
    else:
        rs_ref, mc_ref, mm_ref = out_refs
    j = pl.program_id(1)

    @pl.when(j == 0)
    def _():
        rs_ref[...] = rs0_ref[...]
        mc_ref[...] = mc0_ref[...]
        mm_ref[...] = mm0_ref[...]

    c = CHUNK
    row = lax.broadcasted_iota(I32, (c, c), 0)
    col = lax.broadcasted_iota(I32, (c, c), 1)
    tri_le = (row <= col)
    tri_ge = (row >= col)
    eye = row == col
    ones_blk = jnp.ones((c, LANES), BF16)
    lane = lax.broadcasted_iota(I32, (HEADS, c), 1)

    def spread_rows(vecs):
        diag = jnp.concatenate([jnp.where(eye, v, 0.0) for v in vecs], axis=0)
        hi = diag.astype(BF16)
        lo = (diag - hi.astype(F32)).astype(BF16)
        out = _dot(hi, ones_blk) + _dot(lo, ones_blk)
        return [out[n * c:(n + 1) * c, :] for n in range(len(vecs))]

    def running_max(a, fwd):
        pm = a
        s = 1
        while s < c:
            if fwd:
                pm = jnp.where(lane >= s, jnp.maximum(pm, pltpu.roll(pm, s, axis=1)), pm)
            else:
                pm = jnp.where(lane < c - s, jnp.maximum(pm, pltpu.roll(pm, c - s, axis=1)), pm)
            s *= 2
        return pm

    n_st = 2 * HEADS
    rs_prev = [rs_ref[0, st] for st in range(n_st)]
    mc_prev = [mc_ref[0, st] for st in range(n_st)]
    mm_prev = [mm_ref[0, st:st + 1, 0:1] for st in range(n_st)]
    heads = [(dr, h) for dr in range(2) for h in range(HEADS)]
    o0 = 2 * r_w

    def cols(ref, base, h):
        return ref[0, :, base + h * LANES:base + (h + 1) * LANES]

    cs_rows = []
    for dr in range(2):
        gr = (grf_ref, grb_ref)[dr][0]
        tri = jnp.where(tri_le if dr == 0 else tri_ge, 1.0, 0.0).astype(BF16)
        cs_rows.append((gr, sum(_dot(piece, tri) for piece in _split3(gr))))
    ret_upd, ret_sc, ret_in, ml_sc, ml_in = [], [], [], [], []
    for dr, h in heads:
        st = dr * HEADS + h
        p_ref, kt_ref = (pf_ref, pb_ref)[dr], (ktf_ref, ktb_ref)[dr]
        kt = kt_ref[0, h * LANES:(h + 1) * LANES, :]
        ks = (kt.astype(F32) * kd_ref[st:st + 1, :]).astype(BF16)
        ret_upd.append(_dot(ks, cols(p_ref, r_w, h)))
        if with_output:
            q, mq = cols(p_ref, 0, h), cols(p_ref, o0, h)
            ret_sc.append(_dot(q, kt))
            ret_in.append(_dot(q, rs_prev[st].astype(BF16)))
            ml_sc.append(_dot(mq, kt_ref[0, r_w + h * LANES:r_w + (h + 1) * LANES, :]))
            ml_in.append(_dot(mq, mc_prev[st].astype(BF16)))

    gate = []
    for dr in range(2):
        gr, cs_row = cs_rows[dr]
        g0 = dr * 2 * HEADS
        a_rows = gr[g0:g0 + HEADS, :] - cs_row[g0 + HEADS:g0 + 2 * HEADS, :]
        pm_rows = running_max(a_rows, dr == 0)
        last = c - 1 if dr == 0 else 0
        for h in range(HEADS):
            st = dr * HEADS + h
            b_row = cs_row[g0 + HEADS + h:g0 + HEADS + h + 1, :]
            a_row, pm_row = a_rows[h:h + 1, :], pm_rows[h:h + 1, :]
            b_tot = b_row[:, last:last + 1]
            m_prev = mm_prev[st]
            m_next = b_tot + jnp.maximum(m_prev, pm_row[:, last:last + 1])
            gate.append(dict(b_row=b_row, a_row=a_row, pm_row=pm_row, m_prev=m_prev, m_next=m_next,
                             decay_prev=jnp.exp(b_tot + m_prev - m_next),
                             ws_row=jnp.exp(b_tot + a_row - m_next)))
    ml_upd, spread = [], []
    for dr, h in heads:
        st = dr * HEADS + h
        p_ref, kt_ref = (pf_ref, pb_ref)[dr], (ktf_ref, ktb_ref)[dr]
        mkt = kt_ref[0, r_w + h * LANES:r_w + (h + 1) * LANES, :]
        kw = (mkt.astype(F32) * gate[st]["ws_row"]).astype(BF16)
        v_ext = jnp.concatenate([cols(p_ref, o0 + r_w, h), ones_blk], axis=1)
        ml_upd.append(_dot(kw, v_ext))
        if with_output:
            spread.append(spread_rows([gate[st]["b_row"], gate[st]["pm_row"]]))

    if with_output:
        ret_out, ml_out, stab = [], [], []
        for dr, h in heads:
            st = dr * HEADS + h
            p_ref = (pf_ref, pb_ref)[dr]
            sc = (ret_sc[st] * intra_ref[st]).astype(BF16)
            ret_out.append(_dot(sc, cols(p_ref, r_w, h)))
            b_sp, pm_sp = spread[st]
            c_sp = jnp.maximum(gate[st]["m_prev"], pm_sp)
            causal = tri_ge if dr == 0 else tri_le
            w = jnp.where(causal, jnp.exp(gate[st]["a_row"] - c_sp), 0.0)
            v_ext = jnp.concatenate([cols(p_ref, o0 + r_w, h), ones_blk], axis=1)
            ml_out.append(_dot((ml_sc[st] * w).astype(BF16), v_ext))
            stab.append((jnp.exp(gate[st]["m_prev"] - c_sp), jnp.exp(-(b_sp + c_sp))))

    for dr, h in heads:
        st = dr * HEADS + h
        if with_output:
            o_ref = (of_ref, ob_ref)[dr]
            o_ref[0, :, h * LANES:(h + 1) * LANES] = (ret_out[st] + qd_ref[st] * ret_in[st]).astype(BF16)
            inter, floor = stab[st]
            hx = ml_out[st] + jnp.concatenate([inter, inter], axis=1) * ml_in[st]
            hout = hx[:, :LANES] / jnp.maximum(jnp.abs(hx[:, LANES:]), floor)
            o_ref[0, :, r_w + h * LANES:r_w + (h + 1) * LANES] = hout.astype(BF16)
    for dr, h in heads:
        st = dr * HEADS + h
        rs_ref[0, st] = rs_prev[st] * cd_ref[st:st + 1, :] + ret_upd[st]
        mc_ref[0, st] = gate[st]["decay_prev"] * mc_prev[st] + ml_upd[st]
        mm_ref[0, st:st + 1, :] = jnp.broadcast_to(gate[st]["m_next"], (1, LANES))


def _scan(p, kt, g_row, tabs, states, with_output):
    b, n, cols = p.shape
    nch = n // CHUNK
    r_w = cols // 6
    intra, kd, qd, cd = tabs
    rs0, mc0, mm0 = states
    nst = 2 * HEADS
    fwd3 = lambda bi, j: (bi, j, 0)
    bwd3 = lambda bi, j: (bi, nch - 1 - j, 0)
    c3 = lambda bi, j: (0, 0, 0)
    st4 = lambda bi, j: (bi, 0, 0, 0)
    in_specs = [
        pl.BlockSpec((1, CHUNK, 4 * r_w), fwd3),
        pl.BlockSpec((1, CHUNK, 4 * r_w), bwd3),
        pl.BlockSpec((1, 2 * r_w, CHUNK), lambda bi, j: (bi, 0, j)),
        pl.BlockSpec((1, 2 * r_w, CHUNK), lambda bi, j: (bi, 0, nch - 1 - j)),
        pl.BlockSpec((1, 16, CHUNK), lambda bi, j: (bi, 0, j)),
        pl.BlockSpec((1, 16, CHUNK), lambda bi, j: (bi, 0, nch - 1 - j)),
        pl.BlockSpec((nst, CHUNK, LANES), c3),
        pl.BlockSpec((nst, CHUNK), lambda bi, j: (0, 0)),
        pl.BlockSpec((nst, CHUNK, LANES), c3),
        pl.BlockSpec((nst, LANES), lambda bi, j: (0, 0)),
        pl.BlockSpec((1, nst, LANES, LANES), st4),
        pl.BlockSpec((1, nst, LANES, 2 * LANES), st4),
        pl.BlockSpec((1, nst, LANES), lambda bi, j: (bi, 0, 0)),
    ]
    st_specs = [
        pl.BlockSpec((1, nst, LANES, LANES), st4),
        pl.BlockSpec((1, nst, LANES, 2 * LANES), st4),
        pl.BlockSpec((1, nst, LANES), lambda bi, j: (bi, 0, 0)),
    ]
    st_shapes = [
        jax.ShapeDtypeStruct((b, nst, LANES, LANES), F32),
        jax.ShapeDtypeStruct((b, nst, LANES, 2 * LANES), F32),
        jax.ShapeDtypeStruct((b, nst, LANES), F32),
    ]
    if with_output:
        out_specs = [pl.BlockSpec((1, CHUNK, 2 * r_w), fwd3), pl.BlockSpec((1, CHUNK, 2 * r_w), bwd3)] + st_specs
        out_shape = [jax.ShapeDtypeStruct((b, n, 2 * r_w), BF16)] * 2 + st_shapes
    else:
        out_specs, out_shape = st_specs, st_shapes
    kern = functools.partial(_scan_kernel, with_output=with_output, r_w=r_w)
    return pl.pallas_call(
        kern,
        grid=(b, nch),
        in_specs=in_specs,
        out_specs=out_specs,
        out_shape=out_shape,
        compiler_params=pltpu.CompilerParams(dimension_semantics=("parallel", "arbitrary")),
        name="scan_out" if with_output else "scan_state",
    )(p, p, kt, kt, g_row, g_row, intra, kd, qd, cd, rs0, mc0, mm0)


def _post_kernel(x_ref, of_ref, ob_ref, rg_ref, mo_ref, mod_ref, ng_ref, hg_ref, wo_ref, wrh_ref, wrl_ref,
                 rb_ref, su_ref, x1_ref, h2_ref, rk_ref, wd_ref, cnt_ref, *, ts, d, n_exp):
    s = of_ref[0].astype(F32) + ob_ref[0].astype(F32)
    parts = []
    for gi in range(2 * HEADS):
        sl = s[:, gi * LANES:(gi + 1) * LANES]
        mu = jnp.mean(sl, axis=-1, keepdims=True)
        dv = sl - mu
        var = jnp.mean(dv * dv, axis=-1, keepdims=True)
        y = dv * lax.rsqrt(var + EPS) * hg_ref[:, gi * LANES:(gi + 1) * LANES]
        if gi < HEADS:
            gate = _silu(rg_ref[0, :, gi * LANES:(gi + 1) * LANES].astype(F32))
        else:
            gate = _sigmoid(mo_ref[0, :, (gi - HEADS) * LANES:(gi - HEADS + 1) * LANES].astype(F32))
        parts.append((y * gate).astype(BF16))
    mixed = jnp.concatenate(parts, axis=1)
    y = _dot(mixed, wo_ref[...])
    g1 = mod_ref[0, 2:3, :]
    sh2 = mod_ref[0, 3:4, :]
    sc2 = mod_ref[0, 4:5, :]
    x1 = x_ref[0] + g1 * _rms(y, ng_ref[1:2, :])
    x1_ref[0] = x1
    h2 = _rms(x1, ng_ref[2:3, :]) * (1.0 + sc2) + sh2
    h_hi = h2.astype(BF16)
    h2_ref[...] = h_hi

    h_lo = (h2 - h_hi.astype(F32)).astype(BF16)
    logits = _dot_nt(wrh_ref[...], h_hi) + _dot_nt(wrh_ref[...], h_lo) + _dot_nt(wrl_ref[...], h_hi)
    scores = _sigmoid(logits)
    sel = scores + rb_ref[...]
    gsz = n_exp // N_GROUPS
    iota_g = lax.broadcasted_iota(I32, (gsz, ts), 0).astype(F32)
    grp = []
    for gi in range(N_GROUPS):
        blk = sel[gi * gsz:(gi + 1) * gsz, :]
        m1 = jnp.max(blk, axis=0, keepdims=True)
        i1 = jnp.min(jnp.where(blk == m1, iota_g, float(gsz)), axis=0, keepdims=True)
        m2 = jnp.max(jnp.where(iota_g == i1, NEG_INF, blk), axis=0, keepdims=True)
        grp.append(m1 + m2)
    masked_parts = []
    for gi in range(N_GROUPS):
        rank = jnp.zeros((1, ts), F32)
        for gj in range(N_GROUPS):
            if gj == gi:
                continue
            beats = (grp[gj] >= grp[gi]) if gj < gi else (grp[gj] > grp[gi])
            rank = rank + jnp.where(beats, 1.0, 0.0)
        keep = rank < float(TOPK_GROUPS)
        masked_parts.append(jnp.where(keep, sel[gi * gsz:(gi + 1) * gsz, :], NEG_INF))
    masked = jnp.concatenate(masked_parts, axis=0)

    iota_e = lax.broadcasted_iota(I32, (n_exp, ts), 0).astype(F32)
    selmask = jnp.zeros((n_exp, ts), F32)
    for _ in range(TOP_K):
        mx = jnp.max(masked, axis=0, keepdims=True)
        ei = jnp.min(jnp.where(masked == mx, iota_e, float(n_exp)), axis=0, keepdims=True)
        hit = iota_e == ei
        selmask = jnp.where(hit, 1.0, selmask)
        masked = jnp.where(hit, NEG_INF, masked)
    picked = selmask > 0.0
    wsel = jnp.where(picked, scores, 0.0)
    wd_ref[...] = wsel / jnp.sum(wsel, axis=0, keepdims=True) * ROUTED_SCALE
    rank = _dot(selmask.astype(BF16), su_ref[...])
    rk_ref[...] = jnp.where(picked, rank, -1.0)
    cnt_ref[0] = _dot_nt(jnp.ones((8, ts), BF16), selmask.astype(BF16))


def _post(x, o_f, o_b, p, mod3, norm_g, head_g, w_out, wr_hi, wr_lo, rbias, ts):
    b, n, d = x.shape
    nt = n // ts
    t_all = b * n
    n_exp = wr_hi.shape[0]
    r_w = d // 2
    su = jnp.where(lax.broadcasted_iota(I32, (ts, ts), 0) < lax.broadcasted_iota(I32, (ts, ts), 1),
                   1.0, 0.0).astype(BF16)
    tok3 = lambda bi, i: (bi, i, 0)
    c2 = lambda bi, i: (0, 0)
    flat = lambda bi, i: (0, bi * nt + i)
    kern = functools.partial(_post_kernel, ts=ts, d=d, n_exp=n_exp)
    return pl.pallas_call(
        kern,
        grid=(b, nt),
        in_specs=[
            pl.BlockSpec((1, ts, d), tok3),
            pl.BlockSpec((1, ts, d), tok3),
            pl.BlockSpec((1, ts, d), tok3),
            pl.BlockSpec((1, ts, r_w), lambda bi, i: (bi, i, 4)),
            pl.BlockSpec((1, ts, r_w), lambda bi, i: (bi, i, 5)),
            pl.BlockSpec((1, N_MOD, d), lambda bi, i: (bi, 0, 0)),
            pl.BlockSpec((4, d), c2),
            pl.BlockSpec((1, d), c2),
            pl.BlockSpec((d, d), c2),
            pl.BlockSpec((n_exp, d), c2),
            pl.BlockSpec((n_exp, d), c2),
            pl.BlockSpec((n_exp, 1), c2),
            pl.BlockSpec((ts, ts), c2),
        ],
        out_specs=[
            pl.BlockSpec((1, ts, d), tok3),
            pl.BlockSpec((ts, d), lambda bi, i: (bi * nt + i, 0)),
            pl.BlockSpec((n_exp, ts), flat),
            pl.BlockSpec((n_exp, ts), flat),
            pl.BlockSpec((1, 8, n_exp), lambda bi, i: (bi * nt + i, 0, 0)),
        ],
        out_shape=[
            jax.ShapeDtypeStruct((b, n, d), F32),
            jax.ShapeDtypeStruct((t_all, d), BF16),
            jax.ShapeDtypeStruct((n_exp, t_all), F32),
            jax.ShapeDtypeStruct((n_exp, t_all), F32),
            jax.ShapeDtypeStruct((b * nt, 8, n_exp), F32),
        ],
        compiler_params=pltpu.CompilerParams(dimension_semantics=("parallel", "parallel")),
        name="post",
    )(x, o_f, o_b, p, p, mod3, norm_g, head_g, w_out, wr_hi, wr_lo, rbias, su)


def _slot_transfers(meta_ref, stage, buf, hbm, sems, n_exp, to_hbm, wait):
    for e in range(n_exp):
        start = pl.multiple_of(meta_ref[0, 0, e], ROW_ALIGN)
        fits_half = meta_ref[0, 0, 4 * n_exp + 2 + e] != 0
        for rows, cond in ((HALF_ROWS, fits_half), (SLOT_ROWS, jnp.logical_not(fits_half))):
            @pl.when(cond)
            def _():
                run = hbm.at[pl.ds(start, rows)]
                slot = stage.at[buf, pl.ds(e * SLOT_ROWS, rows)]
                cp = (pltpu.make_async_copy(slot, run, sems.at[buf]) if to_hbm
                      else pltpu.make_async_copy(run, slot, sems.at[buf]))
                if wait:
                    cp.wait()
                else:
                    cp.start(priority=e % 2)


def _overflow_copy(meta_ref, spill, hbm, sem, n_exp, j, i, to_hbm):
    e = meta_ref[0, 0, n_exp + j]
    src = meta_ref[0, 0, 3 * n_exp + j] + ROW_ALIGN * i
    dst = meta_ref[0, 0, e] + SLOT_ROWS + ROW_ALIGN * i
    piece = spill.at[pl.ds(pl.multiple_of(src, ROW_ALIGN), ROW_ALIGN)]
    rows = hbm.at[pl.ds(pl.multiple_of(dst, ROW_ALIGN), ROW_ALIGN)]
    return pltpu.make_async_copy(piece, rows, sem) if to_hbm else pltpu.make_async_copy(rows, piece, sem)


def _for_overflow_pieces(meta_ref, n_exp, fn):
    def per_expert(j, carry):
        def per_piece(i, c2):
            fn(j, i)
            return c2
        return lax.fori_loop(0, meta_ref[0, 0, 2 * n_exp + j], per_piece, carry)
    lax.fori_loop(0, meta_ref[0, 0, 4 * n_exp], per_expert, 0)


def _spill_matrix_rows(meta_ref, rk_ref, wd_ref, base, ts, n_exp):
    rows = (lax.broadcasted_iota(I32, (SPILL_CHUNK, ts), 0) + base).astype(F32)

    def per_expert(j, hit):
        e = meta_ref[0, 0, n_exp + j]
        rk = rk_ref[pl.ds(e, 1), :]
        val = 1.0 if wd_ref is None else wd_ref[pl.ds(e, 1), :]
        target = jnp.where(rk >= SLOT_ROWS, rk - SLOT_ROWS + meta_ref[0, 0, 3 * n_exp + j].astype(F32), -1.0)
        return jnp.where(target == rows, val, hit)

    return lax.fori_loop(0, meta_ref[0, 0, 4 * n_exp], per_expert,
                         jnp.zeros((SPILL_CHUNK, ts), F32)).astype(BF16)


def _zero_fill(seg_ref, xs_hbm, stage, sem, n_exp, n_blocks):
    tail = SLOT_ROWS + EXPERT_BLOCK
    stage[0, 0:tail, :] = jnp.zeros((tail, stage.shape[2]), stage.dtype)
    tails = []
    for e in range(n_exp):
        start = jnp.maximum(seg_ref[0, 0, e] - tail, 0)
        tails.append(pltpu.make_async_copy(stage.at[0, pl.ds(0, tail)],
                                           xs_hbm.at[pl.ds(pl.multiple_of(start, ROW_ALIGN), tail)], sem))
    for cp in tails:
        cp.start()
    n_used = seg_ref[0, 0, n_exp]

    def block_copy(i):
        row = pl.multiple_of(i * EXPERT_BLOCK, EXPERT_BLOCK)
        return pltpu.make_async_copy(stage.at[0, pl.ds(0, EXPERT_BLOCK)], xs_hbm.at[pl.ds(row, EXPERT_BLOCK)], sem)

    def start_block(i, carry):
        block_copy(i).start()
        return carry

    def wait_block(i, carry):
        block_copy(i).wait()
        return carry

    lax.fori_loop(n_used, n_blocks, start_block, 0)
    for cp in tails:
        cp.wait()
    lax.fori_loop(n_used, n_blocks, wait_block, 0)


def _dispatch_kernel(meta_ref, prv_ref, seg_ref, x_ref, rk_ref, xs_hbm, stage, spill, sems, sem_ov, *,
                     ts, n_exp, n_blocks):
    i = pl.program_id(0)
    buf = i % 2

    @pl.when(i == 0)
    def _():
        _zero_fill(seg_ref, xs_hbm, stage, sem_ov, n_exp, n_blocks)

    x = x_ref[...]
    slot_row = lax.broadcasted_iota(I32, (SLOT_ROWS, ts), 0).astype(F32)
    group_rows = SLOT_GROUP * SLOT_ROWS
    for g in range(n_exp // SLOT_GROUP):
        pick = jnp.concatenate(
            [jnp.where(rk_ref[e:e + 1, :] == slot_row, 1.0, 0.0)
             for e in range(g * SLOT_GROUP, (g + 1) * SLOT_GROUP)], axis=0).astype(BF16)
        stage[buf, g * group_rows:(g + 1) * group_rows, :] = _dot(pick, x).astype(BF16)
    n_spill = meta_ref[0, 0, 4 * n_exp + 1]

    @pl.when(i > 0)
    def _():
        _for_overflow_pieces(prv_ref, n_exp,
                             lambda e, k: _overflow_copy(prv_ref, spill, xs_hbm, sem_ov, n_exp, e, k, True).wait())

    def spill_chunk(ci, carry):
        base = pl.multiple_of(ci * SPILL_CHUNK, SPILL_CHUNK)
        spill[pl.ds(base, SPILL_CHUNK), :] = _dot(
            _spill_matrix_rows(meta_ref, rk_ref, None, base, ts, n_exp), x).astype(BF16)
        return carry

    lax.fori_loop(0, n_spill, spill_chunk, 0)

    @pl.when(i > 0)
    def _():
        _slot_transfers(prv_ref, stage, 1 - buf, xs_hbm, sems, n_exp, True, wait=True)

    _slot_transfers(meta_ref, stage, buf, xs_hbm, sems, n_exp, True, wait=False)
    _for_overflow_pieces(meta_ref, n_exp,
                         lambda e, k: _overflow_copy(meta_ref, spill, xs_hbm, sem_ov, n_exp, e, k, True).start())

    @pl.when(i == pl.num_programs(0) - 1)
    def _():
        _slot_transfers(meta_ref, stage, buf, xs_hbm, sems, n_exp, True, wait=True)
        _for_overflow_pieces(meta_ref, n_exp,
                             lambda e, k: _overflow_copy(meta_ref, spill, xs_hbm, sem_ov, n_exp, e, k, True).wait())


def _dispatch(h2, rank, meta, seg, p_rows, ts, n_exp):
    t_all, d = h2.shape
    nt = t_all // ts
    kern = functools.partial(_dispatch_kernel, ts=ts, n_exp=n_exp, n_blocks=p_rows // EXPERT_BLOCK)
    return pl.pallas_call(
        kern,
        grid=(nt,),
        in_specs=[
            pl.BlockSpec((1, 1, meta.shape[2]), lambda i: (i, 0, 0), memory_space=pltpu.SMEM),
            pl.BlockSpec((1, 1, meta.shape[2]), lambda i: (jnp.maximum(i - 1, 0), 0, 0), memory_space=pltpu.SMEM),
            pl.BlockSpec((1, 1, seg.shape[2]), lambda i: (0, 0, 0), memory_space=pltpu.SMEM),
            pl.BlockSpec((ts, d), lambda i: (i, 0)),
            pl.BlockSpec((n_exp, ts), lambda i: (0, i)),
        ],
        out_specs=pl.BlockSpec(memory_space=pl.ANY),
        out_shape=jax.ShapeDtypeStruct((p_rows, d), BF16),
        scratch_shapes=[pltpu.VMEM((2, n_exp * SLOT_ROWS, d), BF16),
                        pltpu.VMEM((ts * TOP_K, d), BF16),
                        pltpu.SemaphoreType.DMA((2,)), pltpu.SemaphoreType.DMA(())],
        compiler_params=pltpu.CompilerParams(dimension_semantics=("arbitrary",), has_side_effects=True),
        name="dispatch",
    )(meta, meta, seg, h2, rank)


def _expert_kernel(be_ref, nu_ref, xs_ref, wg_ref, wu_ref, wd_ref, ys_ref):
    del be_ref
    i = pl.program_id(0)

    @pl.when(i < nu_ref[0])
    def _():
        xb = xs_ref[...]
        a = _silu(_dot(xb, wg_ref[0])) * _dot(xb, wu_ref[0])
        ys_ref[...] = _dot(a.astype(BF16), wd_ref[0]).astype(BF16)


def _experts(xs, block_e, n_used, w_gate, w_up, w_down, blk):
    p_rows, dw = xs.shape
    n_exp, d, ff = w_gate.shape
    nb = p_rows // blk
    used = lambda i, nu: jnp.minimum(i, nu[0] - 1)
    grid_spec = pltpu.PrefetchScalarGridSpec(
        num_scalar_prefetch=2,
        grid=(nb,),
        in_specs=[
            pl.BlockSpec((blk, dw), lambda i, be, nu: (used(i, nu), 0)),
            pl.BlockSpec((1, d, ff), lambda i, be, nu: (be[used(i, nu)], 0, 0)),
            pl.BlockSpec((1, d, ff), lambda i, be, nu: (be[used(i, nu)], 0, 0)),
            pl.BlockSpec((1, ff, d), lambda i, be, nu: (be[used(i, nu)], 0, 0)),
        ],
        out_specs=pl.BlockSpec((blk, dw), lambda i, be, nu: (used(i, nu), 0)),
    )
    return pl.pallas_call(
        _expert_kernel,
        grid_spec=grid_spec,
        out_shape=jax.ShapeDtypeStruct((p_rows, dw), BF16),
        input_output_aliases={2: 0},
        compiler_params=pltpu.CompilerParams(dimension_semantics=("arbitrary",)),
        name="experts",
    )(block_e, n_used, xs, w_gate, w_up, w_down)


def _combine_kernel(meta_ref, nxt_ref, ys_hbm, rkt_ref, wdt_ref, rk_ref, wd_ref, ex_ref, rp_ref, x1_ref, h2_ref,
                    mod_ref, ng_ref, sg_ref, su_ref, sd_ref, o_ref, stage, spill, acc, sems, sem_ov, *,
                    ts, n_exp):
    i = pl.program_id(0)
    buf = i % 2
    n_spill = meta_ref[0, 0, 4 * n_exp + 1]
    n_groups = n_exp // SLOT_GROUP

    @pl.when(i == 0)
    def _():
        stage[...] = jnp.zeros(stage.shape, stage.dtype)
        _slot_transfers(meta_ref, stage, 0, ys_hbm, sems, n_exp, False, wait=False)

    @pl.when(i < pl.num_programs(0) - 1)
    def _():
        _slot_transfers(nxt_ref, stage, 1 - buf, ys_hbm, sems, n_exp, False, wait=False)

    def clear_chunk(ci, carry):
        base = pl.multiple_of(ci * SPILL_CHUNK, SPILL_CHUNK)
        spill[pl.ds(base, SPILL_CHUNK), :] = jnp.zeros((SPILL_CHUNK, spill.shape[1]), spill.dtype)
        return carry

    lax.fori_loop(0, n_spill, clear_chunk, 0)
    _for_overflow_pieces(meta_ref, n_exp,
                         lambda e, k: _overflow_copy(meta_ref, spill, ys_hbm, sem_ov, n_exp, e, k, False).start())

    xb = h2_ref[...]
    a = _silu(_dot(xb, sg_ref[...])) * _dot(xb, su_ref[...])
    tot = _dot(a.astype(BF16), sd_ref[...])

    rank_lanes = _dot(rkt_ref[...].astype(BF16), ex_ref[...])
    weight_lanes = _dot(wdt_ref[...].astype(BF16), ex_ref[...])

    _slot_transfers(meta_ref, stage, buf, ys_hbm, sems, n_exp, False, wait=True)
    group_rows = SLOT_GROUP * SLOT_ROWS
    for g in range(n_groups):
        cols = slice(g * group_rows, (g + 1) * group_rows)
        unmix = jnp.where(rank_lanes[:, cols] == rp_ref[:, cols], weight_lanes[:, cols], 0.0).astype(BF16)
        tot = tot + _dot(unmix, stage[buf, g * group_rows:(g + 1) * group_rows, :])
    acc[...] = tot

    _for_overflow_pieces(meta_ref, n_exp,
                         lambda e, k: _overflow_copy(meta_ref, spill, ys_hbm, sem_ov, n_exp, e, k, False).wait())

    def spill_chunk(ci, carry):
        base = pl.multiple_of(ci * SPILL_CHUNK, SPILL_CHUNK)
        acc[...] += _dot_tn(_spill_matrix_rows(meta_ref, rk_ref, wd_ref, base, ts, n_exp),
                            spill[pl.ds(base, SPILL_CHUNK), :])
        return carry

    lax.fori_loop(0, n_spill, spill_chunk, 0)
    g2 = mod_ref[0, 5:6, :]
    o_ref[...] = x1_ref[...] + g2 * _rms(acc[...], ng_ref[3:4, :])


def _combine(ys, meta, rank_tm, wd_tm, rank, wd, x1_flat, h2, mod3, norm_g, ws_gate, ws_up, ws_down,
             n_seq, ts, n_exp):
    t_all, d = x1_flat.shape
    nt = t_all // ts
    per_b = n_seq // ts
    ff = ws_gate.shape[1]
    lanes = n_exp * SLOT_ROWS
    lane = lax.broadcasted_iota(I32, (n_exp, lanes), 1)
    expand = jnp.where(lane // SLOT_ROWS == lax.broadcasted_iota(I32, (n_exp, lanes), 0), 1.0, 0.0).astype(BF16)
    slot_rank = (jnp.arange(lanes, dtype=I32) % SLOT_ROWS).astype(F32).reshape(1, lanes)
    c2 = lambda i: (0, 0)
    kern = functools.partial(_combine_kernel, ts=ts, n_exp=n_exp)
    return pl.pallas_call(
        kern,
        grid=(nt,),
        in_specs=[
            pl.BlockSpec((1, 1, meta.shape[2]), lambda i: (i, 0, 0), memory_space=pltpu.SMEM),
            pl.BlockSpec((1, 1, meta.shape[2]), lambda i: (jnp.minimum(i + 1, nt - 1), 0, 0),
                         memory_space=pltpu.SMEM),
            pl.BlockSpec(memory_space=pl.ANY),
            pl.BlockSpec((ts, n_exp), lambda i: (i, 0)),
            pl.BlockSpec((ts, n_exp), lambda i: (i, 0)),
            pl.BlockSpec((n_exp, ts), lambda i: (0, i)),
            pl.BlockSpec((n_exp, ts), lambda i: (0, i)),
            pl.BlockSpec((n_exp, lanes), c2),
            pl.BlockSpec((1, lanes), c2),
            pl.BlockSpec((ts, d), lambda i: (i, 0)),
            pl.BlockSpec((ts, d), lambda i: (i, 0)),
            pl.BlockSpec((1, N_MOD, d), lambda i: (i // per_b, 0, 0)),
            pl.BlockSpec((4, d), c2),
            pl.BlockSpec((d, ff), c2),
            pl.BlockSpec((d, ff), c2),
            pl.BlockSpec((ff, d), c2),
        ],
        out_specs=pl.BlockSpec((ts, d), lambda i: (i, 0)),
        out_shape=jax.ShapeDtypeStruct((t_all, d), F32),
        scratch_shapes=[pltpu.VMEM((2, lanes, d), BF16), pltpu.VMEM((ts * TOP_K, d), BF16),
                        pltpu.VMEM((ts, d), F32),
                        pltpu.SemaphoreType.DMA((2,)), pltpu.SemaphoreType.DMA(())],
        compiler_params=pltpu.CompilerParams(dimension_semantics=("arbitrary",)),
        name="combine",
    )(meta, meta, ys, rank_tm, wd_tm, rank, wd, expand, slot_rank, x1_flat, h2, mod3, norm_g,
      ws_gate, ws_up, ws_down)


def _rope_tables(n):
    rows = jnp.repeat(jnp.arange(n // GRID_W, dtype=F32), GRID_W)
    cols = jnp.tile(jnp.arange(GRID_W, dtype=F32), n // GRID_W)
    quarter = LANES // 4
    freqs = ROPE_BASE ** (-jnp.arange(quarter, dtype=F32) / quarter)
    ang = jnp.concatenate([rows[:, None] * freqs, cols[:, None] * freqs], axis=-1)
    cos, sin = jnp.cos(ang), jnp.sin(ang)
    return jnp.concatenate([cos, cos], axis=-1), jnp.concatenate([-sin, sin], axis=-1)


def _retention_tables(log_decay):
    lg = -jnp.exp(log_decay.astype(F32))
    idx = jnp.arange(CHUNK, dtype=F32)
    rel = idx[:, None] - idx[None, :]
    lg3 = lg[:, :, None, None]
    intra_f = jnp.where(rel >= 0, jnp.exp(jnp.maximum(rel, 0.0) * lg3[0]), 0.0)
    intra_b = jnp.where(rel <= 0, jnp.exp(jnp.maximum(-rel, 0.0) * lg3[1]), 0.0)
    kd_f = jnp.exp((CHUNK - 1 - idx)[None, :] * lg[0][:, None])
    kd_b = jnp.exp(idx[None, :] * lg[1][:, None])
    qd_f = jnp.exp((idx + 1)[None, :] * lg[0][:, None])
    qd_b = jnp.exp((CHUNK - idx)[None, :] * lg[1][:, None])
    bc = lambda t: jnp.broadcast_to(t[:, :, None], (HEADS, CHUNK, LANES))
    intra = jnp.concatenate([intra_f, intra_b], axis=0)
    kd = jnp.concatenate([kd_f, kd_b], axis=0)
    qd = jnp.concatenate([bc(qd_f), bc(qd_b)], axis=0)
    cd = jnp.broadcast_to(jnp.exp(CHUNK * lg).reshape(2 * HEADS, 1), (2 * HEADS, LANES))
    return intra, kd, qd, cd


def kernel(x, c, ctx, c_ctx, w_mod, b_mod, norm_g, w_in, ret_log_decay, ret_norm_g, mlstm_conv_w,
           mlstm_conv_b, mlstm_gate_b, mlstm_norm_g, w_out, w_router, router_bias, w_gate, w_up, w_down,
           ws_gate, ws_up, ws_down):
    b, n, d = x.shape
    n_ctx = ctx.shape[1]
    depth = w_mod.shape[0]
    assert depth == 1, "only the single-layer configuration is implemented"
    assert d // 2 // HEADS == LANES
    n_exp = w_router.shape[2]
    t_all = b * n
    r_w = d // 2
    main_cols = 8 * r_w
    l = 0

    pad = (-(b + 1)) % 8
    cc = jnp.concatenate([c, c_ctx[None, :], jnp.zeros((pad, d), F32)], axis=0)
    mod3 = _modulation(cc, w_mod[l], b_mod[l]).reshape(b + 1 + pad, N_MOD, d)

    w_groups = w_in[l, :, :main_cols].astype(BF16).reshape(d, 8, r_w)
    w_main = w_groups[:, jnp.array([0, 1, 2, 4, 5, 6, 3, 7]), :].reshape(d, main_cols)
    wg = w_in[l, :, main_cols:].astype(BF16)
    wgt = wg.T
    gb = mlstm_gate_b[l].reshape(-1).astype(F32)
    tabs = _retention_tables(ret_log_decay[l])
    head_g = jnp.concatenate([ret_norm_g[l], mlstm_norm_g[l]]).reshape(1, d).astype(F32)
    wr = w_router[l].T.astype(F32)
    wr_hi = wr.astype(BF16)
    wr_lo = (wr - wr_hi.astype(F32)).astype(BF16)

    def inproj(seq, mod_row, ts):
        cos2, sin2 = _rope_tables(n) if mod_row is None else (
            jnp.ones((seq.shape[1], LANES), F32), jnp.zeros((seq.shape[1], LANES), F32))
        return _inproj(seq, mod3, mod_row, norm_g[l, 0:1], w_main, wgt, mlstm_conv_w[l],
                       mlstm_conv_b[l].reshape(1, -1), gb.reshape(16, 1), cos2, sin2, ts)

    nst = 2 * HEADS
    zero_states = (jnp.zeros((b, nst, LANES, LANES), F32), jnp.zeros((b, nst, LANES, 2 * LANES), F32),
                   jnp.zeros((b, nst, LANES), F32))
    p_c, kt_c, gr_c = inproj(ctx, b, min(n_ctx, 512))
    ctx_states = _scan(p_c, kt_c, gr_c, tabs, zero_states, with_output=False)

    ts = min(n, 512)
    p_l, kt_l, gr_l = inproj(x, None, ts)
    o_f, o_b, _, _, _ = _scan(p_l, kt_l, gr_l, tabs, tuple(ctx_states), with_output=True)
    ts_moe = MOE_TILE
    x1, h2, rank, wdense, tile_cnt = _post(
        x, o_f, o_b, p_l, mod3, norm_g[l], head_g, w_out[l].astype(BF16), wr_hi, wr_lo,
        router_bias[l].reshape(n_exp, 1).astype(F32), ts_moe)

    nt = t_all // ts_moe
    cnt = tile_cnt[:, 0, :].astype(I32)
    run_rows = (cnt + ROW_ALIGN - 1) // ROW_ALIGN * ROW_ALIGN
    seg_cap = (jnp.sum(run_rows, axis=0) + SLOT_ROWS + EXPERT_BLOCK - 1) // EXPERT_BLOCK * EXPERT_BLOCK
    seg_end = jnp.cumsum(seg_cap)
    run_start = (seg_end - seg_cap)[None, :] + jnp.cumsum(run_rows, axis=0) - run_rows
    ov_rows = jnp.maximum(run_rows - SLOT_ROWS, 0)
    ov_off = jnp.cumsum(ov_rows, axis=1) - ov_rows
    n_spill = (jnp.sum(ov_rows, axis=1, keepdims=True) + SPILL_CHUNK - 1) // SPILL_CHUNK
    spills = ov_rows > 0
    n_ov = jnp.sum(spills.astype(I32), axis=1, keepdims=True)
    nth = jnp.cumsum(spills.astype(I32), axis=1) - 1
    is_jth = spills[:, None, :] & (nth[:, None, :] == jnp.arange(n_exp, dtype=I32)[None, :, None])
    compact = lambda v: jnp.sum(jnp.where(is_jth, v[:, None, :], 0), axis=2)
    ov_e = compact(jnp.broadcast_to(jnp.arange(n_exp, dtype=I32)[None, :], cnt.shape))
    fits_half = (run_rows <= HALF_ROWS).astype(I32)
    meta = jnp.concatenate([run_start, ov_e, compact(ov_rows // ROW_ALIGN), compact(ov_off), n_ov, n_spill,
                            fits_half], axis=1).astype(I32)
    meta = jnp.pad(meta, ((0, 0), (0, (-meta.shape[1]) % LANES))).reshape(nt, 1, -1)
    p_rows = -(-(t_all * TOP_K + nt * n_exp * (ROW_ALIGN - 1) + n_exp * (SLOT_ROWS + EXPERT_BLOCK - 1))
               // EXPERT_BLOCK) * EXPERT_BLOCK
    nb = p_rows // EXPERT_BLOCK
    blk_first = jnp.arange(nb, dtype=I32) * EXPERT_BLOCK
    block_e = jnp.minimum(jnp.sum((seg_end[None, :] <= blk_first[:, None]).astype(I32), axis=1), n_exp - 1)
    n_used = (seg_end[-1:] // EXPERT_BLOCK).astype(I32)
    seg = jnp.concatenate([seg_end.astype(I32), n_used])
    seg = jnp.pad(seg, (0, (-seg.shape[0]) % LANES)).reshape(1, 1, -1)

    xs = _dispatch(h2, rank, meta, seg, p_rows, ts_moe, n_exp)
    ys = _experts(xs, block_e, n_used, w_gate[l].astype(BF16), w_up[l].astype(BF16), w_down[l].astype(BF16),
                  EXPERT_BLOCK)
    out = _combine(ys, meta, rank.T, wdense.T, rank, wdense, x1.reshape(t_all, d), h2, mod3, norm_g[l],
                   ws_gate[l].astype(BF16), ws_up[l].astype(BF16), ws_down[l].astype(BF16), n, ts_moe, n_exp)
    return out.reshape(b, n, d)
```

```python
import functools

import jax
import jax.numpy as jnp
from jax import lax
from jax.experimental import pallas as pl
from jax.experimental.pallas import tpu as pltpu

F32 = jnp.float32
BF16 = jnp.bfloat16
I32 = jnp.int32

EPS = 1e-6
LANES = 128
CHUNK = 128
HEADS = 4
GRID_W = 64
ROPE_BASE = 10000.0
N_GROUPS = 8
TOPK_GROUPS = 4
TOP_K = 8
ROUTED_SCALE = 2.5
N_MOD = 6
MOE_TILE = 256
SPILL_CHUNK = 256
ROW_ALIGN = 16
SLOT_ROWS = 64
HALF_ROWS = 32
SPILL_PIECE = 64
SLOT_GROUP = 8
EXPERT_BLOCK = 512
NEG_INF = float("-inf")
P_SLOT = {0: 0, 2: 1, 3: 2, 5: 3, 6: 4, 7: 5}


def _sigmoid(v):
    return 1.0 / (1.0 + jnp.exp(-v))


def _silu(v):
    return v * _sigmoid(v)


def _log_sigmoid(v):
    return jnp.minimum(v, 0.0) - jnp.log(1.0 + jnp.exp(-jnp.abs(v)))


def _dot(a, b):
    return jnp.dot(a, b, preferred_element_type=F32)


def _dot_nt(a, b):
    return lax.dot_general(a, b, (((1,), (1,)), ((), ())), preferred_element_type=F32)


def _dot_tn(a, b):
    return lax.dot_general(a, b, (((0,), (0,)), ((), ())), preferred_element_type=F32)


def _split3(a):
    hi = a.astype(BF16)
    r = a - hi.astype(F32)
    mid = r.astype(BF16)
    lo = (r - mid.astype(F32)).astype(BF16)
    return hi, mid, lo


def _rms(v, g):
    ms = jnp.mean(v * v, axis=-1, keepdims=True)
    return v * lax.rsqrt(ms + EPS) * g


def _mod_kernel(c_ref, w_ref, b_ref, o_ref):
    a = _silu(c_ref[...])
    o_ref[...] = jnp.dot(a, w_ref[...], preferred_element_type=F32,
                         precision=lax.Precision.HIGHEST) + b_ref[...]


def _modulation(cc, w_mod, b_mod):
    rows, d = cc.shape
    cols = w_mod.shape[1]
    tn = d
    return pl.pallas_call(
        _mod_kernel,
        grid=(cols // tn,),
        in_specs=[pl.BlockSpec((rows, d), lambda j: (0, 0)),
                  pl.BlockSpec((d, tn), lambda j: (0, j)),
                  pl.BlockSpec((1, tn), lambda j: (0, j))],
        out_specs=pl.BlockSpec((rows, tn), lambda j: (0, j)),
        out_shape=jax.ShapeDtypeStruct((rows, cols), F32),
        name="mod",
    )(cc, w_mod, b_mod.reshape(1, cols))


def _inproj_kernel(x_ref, xp_ref, xn_ref, mod_ref, g_ref, w_ref, wgt_ref, cw_ref, cb_ref,
                   gbr_ref, cos_ref, sin_ref, p_ref, kt_ref, gr_ref, *, ts, d):
    i = pl.program_id(1)
    last = pl.num_programs(1) - 1
    r_w = d // 2
    shift = mod_ref[0, 0:1, :]
    scale = mod_ref[0, 1:2, :]
    g = g_ref[...]

    def normmod(v):
        return _rms(v, g) * (1.0 + scale) + shift

    hb = normmod(x_ref[0]).astype(BF16)
    halo = jnp.concatenate([xp_ref[0], xn_ref[0]], axis=0)
    ph = _dot(normmod(halo).astype(BF16), w_ref[:, 3 * r_w:5 * r_w])
    prev_row = jnp.where(i == 0, 0.0, ph[7:8, :])
    next_row = jnp.where(i == last, 0.0, ph[8:9, :])

    cos2 = cos_ref[...]
    sin2 = sin_ref[...]
    rows = lax.broadcasted_iota(I32, (ts, r_w), 0)
    qscale = LANES ** -0.5

    for j in range(8):
        acc = _dot(hb, w_ref[:, j * r_w:(j + 1) * r_w])
        if j in (0, 1):
            if j == 0:
                acc = acc * qscale
            parts = []
            for h in range(HEADS):
                t = acc[:, h * LANES:(h + 1) * LANES]
                parts.append(t * cos2 + pltpu.roll(t, LANES // 2, axis=1) * sin2)
            acc = jnp.concatenate(parts, axis=1)
        elif j in (3, 4):
            c0 = (j - 3) * r_w
            pr = prev_row[:, c0:c0 + r_w]
            nx = next_row[:, c0:c0 + r_w]
            down = jnp.where(rows == 0, pr, pltpu.roll(acc, 1, axis=0))
            up = jnp.where(rows == ts - 1, nx, pltpu.roll(acc, ts - 1, axis=0))
            cw = cw_ref[:, c0:c0 + r_w]
            acc = down * cw[0:1, :] + acc * cw[1:2, :] + up * cw[2:3, :] + cb_ref[:, c0:c0 + r_w]
            acc = _silu(acc)
            if j == 4:
                acc = acc * qscale
        if j in (1, 4):
            kt_ref[0, (j // 4) * r_w:(j // 4 + 1) * r_w, :] = acc.T.astype(BF16)
        else:
            slot = P_SLOT[j]
            p_ref[0, :, slot * r_w:(slot + 1) * r_w] = acc.astype(BF16)

    gr = _dot_nt(wgt_ref[...], hb) + gbr_ref[...]
    ch_r = lax.broadcasted_iota(I32, gr.shape, 0)
    gr_ref[0] = jnp.where((ch_r // HEADS) % 2 == 1, _log_sigmoid(gr), gr)


def _inproj(x, mod3, mod_row, g, w_main, wgt, conv_w, conv_b, gb_col, cos2, sin2, ts):
    b, n, d = x.shape
    nt = n // ts
    nb8 = n // 8
    hb = ts // 8
    cols = w_main.shape[1]
    p_cols = cols // 8 * 6
    if mod_row is None:
        mod_map = lambda bi, i: (bi, 0, 0)
    else:
        mod_map = lambda bi, i: (mod_row, 0, 0)
    const2 = lambda bi, i: (0, 0)
    kern = functools.partial(_inproj_kernel, ts=ts, d=d)
    return pl.pallas_call(
        kern,
        grid=(b, nt),
        in_specs=[
            pl.BlockSpec((1, ts, d), lambda bi, i: (bi, i, 0)),
            pl.BlockSpec((1, 8, d), lambda bi, i: (bi, jnp.maximum(i * hb - 1, 0), 0)),
            pl.BlockSpec((1, 8, d), lambda bi, i: (bi, jnp.minimum((i + 1) * hb, nb8 - 1), 0)),
            pl.BlockSpec((1, N_MOD, d), mod_map),
            pl.BlockSpec((1, d), const2),
            pl.BlockSpec((d, cols), const2),
            pl.BlockSpec((16, d), const2),
            pl.BlockSpec((3, d), const2),
            pl.BlockSpec((1, d), const2),
            pl.BlockSpec((16, 1), const2),
            pl.BlockSpec((ts, LANES), lambda bi, i: (i, 0)),
            pl.BlockSpec((ts, LANES), lambda bi, i: (i, 0)),
        ],
        out_specs=[
            pl.BlockSpec((1, ts, p_cols), lambda bi, i: (bi, i, 0)),
            pl.BlockSpec((1, d, ts), lambda bi, i: (bi, 0, i)),
            pl.BlockSpec((1, 16, ts), lambda bi, i: (bi, 0, i)),
        ],
        out_shape=[
            jax.ShapeDtypeStruct((b, n, p_cols), BF16),
            jax.ShapeDtypeStruct((b, d, n), BF16),
            jax.ShapeDtypeStruct((b, 16, n), F32),
        ],
        compiler_params=pltpu.CompilerParams(dimension_semantics=("parallel", "parallel")),
        name="inproj",
    )(x, x, x, mod3, g, w_main, wgt, conv_w, conv_b, gb_col, cos2, sin2)


def _scan_kernel(pf_ref, pb_ref, ktf_ref, ktb_ref, grf_ref, grb_ref, intra_ref, kd_ref, qd_ref, cd_ref,
                 rs0_ref, mc0_ref, mm0_ref, *out_refs, with_output, r_w):
    if with_output:
        of_ref, ob_ref, rs_ref, mc_ref, mm_ref = out_refs
    else:
        rs_ref, mc_ref, mm_ref = out_refs
    j = pl.program_id(1)

    @pl.when(j == 0)
    def _():
        rs_ref[...] = rs0_ref[...]
        mc_ref[...] = mc0_ref[...]
        mm_ref[...] = mm0_ref[...]

    c = CHUNK
    row = lax.broadcasted_iota(I32, (c, c), 0)
    col = lax.broadcasted_iota(I32, (c, c), 1)
    tri_le = (row <= col)
    tri_ge = (row >= col)
    eye = row == col
    ones_blk = jnp.ones((c, LANES), BF16)
    lane = lax.broadcasted_iota(I32, (HEADS, c), 1)

    def spread_rows(vecs):
        diag = jnp.concatenate([jnp.where(eye, v, 0.0) for v in vecs], axis=0)
        hi = diag.astype(BF16)
        lo = (diag - hi.astype(F32)).astype(BF16)
        out = _dot(hi, ones_blk) + _dot(lo, ones_blk)
        return [out[n * c:(n + 1) * c, :] for n in range(len(vecs))]

    def running_max(a, fwd):
        pm = a
        s = 1
        while s < c:
            if fwd:
                pm = jnp.where(lane >= s, jnp.maximum(pm, pltpu.roll(pm, s, axis=1)), pm)
            else:
                pm = jnp.where(lane < c - s, jnp.maximum(pm, pltpu.roll(pm, c - s, axis=1)), pm)
            s *= 2
        return pm

    n_st = 2 * HEADS
    rs_prev = [rs_ref[0, st] for st in range(n_st)]
    mc_prev = [mc_ref[0, st] for st in range(n_st)]
    mm_prev = [mm_ref[0, st:st + 1, 0:1] for st in range(n_st)]
    heads = [(dr, h) for dr in range(2) for h in range(HEADS)]
    o0 = 2 * r_w

    def cols(ref, base, h):
        return ref[0, :, base + h * LANES:base + (h + 1) * LANES]

    cs_rows = []
    for dr in range(2):
        gr = (grf_ref, grb_ref)[dr][0]
        tri = jnp.where(tri_le if dr == 0 else tri_ge, 1.0, 0.0).astype(BF16)
        cs_rows.append((gr, sum(_dot(piece, tri) for piece in _split3(gr))))
    ret_upd, ret_sc, ret_in, ml_sc, ml_in = [], [], [], [], []
    for dr, h in heads:
        st = dr * HEADS + h
        p_ref, kt_ref = (pf_ref, pb_ref)[dr], (ktf_ref, ktb_ref)[dr]
        kt = kt_ref[0, h * LANES:(h + 1) * LANES, :]
        ks = (kt.astype(F32) * kd_ref[st:st + 1, :]).astype(BF16)
        ret_upd.append(_dot(ks, cols(p_ref, r_w, h)))
        if with_output:
            q, mq = cols(p_ref, 0, h), cols(p_ref, o0, h)
            ret_sc.append(_dot(q, kt))
            ret_in.append(_dot(q, rs_prev[st].astype(BF16)))
            ml_sc.append(_dot(mq, kt_ref[0, r_w + h * LANES:r_w + (h + 1) * LANES, :]))
            ml_in.append(_dot(mq, mc_prev[st].astype(BF16)))

    gate = []
    for dr in range(2):
        gr, cs_row = cs_rows[dr]
        g0 = dr * 2 * HEADS
        a_rows = gr[g0:g0 + HEADS, :] - cs_row[g0 + HEADS:g0 + 2 * HEADS, :]
        pm_rows = running_max(a_rows, dr == 0)
        last = c - 1 if dr == 0 else 0
        for h in range(HEADS):
            st = dr * HEADS + h
            b_row = cs_row[g0 + HEADS + h:g0 + HEADS + h + 1, :]
            a_row, pm_row = a_rows[h:h + 1, :], pm_rows[h:h + 1, :]
            b_tot = b_row[:, last:last + 1]
            m_prev = mm_prev[st]
            m_next = b_tot + jnp.maximum(m_prev, pm_row[:, last:last + 1])
            gate.append(dict(b_row=b_row, a_row=a_row, pm_row=pm_row, m_prev=m_prev, m_next=m_next,
                             decay_prev=jnp.exp(b_tot + m_prev - m_next),
                             ws_row=jnp.exp(b_tot + a_row - m_next)))
    ml_upd, spread = [], []
    for dr, h in heads:
        st = dr * HEADS + h
        p_ref, kt_ref = (pf_ref, pb_ref)[dr], (ktf_ref, ktb_ref)[dr]
        mkt = kt_ref[0, r_w + h * LANES:r_w + (h + 1) * LANES, :]
        kw = (mkt.astype(F32) * gate[st]["ws_row"]).astype(BF16)
        v_ext = jnp.concatenate([cols(p_ref, o0 + r_w, h), ones_blk], axis=1)
        ml_upd.append(_dot(kw, v_ext))
        if with_output:
            spread.append(spread_rows([gate[st]["b_row"], gate[st]["pm_row"]]))

    if with_output:
        ret_out, ml_out, stab = [], [], []
        for dr, h in heads:
            st = dr * HEADS + h
            p_ref = (pf_ref, pb_ref)[dr]
            sc = (ret_sc[st] * intra_ref[st]).astype(BF16)
            ret_out.append(_dot(sc, cols(p_ref, r_w, h)))
            b_sp, pm_sp = spread[st]
            c_sp = jnp.maximum(gate[st]["m_prev"], pm_sp)
            causal = tri_ge if dr == 0 else tri_le
            w = jnp.where(causal, jnp.exp(gate[st]["a_row"] - c_sp), 0.0)
            v_ext = jnp.concatenate([cols(p_ref, o0 + r_w, h), ones_blk], axis=1)
            ml_out.append(_dot((ml_sc[st] * w).astype(BF16), v_ext))
            stab.append((jnp.exp(gate[st]["m_prev"] - c_sp), jnp.exp(-(b_sp + c_sp))))

    for dr, h in heads:
        st = dr * HEADS + h
        if with_output:
            o_ref = (of_ref, ob_ref)[dr]
            o_ref[0, :, h * LANES:(h + 1) * LANES] = (ret_out[st] + qd_ref[st] * ret_in[st]).astype(BF16)
            inter, floor = stab[st]
            hx = ml_out[st] + jnp.concatenate([inter, inter], axis=1) * ml_in[st]
            hout = hx[:, :LANES] / jnp.maximum(jnp.abs(hx[:, LANES:]), floor)
            o_ref[0, :, r_w + h * LANES:r_w + (h + 1) * LANES] = hout.astype(BF16)
    for dr, h in heads:
        st = dr * HEADS + h
        rs_ref[0, st] = rs_prev[st] * cd_ref[st:st + 1, :] + ret_upd[st]
        mc_ref[0, st] = gate[st]["decay_prev"] * mc_prev[st] + ml_upd[st]
        mm_ref[0, st:st + 1, :] = jnp.broadcast_to(gate[st]["m_next"], (1, LANES))


def _scan(p, kt, g_row, tabs, states, with_output):
    b, n, cols = p.shape
    nch = n // CHUNK
    r_w = cols // 6
    intra, kd, qd, cd = tabs
    rs0, mc0, mm0 = states
    nst = 2 * HEADS
    fwd3 = lambda bi, j: (bi, j, 0)
    bwd3 = lambda bi, j: (bi, nch - 1 - j, 0)
    c3 = lambda bi, j: (0, 0, 0)
    st4 = lambda bi, j: (bi, 0, 0, 0)
    in_specs = [
        pl.BlockSpec((1, CHUNK, 4 * r_w), fwd3),
        pl.BlockSpec((1, CHUNK, 4 * r_w), bwd3),
        pl.BlockSpec((1, 2 * r_w, CHUNK), lambda bi, j: (bi, 0, j)),
        pl.BlockSpec((1, 2 * r_w, CHUNK), lambda bi, j: (bi, 0, nch - 1 - j)),
        pl.BlockSpec((1, 16, CHUNK), lambda bi, j: (bi, 0, j)),
        pl.BlockSpec((1, 16, CHUNK), lambda bi, j: (bi, 0, nch - 1 - j)),
        pl.BlockSpec((nst, CHUNK, LANES), c3),
        pl.BlockSpec((nst, CHUNK), lambda bi, j: (0, 0)),
        pl.BlockSpec((nst, CHUNK, LANES), c3),
        pl.BlockSpec((nst, LANES), lambda bi, j: (0, 0)),
        pl.BlockSpec((1, nst, LANES, LANES), st4),
        pl.BlockSpec((1, nst, LANES, 2 * LANES), st4),
        pl.BlockSpec((1, nst, LANES), lambda bi, j: (bi, 0, 0)),
    ]
    st_specs = [
        pl.BlockSpec((1, nst, LANES, LANES), st4),
        pl.BlockSpec((1, nst, LANES, 2 * LANES), st4),
        pl.BlockSpec((1, nst, LANES), lambda bi, j: (bi, 0, 0)),
    ]
    st_shapes = [
        jax.ShapeDtypeStruct((b, nst, LANES, LANES), F32),
        jax.ShapeDtypeStruct((b, nst, LANES, 2 * LANES), F32),
        jax.ShapeDtypeStruct((b, nst, LANES), F32),
    ]
    if with_output:
        out_specs = [pl.BlockSpec((1, CHUNK, 2 * r_w), fwd3), pl.BlockSpec((1, CHUNK, 2 * r_w), bwd3)] + st_specs
        out_shape = [jax.ShapeDtypeStruct((b, n, 2 * r_w), BF16)] * 2 + st_shapes
    else:
        out_specs, out_shape = st_specs, st_shapes
    kern = functools.partial(_scan_kernel, with_output=with_output, r_w=r_w)
    return pl.pallas_call(
        kern,
        grid=(b, nch),
        in_specs=in_specs,
        out_specs=out_specs,
        out_shape=out_shape,
        compiler_params=pltpu.CompilerParams(dimension_semantics=("parallel", "arbitrary")),
        name="scan_out" if with_output else "scan_state",
    )(p, p, kt, kt, g_row, g_row, intra, kd, qd, cd, rs0, mc0, mm0)


def _post_kernel(x_ref, of_ref, ob_ref, rg_ref, mo_ref, mod_ref, ng_ref, hg_ref, wo_ref, wrh_ref, wrl_ref,
                 rb_ref, su_ref, x1_ref, h2_ref, rk_ref, wd_ref, cnt_ref, *, ts, d, n_exp):
    s = of_ref[0].astype(F32) + ob_ref[0].astype(F32)
    parts = []
    for gi in range(2 * HEADS):
        sl = s[:, gi * LANES:(gi + 1) * LANES]
        mu = jnp.mean(sl, axis=-1, keepdims=True)
        dv = sl - mu
        var = jnp.mean(dv * dv, axis=-1, keepdims=True)
        y = dv * lax.rsqrt(var + EPS) * hg_ref[:, gi * LANES:(gi + 1) * LANES]
        if gi < HEADS:
            gate = _silu(rg_ref[0, :, gi * LANES:(gi + 1) * LANES].astype(F32))
        else:
            gate = _sigmoid(mo_ref[0, :, (gi - HEADS) * LANES:(gi - HEADS + 1) * LANES].astype(F32))
        parts.append((y * gate).astype(BF16))
    mixed = jnp.concatenate(parts, axis=1)
    y = _dot(mixed, wo_ref[...])
    g1 = mod_ref[0, 2:3, :]
    sh2 = mod_ref[0, 3:4, :]
    sc2 = mod_ref[0, 4:5, :]
    x1 = x_ref[0] + g1 * _rms(y, ng_ref[1:2, :])
    x1_ref[0] = x1
    h2 = _rms(x1, ng_ref[2:3, :]) * (1.0 + sc2) + sh2
    h_hi = h2.astype(BF16)
    h2_ref[...] = h_hi

    h_lo = (h2 - h_hi.astype(F32)).astype(BF16)
    logits = _dot_nt(wrh_ref[...], h_hi) + _dot_nt(wrh_ref[...], h_lo) + _dot_nt(wrl_ref[...], h_hi)
    scores = _sigmoid(logits)
    sel = scores + rb_ref[...]
    gsz = n_exp // N_GROUPS
    iota_g = lax.broadcasted_iota(I32, (gsz, ts), 0).astype(F32)
    grp = []
    for gi in range(N_GROUPS):
        blk = sel[gi * gsz:(gi + 1) * gsz, :]
        m1 = jnp.max(blk, axis=0, keepdims=True)
        i1 = jnp.min(jnp.where(blk == m1, iota_g, float(gsz)), axis=0, keepdims=True)
        m2 = jnp.max(jnp.where(iota_g == i1, NEG_INF, blk), axis=0, keepdims=True)
        grp.append(m1 + m2)
    masked_parts = []
    for gi in range(N_GROUPS):
        rank = jnp.zeros((1, ts), F32)
        for gj in range(N_GROUPS):
            if gj == gi:
                continue
            beats = (grp[gj] >= grp[gi]) if gj < gi else (grp[gj] > grp[gi])
            rank = rank + jnp.where(beats, 1.0, 0.0)
        keep = rank < float(TOPK_GROUPS)
        masked_parts.append(jnp.where(keep, sel[gi * gsz:(gi + 1) * gsz, :], NEG_INF))
    masked = jnp.concatenate(masked_parts, axis=0)

    iota_e = lax.broadcasted_iota(I32, (n_exp, ts), 0).astype(F32)
    selmask = jnp.zeros((n_exp, ts), F32)
    for _ in range(TOP_K):
        mx = jnp.max(masked, axis=0, keepdims=True)
        ei = jnp.min(jnp.where(masked == mx, iota_e, float(n_exp)), axis=0, keepdims=True)
        hit = iota_e == ei
        selmask = jnp.where(hit, 1.0, selmask)
        masked = jnp.where(hit, NEG_INF, masked)
    picked = selmask > 0.0
    wsel = jnp.where(picked, scores, 0.0)
    wd_ref[...] = wsel / jnp.sum(wsel, axis=0, keepdims=True) * ROUTED_SCALE
    rank = _dot(selmask.astype(BF16), su_ref[...])
    rk_ref[...] = jnp.where(picked, rank, -1.0)
    cnt_ref[0] = _dot_nt(jnp.ones((8, ts), BF16), selmask.astype(BF16))


def _post(x, o_f, o_b, p, mod3, norm_g, head_g, w_out, wr_hi, wr_lo, rbias, ts):
    b, n, d = x.shape
    nt = n // ts
    t_all = b * n
    n_exp = wr_hi.shape[0]
    r_w = d // 2
    su = jnp.where(lax.broadcasted_iota(I32, (ts, ts), 0) < lax.broadcasted_iota(I32, (ts, ts), 1),
                   1.0, 0.0).astype(BF16)
    tok3 = lambda bi, i: (bi, i, 0)
    c2 = lambda bi, i: (0, 0)
    flat = lambda bi, i: (0, bi * nt + i)
    kern = functools.partial(_post_kernel, ts=ts, d=d, n_exp=n_exp)
    return pl.pallas_call(
        kern,
        grid=(b, nt),
        in_specs=[
            pl.BlockSpec((1, ts, d), tok3),
            pl.BlockSpec((1, ts, d), tok3),
            pl.BlockSpec((1, ts, d), tok3),
            pl.BlockSpec((1, ts, r_w), lambda bi, i: (bi, i, 4)),
            pl.BlockSpec((1, ts, r_w), lambda bi, i: (bi, i, 5)),
            pl.BlockSpec((1, N_MOD, d), lambda bi, i: (bi, 0, 0)),
            pl.BlockSpec((4, d), c2),
            pl.BlockSpec((1, d), c2),
            pl.BlockSpec((d, d), c2),
            pl.BlockSpec((n_exp, d), c2),
            pl.BlockSpec((n_exp, d), c2),
            pl.BlockSpec((n_exp, 1), c2),
            pl.BlockSpec((ts, ts), c2),
        ],
        out_specs=[
            pl.BlockSpec((1, ts, d), tok3),
            pl.BlockSpec((ts, d), lambda bi, i: (bi * nt + i, 0)),
            pl.BlockSpec((n_exp, ts), flat),
            pl.BlockSpec((n_exp, ts), flat),
            pl.BlockSpec((1, 8, n_exp), lambda bi, i: (bi * nt + i, 0, 0)),
        ],
        out_shape=[
            jax.ShapeDtypeStruct((b, n, d), F32),
            jax.ShapeDtypeStruct((t_all, d), BF16),
            jax.ShapeDtypeStruct((n_exp, t_all), F32),
            jax.ShapeDtypeStruct((n_exp, t_all), F32),
            jax.ShapeDtypeStruct((b * nt, 8, n_exp), F32),
        ],
        compiler_params=pltpu.CompilerParams(dimension_semantics=("parallel", "parallel")),
        name="post",
    )(x, o_f, o_b, p, p, mod3, norm_g, head_g, w_out, wr_hi, wr_lo, rbias, su)


def _slot_transfers(meta_ref, stage, buf, hbm, sems, n_exp, to_hbm, wait):
    for e in range(n_exp):
        start = pl.multiple_of(meta_ref[0, 0, e], ROW_ALIGN)
        fits_half = meta_ref[0, 0, 4 * n_exp + 2 + e] != 0
        for rows, cond in ((HALF_ROWS, fits_half), (SLOT_ROWS, jnp.logical_not(fits_half))):
            @pl.when(cond)
            def _():
                run = hbm.at[pl.ds(start, rows)]
                slot = stage.at[buf, pl.ds(e * SLOT_ROWS, rows)]
                cp = (pltpu.make_async_copy(slot, run, sems.at[buf]) if to_hbm
                      else pltpu.make_async_copy(run, slot, sems.at[buf]))
                if wait:
                    cp.wait()
                else:
                    cp.start(priority=e % 2)


def _overflow_copy(meta_ref, spill, hbm, sem, n_exp, j, i, to_hbm, piece_rows=ROW_ALIGN):
    e = meta_ref[0, 0, n_exp + j]
    src = meta_ref[0, 0, 3 * n_exp + j] + piece_rows * i
    dst = meta_ref[0, 0, e] + SLOT_ROWS + piece_rows * i
    piece = spill.at[pl.ds(pl.multiple_of(src, ROW_ALIGN), piece_rows)]
    rows = hbm.at[pl.ds(pl.multiple_of(dst, ROW_ALIGN), piece_rows)]
    return pltpu.make_async_copy(piece, rows, sem) if to_hbm else pltpu.make_async_copy(rows, piece, sem)


def _for_overflow_pieces(meta_ref, n_exp, fn, piece_rows=ROW_ALIGN):
    per = piece_rows // ROW_ALIGN

    def per_expert(j, carry):
        def per_piece(i, c2):
            fn(j, i)
            return c2
        return lax.fori_loop(0, (meta_ref[0, 0, 2 * n_exp + j] + per - 1) // per, per_piece, carry)
    lax.fori_loop(0, meta_ref[0, 0, 4 * n_exp], per_expert, 0)


def _spill_matrix_rows(meta_ref, rk_ref, wd_ref, base, ts, n_exp):
    rows = (lax.broadcasted_iota(I32, (SPILL_CHUNK, ts), 0) + base).astype(F32)

    def per_expert(j, hit):
        e = meta_ref[0, 0, n_exp + j]
        rk = rk_ref[pl.ds(e, 1), :]
        val = 1.0 if wd_ref is None else wd_ref[pl.ds(e, 1), :]
        target = jnp.where(rk >= SLOT_ROWS, rk - SLOT_ROWS + meta_ref[0, 0, 3 * n_exp + j].astype(F32), -1.0)
        return jnp.where(target == rows, val, hit)

    return lax.fori_loop(0, meta_ref[0, 0, 4 * n_exp], per_expert,
                         jnp.zeros((SPILL_CHUNK, ts), F32)).astype(BF16)


def _zero_fill(seg_ref, xs_hbm, stage, sem, n_exp, n_blocks):
    tail = SLOT_ROWS + EXPERT_BLOCK
    stage[0, 0:tail, :] = jnp.zeros((tail, stage.shape[2]), stage.dtype)
    tails = []
    for e in range(n_exp):
        start = jnp.maximum(seg_ref[0, 0, e] - tail, 0)
        tails.append(pltpu.make_async_copy(stage.at[0, pl.ds(0, tail)],
                                           xs_hbm.at[pl.ds(pl.multiple_of(start, ROW_ALIGN), tail)], sem))
    for cp in tails:
        cp.start()
    n_used = seg_ref[0, 0, n_exp]

    def block_copy(i):
        row = pl.multiple_of(i * EXPERT_BLOCK, EXPERT_BLOCK)
        return pltpu.make_async_copy(stage.at[0, pl.ds(0, EXPERT_BLOCK)], xs_hbm.at[pl.ds(row, EXPERT_BLOCK)], sem)

    def start_block(i, carry):
        block_copy(i).start()
        return carry

    def wait_block(i, carry):
        block_copy(i).wait()
        return carry

    lax.fori_loop(n_used, n_blocks, start_block, 0)
    for cp in tails:
        cp.wait()
    lax.fori_loop(n_used, n_blocks, wait_block, 0)


def _dispatch_kernel(meta_ref, prv_ref, seg_ref, x_ref, rk_ref, xs_hbm, stage, spill, sems, sem_ov, *,
                     ts, n_exp, n_blocks):
    i = pl.program_id(0)
    buf = i % 2

    @pl.when(i == 0)
    def _():
        _zero_fill(seg_ref, xs_hbm, stage, sem_ov, n_exp, n_blocks)
        spill[...] = jnp.zeros(spill.shape, spill.dtype)

    x = x_ref[...]
    slot_row = lax.broadcasted_iota(I32, (SLOT_ROWS, ts), 0).astype(F32)
    group_rows = SLOT_GROUP * SLOT_ROWS
    for g in range(n_exp // SLOT_GROUP):
        pick = jnp.concatenate(
            [jnp.where(rk_ref[e:e + 1, :] == slot_row, 1.0, 0.0)
             for e in range(g * SLOT_GROUP, (g + 1) * SLOT_GROUP)], axis=0).astype(BF16)
        stage[buf, g * group_rows:(g + 1) * group_rows, :] = _dot(pick, x).astype(BF16)
    n_spill = meta_ref[0, 0, 4 * n_exp + 1]

    @pl.when(i > 0)
    def _():
        _for_overflow_pieces(prv_ref, n_exp,
                             lambda e, k: _overflow_copy(prv_ref, spill, xs_hbm, sem_ov, n_exp, e, k, True,
                                                         SPILL_PIECE).wait(), SPILL_PIECE)

    def spill_chunk(ci, carry):
        base = pl.multiple_of(ci * SPILL_CHUNK, SPILL_CHUNK)
        spill[pl.ds(base, SPILL_CHUNK), :] = _dot(
            _spill_matrix_rows(meta_ref, rk_ref, None, base, ts, n_exp), x).astype(BF16)
        return carry

    lax.fori_loop(0, n_spill, spill_chunk, 0)

    @pl.when(i > 0)
    def _():
        _slot_transfers(prv_ref, stage, 1 - buf, xs_hbm, sems, n_exp, True, wait=True)

    _slot_transfers(meta_ref, stage, buf, xs_hbm, sems, n_exp, True, wait=False)
    _for_overflow_pieces(meta_ref, n_exp,
                         lambda e, k: _overflow_copy(meta_ref, spill, xs_hbm, sem_ov, n_exp, e, k, True,
                                                     SPILL_PIECE).start(), SPILL_PIECE)

    @pl.when(i == pl.num_programs(0) - 1)
    def _():
        _slot_transfers(meta_ref, stage, buf, xs_hbm, sems, n_exp, True, wait=True)
        _for_overflow_pieces(meta_ref, n_exp,
                             lambda e, k: _overflow_copy(meta_ref, spill, xs_hbm, sem_ov, n_exp, e, k, True,
                                                         SPILL_PIECE).wait(), SPILL_PIECE)


def _dispatch(h2, rank, meta, seg, p_rows, ts, n_exp):
    t_all, d = h2.shape
    nt = t_all // ts
    kern = functools.partial(_dispatch_kernel, ts=ts, n_exp=n_exp, n_blocks=p_rows // EXPERT_BLOCK)
    return pl.pallas_call(
        kern,
        grid=(nt,),
        in_specs=[
            pl.BlockSpec((1, 1, meta.shape[2]), lambda i: (i, 0, 0), memory_space=pltpu.SMEM),
            pl.BlockSpec((1, 1, meta.shape[2]), lambda i: (jnp.maximum(i - 1, 0), 0, 0), memory_space=pltpu.SMEM),
            pl.BlockSpec((1, 1, seg.shape[2]), lambda i: (0, 0, 0), memory_space=pltpu.SMEM),
            pl.BlockSpec((ts, d), lambda i: (i, 0)),
            pl.BlockSpec((n_exp, ts), lambda i: (0, i)),
        ],
        out_specs=pl.BlockSpec(memory_space=pl.ANY),
        out_shape=jax.ShapeDtypeStruct((p_rows, d), BF16),
        scratch_shapes=[pltpu.VMEM((2, n_exp * SLOT_ROWS, d), BF16),
                        pltpu.VMEM((ts * TOP_K + SPILL_PIECE, d), BF16),
                        pltpu.SemaphoreType.DMA((2,)), pltpu.SemaphoreType.DMA(())],
        compiler_params=pltpu.CompilerParams(dimension_semantics=("arbitrary",), has_side_effects=True),
        name="dispatch",
    )(meta, meta, seg, h2, rank)


def _expert_kernel(be_ref, nu_ref, xs_ref, wg_ref, wu_ref, wd_ref, ys_ref):
    del be_ref
    i = pl.program_id(0)

    @pl.when(i < nu_ref[0])
    def _():
        xb = xs_ref[...]
        a = _silu(_dot(xb, wg_ref[0])) * _dot(xb, wu_ref[0])
        ys_ref[...] = _dot(a.astype(BF16), wd_ref[0]).astype(BF16)


def _experts(xs, block_e, n_used, w_gate, w_up, w_down, blk):
    p_rows, dw = xs.shape
    n_exp, d, ff = w_gate.shape
    nb = p_rows // blk
    used = lambda i, nu: jnp.minimum(i, nu[0] - 1)
    grid_spec = pltpu.PrefetchScalarGridSpec(
        num_scalar_prefetch=2,
        grid=(nb,),
        in_specs=[
            pl.BlockSpec((blk, dw), lambda i, be, nu: (used(i, nu), 0)),
            pl.BlockSpec((1, d, ff), lambda i, be, nu: (be[used(i, nu)], 0, 0)),
            pl.BlockSpec((1, d, ff), lambda i, be, nu: (be[used(i, nu)], 0, 0)),
            pl.BlockSpec((1, ff, d), lambda i, be, nu: (be[used(i, nu)], 0, 0)),
        ],
        out_specs=pl.BlockSpec((blk, dw), lambda i, be, nu: (used(i, nu), 0)),
    )
    return pl.pallas_call(
        _expert_kernel,
        grid_spec=grid_spec,
        out_shape=jax.ShapeDtypeStruct((p_rows, dw), BF16),
        input_output_aliases={2: 0},
        compiler_params=pltpu.CompilerParams(dimension_semantics=("arbitrary",)),
        name="experts",
    )(block_e, n_used, xs, w_gate, w_up, w_down)


def _combine_kernel(meta_ref, nxt_ref, ys_hbm, rkt_ref, wdt_ref, rk_ref, wd_ref, ex_ref, rp_ref, x1_ref, h2_ref,
                    mod_ref, ng_ref, sg_ref, su_ref, sd_ref, o_ref, stage, spill, acc, sems, sem_ov, *,
                    ts, n_exp):
    i = pl.program_id(0)
    buf = i % 2
    n_spill = meta_ref[0, 0, 4 * n_exp + 1]
    n_groups = n_exp // SLOT_GROUP

    @pl.when(i == 0)
    def _():
        stage[...] = jnp.zeros(stage.shape, stage.dtype)
        _slot_transfers(meta_ref, stage, 0, ys_hbm, sems, n_exp, False, wait=False)

    @pl.when(i < pl.num_programs(0) - 1)
    def _():
        _slot_transfers(nxt_ref, stage, 1 - buf, ys_hbm, sems, n_exp, False, wait=False)

    def clear_chunk(ci, carry):
        base = pl.multiple_of(ci * SPILL_CHUNK, SPILL_CHUNK)
        spill[pl.ds(base, SPILL_CHUNK), :] = jnp.zeros((SPILL_CHUNK, spill.shape[1]), spill.dtype)
        return carry

    lax.fori_loop(0, n_spill, clear_chunk, 0)
    _for_overflow_pieces(meta_ref, n_exp,
                         lambda e, k: _overflow_copy(meta_ref, spill, ys_hbm, sem_ov, n_exp, e, k, False).start())

    xb = h2_ref[...]
    a = _silu(_dot(xb, sg_ref[...])) * _dot(xb, su_ref[...])
    tot = _dot(a.astype(BF16), sd_ref[...])

    rank_lanes = _dot(rkt_ref[...].astype(BF16), ex_ref[...])
    weight_lanes = _dot(wdt_ref[...].astype(BF16), ex_ref[...])

    _slot_transfers(meta_ref, stage, buf, ys_hbm, sems, n_exp, False, wait=True)
    group_rows = SLOT_GROUP * SLOT_ROWS
    for g in range(n_groups):
        cols = slice(g * group_rows, (g + 1) * group_rows)
        unmix = jnp.where(rank_lanes[:, cols] == rp_ref[:, cols], weight_lanes[:, cols], 0.0).astype(BF16)
        tot = tot + _dot(unmix, stage[buf, g * group_rows:(g + 1) * group_rows, :])
    acc[...] = tot

    _for_overflow_pieces(meta_ref, n_exp,
                         lambda e, k: _overflow_copy(meta_ref, spill, ys_hbm, sem_ov, n_exp, e, k, False).wait())

    def spill_chunk(ci, carry):
        base = pl.multiple_of(ci * SPILL_CHUNK, SPILL_CHUNK)
        acc[...] += _dot_tn(_spill_matrix_rows(meta_ref, rk_ref, wd_ref, base, ts, n_exp),
                            spill[pl.ds(base, SPILL_CHUNK), :])
        return carry

    lax.fori_loop(0, n_spill, spill_chunk, 0)
    g2 = mod_ref[0, 5:6, :]
    o_ref[...] = x1_ref[...] + g2 * _rms(acc[...], ng_ref[3:4, :])


def _combine(ys, meta, rank_tm, wd_tm, rank, wd, x1_flat, h2, mod3, norm_g, ws_gate, ws_up, ws_down,
             n_seq, ts, n_exp):
    t_all, d = x1_flat.shape
    nt = t_all // ts
    per_b = n_seq // ts
    ff = ws_gate.shape[1]
    lanes = n_exp * SLOT_ROWS
    lane = lax.broadcasted_iota(I32, (n_exp, lanes), 1)
    expand = jnp.where(lane // SLOT_ROWS == lax.broadcasted_iota(I32, (n_exp, lanes), 0), 1.0, 0.0).astype(BF16)
    slot_rank = (jnp.arange(lanes, dtype=I32) % SLOT_ROWS).astype(F32).reshape(1, lanes)
    c2 = lambda i: (0, 0)
    kern = functools.partial(_combine_kernel, ts=ts, n_exp=n_exp)
    return pl.pallas_call(
        kern,
        grid=(nt,),
        in_specs=[
            pl.BlockSpec((1, 1, meta.shape[2]), lambda i: (i, 0, 0), memory_space=pltpu.SMEM),
            pl.BlockSpec((1, 1, meta.shape[2]), lambda i: (jnp.minimum(i + 1, nt - 1), 0, 0),
                         memory_space=pltpu.SMEM),
            pl.BlockSpec(memory_space=pl.ANY),
            pl.BlockSpec((ts, n_exp), lambda i: (i, 0)),
            pl.BlockSpec((ts, n_exp), lambda i: (i, 0)),
            pl.BlockSpec((n_exp, ts), lambda i: (0, i)),
            pl.BlockSpec((n_exp, ts), lambda i: (0, i)),
            pl.BlockSpec((n_exp, lanes), c2),
            pl.BlockSpec((1, lanes), c2),
            pl.BlockSpec((ts, d), lambda i: (i, 0)),
            pl.BlockSpec((ts, d), lambda i: (i, 0)),
            pl.BlockSpec((1, N_MOD, d), lambda i: (i // per_b, 0, 0)),
            pl.BlockSpec((4, d), c2),
            pl.BlockSpec((d, ff), c2),
            pl.BlockSpec((d, ff), c2),
            pl.BlockSpec((ff, d), c2),
        ],
        out_specs=pl.BlockSpec((ts, d), lambda i: (i, 0)),
        out_shape=jax.ShapeDtypeStruct((t_all, d), F32),
        scratch_shapes=[pltpu.VMEM((2, lanes, d), BF16), pltpu.VMEM((ts * TOP_K, d), BF16),
                        pltpu.VMEM((ts, d), F32),
                        pltpu.SemaphoreType.DMA((2,)), pltpu.SemaphoreType.DMA(())],
        compiler_params=pltpu.CompilerParams(dimension_semantics=("arbitrary",)),
        name="combine",
    )(meta, meta, ys, rank_tm, wd_tm, rank, wd, expand, slot_rank, x1_flat, h2, mod3, norm_g,
      ws_gate, ws_up, ws_down)


def _rope_tables(n):
    rows = jnp.repeat(jnp.arange(n // GRID_W, dtype=F32), GRID_W)
    cols = jnp.tile(jnp.arange(GRID_W, dtype=F32), n // GRID_W)
    quarter = LANES // 4
    freqs = ROPE_BASE ** (-jnp.arange(quarter, dtype=F32) / quarter)
    ang = jnp.concatenate([rows[:, None] * freqs, cols[:, None] * freqs], axis=-1)
    cos, sin = jnp.cos(ang), jnp.sin(ang)
    return jnp.concatenate([cos, cos], axis=-1), jnp.concatenate([-sin, sin], axis=-1)


def _retention_tables(log_decay):
    lg = -jnp.exp(log_decay.astype(F32))
    idx = jnp.arange(CHUNK, dtype=F32)
    rel = idx[:, None] - idx[None, :]
    lg3 = lg[:, :, None, None]
    intra_f = jnp.where(rel >= 0, jnp.exp(jnp.maximum(rel, 0.0) * lg3[0]), 0.0)
    intra_b = jnp.where(rel <= 0, jnp.exp(jnp.maximum(-rel, 0.0) * lg3[1]), 0.0)
    kd_f = jnp.exp((CHUNK - 1 - idx)[None, :] * lg[0][:, None])
    kd_b = jnp.exp(idx[None, :] * lg[1][:, None])
    qd_f = jnp.exp((idx + 1)[None, :] * lg[0][:, None])
    qd_b = jnp.exp((CHUNK - idx)[None, :] * lg[1][:, None])
    bc = lambda t: jnp.broadcast_to(t[:, :, None], (HEADS, CHUNK, LANES))
    intra = jnp.concatenate([intra_f, intra_b], axis=0)
    kd = jnp.concatenate([kd_f, kd_b], axis=0)
    qd = jnp.concatenate([bc(qd_f), bc(qd_b)], axis=0)
    cd = jnp.broadcast_to(jnp.exp(CHUNK * lg).reshape(2 * HEADS, 1), (2 * HEADS, LANES))
    return intra, kd, qd, cd


def kernel(x, c, ctx, c_ctx, w_mod, b_mod, norm_g, w_in, ret_log_decay, ret_norm_g, mlstm_conv_w,
           mlstm_conv_b, mlstm_gate_b, mlstm_norm_g, w_out, w_router, router_bias, w_gate, w_up, w_down,
           ws_gate, ws_up, ws_down):
    b, n, d = x.shape
    n_ctx = ctx.shape[1]
    depth = w_mod.shape[0]
    assert depth == 1, "only the single-layer configuration is implemented"
    assert d // 2 // HEADS == LANES
    n_exp = w_router.shape[2]
    t_all = b * n
    r_w = d // 2
    main_cols = 8 * r_w
    l = 0

    pad = (-(b + 1)) % 8
    cc = jnp.concatenate([c, c_ctx[None, :], jnp.zeros((pad, d), F32)], axis=0)
    mod3 = _modulation(cc, w_mod[l], b_mod[l]).reshape(b + 1 + pad, N_MOD, d)

    w_groups = w_in[l, :, :main_cols].astype(BF16).reshape(d, 8, r_w)
    w_main = w_groups[:, jnp.array([0, 1, 2, 4, 5, 6, 3, 7]), :].reshape(d, main_cols)
    wg = w_in[l, :, main_cols:].astype(BF16)
    wgt = wg.T
    gb = mlstm_gate_b[l].reshape(-1).astype(F32)
    tabs = _retention_tables(ret_log_decay[l])
    head_g = jnp.concatenate([ret_norm_g[l], mlstm_norm_g[l]]).reshape(1, d).astype(F32)
    wr = w_router[l].T.astype(F32)
    wr_hi = wr.astype(BF16)
    wr_lo = (wr - wr_hi.astype(F32)).astype(BF16)

    def inproj(seq, mod_row, ts):
        cos2, sin2 = _rope_tables(n) if mod_row is None else (
            jnp.ones((seq.shape[1], LANES), F32), jnp.zeros((seq.shape[1], LANES), F32))
        return _inproj(seq, mod3, mod_row, norm_g[l, 0:1], w_main, wgt, mlstm_conv_w[l],
                       mlstm_conv_b[l].reshape(1, -1), gb.reshape(16, 1), cos2, sin2, ts)

    nst = 2 * HEADS
    zero_states = (jnp.zeros((b, nst, LANES, LANES), F32), jnp.zeros((b, nst, LANES, 2 * LANES), F32),
                   jnp.zeros((b, nst, LANES), F32))
    p_c, kt_c, gr_c = inproj(ctx, b, min(n_ctx, 512))
    ctx_states = _scan(p_c, kt_c, gr_c, tabs, zero_states, with_output=False)

    ts = min(n, 512)
    p_l, kt_l, gr_l = inproj(x, None, ts)
    o_f, o_b, _, _, _ = _scan(p_l, kt_l, gr_l, tabs, tuple(ctx_states), with_output=True)
    ts_moe = MOE_TILE
    x1, h2, rank, wdense, tile_cnt = _post(
        x, o_f, o_b, p_l, mod3, norm_g[l], head_g, w_out[l].astype(BF16), wr_hi, wr_lo,
        router_bias[l].reshape(n_exp, 1).astype(F32), ts_moe)

    nt = t_all // ts_moe
    cnt = tile_cnt[:, 0, :].astype(I32)
    run_rows = (cnt + ROW_ALIGN - 1) // ROW_ALIGN * ROW_ALIGN
    seg_cap = (jnp.sum(run_rows, axis=0) + SLOT_ROWS + EXPERT_BLOCK - 1) // EXPERT_BLOCK * EXPERT_BLOCK
    seg_end = jnp.cumsum(seg_cap)
    run_start = (seg_end - seg_cap)[None, :] + jnp.cumsum(run_rows, axis=0) - run_rows
    ov_rows = jnp.maximum(run_rows - SLOT_ROWS, 0)
    ov_off = jnp.cumsum(ov_rows, axis=1) - ov_rows
    n_spill = (jnp.sum(ov_rows, axis=1, keepdims=True) + SPILL_CHUNK - 1) // SPILL_CHUNK
    spills = ov_rows > 0
    n_ov = jnp.sum(spills.astype(I32), axis=1, keepdims=True)
    nth = jnp.cumsum(spills.astype(I32), axis=1) - 1
    is_jth = spills[:, None, :] & (nth[:, None, :] == jnp.arange(n_exp, dtype=I32)[None, :, None])
    compact = lambda v: jnp.sum(jnp.where(is_jth, v[:, None, :], 0), axis=2)
    ov_e = compact(jnp.broadcast_to(jnp.arange(n_exp, dtype=I32)[None, :], cnt.shape))
    fits_half = (run_rows <= HALF_ROWS).astype(I32)
    meta = jnp.concatenate([run_start, ov_e, compact(ov_rows // ROW_ALIGN), compact(ov_off), n_ov, n_spill,
                            fits_half], axis=1).astype(I32)
    meta = jnp.pad(meta, ((0, 0), (0, (-meta.shape[1]) % LANES))).reshape(nt, 1, -1)
    p_rows = -(-(t_all * TOP_K + nt * n_exp * (ROW_ALIGN - 1) + n_exp * (SLOT_ROWS + EXPERT_BLOCK - 1))
               // EXPERT_BLOCK) * EXPERT_BLOCK
    nb = p_rows // EXPERT_BLOCK
    blk_first = jnp.arange(nb, dtype=I32) * EXPERT_BLOCK
    block_e = jnp.minimum(jnp.sum((seg_end[None, :] <= blk_first[:, None]).astype(I32), axis=1), n_exp - 1)
    n_used = (seg_end[-1:] // EXPERT_BLOCK).astype(I32)
    seg = jnp.concatenate([seg_end.astype(I32), n_used])
    seg = jnp.pad(seg, (0, (-seg.shape[0]) % LANES)).reshape(1, 1, -1)

    xs = _dispatch(h2, rank, meta, seg, p_rows, ts_moe, n_exp)
    ys = _experts(xs, block_e, n_used, w_gate[l].astype(BF16), w_up[l].astype(BF16), w_down[l].astype(BF16),
                  EXPERT_BLOCK)
    out = _combine(ys, meta, rank.T, wdense.T, rank, wdense, x1.reshape(t_all, d), h2, mod3, norm_g[l],
                   ws_gate[l].astype(BF16), ws_up[l].astype(BF16), ws_down[l].astype(BF16), n, ts_moe, n_exp)
    return out.reshape(b, n, d)
```

```python
import functools

import jax
import jax.numpy as jnp
from jax import lax
from jax.experimental import pallas as pl
from jax.experimental.pallas import tpu as pltpu

F32 = jnp.float32
BF16 = jnp.bfloat16
I32 = jnp.int32

EPS = 1e-6
LANES = 128
CHUNK = 128
HEADS = 4
GRID_W = 64
ROPE_BASE = 10000.0
N_GROUPS = 8
TOPK_GROUPS = 4
TOP_K = 8
ROUTED_SCALE = 2.5
N_MOD = 6
MOE_TILE = 256
SPILL_CHUNK = 256
ROW_ALIGN = 16
SLOT_ROWS = 64
HALF_ROWS = 32
SPILL_PIECE = 64
SLOT_GROUP = 8
EXPERT_BLOCK = 1024
NEG_INF = float("-inf")
P_SLOT = {0: 0, 2: 1, 3: 2, 5: 3, 6: 4, 7: 5}


def _sigmoid(v):
    return 1.0 / (1.0 + jnp.exp(-v))


def _silu(v):
    return v * _sigmoid(v)


def _log_sigmoid(v):
    return jnp.minimum(v, 0.0) - jnp.log(1.0 + jnp.exp(-jnp.abs(v)))


def _dot(a, b):
    return jnp.dot(a, b, preferred_element_type=F32)


def _dot_nt(a, b):
    return lax.dot_general(a, b, (((1,), (1,)), ((), ())), preferred_element_type=F32)


def _dot_tn(a, b):
    return lax.dot_general(a, b, (((0,), (0,)), ((), ())), preferred_element_type=F32)


def _split3(a):
    hi = a.astype(BF16)
    r = a - hi.astype(F32)
    mid = r.astype(BF16)
    lo = (r - mid.astype(F32)).astype(BF16)
    return hi, mid, lo


def _rms(v, g):
    ms = jnp.mean(v * v, axis=-1, keepdims=True)
    return v * lax.rsqrt(ms + EPS) * g


def _mod_kernel(c_ref, w_ref, b_ref, o_ref):
    a = _silu(c_ref[...])
    o_ref[...] = jnp.dot(a, w_ref[...], preferred_element_type=F32,
                         precision=lax.Precision.HIGHEST) + b_ref[...]


def _modulation(cc, w_mod, b_mod):
    rows, d = cc.shape
    cols = w_mod.shape[1]
    tn = d
    return pl.pallas_call(
        _mod_kernel,
        grid=(cols // tn,),
        in_specs=[pl.BlockSpec((rows, d), lambda j: (0, 0)),
                  pl.BlockSpec((d, tn), lambda j: (0, j)),
                  pl.BlockSpec((1, tn), lambda j: (0, j))],
        out_specs=pl.BlockSpec((rows, tn), lambda j: (0, j)),
        out_shape=jax.ShapeDtypeStruct((rows, cols), F32),
        name="mod",
    )(cc, w_mod, b_mod.reshape(1, cols))


def _inproj_kernel(x_ref, xp_ref, xn_ref, mod_ref, g_ref, w_ref, wgt_ref, cw_ref, cb_ref,
                   gbr_ref, cos_ref, sin_ref, p_ref, kt_ref, gr_ref, *, ts, d):
    i = pl.program_id(1)
    last = pl.num_programs(1) - 1
    r_w = d // 2
    shift = mod_ref[0, 0:1, :]
    scale = mod_ref[0, 1:2, :]
    g = g_ref[...]

    def normmod(v):
        return _rms(v, g) * (1.0 + scale) + shift

    hb = normmod(x_ref[0]).astype(BF16)
    halo = jnp.concatenate([xp_ref[0], xn_ref[0]], axis=0)
    ph = _dot(normmod(halo).astype(BF16), w_ref[:, 3 * r_w:5 * r_w])
    prev_row = jnp.where(i == 0, 0.0, ph[7:8, :])
    next_row = jnp.where(i == last, 0.0, ph[8:9, :])

    cos2 = cos_ref[...]
    sin2 = sin_ref[...]
    rows = lax.broadcasted_iota(I32, (ts, r_w), 0)
    qscale = LANES ** -0.5

    for j in range(8):
        acc = _dot(hb, w_ref[:, j * r_w:(j + 1) * r_w])
        if j in (0, 1):
            if j == 0:
                acc = acc * qscale
            parts = []
            for h in range(HEADS):
                t = acc[:, h * LANES:(h + 1) * LANES]
                parts.append(t * cos2 + pltpu.roll(t, LANES // 2, axis=1) * sin2)
            acc = jnp.concatenate(parts, axis=1)
        elif j in (3, 4):
            c0 = (j - 3) * r_w
            pr = prev_row[:, c0:c0 + r_w]
            nx = next_row[:, c0:c0 + r_w]
            down = jnp.where(rows == 0, pr, pltpu.roll(acc, 1, axis=0))
            up = jnp.where(rows == ts - 1, nx, pltpu.roll(acc, ts - 1, axis=0))
            cw = cw_ref[:, c0:c0 + r_w]
            acc = down * cw[0:1, :] + acc * cw[1:2, :] + up * cw[2:3, :] + cb_ref[:, c0:c0 + r_w]
            acc = _silu(acc)
            if j == 4:
                acc = acc * qscale
        if j in (1, 4):
            kt_ref[0, (j // 4) * r_w:(j // 4 + 1) * r_w, :] = acc.T.astype(BF16)
        else:
            slot = P_SLOT[j]
            p_ref[0, :, slot * r_w:(slot + 1) * r_w] = acc.astype(BF16)

    gr = _dot_nt(wgt_ref[...], hb) + gbr_ref[...]
    ch_r = lax.broadcasted_iota(I32, gr.shape, 0)
    gr_ref[0] = jnp.where((ch_r // HEADS) % 2 == 1, _log_sigmoid(gr), gr)


def _inproj(x, mod3, mod_row, g, w_main, wgt, conv_w, conv_b, gb_col, cos2, sin2, ts):
    b, n, d = x.shape
    nt = n // ts
    nb8 = n // 8
    hb = ts // 8
    cols = w_main.shape[1]
    p_cols = cols // 8 * 6
    if mod_row is None:
        mod_map = lambda bi, i: (bi, 0, 0)
    else:
        mod_map = lambda bi, i: (mod_row, 0, 0)
    const2 = lambda bi, i: (0, 0)
    kern = functools.partial(_inproj_kernel, ts=ts, d=d)
    return pl.pallas_call(
        kern,
        grid=(b, nt),
        in_specs=[
            pl.BlockSpec((1, ts, d), lambda bi, i: (bi, i, 0)),
            pl.BlockSpec((1, 8, d), lambda bi, i: (bi, jnp.maximum(i * hb - 1, 0), 0)),
            pl.BlockSpec((1, 8, d), lambda bi, i: (bi, jnp.minimum((i + 1) * hb, nb8 - 1), 0)),
            pl.BlockSpec((1, N_MOD, d), mod_map),
            pl.BlockSpec((1, d), const2),
            pl.BlockSpec((d, cols), const2),
            pl.BlockSpec((16, d), const2),
            pl.BlockSpec((3, d), const2),
            pl.BlockSpec((1, d), const2),
            pl.BlockSpec((16, 1), const2),
            pl.BlockSpec((ts, LANES), lambda bi, i: (i, 0)),
            pl.BlockSpec((ts, LANES), lambda bi, i: (i, 0)),
        ],
        out_specs=[
            pl.BlockSpec((1, ts, p_cols), lambda bi, i: (bi, i, 0)),
            pl.BlockSpec((1, d, ts), lambda bi, i: (bi, 0, i)),
            pl.BlockSpec((1, 16, ts), lambda bi, i: (bi, 0, i)),
        ],
        out_shape=[
            jax.ShapeDtypeStruct((b, n, p_cols), BF16),
            jax.ShapeDtypeStruct((b, d, n), BF16),
            jax.ShapeDtypeStruct((b, 16, n), F32),
        ],
        compiler_params=pltpu.CompilerParams(dimension_semantics=("parallel", "parallel")),
        name="inproj",
    )(x, x, x, mod3, g, w_main, wgt, conv_w, conv_b, gb_col, cos2, sin2)


def _scan_kernel(pf_ref, pb_ref, ktf_ref, ktb_ref, grf_ref, grb_ref, intra_ref, kd_ref, qd_ref, cd_ref,
                 rs0_ref, mc0_ref, mm0_ref, *out_refs, with_output, r_w):
    if with_output:
        of_ref, ob_ref, rs_ref, mc_ref, mm_ref = out_refs
    else:
        rs_ref, mc_ref, mm_ref = out_refs
    j = pl.program_id(1)

    @pl.when(j == 0)
    def _():
        rs_ref[...] = rs0_ref[...]
        mc_ref[...] = mc0_ref[...]
        mm_ref[...] = mm0_ref[...]

    c = CHUNK
    row = lax.broadcasted_iota(I32, (c, c), 0)
    col = lax.broadcasted_iota(I32, (c, c), 1)
    tri_le = (row <= col)
    tri_ge = (row >= col)
    eye = row == col
    ones_blk = jnp.ones((c, LANES), BF16)
    lane = lax.broadcasted_iota(I32, (HEADS, c), 1)

    def spread_rows(vecs):
        diag = jnp.concatenate([jnp.where(eye, v, 0.0) for v in vecs], axis=0)
        hi = diag.astype(BF16)
        lo = (diag - hi.astype(F32)).astype(BF16)
        out = _dot(hi, ones_blk) + _dot(lo, ones_blk)
        return [out[n * c:(n + 1) * c, :] for n in range(len(vecs))]

    def running_max(a, fwd):
        pm = a
        s = 1
        while s < c:
            if fwd:
                pm = jnp.where(lane >= s, jnp.maximum(pm, pltpu.roll(pm, s, axis=1)), pm)
            else:
                pm = jnp.where(lane < c - s, jnp.maximum(pm, pltpu.roll(pm, c - s, axis=1)), pm)
            s *= 2
        return pm

    n_st = 2 * HEADS
    rs_prev = [rs_ref[0, st] for st in range(n_st)]
    mc_prev = [mc_ref[0, st] for st in range(n_st)]
    mm_prev = [mm_ref[0, st:st + 1, 0:1] for st in range(n_st)]
    heads = [(dr, h) for dr in range(2) for h in range(HEADS)]
    o0 = 2 * r_w

    def cols(ref, base, h):
        return ref[0, :, base + h * LANES:base + (h + 1) * LANES]

    cs_rows = []
    for dr in range(2):
        gr = (grf_ref, grb_ref)[dr][0]
        tri = jnp.where(tri_le if dr == 0 else tri_ge, 1.0, 0.0).astype(BF16)
        cs_rows.append((gr, sum(_dot(piece, tri) for piece in _split3(gr))))
    ret_upd, ret_sc, ret_in, ml_sc, ml_in = [], [], [], [], []
    for dr, h in heads:
        st = dr * HEADS + h
        p_ref, kt_ref = (pf_ref, pb_ref)[dr], (ktf_ref, ktb_ref)[dr]
        kt = kt_ref[0, h * LANES:(h + 1) * LANES, :]
        ks = (kt.astype(F32) * kd_ref[st:st + 1, :]).astype(BF16)
        ret_upd.append(_dot(ks, cols(p_ref, r_w, h)))
        if with_output:
            q, mq = cols(p_ref, 0, h), cols(p_ref, o0, h)
            ret_sc.append(_dot(q, kt))
            ret_in.append(_dot(q, rs_prev[st].astype(BF16)))
            ml_sc.append(_dot(mq, kt_ref[0, r_w + h * LANES:r_w + (h + 1) * LANES, :]))
            ml_in.append(_dot(mq, mc_prev[st].astype(BF16)))

    gate = []
    for dr in range(2):
        gr, cs_row = cs_rows[dr]
        g0 = dr * 2 * HEADS
        a_rows = gr[g0:g0 + HEADS, :] - cs_row[g0 + HEADS:g0 + 2 * HEADS, :]
        pm_rows = running_max(a_rows, dr == 0)
        last = c - 1 if dr == 0 else 0
        for h in range(HEADS):
            st = dr * HEADS + h
            b_row = cs_row[g0 + HEADS + h:g0 + HEADS + h + 1, :]
            a_row, pm_row = a_rows[h:h + 1, :], pm_rows[h:h + 1, :]
            b_tot = b_row[:, last:last + 1]
            m_prev = mm_prev[st]
            m_next = b_tot + jnp.maximum(m_prev, pm_row[:, last:last + 1])
            gate.append(dict(b_row=b_row, a_row=a_row, pm_row=pm_row, m_prev=m_prev, m_next=m_next,
                             decay_prev=jnp.exp(b_tot + m_prev - m_next),
                             ws_row=jnp.exp(b_tot + a_row - m_next)))
    ml_upd, spread = [], []
    for dr, h in heads:
        st = dr * HEADS + h
        p_ref, kt_ref = (pf_ref, pb_ref)[dr], (ktf_ref, ktb_ref)[dr]
        mkt = kt_ref[0, r_w + h * LANES:r_w + (h + 1) * LANES, :]
        kw = (mkt.astype(F32) * gate[st]["ws_row"]).astype(BF16)
        v_ext = jnp.concatenate([cols(p_ref, o0 + r_w, h), ones_blk], axis=1)
        ml_upd.append(_dot(kw, v_ext))
        if with_output:
            spread.append(spread_rows([gate[st]["b_row"], gate[st]["pm_row"]]))

    if with_output:
        ret_out, ml_out, stab = [], [], []
        for dr, h in heads:
            st = dr * HEADS + h
            p_ref = (pf_ref, pb_ref)[dr]
            sc = (ret_sc[st] * intra_ref[st]).astype(BF16)
            ret_out.append(_dot(sc, cols(p_ref, r_w, h)))
            b_sp, pm_sp = spread[st]
            c_sp = jnp.maximum(gate[st]["m_prev"], pm_sp)
            causal = tri_ge if dr == 0 else tri_le
            w = jnp.where(causal, jnp.exp(gate[st]["a_row"] - c_sp), 0.0)
            v_ext = jnp.concatenate([cols(p_ref, o0 + r_w, h), ones_blk], axis=1)
            ml_out.append(_dot((ml_sc[st] * w).astype(BF16), v_ext))
            stab.append((jnp.exp(gate[st]["m_prev"] - c_sp), jnp.exp(-(b_sp + c_sp))))

    for dr, h in heads:
        st = dr * HEADS + h
        if with_output:
            o_ref = (of_ref, ob_ref)[dr]
            o_ref[0, :, h * LANES:(h + 1) * LANES] = (ret_out[st] + qd_ref[st] * ret_in[st]).astype(BF16)
            inter, floor = stab[st]
            hx = ml_out[st] + jnp.concatenate([inter, inter], axis=1) * ml_in[st]
            hout = hx[:, :LANES] / jnp.maximum(jnp.abs(hx[:, LANES:]), floor)
            o_ref[0, :, r_w + h * LANES:r_w + (h + 1) * LANES] = hout.astype(BF16)
    for dr, h in heads:
        st = dr * HEADS + h
        rs_ref[0, st] = rs_prev[st] * cd_ref[st:st + 1, :] + ret_upd[st]
        mc_ref[0, st] = gate[st]["decay_prev"] * mc_prev[st] + ml_upd[st]
        mm_ref[0, st:st + 1, :] = jnp.broadcast_to(gate[st]["m_next"], (1, LANES))


def _scan(p, kt, g_row, tabs, states, with_output):
    b, n, cols = p.shape
    nch = n // CHUNK
    r_w = cols // 6
    intra, kd, qd, cd = tabs
    rs0, mc0, mm0 = states
    nst = 2 * HEADS
    fwd3 = lambda bi, j: (bi, j, 0)
    bwd3 = lambda bi, j: (bi, nch - 1 - j, 0)
    c3 = lambda bi, j: (0, 0, 0)
    st4 = lambda bi, j: (bi, 0, 0, 0)
    in_specs = [
        pl.BlockSpec((1, CHUNK, 4 * r_w), fwd3),
        pl.BlockSpec((1, CHUNK, 4 * r_w), bwd3),
        pl.BlockSpec((1, 2 * r_w, CHUNK), lambda bi, j: (bi, 0, j)),
        pl.BlockSpec((1, 2 * r_w, CHUNK), lambda bi, j: (bi, 0, nch - 1 - j)),
        pl.BlockSpec((1, 16, CHUNK), lambda bi, j: (bi, 0, j)),
        pl.BlockSpec((1, 16, CHUNK), lambda bi, j: (bi, 0, nch - 1 - j)),
        pl.BlockSpec((nst, CHUNK, LANES), c3),
        pl.BlockSpec((nst, CHUNK), lambda bi, j: (0, 0)),
        pl.BlockSpec((nst, CHUNK, LANES), c3),
        pl.BlockSpec((nst, LANES), lambda bi, j: (0, 0)),
        pl.BlockSpec((1, nst, LANES, LANES), st4),
        pl.BlockSpec((1, nst, LANES, 2 * LANES), st4),
        pl.BlockSpec((1, nst, LANES), lambda bi, j: (bi, 0, 0)),
    ]
    st_specs = [
        pl.BlockSpec((1, nst, LANES, LANES), st4),
        pl.BlockSpec((1, nst, LANES, 2 * LANES), st4),
        pl.BlockSpec((1, nst, LANES), lambda bi, j: (bi, 0, 0)),
    ]
    st_shapes = [
        jax.ShapeDtypeStruct((b, nst, LANES, LANES), F32),
        jax.ShapeDtypeStruct((b, nst, LANES, 2 * LANES), F32),
        jax.ShapeDtypeStruct((b, nst, LANES), F32),
    ]
    if with_output:
        out_specs = [pl.BlockSpec((1, CHUNK, 2 * r_w), fwd3), pl.BlockSpec((1, CHUNK, 2 * r_w), bwd3)] + st_specs
        out_shape = [jax.ShapeDtypeStruct((b, n, 2 * r_w), BF16)] * 2 + st_shapes
    else:
        out_specs, out_shape = st_specs, st_shapes
    kern = functools.partial(_scan_kernel, with_output=with_output, r_w=r_w)
    return pl.pallas_call(
        kern,
        grid=(b, nch),
        in_specs=in_specs,
        out_specs=out_specs,
        out_shape=out_shape,
        compiler_params=pltpu.CompilerParams(dimension_semantics=("parallel", "arbitrary")),
        name="scan_out" if with_output else "scan_state",
    )(p, p, kt, kt, g_row, g_row, intra, kd, qd, cd, rs0, mc0, mm0)


def _post_kernel(x_ref, of_ref, ob_ref, rg_ref, mo_ref, mod_ref, ng_ref, hg_ref, wo_ref, wrh_ref, wrl_ref,
                 rb_ref, su_ref, x1_ref, h2_ref, rk_ref, wd_ref, cnt_ref, *, ts, d, n_exp):
    s = of_ref[0].astype(F32) + ob_ref[0].astype(F32)
    parts = []
    for gi in range(2 * HEADS):
        sl = s[:, gi * LANES:(gi + 1) * LANES]
        mu = jnp.mean(sl, axis=-1, keepdims=True)
        dv = sl - mu
        var = jnp.mean(dv * dv, axis=-1, keepdims=True)
        y = dv * lax.rsqrt(var + EPS) * hg_ref[:, gi * LANES:(gi + 1) * LANES]
        if gi < HEADS:
            gate = _silu(rg_ref[0, :, gi * LANES:(gi + 1) * LANES].astype(F32))
        else:
            gate = _sigmoid(mo_ref[0, :, (gi - HEADS) * LANES:(gi - HEADS + 1) * LANES].astype(F32))
        parts.append((y * gate).astype(BF16))
    mixed = jnp.concatenate(parts, axis=1)
    y = _dot(mixed, wo_ref[...])
    g1 = mod_ref[0, 2:3, :]
    sh2 = mod_ref[0, 3:4, :]
    sc2 = mod_ref[0, 4:5, :]
    x1 = x_ref[0] + g1 * _rms(y, ng_ref[1:2, :])
    x1_ref[0] = x1
    h2 = _rms(x1, ng_ref[2:3, :]) * (1.0 + sc2) + sh2
    h_hi = h2.astype(BF16)
    h2_ref[...] = h_hi

    h_lo = (h2 - h_hi.astype(F32)).astype(BF16)
    logits = _dot_nt(wrh_ref[...], h_hi) + _dot_nt(wrh_ref[...], h_lo) + _dot_nt(wrl_ref[...], h_hi)
    scores = _sigmoid(logits)
    sel = scores + rb_ref[...]
    gsz = n_exp // N_GROUPS
    iota_g = lax.broadcasted_iota(I32, (gsz, ts), 0).astype(F32)
    grp = []
    for gi in range(N_GROUPS):
        blk = sel[gi * gsz:(gi + 1) * gsz, :]
        m1 = jnp.max(blk, axis=0, keepdims=True)
        i1 = jnp.min(jnp.where(blk == m1, iota_g, float(gsz)), axis=0, keepdims=True)
        m2 = jnp.max(jnp.where(iota_g == i1, NEG_INF, blk), axis=0, keepdims=True)
        grp.append(m1 + m2)
    masked_parts = []
    for gi in range(N_GROUPS):
        rank = jnp.zeros((1, ts), F32)
        for gj in range(N_GROUPS):
            if gj == gi:
                continue
            beats = (grp[gj] >= grp[gi]) if gj < gi else (grp[gj] > grp[gi])
            rank = rank + jnp.where(beats, 1.0, 0.0)
        keep = rank < float(TOPK_GROUPS)
        masked_parts.append(jnp.where(keep, sel[gi * gsz:(gi + 1) * gsz, :], NEG_INF))
    masked = jnp.concatenate(masked_parts, axis=0)

    iota_e = lax.broadcasted_iota(I32, (n_exp, ts), 0).astype(F32)
    selmask = jnp.zeros((n_exp, ts), F32)
    for _ in range(TOP_K):
        mx = jnp.max(masked, axis=0, keepdims=True)
        ei = jnp.min(jnp.where(masked == mx, iota_e, float(n_exp)), axis=0, keepdims=True)
        hit = iota_e == ei
        selmask = jnp.where(hit, 1.0, selmask)
        masked = jnp.where(hit, NEG_INF, masked)
    picked = selmask > 0.0
    wsel = jnp.where(picked, scores, 0.0)
    wd_ref[...] = wsel / jnp.sum(wsel, axis=0, keepdims=True) * ROUTED_SCALE
    rank = _dot(selmask.astype(BF16), su_ref[...])
    rk_ref[...] = jnp.where(picked, rank, -1.0)
    cnt_ref[0] = _dot_nt(jnp.ones((8, ts), BF16), selmask.astype(BF16))


def _post(x, o_f, o_b, p, mod3, norm_g, head_g, w_out, wr_hi, wr_lo, rbias, ts):
    b, n, d = x.shape
    nt = n // ts
    t_all = b * n
    n_exp = wr_hi.shape[0]
    r_w = d // 2
    su = jnp.where(lax.broadcasted_iota(I32, (ts, ts), 0) < lax.broadcasted_iota(I32, (ts, ts), 1),
                   1.0, 0.0).astype(BF16)
    tok3 = lambda bi, i: (bi, i, 0)
    c2 = lambda bi, i: (0, 0)
    flat = lambda bi, i: (0, bi * nt + i)
    kern = functools.partial(_post_kernel, ts=ts, d=d, n_exp=n_exp)
    return pl.pallas_call(
        kern,
        grid=(b, nt),
        in_specs=[
            pl.BlockSpec((1, ts, d), tok3),
            pl.BlockSpec((1, ts, d), tok3),
            pl.BlockSpec((1, ts, d), tok3),
            pl.BlockSpec((1, ts, r_w), lambda bi, i: (bi, i, 4)),
            pl.BlockSpec((1, ts, r_w), lambda bi, i: (bi, i, 5)),
            pl.BlockSpec((1, N_MOD, d), lambda bi, i: (bi, 0, 0)),
            pl.BlockSpec((4, d), c2),
            pl.BlockSpec((1, d), c2),
            pl.BlockSpec((d, d), c2),
            pl.BlockSpec((n_exp, d), c2),
            pl.BlockSpec((n_exp, d), c2),
            pl.BlockSpec((n_exp, 1), c2),
            pl.BlockSpec((ts, ts), c2),
        ],
        out_specs=[
            pl.BlockSpec((1, ts, d), tok3),
            pl.BlockSpec((ts, d), lambda bi, i: (bi * nt + i, 0)),
            pl.BlockSpec((n_exp, ts), flat),
            pl.BlockSpec((n_exp, ts), flat),
            pl.BlockSpec((1, 8, n_exp), lambda bi, i: (bi * nt + i, 0, 0)),
        ],
        out_shape=[
            jax.ShapeDtypeStruct((b, n, d), F32),
            jax.ShapeDtypeStruct((t_all, d), BF16),
            jax.ShapeDtypeStruct((n_exp, t_all), F32),
            jax.ShapeDtypeStruct((n_exp, t_all), F32),
            jax.ShapeDtypeStruct((b * nt, 8, n_exp), F32),
        ],
        compiler_params=pltpu.CompilerParams(dimension_semantics=("parallel", "parallel")),
        name="post",
    )(x, o_f, o_b, p, p, mod3, norm_g, head_g, w_out, wr_hi, wr_lo, rbias, su)


def _slot_transfers(meta_ref, stage, buf, hbm, sems, n_exp, to_hbm, wait):
    for e in range(n_exp):
        start = pl.multiple_of(meta_ref[0, 0, e], ROW_ALIGN)
        fits_half = meta_ref[0, 0, 4 * n_exp + 2 + e] != 0
        for rows, cond in ((HALF_ROWS, fits_half), (SLOT_ROWS, jnp.logical_not(fits_half))):
            @pl.when(cond)
            def _():
                run = hbm.at[pl.ds(start, rows)]
                slot = stage.at[buf, pl.ds(e * SLOT_ROWS, rows)]
                cp = (pltpu.make_async_copy(slot, run, sems.at[buf]) if to_hbm
                      else pltpu.make_async_copy(run, slot, sems.at[buf]))
                if wait:
                    cp.wait()
                else:
                    cp.start(priority=e % 2)


def _overflow_copy(meta_ref, spill, hbm, sem, n_exp, j, i, to_hbm, piece_rows=ROW_ALIGN):
    e = meta_ref[0, 0, n_exp + j]
    src = meta_ref[0, 0, 3 * n_exp + j] + piece_rows * i
    dst = meta_ref[0, 0, e] + SLOT_ROWS + piece_rows * i
    piece = spill.at[pl.ds(pl.multiple_of(src, ROW_ALIGN), piece_rows)]
    rows = hbm.at[pl.ds(pl.multiple_of(dst, ROW_ALIGN), piece_rows)]
    return pltpu.make_async_copy(piece, rows, sem) if to_hbm else pltpu.make_async_copy(rows, piece, sem)


def _for_overflow_pieces(meta_ref, n_exp, fn, piece_rows=ROW_ALIGN):
    per = piece_rows // ROW_ALIGN

    def per_expert(j, carry):
        def per_piece(i, c2):
            fn(j, i)
            return c2
        return lax.fori_loop(0, (meta_ref[0, 0, 2 * n_exp + j] + per - 1) // per, per_piece, carry)
    lax.fori_loop(0, meta_ref[0, 0, 4 * n_exp], per_expert, 0)


def _spill_matrix_rows(meta_ref, rk_ref, wd_ref, base, ts, n_exp):
    rows = (lax.broadcasted_iota(I32, (SPILL_CHUNK, ts), 0) + base).astype(F32)

    def per_expert(j, hit):
        e = meta_ref[0, 0, n_exp + j]
        rk = rk_ref[pl.ds(e, 1), :]
        val = 1.0 if wd_ref is None else wd_ref[pl.ds(e, 1), :]
        target = jnp.where(rk >= SLOT_ROWS, rk - SLOT_ROWS + meta_ref[0, 0, 3 * n_exp + j].astype(F32), -1.0)
        return jnp.where(target == rows, val, hit)

    return lax.fori_loop(0, meta_ref[0, 0, 4 * n_exp], per_expert,
                         jnp.zeros((SPILL_CHUNK, ts), F32)).astype(BF16)


def _zero_fill(seg_ref, xs_hbm, stage, sem, n_exp, n_blocks):
    tail = SLOT_ROWS + EXPERT_BLOCK
    stage[0, 0:tail, :] = jnp.zeros((tail, stage.shape[2]), stage.dtype)
    tails = []
    for e in range(n_exp):
        start = jnp.maximum(seg_ref[0, 0, e] - tail, 0)
        tails.append(pltpu.make_async_copy(stage.at[0, pl.ds(0, tail)],
                                           xs_hbm.at[pl.ds(pl.multiple_of(start, ROW_ALIGN), tail)], sem))
    for cp in tails:
        cp.start()
    n_used = seg_ref[0, 0, n_exp]

    def block_copy(i):
        row = pl.multiple_of(i * EXPERT_BLOCK, EXPERT_BLOCK)
        return pltpu.make_async_copy(stage.at[0, pl.ds(0, EXPERT_BLOCK)], xs_hbm.at[pl.ds(row, EXPERT_BLOCK)], sem)

    def start_block(i, carry):
        block_copy(i).start()
        return carry

    def wait_block(i, carry):
        block_copy(i).wait()
        return carry

    lax.fori_loop(n_used, n_blocks, start_block, 0)
    for cp in tails:
        cp.wait()
    lax.fori_loop(n_used, n_blocks, wait_block, 0)


def _dispatch_kernel(meta_ref, prv_ref, seg_ref, x_ref, rk_ref, xs_hbm, stage, spill, sems, sem_ov, *,
                     ts, n_exp, n_blocks):
    i = pl.program_id(0)
    buf = i % 2

    @pl.when(i == 0)
    def _():
        _zero_fill(seg_ref, xs_hbm, stage, sem_ov, n_exp, n_blocks)
        spill[...] = jnp.zeros(spill.shape, spill.dtype)

    x = x_ref[...]
    slot_row = lax.broadcasted_iota(I32, (SLOT_ROWS, ts), 0).astype(F32)
    group_rows = SLOT_GROUP * SLOT_ROWS
    for g in range(n_exp // SLOT_GROUP):
        pick = jnp.concatenate(
            [jnp.where(rk_ref[e:e + 1, :] == slot_row, 1.0, 0.0)
             for e in range(g * SLOT_GROUP, (g + 1) * SLOT_GROUP)], axis=0).astype(BF16)
        stage[buf, g * group_rows:(g + 1) * group_rows, :] = _dot(pick, x).astype(BF16)
    n_spill = meta_ref[0, 0, 4 * n_exp + 1]

    @pl.when(i > 0)
    def _():
        _for_overflow_pieces(prv_ref, n_exp,
                             lambda e, k: _overflow_copy(prv_ref, spill, xs_hbm, sem_ov, n_exp, e, k, True,
                                                         SPILL_PIECE).wait(), SPILL_PIECE)

    def spill_chunk(ci, carry):
        base = pl.multiple_of(ci * SPILL_CHUNK, SPILL_CHUNK)
        spill[pl.ds(base, SPILL_CHUNK), :] = _dot(
            _spill_matrix_rows(meta_ref, rk_ref, None, base, ts, n_exp), x).astype(BF16)
        return carry

    lax.fori_loop(0, n_spill, spill_chunk, 0)

    @pl.when(i > 0)
    def _():
        _slot_transfers(prv_ref, stage, 1 - buf, xs_hbm, sems, n_exp, True, wait=True)

    _slot_transfers(meta_ref, stage, buf, xs_hbm, sems, n_exp, True, wait=False)
    _for_overflow_pieces(meta_ref, n_exp,
                         lambda e, k: _overflow_copy(meta_ref, spill, xs_hbm, sem_ov, n_exp, e, k, True,
                                                     SPILL_PIECE).start(), SPILL_PIECE)

    @pl.when(i == pl.num_programs(0) - 1)
    def _():
        _slot_transfers(meta_ref, stage, buf, xs_hbm, sems, n_exp, True, wait=True)
        _for_overflow_pieces(meta_ref, n_exp,
                             lambda e, k: _overflow_copy(meta_ref, spill, xs_hbm, sem_ov, n_exp, e, k, True,
                                                         SPILL_PIECE).wait(), SPILL_PIECE)


def _dispatch(h2, rank, meta, seg, p_rows, ts, n_exp):
    t_all, d = h2.shape
    nt = t_all // ts
    kern = functools.partial(_dispatch_kernel, ts=ts, n_exp=n_exp, n_blocks=p_rows // EXPERT_BLOCK)
    return pl.pallas_call(
        kern,
        grid=(nt,),
        in_specs=[
            pl.BlockSpec((1, 1, meta.shape[2]), lambda i: (i, 0, 0), memory_space=pltpu.SMEM),
            pl.BlockSpec((1, 1, meta.shape[2]), lambda i: (jnp.maximum(i - 1, 0), 0, 0), memory_space=pltpu.SMEM),
            pl.BlockSpec((1, 1, seg.shape[2]), lambda i: (0, 0, 0), memory_space=pltpu.SMEM),
            pl.BlockSpec((ts, d), lambda i: (i, 0)),
            pl.BlockSpec((n_exp, ts), lambda i: (0, i)),
        ],
        out_specs=pl.BlockSpec(memory_space=pl.ANY),
        out_shape=jax.ShapeDtypeStruct((p_rows, d), BF16),
        scratch_shapes=[pltpu.VMEM((2, n_exp * SLOT_ROWS, d), BF16),
                        pltpu.VMEM((ts * TOP_K + SPILL_PIECE, d), BF16),
                        pltpu.SemaphoreType.DMA((2,)), pltpu.SemaphoreType.DMA(())],
        compiler_params=pltpu.CompilerParams(dimension_semantics=("arbitrary",), has_side_effects=True),
        name="dispatch",
    )(meta, meta, seg, h2, rank)


def _expert_kernel(be_ref, nu_ref, xs_ref, wg_ref, wu_ref, wd_ref, ys_ref):
    del be_ref
    i = pl.program_id(0)

    @pl.when(i < nu_ref[0])
    def _():
        xb = xs_ref[...]
        a = _silu(_dot(xb, wg_ref[0])) * _dot(xb, wu_ref[0])
        ys_ref[...] = _dot(a.astype(BF16), wd_ref[0]).astype(BF16)


def _experts(xs, block_e, n_used, w_gate, w_up, w_down, blk):
    p_rows, dw = xs.shape
    n_exp, d, ff = w_gate.shape
    nb = p_rows // blk
    used = lambda i, nu: jnp.minimum(i, nu[0] - 1)
    grid_spec = pltpu.PrefetchScalarGridSpec(
        num_scalar_prefetch=2,
        grid=(nb,),
        in_specs=[
            pl.BlockSpec((blk, dw), lambda i, be, nu: (used(i, nu), 0)),
            pl.BlockSpec((1, d, ff), lambda i, be, nu: (be[used(i, nu)], 0, 0)),
            pl.BlockSpec((1, d, ff), lambda i, be, nu: (be[used(i, nu)], 0, 0)),
            pl.BlockSpec((1, ff, d), lambda i, be, nu: (be[used(i, nu)], 0, 0)),
        ],
        out_specs=pl.BlockSpec((blk, dw), lambda i, be, nu: (used(i, nu), 0)),
    )
    return pl.pallas_call(
        _expert_kernel,
        grid_spec=grid_spec,
        out_shape=jax.ShapeDtypeStruct((p_rows, dw), BF16),
        input_output_aliases={2: 0},
        compiler_params=pltpu.CompilerParams(dimension_semantics=("arbitrary",)),
        name="experts",
    )(block_e, n_used, xs, w_gate, w_up, w_down)


def _combine_kernel(meta_ref, nxt_ref, ys_hbm, rkt_ref, wdt_ref, rk_ref, wd_ref, ex_ref, rp_ref, x1_ref, h2_ref,
                    mod_ref, ng_ref, sg_ref, su_ref, sd_ref, o_ref, stage, spill, acc, sems, sem_ov, *,
                    ts, n_exp):
    i = pl.program_id(0)
    buf = i % 2
    n_spill = meta_ref[0, 0, 4 * n_exp + 1]
    n_groups = n_exp // SLOT_GROUP

    @pl.when(i == 0)
    def _():
        stage[...] = jnp.zeros(stage.shape, stage.dtype)
        _slot_transfers(meta_ref, stage, 0, ys_hbm, sems, n_exp, False, wait=False)

    @pl.when(i < pl.num_programs(0) - 1)
    def _():
        _slot_transfers(nxt_ref, stage, 1 - buf, ys_hbm, sems, n_exp, False, wait=False)

    def clear_chunk(ci, carry):
        base = pl.multiple_of(ci * SPILL_CHUNK, SPILL_CHUNK)
        spill[pl.ds(base, SPILL_CHUNK), :] = jnp.zeros((SPILL_CHUNK, spill.shape[1]), spill.dtype)
        return carry

    lax.fori_loop(0, n_spill, clear_chunk, 0)
    _for_overflow_pieces(meta_ref, n_exp,
                         lambda e, k: _overflow_copy(meta_ref, spill, ys_hbm, sem_ov, n_exp, e, k, False).start())

    xb = h2_ref[...]
    a = _silu(_dot(xb, sg_ref[...])) * _dot(xb, su_ref[...])
    tot = _dot(a.astype(BF16), sd_ref[...])

    rank_lanes = _dot(rkt_ref[...].astype(BF16), ex_ref[...])
    weight_lanes = _dot(wdt_ref[...].astype(BF16), ex_ref[...])

    _slot_transfers(meta_ref, stage, buf, ys_hbm, sems, n_exp, False, wait=True)
    group_rows = SLOT_GROUP * SLOT_ROWS
    for g in range(n_groups):
        cols = slice(g * group_rows, (g + 1) * group_rows)
        unmix = jnp.where(rank_lanes[:, cols] == rp_ref[:, cols], weight_lanes[:, cols], 0.0).astype(BF16)
        tot = tot + _dot(unmix, stage[buf, g * group_rows:(g + 1) * group_rows, :])
    acc[...] = tot

    _for_overflow_pieces(meta_ref, n_exp,
                         lambda e, k: _overflow_copy(meta_ref, spill, ys_hbm, sem_ov, n_exp, e, k, False).wait())

    def spill_chunk(ci, carry):
        base = pl.multiple_of(ci * SPILL_CHUNK, SPILL_CHUNK)
        acc[...] += _dot_tn(_spill_matrix_rows(meta_ref, rk_ref, wd_ref, base, ts, n_exp),
                            spill[pl.ds(base, SPILL_CHUNK), :])
        return carry

    lax.fori_loop(0, n_spill, spill_chunk, 0)
    g2 = mod_ref[0, 5:6, :]
    o_ref[...] = x1_ref[...] + g2 * _rms(acc[...], ng_ref[3:4, :])


def _combine(ys, meta, rank_tm, wd_tm, rank, wd, x1_flat, h2, mod3, norm_g, ws_gate, ws_up, ws_down,
             n_seq, ts, n_exp):
    t_all, d = x1_flat.shape
    nt = t_all // ts
    per_b = n_seq // ts
    ff = ws_gate.shape[1]
    lanes = n_exp * SLOT_ROWS
    lane = lax.broadcasted_iota(I32, (n_exp, lanes), 1)
    expand = jnp.where(lane // SLOT_ROWS == lax.broadcasted_iota(I32, (n_exp, lanes), 0), 1.0, 0.0).astype(BF16)
    slot_rank = (jnp.arange(lanes, dtype=I32) % SLOT_ROWS).astype(F32).reshape(1, lanes)
    c2 = lambda i: (0, 0)
    kern = functools.partial(_combine_kernel, ts=ts, n_exp=n_exp)
    return pl.pallas_call(
        kern,
        grid=(nt,),
        in_specs=[
            pl.BlockSpec((1, 1, meta.shape[2]), lambda i: (i, 0, 0), memory_space=pltpu.SMEM),
            pl.BlockSpec((1, 1, meta.shape[2]), lambda i: (jnp.minimum(i + 1, nt - 1), 0, 0),
                         memory_space=pltpu.SMEM),
            pl.BlockSpec(memory_space=pl.ANY),
            pl.BlockSpec((ts, n_exp), lambda i: (i, 0)),
            pl.BlockSpec((ts, n_exp), lambda i: (i, 0)),
            pl.BlockSpec((n_exp, ts), lambda i: (0, i)),
            pl.BlockSpec((n_exp, ts), lambda i: (0, i)),
            pl.BlockSpec((n_exp, lanes), c2),
            pl.BlockSpec((1, lanes), c2),
            pl.BlockSpec((ts, d), lambda i: (i, 0)),
            pl.BlockSpec((ts, d), lambda i: (i, 0)),
            pl.BlockSpec((1, N_MOD, d), lambda i: (i // per_b, 0, 0)),
            pl.BlockSpec((4, d), c2),
            pl.BlockSpec((d, ff), c2),
            pl.BlockSpec((d, ff), c2),
            pl.BlockSpec((ff, d), c2),
        ],
        out_specs=pl.BlockSpec((ts, d), lambda i: (i, 0)),
        out_shape=jax.ShapeDtypeStruct((t_all, d), F32),
        scratch_shapes=[pltpu.VMEM((2, lanes, d), BF16), pltpu.VMEM((ts * TOP_K, d), BF16),
                        pltpu.VMEM((ts, d), F32),
                        pltpu.SemaphoreType.DMA((2,)), pltpu.SemaphoreType.DMA(())],
        compiler_params=pltpu.CompilerParams(dimension_semantics=("arbitrary",)),
        name="combine",
    )(meta, meta, ys, rank_tm, wd_tm, rank, wd, expand, slot_rank, x1_flat, h2, mod3, norm_g,
      ws_gate, ws_up, ws_down)


def _rope_tables(n):
    rows = jnp.repeat(jnp.arange(n // GRID_W, dtype=F32), GRID_W)
    cols = jnp.tile(jnp.arange(GRID_W, dtype=F32), n // GRID_W)
    quarter = LANES // 4
    freqs = ROPE_BASE ** (-jnp.arange(quarter, dtype=F32) / quarter)
    ang = jnp.concatenate([rows[:, None] * freqs, cols[:, None] * freqs], axis=-1)
    cos, sin = jnp.cos(ang), jnp.sin(ang)
    return jnp.concatenate([cos, cos], axis=-1), jnp.concatenate([-sin, sin], axis=-1)


def _retention_tables(log_decay):
    lg = -jnp.exp(log_decay.astype(F32))
    idx = jnp.arange(CHUNK, dtype=F32)
    rel = idx[:, None] - idx[None, :]
    lg3 = lg[:, :, None, None]
    intra_f = jnp.where(rel >= 0, jnp.exp(jnp.maximum(rel, 0.0) * lg3[0]), 0.0)
    intra_b = jnp.where(rel <= 0, jnp.exp(jnp.maximum(-rel, 0.0) * lg3[1]), 0.0)
    kd_f = jnp.exp((CHUNK - 1 - idx)[None, :] * lg[0][:, None])
    kd_b = jnp.exp(idx[None, :] * lg[1][:, None])
    qd_f = jnp.exp((idx + 1)[None, :] * lg[0][:, None])
    qd_b = jnp.exp((CHUNK - idx)[None, :] * lg[1][:, None])
    bc = lambda t: jnp.broadcast_to(t[:, :, None], (HEADS, CHUNK, LANES))
    intra = jnp.concatenate([intra_f, intra_b], axis=0)
    kd = jnp.concatenate([kd_f, kd_b], axis=0)
    qd = jnp.concatenate([bc(qd_f), bc(qd_b)], axis=0)
    cd = jnp.broadcast_to(jnp.exp(CHUNK * lg).reshape(2 * HEADS, 1), (2 * HEADS, LANES))
    return intra, kd, qd, cd


def kernel(x, c, ctx, c_ctx, w_mod, b_mod, norm_g, w_in, ret_log_decay, ret_norm_g, mlstm_conv_w,
           mlstm_conv_b, mlstm_gate_b, mlstm_norm_g, w_out, w_router, router_bias, w_gate, w_up, w_down,
           ws_gate, ws_up, ws_down):
    b, n, d = x.shape
    n_ctx = ctx.shape[1]
    depth = w_mod.shape[0]
    assert depth == 1, "only the single-layer configuration is implemented"
    assert d // 2 // HEADS == LANES
    n_exp = w_router.shape[2]
    t_all = b * n
    r_w = d // 2
    main_cols = 8 * r_w
    l = 0

    pad = (-(b + 1)) % 8
    cc = jnp.concatenate([c, c_ctx[None, :], jnp.zeros((pad, d), F32)], axis=0)
    mod3 = _modulation(cc, w_mod[l], b_mod[l]).reshape(b + 1 + pad, N_MOD, d)

    w_groups = w_in[l, :, :main_cols].astype(BF16).reshape(d, 8, r_w)
    w_main = w_groups[:, jnp.array([0, 1, 2, 4, 5, 6, 3, 7]), :].reshape(d, main_cols)
    wg = w_in[l, :, main_cols:].astype(BF16)
    wgt = wg.T
    gb = mlstm_gate_b[l].reshape(-1).astype(F32)
    tabs = _retention_tables(ret_log_decay[l])
    head_g = jnp.concatenate([ret_norm_g[l], mlstm_norm_g[l]]).reshape(1, d).astype(F32)
    wr = w_router[l].T.astype(F32)
    wr_hi = wr.astype(BF16)
    wr_lo = (wr - wr_hi.astype(F32)).astype(BF16)

    def inproj(seq, mod_row, ts):
        cos2, sin2 = _rope_tables(n) if mod_row is None else (
            jnp.ones((seq.shape[1], LANES), F32), jnp.zeros((seq.shape[1], LANES), F32))
        return _inproj(seq, mod3, mod_row, norm_g[l, 0:1], w_main, wgt, mlstm_conv_w[l],
                       mlstm_conv_b[l].reshape(1, -1), gb.reshape(16, 1), cos2, sin2, ts)

    nst = 2 * HEADS
    zero_states = (jnp.zeros((b, nst, LANES, LANES), F32), jnp.zeros((b, nst, LANES, 2 * LANES), F32),
                   jnp.zeros((b, nst, LANES), F32))
    p_c, kt_c, gr_c = inproj(ctx, b, min(n_ctx, 512))
    ctx_states = _scan(p_c, kt_c, gr_c, tabs, zero_states, with_output=False)

    ts = min(n, 512)
    p_l, kt_l, gr_l = inproj(x, None, ts)
    o_f, o_b, _, _, _ = _scan(p_l, kt_l, gr_l, tabs, tuple(ctx_states), with_output=True)
    ts_moe = MOE_TILE
    x1, h2, rank, wdense, tile_cnt = _post(
        x, o_f, o_b, p_l, mod3, norm_g[l], head_g, w_out[l].astype(BF16), wr_hi, wr_lo,
        router_bias[l].reshape(n_exp, 1).astype(F32), ts_moe)

    nt = t_all // ts_moe
    cnt = tile_cnt[:, 0, :].astype(I32)
    run_rows = (cnt + ROW_ALIGN - 1) // ROW_ALIGN * ROW_ALIGN
    seg_cap = (jnp.sum(run_rows, axis=0) + SLOT_ROWS + EXPERT_BLOCK - 1) // EXPERT_BLOCK * EXPERT_BLOCK
    seg_end = jnp.cumsum(seg_cap)
    run_start = (seg_end - seg_cap)[None, :] + jnp.cumsum(run_rows, axis=0) - run_rows
    ov_rows = jnp.maximum(run_rows - SLOT_ROWS, 0)
    ov_off = jnp.cumsum(ov_rows, axis=1) - ov_rows
    n_spill = (jnp.sum(ov_rows, axis=1, keepdims=True) + SPILL_CHUNK - 1) // SPILL_CHUNK
    spills = ov_rows > 0
    n_ov = jnp.sum(spills.astype(I32), axis=1, keepdims=True)
    nth = jnp.cumsum(spills.astype(I32), axis=1) - 1
    is_jth = spills[:, None, :] & (nth[:, None, :] == jnp.arange(n_exp, dtype=I32)[None, :, None])
    compact = lambda v: jnp.sum(jnp.where(is_jth, v[:, None, :], 0), axis=2)
    ov_e = compact(jnp.broadcast_to(jnp.arange(n_exp, dtype=I32)[None, :], cnt.shape))
    fits_half = (run_rows <= HALF_ROWS).astype(I32)
    meta = jnp.concatenate([run_start, ov_e, compact(ov_rows // ROW_ALIGN), compact(ov_off), n_ov, n_spill,
                            fits_half], axis=1).astype(I32)
    meta = jnp.pad(meta, ((0, 0), (0, (-meta.shape[1]) % LANES))).reshape(nt, 1, -1)
    p_rows = -(-(t_all * TOP_K + nt * n_exp * (ROW_ALIGN - 1) + n_exp * (SLOT_ROWS + EXPERT_BLOCK - 1))
               // EXPERT_BLOCK) * EXPERT_BLOCK
    nb = p_rows // EXPERT_BLOCK
    blk_first = jnp.arange(nb, dtype=I32) * EXPERT_BLOCK
    block_e = jnp.minimum(jnp.sum((seg_end[None, :] <= blk_first[:, None]).astype(I32), axis=1), n_exp - 1)
    n_used = (seg_end[-1:] // EXPERT_BLOCK).astype(I32)
    seg = jnp.concatenate([seg_end.astype(I32), n_used])
    seg = jnp.pad(seg, (0, (-seg.shape[0]) % LANES)).reshape(1, 1, -1)

    xs = _dispatch(h2, rank, meta, seg, p_rows, ts_moe, n_exp)
    ys = _experts(xs, block_e, n_used, w_gate[l].astype(BF16), w_up[l].astype(BF16), w_down[l].astype(BF16),
                  EXPERT_BLOCK)
    out = _combine(ys, meta, rank.T, wdense.T, rank, wdense, x1.reshape(t_all, d), h2, mod3, norm_g[l],
                   ws_gate[l].astype(BF16), ws_up[l].astype(BF16), ws_down[l].astype(BF16), n, ts_moe, n_exp)
    return out.reshape(b, n, d)
```
